```python
import math
import jax
import jax.numpy as jnp
from jax import lax
import numpy as np

D_MODEL = 2048
BATCH = 2
SEQ = 8192
DEPTH = 2

GRID_W = 64
CTX_LEN = 256
N_BRANCH = 4
BRANCH_W = D_MODEL // 4
CHUNK = 64
CONV_W = 5
RMS_EPS = 1e-6
NEG_INF = -1e30
ROPE_BASE = 10000.0
GLA_H = 4
GLA_DV = BRANCH_W // GLA_H
GLA_DK = GLA_DV // 2
GLA_LR = 16
GLA_TAU = 16.0
NA_H = 4
NA_D = BRANCH_W // NA_H
NA_WIN_R = 8
NA_WIN_C = 16
GDN_H = 4
GDN_D = BRANCH_W // GDN_H
M2_P = 64
M2_H = BRANCH_W // M2_P
M2_N = 128
M2_G = 2
M2_CONV_CH = BRANCH_W + 2 * M2_G * M2_N
D_FF = ((8 * D_MODEL + 3 * 256 - 1) // (3 * 256)) * 256
GLA_IN = 2 * GLA_H * GLA_DK + 2 * BRANCH_W + 2 * GLA_LR
NA_IN = 3 * BRANCH_W
GDN_IN = 4 * BRANCH_W + 4 * GDN_H
M2_IN = BRANCH_W + M2_CONV_CH + 2 * M2_H
GATE_IN = N_BRANCH * D_MODEL
IN_COLS = GLA_IN + NA_IN + GDN_IN + M2_IN + GATE_IN

kernel_name = 'hybrid_gla_natten_gdn_ssd_dit'


def split_cols(t, sizes):
    return jnp.split(t, np.cumsum(sizes)[:-1].tolist(), axis=-1)


def rmsnorm(x, g):
    xf = x.astype(jnp.float32)
    y = xf * lax.rsqrt(jnp.mean(xf * xf, axis=-1, keepdims=True) + RMS_EPS)
    return (y * g.astype(jnp.float32)).astype(x.dtype)


def l2norm(x):
    xf = x.astype(jnp.float32)
    return xf * lax.rsqrt(jnp.sum(xf * xf, axis=-1, keepdims=True) + RMS_EPS)


def lower_tri(strict):
    return jnp.tril(jnp.ones((CHUNK, CHUNK), dtype=bool), -1 if strict else 0)


def seg_decay(b, mask):
    diff = b[..., :, None] - b[..., None, :]
    return jnp.where(mask, jnp.exp(jnp.where(mask, diff, 0.0)), 0.0)


def short_conv(x, w):
    return lax.conv_general_dilated(
        x, w.astype(x.dtype)[:, None, :], window_strides=(1,),
        padding=[(CONV_W // 2, CONV_W // 2)],
        dimension_numbers=('NWC', 'WIO', 'NWC'), feature_group_count=x.shape[-1])


def axial_rope(T, dim):
    n_freq = dim // 4
    freqs = ROPE_BASE ** (-jnp.arange(n_freq, dtype=jnp.float32) / n_freq)
    t = jnp.arange(T)
    row = (t // GRID_W).astype(jnp.float32)
    col = (t % GRID_W).astype(jnp.float32)
    ang = jnp.concatenate([row[:, None] * freqs, col[:, None] * freqs], axis=-1)
    return jnp.cos(ang), jnp.sin(ang)


def apply_rope(x, cos, sin):
    half = x.shape[-1] // 2
    x1, x2 = x[..., :half], x[..., half:]
    c = cos[None, :, None, :].astype(x.dtype)
    s = sin[None, :, None, :].astype(x.dtype)
    return jnp.concatenate([x1 * c - x2 * s, x1 * s + x2 * c], axis=-1)


def scan_states(decay, dS, S0):
    def step(S, xs):
        d, ds = xs
        return d * S + ds, S
    S_fin, S_starts = lax.scan(step, S0, (jnp.moveaxis(decay, 1, 0), jnp.moveaxis(dS, 1, 0)))
    return jnp.moveaxis(S_starts, 0, 1), S_fin


def gla_chunk(q, k, v, log_a, S0, with_out):
    B_, T, H, dk = q.shape
    dv = v.shape[-1]
    n = T // CHUNK
    q, k, v, log_a = (t.astype(jnp.float32).reshape(B_, n, CHUNK, H, t.shape[-1]) for t in (q, k, v, log_a))
    b = jnp.cumsum(log_a, axis=2)
    b_last = b[:, :, -1]
    dS = jnp.einsum('bnlhk,bnlhv->bnhkv', k * jnp.exp(b_last[:, :, None] - b), v)
    if S0 is None:
        S0 = jnp.zeros((B_, H, dk, dv), jnp.float32)
    S_starts, S_fin = scan_states(jnp.exp(b_last)[..., None], dS, S0)
    if not with_out:
        return None, S_fin
    q_in = q * jnp.exp(b)
    k_in = k * jnp.exp(-b)
    att = jnp.where(lower_tri(False), jnp.einsum('bnlhk,bnshk->bnhls', q_in, k_in), 0.0)
    o = jnp.einsum('bnlhk,bnhkv->bnlhv', q_in, S_starts) + jnp.einsum('bnhls,bnshv->bnlhv', att, v)
    return o.reshape(B_, T, H, dv), S_fin


def gdn_chunk(q, k, v, beta, log_a, S0, with_out):
    B_, T, H, dk = q.shape
    dv = v.shape[-1]
    n = T // CHUNK
    q, k, v = (t.astype(jnp.float32).reshape(B_, n, CHUNK, H, t.shape[-1]) for t in (q, k, v))
    beta = beta.astype(jnp.float32).reshape(B_, n, CHUNK, H)
    b = jnp.cumsum(log_a.astype(jnp.float32).reshape(B_, n, CHUNK, H), axis=2)
    bh = jnp.swapaxes(b, 2, 3)
    kk = jnp.einsum('bnlhk,bnshk->bnhls', k, k)
    a_mat = jnp.eye(CHUNK, dtype=jnp.float32) + jnp.swapaxes(beta, 2, 3)[..., None] * kk * seg_decay(bh, lower_tri(True))
    rhs = jnp.concatenate([k * (beta * jnp.exp(b))[..., None], v * beta[..., None]], axis=-1)
    sol = lax.linalg.triangular_solve(a_mat, jnp.transpose(rhs, (0, 1, 3, 2, 4)),
                                      left_side=True, lower=True, unit_diagonal=True)
    w, u0 = sol[..., :dk], sol[..., dk:]
    b_last = b[:, :, -1]
    k_end = k * jnp.exp(b_last[:, :, None] - b)[..., None]
    if S0 is None:
        S0 = jnp.zeros((B_, H, dk, dv), jnp.float32)

    def step(S, xs):
        w_n, u0_n, k_n, g_n = xs
        u_n = u0_n - jnp.einsum('bhlk,bhkv->bhlv', w_n, S)
        S_next = jnp.exp(g_n)[..., None, None] * S + jnp.einsum('blhk,bhlv->bhkv', k_n, u_n)
        return S_next, (S, u_n)

    S_fin, (S_starts, u) = lax.scan(step, S0, tuple(jnp.moveaxis(t, 1, 0) for t in (w, u0, k_end, b_last)))
    if not with_out:
        return None, S_fin
    S_starts = jnp.moveaxis(S_starts, 0, 1)
    u = jnp.moveaxis(u, 0, 1)
    qk = jnp.einsum('bnlhk,bnshk->bnhls', q, k) * seg_decay(bh, lower_tri(False))
    o = jnp.exp(b)[..., None] * jnp.einsum('bnlhk,bnhkv->bnlhv', q, S_starts) + jnp.einsum('bnhls,bnhsv->bnlhv', qk, u)
    return o.reshape(B_, T, H, dv), S_fin


def ssd_chunk(cq, bk, xv, log_a, S0, with_out):
    B_, T, H, ns = cq.shape
    p = xv.shape[-1]
    n = T // CHUNK
    cq, bk, xv = (t.astype(jnp.float32).reshape(B_, n, CHUNK, H, t.shape[-1]) for t in (cq, bk, xv))
    b = jnp.cumsum(log_a.astype(jnp.float32).reshape(B_, n, CHUNK, H), axis=2)
    b_last = b[:, :, -1]
    dS = jnp.einsum('bnlhd,bnlhp->bnhdp', bk * jnp.exp(b_last[:, :, None] - b)[..., None], xv)
    if S0 is None:
        S0 = jnp.zeros((B_, H, ns, p), jnp.float32)
    S_starts, S_fin = scan_states(jnp.exp(b_last)[..., None, None], dS, S0)
    if not with_out:
        return None, S_fin
    scores = jnp.einsum('bnlhd,bnshd->bnhls', cq, bk) * seg_decay(jnp.swapaxes(b, 2, 3), lower_tri(False))
    o = jnp.exp(b)[..., None] * jnp.einsum('bnlhd,bnhdp->bnlhp', cq, S_starts) + jnp.einsum('bnhls,bnshp->bnlhp', scores, xv)
    return o.reshape(B_, T, H, p), S_fin


def bidir_scan(chunk_fn, lat_dirs, ctx_dirs, with_ctx_out):
    y_lat, y_ctx = [], []
    for d in range(2):
        flip = (lambda t: jnp.flip(t, axis=1)) if d == 1 else (lambda t: t)
        o_c, s_c = chunk_fn(*map(flip, ctx_dirs[d]), None, with_ctx_out)
        o_l, _ = chunk_fn(*map(flip, lat_dirs[d]), s_c, True)
        y_lat.append(flip(o_l))
        if with_ctx_out:
            y_ctx.append(flip(o_c))
    return y_lat[0] + y_lat[1], (y_ctx[0] + y_ctx[1] if with_ctx_out else None)


def gla_mixer(pl, pc, a2, ab, norm_g, cos, sin, with_ctx_out):
    def prep(p, rope):
        B_, T = p.shape[:2]
        q, k, v, g, lr_f, lr_b = split_cols(p, [GLA_H * GLA_DK] * 2 + [BRANCH_W] * 2 + [GLA_LR] * 2)
        q = q.reshape(B_, T, GLA_H, GLA_DK)
        k = k.reshape(B_, T, GLA_H, GLA_DK)
        v = v.reshape(B_, T, GLA_H, GLA_DV)
        if rope:
            q = apply_rope(q, cos, sin)
            k = apply_rope(k, cos, sin)
        q = q * GLA_DK ** -0.5
        dirs = tuple(
            (q, k, v, (jax.nn.log_sigmoid(lr.astype(jnp.float32) @ a2[d] + ab[d]) / GLA_TAU).reshape(B_, T, GLA_H, GLA_DK))
            for d, lr in enumerate((lr_f, lr_b)))
        return dirs, g

    lat_dirs, g_l = prep(pl, True)
    ctx_dirs, g_c = prep(pc, False)
    o_l, o_c = bidir_scan(gla_chunk, lat_dirs, ctx_dirs, with_ctx_out)

    def finish(o, g):
        B_, T = g.shape[:2]
        gate = jax.nn.silu(g.astype(jnp.float32)).reshape(B_, T, GLA_H, GLA_DV)
        return (rmsnorm(o, norm_g) * gate).reshape(B_, T, BRANCH_W).astype(g.dtype)

    return finish(o_l, g_l), (finish(o_c, g_c) if with_ctx_out else None)


def na_mixer(pl, pc, rpb, rows, with_ctx_out):
    B_, T = pl.shape[:2]
    Lc = pc.shape[1]
    scale = NA_D ** -0.5
    q, k, v = (t.reshape(B_, T, NA_H, NA_D) for t in split_cols(pl, [BRANCH_W] * 3))
    qc, kc, vc = (t.reshape(B_, Lc, NA_H, NA_D) for t in split_cols(pc, [BRANCH_W] * 3))
    kr = min(NA_WIN_R, rows)
    r = jnp.arange(rows)
    row_idx = jnp.clip(r - NA_WIN_R // 2, 0, rows - kr)[:, None] + jnp.arange(kr)[None, :]
    ci = jnp.arange(GRID_W)
    c0 = jnp.clip(ci - NA_WIN_C // 2, 0, GRID_W - NA_WIN_C)
    col_ok = (ci[None, :] >= c0[:, None]) & (ci[None, :] < c0[:, None] + NA_WIN_C)
    dr = row_idx - r[:, None] + NA_WIN_R - 1
    dc = jnp.clip(ci[None, :] - ci[:, None], 1 - NA_WIN_C, NA_WIN_C - 1) + NA_WIN_C - 1
    bias = rpb[:, dr[:, None, :, None], dc[None, :, None, :]]
    qg = q.reshape(B_, rows, GRID_W, NA_H, NA_D) * scale
    kg = k.reshape(B_, rows, GRID_W, NA_H, NA_D)[:, row_idx]
    vg = v.reshape(B_, rows, GRID_W, NA_H, NA_D)[:, row_idx]
    s_nb = jnp.einsum('bnihd,bnrjhd->bhnirj', qg, kg).astype(jnp.float32) + bias.astype(jnp.float32)[None]
    s_nb = jnp.where(col_ok[:, None, :], s_nb, NEG_INF)
    s_cx = jnp.einsum('bnihd,bchd->bhnic', qg, kc).astype(jnp.float32)
    s = jnp.concatenate([s_nb.reshape(B_, NA_H, rows, GRID_W, kr * GRID_W), s_cx], axis=-1)
    p = jax.nn.softmax(s, axis=-1).astype(v.dtype)
    p_nb = p[..., :kr * GRID_W].reshape(B_, NA_H, rows, GRID_W, kr, GRID_W)
    o = jnp.einsum('bhnirj,bnrjhd->bnihd', p_nb, vg) + jnp.einsum('bhnic,bchd->bnihd', p[..., kr * GRID_W:], vc)
    y_lat = o.reshape(B_, T, BRANCH_W)
    if not with_ctx_out:
        return y_lat, None
    s_c = jnp.einsum('bqhd,bkhd->bhqk', qc * scale, kc).astype(jnp.float32)
    p_c = jax.nn.softmax(s_c, axis=-1).astype(vc.dtype)
    y_ctx = jnp.einsum('bhqk,bkhd->bqhd', p_c, vc).reshape(B_, Lc, BRANCH_W)
    return y_lat, y_ctx


def gdn_mixer(pl, pc, conv_w, a_log, dt_bias, norm_g, with_ctx_out):
    def prep(p):
        B_, T = p.shape[:2]
        qkv, z, b_f, b_b, a_f, a_b = split_cols(p, [3 * BRANCH_W, BRANCH_W] + [GDN_H] * 4)
        qkv = jax.nn.silu(short_conv(qkv, conv_w))
        q, k, v = (t.reshape(B_, T, GDN_H, GDN_D) for t in split_cols(qkv, [BRANCH_W] * 3))
        q = l2norm(q) * GDN_D ** -0.5
        k = l2norm(k)
        dirs = tuple(
            (q, k, v, jax.nn.sigmoid(bb.astype(jnp.float32)),
             -jnp.exp(a_log[d].astype(jnp.float32)) * jax.nn.softplus(aa.astype(jnp.float32) + dt_bias[d]))
            for d, (bb, aa) in enumerate(((b_f, a_f), (b_b, a_b))))
        return dirs, z

    lat_dirs, z_l = prep(pl)
    ctx_dirs, z_c = prep(pc)
    o_l, o_c = bidir_scan(gdn_chunk, lat_dirs, ctx_dirs, with_ctx_out)

    def finish(o, z):
        B_, T = z.shape[:2]
        gate = jax.nn.silu(z.astype(jnp.float32)).reshape(B_, T, GDN_H, GDN_D)
        return (rmsnorm(o, norm_g) * gate).reshape(B_, T, BRANCH_W).astype(z.dtype)

    return finish(o_l, z_l), (finish(o_c, z_c) if with_ctx_out else None)


def m2_mixer(pl, pc, conv_w, conv_b, a_log, dt_bias, d_skip, norm_g, with_ctx_out):
    def prep(p):
        B_, T = p.shape[:2]
        z, xbc, dt_f, dt_b = split_cols(p, [BRANCH_W, M2_CONV_CH, M2_H, M2_H])
        xbc = jax.nn.silu(short_conv(xbc, conv_w) + conv_b)
        xs, bm, cm = split_cols(xbc, [BRANCH_W, M2_G * M2_N, M2_G * M2_N])
        xs = xs.reshape(B_, T, M2_H, M2_P)
        bm = jnp.repeat(bm.reshape(B_, T, M2_G, M2_N), M2_H // M2_G, axis=2)
        cm = jnp.repeat(cm.reshape(B_, T, M2_G, M2_N), M2_H // M2_G, axis=2)
        dirs = []
        for d, dt_raw in enumerate((dt_f, dt_b)):
            dt = jax.nn.softplus(dt_raw.astype(jnp.float32) + dt_bias[d])
            dirs.append((cm, bm, xs.astype(jnp.float32) * dt[..., None], -jnp.exp(a_log[d].astype(jnp.float32)) * dt))
        return tuple(dirs), z, xs

    lat_dirs, z_l, xs_l = prep(pl)
    ctx_dirs, z_c, xs_c = prep(pc)
    o_l, o_c = bidir_scan(ssd_chunk, lat_dirs, ctx_dirs, with_ctx_out)

    def finish(o, z, xs):
        B_, T = z.shape[:2]
        y = (o + d_skip.astype(jnp.float32)[:, None] * xs.astype(jnp.float32)).reshape(B_, T, BRANCH_W)
        return rmsnorm(y * jax.nn.silu(z.astype(jnp.float32)), norm_g).astype(z.dtype)

    return finish(o_l, z_l, xs_l), (finish(o_c, z_c, xs_c) if with_ctx_out else None)


def merge_branches(branches, gate_cols, w_branch, b_merge, w_out):
    gates = jax.nn.sigmoid(gate_cols.reshape(gate_cols.shape[:-1] + (N_BRANCH, D_MODEL)) + b_merge)
    merged = gates[..., 0, :] * (branches[0] @ w_branch[0])
    for i in range(1, N_BRANCH):
        merged = merged + gates[..., i, :] * (branches[i] @ w_branch[i])
    return merged @ w_out


def swiglu(h, w1, w3, w2):
    return (jax.nn.silu(h @ w1) * (h @ w3)) @ w2


def token_mixers(h, hc, p, rows, cos, sin, with_ctx_out):
    sizes = [GLA_IN, NA_IN, GDN_IN, M2_IN, GATE_IN]
    gla_l, na_l, gdn_l, m2_l, gate_l = split_cols(h @ p['w_in'], sizes)
    gla_c, na_c, gdn_c, m2_c, gate_c = split_cols(hc @ p['w_in'], sizes)
    ya = gla_mixer(gla_l, gla_c, p['gla_a2'], p['gla_ab'], p['gla_norm_g'], cos, sin, with_ctx_out)
    yb = na_mixer(na_l, na_c, p['na_rpb'], rows, with_ctx_out)
    yc = gdn_mixer(gdn_l, gdn_c, p['gdn_conv'], p['gdn_a_log'], p['gdn_dt_bias'], p['gdn_norm_g'], with_ctx_out)
    yd = m2_mixer(m2_l, m2_c, p['m2_conv'], p['m2_conv_b'], p['m2_a_log'], p['m2_dt_bias'], p['m2_d'],
                  p['m2_norm_g'], with_ctx_out)
    y_lat = merge_branches((ya[0], yb[0], yc[0], yd[0]), gate_l, p['w_branch'], p['b_merge'], p['w_out'])
    if not with_ctx_out:
        return y_lat, None
    y_ctx = merge_branches((ya[1], yb[1], yc[1], yd[1]), gate_c, p['w_branch'], p['b_merge'], p['w_out'])
    return y_lat, y_ctx


def setup_inputs(seed: int = 0) -> dict:
    key = jax.random.key(seed)
    keys = iter(jax.random.split(key, 48))

    def nrm(shape, scale):
        return scale * jax.random.normal(next(keys), shape, jnp.float32)

    def gain(shape):
        return 1.0 + nrm(shape, 0.1)

    def a_log(shape):
        return jnp.log(jax.random.uniform(next(keys), shape, jnp.float32, 1.0, 16.0))

    def dt_bias(shape):
        dt = jnp.exp(jax.random.uniform(next(keys), shape, jnp.float32, math.log(1e-3), math.log(1e-1)))
        return dt + jnp.log(-jnp.expm1(-dt))

    L, D = DEPTH, D_MODEL
    return {
        'x': nrm((BATCH, SEQ, D), 1.0),
        'c': nrm((BATCH, D), 1.0),
        'ctx': nrm((BATCH, CTX_LEN, D), 1.0),
        'c_ctx': nrm((D,), 1.0),
        'norm1_g': gain((L, D)),
        'norm2_g': gain((L, D)),
        'w_ada': nrm((L, D, 6 * D), 0.5 * D ** -0.5),
        'b_ada': nrm((L, 6 * D), 0.02),
        'w_in': nrm((L, D, IN_COLS), D ** -0.5),
        'b_merge': nrm((L, N_BRANCH, D), 0.1),
        'gla_a2': nrm((L, 2, GLA_LR, GLA_H * GLA_DK), GLA_LR ** -0.5),
        'gla_ab': nrm((L, 2, GLA_H * GLA_DK), 0.1),
        'gla_norm_g': gain((L, GLA_DV)),
        'na_rpb': nrm((L, NA_H, 2 * NA_WIN_R - 1, 2 * NA_WIN_C - 1), 0.1),
        'gdn_conv': nrm((L, CONV_W, 3 * BRANCH_W), CONV_W ** -0.5),
        'gdn_a_log': a_log((L, 2, GDN_H)),
        'gdn_dt_bias': dt_bias((L, 2, GDN_H)),
        'gdn_norm_g': gain((L, GDN_D)),
        'm2_conv': nrm((L, CONV_W, M2_CONV_CH), CONV_W ** -0.5),
        'm2_conv_b': nrm((L, M2_CONV_CH), 0.02),
        'm2_a_log': a_log((L, 2, M2_H)),
        'm2_dt_bias': dt_bias((L, 2, M2_H)),
        'm2_d': gain((L, M2_H)),
        'm2_norm_g': gain((L, BRANCH_W)),
        'w_branch': nrm((L, N_BRANCH, BRANCH_W, D), BRANCH_W ** -0.5),
        'w_out': nrm((L, D, D), D ** -0.5),
        'w_ffn1': nrm((L, D, D_FF), D ** -0.5),
        'w_ffn3': nrm((L, D, D_FF), D ** -0.5),
        'w_ffn2': nrm((L, D_FF, D), D_FF ** -0.5),
        'final_norm_g': gain((D,)),
    }


def reference(x, c, ctx, c_ctx, norm1_g, norm2_g, w_ada, b_ada, w_in, b_merge, gla_a2, gla_ab, gla_norm_g,
              na_rpb, gdn_conv, gdn_a_log, gdn_dt_bias, gdn_norm_g, m2_conv, m2_conv_b, m2_a_log, m2_dt_bias,
              m2_d, m2_norm_g, w_branch, w_out, w_ffn1, w_ffn3, w_ffn2, final_norm_g):
    T = x.shape[1]
    rows = T // GRID_W
    cos, sin = axial_rope(T, GLA_DK)
    for l in range(DEPTH):
        last = l == DEPTH - 1
        p = {
            'w_in': w_in[l], 'b_merge': b_merge[l], 'gla_a2': gla_a2[l], 'gla_ab': gla_ab[l],
            'gla_norm_g': gla_norm_g[l], 'na_rpb': na_rpb[l], 'gdn_conv': gdn_conv[l],
            'gdn_a_log': gdn_a_log[l], 'gdn_dt_bias': gdn_dt_bias[l], 'gdn_norm_g': gdn_norm_g[l],
            'm2_conv': m2_conv[l], 'm2_conv_b': m2_conv_b[l], 'm2_a_log': m2_a_log[l],
            'm2_dt_bias': m2_dt_bias[l], 'm2_d': m2_d[l], 'm2_norm_g': m2_norm_g[l],
            'w_branch': w_branch[l], 'w_out': w_out[l],
        }
        sh1, sc1, g1, sh2, sc2, g2 = jnp.split((jax.nn.silu(c) @ w_ada[l] + b_ada[l])[:, None, :], 6, axis=-1)
        csh1, csc1, cg1, csh2, csc2, cg2 = jnp.split(jax.nn.silu(c_ctx) @ w_ada[l] + b_ada[l], 6, axis=-1)
        h = rmsnorm(x, norm1_g[l]) * (1 + sc1) + sh1
        hc = rmsnorm(ctx, norm1_g[l]) * (1 + csc1) + csh1
        y, yc = token_mixers(h, hc, p, rows, cos, sin, not last)
        x = x + g1 * y
        x = x + g2 * swiglu(rmsnorm(x, norm2_g[l]) * (1 + sc2) + sh2, w_ffn1[l], w_ffn3[l], w_ffn2[l])
        if not last:
            ctx = ctx + cg1 * yc
            ctx = ctx + cg2 * swiglu(rmsnorm(ctx, norm2_g[l]) * (1 + csc2) + csh2, w_ffn1[l], w_ffn3[l], w_ffn2[l])
    return rmsnorm(x, final_norm_g)
```

```python
import functools

import numpy as np
import jax
import jax.numpy as jnp
from jax import lax
from jax.experimental import pallas as pl
from jax.experimental.pallas import tpu as pltpu

D_MODEL = 2048
GRID_W = 64
N_BRANCH = 4
BRANCH_W = D_MODEL // 4
CHUNK = 64
CONV_W = 5
RMS_EPS = 1e-6
NEG_INF = -1e30
ROPE_BASE = 10000.0
GLA_H = 4
GLA_DV = BRANCH_W // GLA_H
GLA_DK = GLA_DV // 2
GLA_LR = 16
GLA_TAU = 16.0
NA_H = 4
NA_D = BRANCH_W // NA_H
NA_WIN_R = 8
NA_WIN_C = 16
GDN_H = 4
GDN_D = BRANCH_W // GDN_H
M2_P = 64
M2_H = BRANCH_W // M2_P
M2_N = 128
M2_G = 2
M2_CONV_CH = BRANCH_W + 2 * M2_G * M2_N
D_FF = ((8 * D_MODEL + 3 * 256 - 1) // (3 * 256)) * 256
GLA_IN = 2 * GLA_H * GLA_DK + 2 * BRANCH_W + 2 * GLA_LR
NA_IN = 3 * BRANCH_W
GDN_IN = 4 * BRANCH_W + 4 * GDN_H
M2_IN = BRANCH_W + M2_CONV_CH + 2 * M2_H
MIX_IN = GLA_IN + NA_IN + GDN_IN + M2_IN

F32 = jnp.float32
BF16 = jnp.bfloat16

LANE = 128
SUBLANE = 8
V7X_VMEM_BYTES = 64 * 1024 * 1024
VMEM_LIMIT = V7X_VMEM_BYTES - 8 * 1024 * 1024

TOK_BLK = 4 * CHUNK
ROWS_PER_BLK = TOK_BLK // GRID_W
HALO = SUBLANE
MOD_ROWS = 8

P_GLA_V, P_GLA_G = 0, 512
P_NA_Q, P_NA_K, P_NA_V = 1024, 1536, 2048
P_GDN_QKV, P_GDN_Z = 2560, 4096
P_M2_Z, P_M2_XBC = 4608, 5120
P_GLA_Q, P_GLA_K = 6144, 6400
P_GLA_LR, P_GDN_SC, P_M2_DT = 6656, 6784, 6912
P_COLS = 7168


def _mix_column_perm():
    perm = np.full((P_COLS,), MIX_IN, np.int32)

    def put(dst, src, n):
        perm[dst:dst + n] = np.arange(src, src + n)

    gla, na, gdn, m2 = 0, GLA_IN, GLA_IN + NA_IN, GLA_IN + NA_IN + GDN_IN
    put(P_GLA_Q, gla, 256)
    put(P_GLA_K, gla + 256, 256)
    put(P_GLA_V, gla + 512, 512)
    put(P_GLA_G, gla + 1024, 512)
    put(P_GLA_LR, gla + 1536, 2 * GLA_LR)
    put(P_NA_Q, na, 512)
    put(P_NA_K, na + 512, 512)
    put(P_NA_V, na + 1024, 512)
    put(P_GDN_QKV, gdn, 1536)
    put(P_GDN_Z, gdn + 1536, 512)
    put(P_GDN_SC, gdn + 2048, 4 * GDN_H)
    put(P_M2_Z, m2, 512)
    put(P_M2_XBC, m2 + 512, M2_CONV_CH)
    put(P_M2_DT, m2 + 512 + M2_CONV_CH, 2 * M2_H)
    return perm


def _cparams(sem):
    return pltpu.CompilerParams(dimension_semantics=sem, vmem_limit_bytes=VMEM_LIMIT)


def _sigmoid(x):
    return 1.0 / (1.0 + jnp.exp(-x))


def _silu(x):
    return x * _sigmoid(x)


def _softplus(x):
    return jnp.maximum(x, 0.0) + jnp.log1p(jnp.exp(-jnp.abs(x)))


def _dot(a, b):
    return jnp.dot(a.astype(BF16), b.astype(BF16), preferred_element_type=F32)


def _dot_nt(a, b):
    return lax.dot_general(a.astype(BF16), b.astype(BF16), (((1,), (1,)), ((), ())),
                           preferred_element_type=F32)


def _dot_tn(a, b):
    return lax.dot_general(a.astype(BF16), b.astype(BF16), (((0,), (0,)), ((), ())),
                           preferred_element_type=F32)


def _dot_hi(a, b):
    return jnp.dot(a, b, precision=lax.Precision.HIGHEST, preferred_element_type=F32)


def _iota2(shape, dim):
    return lax.broadcasted_iota(jnp.int32, shape, dim)


def _idiv(x, n):
    assert n & (n - 1) == 0
    return lax.shift_right_logical(x, jnp.int32(n.bit_length() - 1))


def _imod(x, n):
    assert n & (n - 1) == 0
    return x & (n - 1)


def _chunk_mask(rev, strict=False):
    r = _iota2((TOK_BLK, TOK_BLK), 0)
    c = _iota2((TOK_BLK, TOK_BLK), 1)
    same = _idiv(r, CHUNK) == _idiv(c, CHUNK)
    if rev:
        tri = (c > r) if strict else (c >= r)
    else:
        tri = (c < r) if strict else (c <= r)
    return same & tri


def _chunk_order(rev):
    n = TOK_BLK // CHUNK
    return list(range(n - 1, -1, -1)) if rev else list(range(n))


def _row_split(i, tm, s1):
    r0 = i * tm
    return lax.div(r0, jnp.int32(s1)), lax.rem(r0, jnp.int32(s1))


def _pick_mod(ref, b, nb, is_ctx):
    return jnp.where(is_ctx, ref[nb:nb + 1, :], ref[pl.ds(b, 1), :])


def _ada_kernel(c_ref, w_ref, b_ref, o_ref):
    a = _silu(c_ref[...])
    o_ref[...] = _dot(a, w_ref[...]) + b_ref[...]


def _ada(cvec, w_ada, b_ada):
    L, D, N = w_ada.shape
    tn = 1024
    return pl.pallas_call(
        _ada_kernel,
        grid=(L, N // tn),
        in_specs=[pl.BlockSpec((MOD_ROWS, D), lambda l, j: (0, 0)),
                  pl.BlockSpec((None, D, tn), lambda l, j: (l, 0, j)),
                  pl.BlockSpec((None, 1, tn), lambda l, j: (l, 0, j))],
        out_specs=pl.BlockSpec((None, MOD_ROWS, tn), lambda l, j: (l, 0, j)),
        out_shape=jax.ShapeDtypeStruct((L, MOD_ROWS, N), F32),
        compiler_params=_cparams(("arbitrary", "arbitrary")),
        name="ada_mod",
    )(cvec, w_ada, b_ada.reshape(L, 1, N))


def _normmod_kernel(x_ref, g_ref, sc_ref, sh_ref, o_ref, *, tm, s1, lc, nb):
    b, pos0 = _row_split(pl.program_id(0), tm, s1)
    is_ctx = (pos0 + _iota2((tm, 1), 0)) < lc
    x = x_ref[...]
    y = x * lax.rsqrt(jnp.mean(x * x, axis=-1, keepdims=True) + RMS_EPS) * g_ref[...]
    sc = _pick_mod(sc_ref, b, nb, is_ctx)
    sh = _pick_mod(sh_ref, b, nb, is_ctx)
    o_ref[...] = (y * (1.0 + sc) + sh).astype(o_ref.dtype)


def _normmod(xs, g, sc, sh, *, tm, s1, lc, nb):
    R, D = xs.shape
    return pl.pallas_call(
        functools.partial(_normmod_kernel, tm=tm, s1=s1, lc=lc, nb=nb),
        grid=(R // tm,),
        in_specs=[pl.BlockSpec((tm, D), lambda i: (i, 0)),
                  pl.BlockSpec((1, D), lambda i: (0, 0)),
                  pl.BlockSpec((MOD_ROWS, D), lambda i: (0, 0)),
                  pl.BlockSpec((MOD_ROWS, D), lambda i: (0, 0))],
        out_specs=pl.BlockSpec((tm, D), lambda i: (i, 0)),
        out_shape=jax.ShapeDtypeStruct((R, D), BF16),
        compiler_params=_cparams(("arbitrary",)),
        name="norm_mod",
    )(xs, g.reshape(1, D), sc, sh)


def _mm_kernel(a_ref, w_ref, o_ref):
    o_ref[...] = jnp.dot(a_ref[...], w_ref[...], preferred_element_type=F32).astype(o_ref.dtype)


def _matmul(a, w, *, tm, tn, out_dtype):
    R, K = a.shape
    N = w.shape[1]
    return pl.pallas_call(
        _mm_kernel,
        grid=(R // tm, N // tn),
        in_specs=[pl.BlockSpec((tm, K), lambda i, j: (i, 0)),
                  pl.BlockSpec((K, tn), lambda i, j: (0, j))],
        out_specs=pl.BlockSpec((tm, tn), lambda i, j: (i, j)),
        out_shape=jax.ShapeDtypeStruct((R, N), out_dtype),
        compiler_params=_cparams(("arbitrary", "arbitrary")),
        name="mix_proj",
    )(a, w)


def _mm_res_kernel(a_ref, w_ref, x_ref, gt_ref, o_ref, *, tm, s1, lc, nb):
    b, pos0 = _row_split(pl.program_id(0), tm, s1)
    is_ctx = (pos0 + _iota2((tm, 1), 0)) < lc
    acc = jnp.dot(a_ref[...], w_ref[...], preferred_element_type=F32)
    o_ref[...] = x_ref[...] + _pick_mod(gt_ref, b, nb, is_ctx) * acc


def _matmul_residual(a, w, xs, gate, *, tm, tn, s1, lc, nb, name):
    R, K = a.shape
    N = w.shape[1]
    return pl.pallas_call(
        functools.partial(_mm_res_kernel, tm=tm, s1=s1, lc=lc, nb=nb),
        grid=(R // tm, N // tn),
        in_specs=[pl.BlockSpec((tm, K), lambda i, j: (i, 0)),
                  pl.BlockSpec((K, tn), lambda i, j: (0, j)),
                  pl.BlockSpec((tm, tn), lambda i, j: (i, j)),
                  pl.BlockSpec((MOD_ROWS, tn), lambda i, j: (0, j))],
        out_specs=pl.BlockSpec((tm, tn), lambda i, j: (i, j)),
        out_shape=jax.ShapeDtypeStruct((R, N), F32),
        compiler_params=_cparams(("arbitrary", "arbitrary")),
        name=name,
    )(a, w, xs, gate)


def _ffn_up_kernel(a_ref, w1_ref, w3_ref, o_ref):
    a = a_ref[...]
    u = jnp.dot(a, w1_ref[...], preferred_element_type=F32)
    v = jnp.dot(a, w3_ref[...], preferred_element_type=F32)
    o_ref[...] = (_silu(u) * v).astype(o_ref.dtype)


def _ffn_up(a, w1, w3, *, tm, tn):
    R, K = a.shape
    N = w1.shape[1]
    return pl.pallas_call(
        _ffn_up_kernel,
        grid=(R // tm, N // tn),
        in_specs=[pl.BlockSpec((tm, K), lambda i, j: (i, 0)),
                  pl.BlockSpec((K, tn), lambda i, j: (0, j)),
                  pl.BlockSpec((K, tn), lambda i, j: (0, j))],
        out_specs=pl.BlockSpec((tm, tn), lambda i, j: (i, j)),
        out_shape=jax.ShapeDtypeStruct((R, N), BF16),
        compiler_params=_cparams(("arbitrary", "arbitrary")),
        name="ffn_up",
    )(a, w1, w3)


def _merge_kernel(h_ref, wg_ref, bm_ref, ya_ref, yb_ref, yc_ref, yd_ref, wb_ref, o_ref):
    h = h_ref[...]
    acc = None
    for i, y_ref in enumerate((ya_ref, yb_ref, yc_ref, yd_ref)):
        gate = _sigmoid(jnp.dot(h, wg_ref[i], preferred_element_type=F32) + bm_ref[i])
        term = gate * jnp.dot(y_ref[...], wb_ref[i], preferred_element_type=F32)
        acc = term if acc is None else acc + term
    o_ref[...] = acc.astype(o_ref.dtype)


def _merge(h, wg, b_merge, ys, wb, *, tm, tn):
    R, D = h.shape
    N = wg.shape[2]
    yspec = pl.BlockSpec((tm, BRANCH_W), lambda i, j: (i, 0))
    return pl.pallas_call(
        _merge_kernel,
        grid=(R // tm, N // tn),
        in_specs=[pl.BlockSpec((tm, D), lambda i, j: (i, 0)),
                  pl.BlockSpec((N_BRANCH, D, tn), lambda i, j: (0, 0, j)),
                  pl.BlockSpec((N_BRANCH, 1, tn), lambda i, j: (0, 0, j)),
                  yspec, yspec, yspec, yspec,
                  pl.BlockSpec((N_BRANCH, BRANCH_W, tn), lambda i, j: (0, 0, j))],
        out_specs=pl.BlockSpec((tm, tn), lambda i, j: (i, j)),
        out_shape=jax.ShapeDtypeStruct((R, N), BF16),
        compiler_params=_cparams(("arbitrary", "arbitrary")),
        name="merge",
    )(h, wg, b_merge.reshape(N_BRANCH, 1, N), *ys, wb)


def _conv_kernel(xm_ref, xp_ref, xn_ref, w_ref, b_ref, o_ref, *, nblk):
    blk = lax.rem(pl.program_id(0), jnp.int32(nblk))
    prev_ok = blk >= 2
    next_ok = (blk >= 1) & (blk < nblk - 1)
    xp = jnp.where(prev_ok, xp_ref[...], 0.0)
    xn = jnp.where(next_ok, xn_ref[...], 0.0)
    xpad = jnp.concatenate([xp, xm_ref[...], xn], axis=0)
    w = w_ref[...]
    acc = b_ref[...] + w[0:1, :] * xpad[HALO - 2:HALO - 2 + TOK_BLK, :]
    for j in range(1, CONV_W):
        acc = acc + w[j:j + 1, :] * xpad[HALO - 2 + j:HALO - 2 + j + TOK_BLK, :]
    o_ref[...] = _silu(acc)


def _conv_silu(p, col0, w, bias, *, nblk):
    R = p.shape[0]
    C = w.shape[1]
    cw = 512
    cb0 = col0 // cw
    hb = TOK_BLK // HALO
    nhalo = R // HALO
    wpad = jnp.concatenate([w, jnp.zeros((SUBLANE - CONV_W, C), F32)], axis=0)
    return pl.pallas_call(
        functools.partial(_conv_kernel, nblk=nblk),
        grid=(R // TOK_BLK, C // cw),
        in_specs=[pl.BlockSpec((TOK_BLK, cw), lambda r, c: (r, cb0 + c)),
                  pl.BlockSpec((HALO, cw), lambda r, c: (jnp.maximum(r * hb - 1, 0), cb0 + c)),
                  pl.BlockSpec((HALO, cw), lambda r, c: (jnp.minimum((r + 1) * hb, nhalo - 1), cb0 + c)),
                  pl.BlockSpec((SUBLANE, cw), lambda r, c: (0, c)),
                  pl.BlockSpec((1, cw), lambda r, c: (0, c))],
        out_specs=pl.BlockSpec((TOK_BLK, cw), lambda r, c: (r, c)),
        out_shape=jax.ShapeDtypeStruct((R, C), F32),
        compiler_params=_cparams(("arbitrary", "arbitrary")),
        name="short_conv",
    )(p, p, p, wpad, bias.reshape(1, C))


def _scan_block(rev, nblk):
    if rev:
        return lambda s: jnp.where(s == 0, 0, nblk - s)
    return lambda s: s


def _head_rmsnorm_gate(o, g, gate, n_head, width):
    outs = []
    for h in range(n_head):
        oh = o[:, h * width:(h + 1) * width]
        yh = oh * lax.rsqrt(jnp.mean(oh * oh, axis=-1, keepdims=True) + RMS_EPS) * g
        outs.append(yh * _silu(gate[:, h * width:(h + 1) * width]))
    return jnp.concatenate(outs, axis=1)


def _gla_kernel(q_ref, k_ref, v_ref, g_ref, lr_ref, cos_ref, sin_ref, a2_ref, ab_ref, lm_ref, ng_ref,
                *rest, rev):
    if rev:
        of_ref, y_ref, st_ref = rest
    else:
        o_ref, st_ref = rest

    @pl.when(pl.program_id(1) == 0)
    def _():
        st_ref[...] = jnp.zeros_like(st_ref)

    hk = GLA_H * GLA_DK
    lane = _iota2((TOK_BLK, hk), 1)
    first_half = _imod(lane, GLA_DK) < (GLA_DK // 2)
    cos = cos_ref[...]
    sin = sin_ref[...]

    def rope(x):
        partner = jnp.where(first_half, pltpu.roll(x, hk - GLA_DK // 2, 1), pltpu.roll(x, GLA_DK // 2, 1))
        return x * cos + partner * sin

    q = rope(q_ref[...]) * GLA_DK ** -0.5
    k = rope(k_ref[...])
    v = v_ref[...]
    loga = -_softplus(-(_dot(lr_ref[...], a2_ref[...]) + ab_ref[...])) / GLA_TAU
    b = _dot_hi(lm_ref[...], loga)
    q_in = q * jnp.exp(b)
    k_in = k * jnp.exp(-b)

    mask = _chunk_mask(rev)
    head_of_lane = _idiv(lane, GLA_DK)
    o_heads = []
    for h in range(GLA_H):
        att = _dot_nt(jnp.where(head_of_lane == h, q_in, 0.0), k_in)
        att = jnp.where(mask, att, 0.0)
        o_heads.append(_dot(att, v[:, h * GLA_DV:(h + 1) * GLA_DV]))
    o_intra = jnp.concatenate(o_heads, axis=1)

    st = st_ref[...]
    diag = _idiv(_iota2(st.shape, 0), GLA_DV) == _idiv(_iota2(st.shape, 1), GLA_DK)
    o_inter = [None] * (TOK_BLK // CHUNK)
    for c in _chunk_order(rev):
        rows = slice(c * CHUNK, (c + 1) * CHUNK)
        b_c = b[rows]
        b_last = b_c[0:1] if rev else b_c[CHUNK - 1:CHUNK]
        o_inter[c] = _dot_nt(q_in[rows], st)
        k_end = k[rows] * jnp.exp(b_last - b_c)
        ds = _dot_tn(v[rows], k_end)
        st = st * jnp.exp(b_last) + jnp.where(diag, ds, 0.0)
    st_ref[...] = st
    o = o_intra + jnp.concatenate(o_inter, axis=0)

    if rev:
        y_ref[...] = _head_rmsnorm_gate(of_ref[...] + o, ng_ref[...], g_ref[...], GLA_H, GLA_DV).astype(y_ref.dtype)
    else:
        o_ref[...] = o


def _cumsum_matrix(rev):
    r = np.arange(TOK_BLK)[:, None]
    c = np.arange(TOK_BLK)[None, :]
    same = (r // CHUNK) == (c // CHUNK)
    tri = (c >= r) if rev else (c <= r)
    return jnp.asarray((same & tri).astype(np.float32))


def _gla_dir(p, cos, sin, a2, ab, ng, o_f, *, rev, nbat, nblk):
    R = p.shape[0]
    blk = _scan_block(rev, nblk)
    d = 1 if rev else 0

    def pspec(width, col):
        return pl.BlockSpec((TOK_BLK, width), lambda b, s: (b * nblk + blk(s), col // width))

    a2d = jnp.zeros((LANE, GLA_H * GLA_DK), F32).at[d * GLA_LR:(d + 1) * GLA_LR].set(a2[d])
    const = lambda shape: pl.BlockSpec(shape, lambda b, s: (0, 0))
    in_specs = [pspec(256, P_GLA_Q), pspec(256, P_GLA_K), pspec(512, P_GLA_V), pspec(512, P_GLA_G),
                pspec(LANE, P_GLA_LR),
                pl.BlockSpec((TOK_BLK, 256), lambda b, s: (blk(s), 0)),
                pl.BlockSpec((TOK_BLK, 256), lambda b, s: (blk(s), 0)),
                const((LANE, 256)), const((1, 256)), const((TOK_BLK, TOK_BLK)), const((1, GLA_DV))]
    args = [p, p, p, p, p, cos, sin, a2d, ab[d].reshape(1, -1), _cumsum_matrix(rev), ng.reshape(1, -1)]
    ospec = pl.BlockSpec((TOK_BLK, BRANCH_W), lambda b, s: (b * nblk + blk(s), 0))
    if rev:
        in_specs.append(ospec)
        args.append(o_f)
    return pl.pallas_call(
        functools.partial(_gla_kernel, rev=rev),
        grid=(nbat, nblk),
        in_specs=in_specs,
        out_specs=ospec,
        out_shape=jax.ShapeDtypeStruct((R, BRANCH_W), BF16 if rev else F32),
        scratch_shapes=[pltpu.VMEM((GLA_H * GLA_DV, GLA_H * GLA_DK), F32)],
        compiler_params=_cparams(("arbitrary", "arbitrary")),
        name="gla_bwd" if rev else "gla_fwd",
    )(*args)


def _rope_tables(lc, t):
    n_freq = GLA_DK // 4
    freqs = ROPE_BASE ** (-jnp.arange(n_freq, dtype=F32) / n_freq)
    tt = jnp.arange(t)
    row = (tt // GRID_W).astype(F32)
    col = (tt % GRID_W).astype(F32)
    ang = jnp.concatenate([row[:, None] * freqs, col[:, None] * freqs], axis=-1)
    cos, sin = jnp.cos(ang), jnp.sin(ang)
    cos = jnp.concatenate([jnp.ones((lc, GLA_DK // 2), F32), cos], axis=0)
    sin = jnp.concatenate([jnp.zeros((lc, GLA_DK // 2), F32), sin], axis=0)
    cos_h = jnp.concatenate([cos, cos], axis=1)
    sin_h = jnp.concatenate([-sin, sin], axis=1)
    return jnp.tile(cos_h, (1, GLA_H)), jnp.tile(sin_h, (1, GLA_H))


def _na_kernel(q_ref, kp_ref, kc_ref, kn_ref, kx_ref, vp_ref, vc_ref, vn_ref, vx_ref, bias_ref, y_ref):
    scale = NA_D ** -0.5
    outs = []
    for h in range(NA_H):
        hs = slice(h * NA_D, (h + 1) * NA_D)
        qh = (q_ref[:, hs] * scale).astype(BF16)
        s = jnp.concatenate(
            [_dot_nt(qh, kp_ref[:, hs]) + bias_ref[0, h],
             _dot_nt(qh, kc_ref[:, hs]) + bias_ref[1, h],
             _dot_nt(qh, kn_ref[:, hs]) + bias_ref[2, h],
             _dot_nt(qh, kx_ref[:, hs])], axis=1)
        m = jnp.max(s, axis=-1, keepdims=True)
        e = jnp.exp(s - m)
        p = e / jnp.sum(e, axis=-1, keepdims=True)
        o = (_dot(p[:, 0:TOK_BLK], vp_ref[:, hs]) + _dot(p[:, TOK_BLK:2 * TOK_BLK], vc_ref[:, hs])
             + _dot(p[:, 2 * TOK_BLK:3 * TOK_BLK], vn_ref[:, hs]) + _dot(p[:, 3 * TOK_BLK:], vx_ref[:, hs]))
        outs.append(o)
    y_ref[...] = jnp.concatenate(outs, axis=1).astype(y_ref.dtype)


def _na_bias_tiles(rpb):
    qi = np.arange(TOK_BLK)[:, None]
    ki = np.arange(TOK_BLK)[None, :]
    qr, qc = qi // GRID_W, qi % GRID_W
    kc = ki % GRID_W
    c0 = np.clip(qc - NA_WIN_C // 2, 0, GRID_W - NA_WIN_C)
    col_ok = (kc >= c0) & (kc < c0 + NA_WIN_C)
    dc = np.clip(kc - qc, 1 - NA_WIN_C, NA_WIN_C - 1) + NA_WIN_C - 1
    kinds = []
    for kind in range(3):
        offs = []
        for off in (-1, 0, 1):
            kr = ki // GRID_W + ROWS_PER_BLK * off
            if kind == 0:
                start = np.zeros_like(qr)
            elif kind == 1:
                start = qr - NA_WIN_R // 2
            else:
                start = np.full_like(qr, ROWS_PER_BLK - NA_WIN_R)
            ok = (kr >= start) & (kr < start + NA_WIN_R) & col_ok
            dr = np.clip(kr - qr + NA_WIN_R - 1, 0, 2 * NA_WIN_R - 2)
            offs.append(jnp.where(jnp.asarray(ok)[None], rpb[:, dr, dc], NEG_INF))
        kinds.append(jnp.stack(offs))
    kinds.append(jnp.full_like(kinds[0], NEG_INF))
    return jnp.stack(kinds).astype(F32)


def _na(p, bias, *, nbat, nblk):
    R = p.shape[0]

    def kind(s):
        return jnp.where(s == 0, 3, jnp.where(s == 1, 0, jnp.where(s == nblk - 1, 2, 1)))

    def spec(col, blk):
        return pl.BlockSpec((TOK_BLK, BRANCH_W), lambda b, s: (b * nblk + blk(s), col // BRANCH_W))

    prev = lambda s: jnp.maximum(s - 1, 1)
    cur = lambda s: s
    nxt = lambda s: jnp.minimum(s + 1, nblk - 1)
    ctx = lambda s: 0
    return pl.pallas_call(
        _na_kernel,
        grid=(nbat, nblk),
        in_specs=[spec(P_NA_Q, cur),
                  spec(P_NA_K, prev), spec(P_NA_K, cur), spec(P_NA_K, nxt), spec(P_NA_K, ctx),
                  spec(P_NA_V, prev), spec(P_NA_V, cur), spec(P_NA_V, nxt), spec(P_NA_V, ctx),
                  pl.BlockSpec((None, 3, NA_H, TOK_BLK, TOK_BLK), lambda b, s: (kind(s), 0, 0, 0, 0))],
        out_specs=pl.BlockSpec((TOK_BLK, BRANCH_W), lambda b, s: (b * nblk + s, 0)),
        out_shape=jax.ShapeDtypeStruct((R, BRANCH_W), BF16),
        compiler_params=_cparams(("arbitrary", "arbitrary")),
        name="nbr_attn",
    )(p, p, p, p, p, p, p, p, p, bias)


def _l2norm(x):
    return x * lax.rsqrt(jnp.sum(x * x, axis=-1, keepdims=True) + RMS_EPS)


def _gdn_kernel(q_ref, k_ref, v_ref, z_ref, sc_ref, alog_ref, dtb_ref, lm_ref, ng_ref, *rest, rev):
    if rev:
        of_ref, y_ref, st_ref = rest
    else:
        o_ref, st_ref = rest

    @pl.when(pl.program_id(1) == 0)
    def _():
        st_ref[...] = jnp.zeros_like(st_ref)

    d = 1 if rev else 0
    sc = sc_ref[...]
    beta_all = _sigmoid(sc)
    g_all = -jnp.exp(alog_ref[...]) * _softplus(sc + dtb_ref[...])
    b_all = _dot_hi(lm_ref[...], g_all)
    b_all_t = b_all.T
    eb_all = jnp.exp(b_all)

    m_incl = _chunk_mask(rev)
    m_strict = _chunk_mask(rev, strict=True)
    eye = (_iota2((TOK_BLK, TOK_BLK), 0) == _iota2((TOK_BLK, TOK_BLK), 1)).astype(F32)
    n_chunk = TOK_BLK // CHUNK
    o_heads = []
    for h in range(GDN_H):
        hs = slice(h * GDN_D, (h + 1) * GDN_D)
        qh = _l2norm(q_ref[:, hs]) * GDN_D ** -0.5
        kh = _l2norm(k_ref[:, hs])
        vh = v_ref[:, hs]
        lb, lg = GDN_H * d + h, 2 * GDN_H + GDN_H * d + h
        beta = beta_all[:, lb:lb + 1]
        bcol = b_all[:, lg:lg + 1]
        brow = b_all_t[lg:lg + 1, :]
        ebc = eb_all[:, lg:lg + 1]
        diff = bcol - brow
        dec_incl = jnp.where(m_incl, jnp.exp(jnp.where(m_incl, diff, 0.0)), 0.0)
        dec_strict = jnp.where(m_strict, dec_incl, 0.0)
        kk = _dot_nt(kh, kh)
        attn = _dot_nt(qh, kh) * dec_incl
        nmat = -(beta * kk * dec_strict)
        tinv = eye + nmat
        for _ in range(5):
            nmat = _dot_hi(nmat, nmat)
            tinv = tinv + _dot_hi(tinv, nmat)
        sol = _dot_hi(tinv, jnp.concatenate([kh * (beta * ebc), vh * beta], axis=1))
        w, u0 = sol[:, :GDN_D], sol[:, GDN_D:]

        st = st_ref[h]
        u_parts = [None] * n_chunk
        o_parts = [None] * n_chunk
        for c in _chunk_order(rev):
            rows = slice(c * CHUNK, (c + 1) * CHUNK)
            last = c * CHUNK if rev else (c + 1) * CHUNK - 1
            b_last = bcol[last:last + 1]
            u = u0[rows] - _dot(w[rows], st)
            u_parts[c] = u
            o_parts[c] = ebc[rows] * _dot(qh[rows], st)
            k_end = kh[rows] * jnp.exp(b_last - bcol[rows])
            st = jnp.exp(b_last) * st + _dot_tn(k_end, u)
        st_ref[h] = st
        o_heads.append(jnp.concatenate(o_parts, axis=0) + _dot(attn, jnp.concatenate(u_parts, axis=0)))
    o = jnp.concatenate(o_heads, axis=1)

    if rev:
        y_ref[...] = _head_rmsnorm_gate(of_ref[...] + o, ng_ref[...], z_ref[...], GDN_H, GDN_D).astype(y_ref.dtype)
    else:
        o_ref[...] = o


def _gdn_dir(p, qkv, a_log, dt_bias, ng, o_f, *, rev, nbat, nblk):
    R = p.shape[0]
    blk = _scan_block(rev, nblk)
    d = 1 if rev else 0
    lane0 = 2 * GDN_H + GDN_H * d
    alog_row = jnp.zeros((1, LANE), F32).at[0, lane0:lane0 + GDN_H].set(a_log[d])
    dtb_row = jnp.zeros((1, LANE), F32).at[0, lane0:lane0 + GDN_H].set(dt_bias[d])

    def bspec(width, col):
        return pl.BlockSpec((TOK_BLK, width), lambda b, s: (b * nblk + blk(s), col // width))

    const = lambda shape: pl.BlockSpec(shape, lambda b, s: (0, 0))
    in_specs = [bspec(512, 0), bspec(512, 512), bspec(512, 1024), bspec(512, P_GDN_Z), bspec(LANE, P_GDN_SC),
                const((1, LANE)), const((1, LANE)), const((TOK_BLK, TOK_BLK)), const((1, GDN_D))]
    args = [qkv, qkv, qkv, p, p, alog_row, dtb_row, _cumsum_matrix(rev), ng.reshape(1, -1)]
    ospec = pl.BlockSpec((TOK_BLK, BRANCH_W), lambda b, s: (b * nblk + blk(s), 0))
    if rev:
        in_specs.append(ospec)
        args.append(o_f)
    return pl.pallas_call(
        functools.partial(_gdn_kernel, rev=rev),
        grid=(nbat, nblk),
        in_specs=in_specs,
        out_specs=ospec,
        out_shape=jax.ShapeDtypeStruct((R, BRANCH_W), BF16 if rev else F32),
        scratch_shapes=[pltpu.VMEM((GDN_H, GDN_D, GDN_D), F32)],
        compiler_params=_cparams(("arbitrary", "arbitrary")),
        name="gdn_bwd" if rev else "gdn_fwd",
    )(*args)


def _ssd_kernel(xs_ref, bm_ref, cm_ref, dt_ref, z_ref, alog_ref, dtb_ref, ex_ref, lm_ref, dsk_ref, ng_ref,
                *rest, rev):
    if rev:
        of_ref, y_ref, st_ref = rest
    else:
        o_ref, st_ref = rest

    @pl.when(pl.program_id(1) == 0)
    def _():
        st_ref[...] = jnp.zeros_like(st_ref)

    d = 1 if rev else 0
    heads_per_g = M2_H // M2_G
    gw = heads_per_g * M2_P
    xs = xs_ref[...]
    dt = _softplus(dt_ref[...] + dtb_ref[...])
    loga = -jnp.exp(alog_ref[...]) * dt
    b8 = _dot_hi(lm_ref[...], loga)
    b8_t = b8.T
    ex = ex_ref[...]
    xv = xs * _dot_hi(dt, ex)
    b_e = _dot_hi(b8, ex)
    eb_e = jnp.exp(b_e)

    mask = _chunk_mask(rev)
    lane = _iota2((TOK_BLK, gw), 1)
    o_groups = []
    for g in range(M2_G):
        gs = slice(g * M2_N, (g + 1) * M2_N)
        xs_g = slice(g * gw, (g + 1) * gw)
        cq = cm_ref[:, gs]
        bk = bm_ref[:, gs]
        scores = _dot_nt(cq, bk)
        xv_g = xv[:, xs_g]
        acc = None
        for hh in range(heads_per_g):
            lh = M2_H * d + heads_per_g * g + hh
            diff = b8[:, lh:lh + 1] - b8_t[lh:lh + 1, :]
            dec = jnp.where(mask, jnp.exp(jnp.where(mask, diff, 0.0)), 0.0)
            term = _dot(scores * dec, jnp.where(_idiv(lane, M2_P) == hh, xv_g, 0.0))
            acc = term if acc is None else acc + term

        st = st_ref[g]
        o_parts = [None] * (TOK_BLK // CHUNK)
        for c in _chunk_order(rev):
            rows = slice(c * CHUNK, (c + 1) * CHUNK)
            last = c * CHUNK if rev else (c + 1) * CHUNK - 1
            b_c = b_e[rows, xs_g]
            b_last = b_e[last:last + 1, xs_g]
            o_parts[c] = eb_e[rows, xs_g] * _dot(cq[rows], st)
            ds = _dot_tn(bk[rows], xv_g[rows] * jnp.exp(b_last - b_c))
            st = jnp.exp(b_last) * st + ds
        st_ref[g] = st
        o_groups.append(acc + jnp.concatenate(o_parts, axis=0))
    o = jnp.concatenate(o_groups, axis=1)

    if rev:
        y = (of_ref[...] + o + dsk_ref[...] * xs) * _silu(z_ref[...])
        y = y * lax.rsqrt(jnp.mean(y * y, axis=-1, keepdims=True) + RMS_EPS) * ng_ref[...]
        y_ref[...] = y.astype(y_ref.dtype)
    else:
        o_ref[...] = o


def _ssd_dir(p, xbc, a_log, dt_bias, d_skip, ng, o_f, *, rev, nbat, nblk):
    R = p.shape[0]
    blk = _scan_block(rev, nblk)
    d = 1 if rev else 0
    lane0 = M2_H * d
    alog_row = jnp.zeros((1, LANE), F32).at[0, lane0:lane0 + M2_H].set(a_log[d])
    dtb_row = jnp.zeros((1, LANE), F32).at[0, lane0:lane0 + M2_H].set(dt_bias[d])
    ex = np.zeros((LANE, BRANCH_W), np.float32)
    for h in range(M2_H):
        ex[lane0 + h, h * M2_P:(h + 1) * M2_P] = 1.0
    dsk_row = jnp.repeat(d_skip, M2_P).reshape(1, BRANCH_W)

    def bspec(width, col):
        return pl.BlockSpec((TOK_BLK, width), lambda b, s: (b * nblk + blk(s), col // width))

    const = lambda shape: pl.BlockSpec(shape, lambda b, s: (0, 0))
    in_specs = [bspec(512, 0), bspec(256, 512), bspec(256, 768), bspec(LANE, P_M2_DT), bspec(512, P_M2_Z),
                const((1, LANE)), const((1, LANE)), const((LANE, BRANCH_W)), const((TOK_BLK, TOK_BLK)),
                const((1, BRANCH_W)), const((1, BRANCH_W))]
    args = [xbc, xbc, xbc, p, p, alog_row, dtb_row, jnp.asarray(ex), _cumsum_matrix(rev), dsk_row,
            ng.reshape(1, -1)]
    ospec = pl.BlockSpec((TOK_BLK, BRANCH_W), lambda b, s: (b * nblk + blk(s), 0))
    if rev:
        in_specs.append(ospec)
        args.append(o_f)
    return pl.pallas_call(
        functools.partial(_ssd_kernel, rev=rev),
        grid=(nbat, nblk),
        in_specs=in_specs,
        out_specs=ospec,
        out_shape=jax.ShapeDtypeStruct((R, BRANCH_W), BF16 if rev else F32),
        scratch_shapes=[pltpu.VMEM((M2_G, M2_N, (M2_H // M2_G) * M2_P), F32)],
        compiler_params=_cparams(("arbitrary", "arbitrary")),
        name="ssd_bwd" if rev else "ssd_fwd",
    )(*args)


def _final_norm_kernel(x_ref, g_ref, o_ref):
    x = x_ref[...]
    o_ref[...] = x * lax.rsqrt(jnp.mean(x * x, axis=-1, keepdims=True) + RMS_EPS) * g_ref[...]


def _final_norm(xs, g, *, nbat, nblk, lc):
    D = xs.shape[1]
    cb = lc // TOK_BLK
    nlat = nblk - cb
    out = pl.pallas_call(
        _final_norm_kernel,
        grid=(nbat, nlat),
        in_specs=[pl.BlockSpec((TOK_BLK, D), lambda b, s: (b * nblk + cb + s, 0)),
                  pl.BlockSpec((1, D), lambda b, s: (0, 0))],
        out_specs=pl.BlockSpec((TOK_BLK, D), lambda b, s: (b * nlat + s, 0)),
        out_shape=jax.ShapeDtypeStruct((nbat * nlat * TOK_BLK, D), F32),
        compiler_params=_cparams(("arbitrary", "arbitrary")),
        name="final_norm",
    )(xs, g.reshape(1, D))
    return out.reshape(nbat, nlat * TOK_BLK, D)


def _row_tile(s1):
    for tm in range(1056, 0, -16):
        if s1 % tm == 0:
            return tm
    raise ValueError(f"no row tile for sequence length {s1}")


def kernel(x, c, ctx, c_ctx, norm1_g, norm2_g, w_ada, b_ada, w_in, b_merge, gla_a2, gla_ab, gla_norm_g, na_rpb, gdn_conv, gdn_a_log, gdn_dt_bias, gdn_norm_g, m2_conv, m2_conv_b, m2_a_log, m2_dt_bias, m2_d, m2_norm_g, w_branch, w_out, w_ffn1, w_ffn3, w_ffn2, final_norm_g):
    nbat, t, D = x.shape
    lc = ctx.shape[1]
    depth = w_in.shape[0]
    assert D == D_MODEL and lc == TOK_BLK and t % TOK_BLK == 0 and t // GRID_W >= 3 * ROWS_PER_BLK
    assert nbat + 1 <= MOD_ROWS
    s1 = lc + t
    nblk = s1 // TOK_BLK
    tm = _row_tile(s1)
    geo = dict(tm=tm, s1=s1, lc=lc, nb=nbat)

    xs = jnp.concatenate([ctx, x], axis=1).reshape(nbat * s1, D)
    cvec = jnp.concatenate([c, c_ctx[None], jnp.zeros((MOD_ROWS - nbat - 1, D), F32)], axis=0)
    mods = _ada(cvec, w_ada, b_ada).reshape(depth, MOD_ROWS, 6, D).transpose(0, 2, 1, 3)
    cos, sin = _rope_tables(lc, t)
    perm = _mix_column_perm()

    for l in range(depth):
        sh1, sc1, g1, sh2, sc2, g2 = (mods[l, i] for i in range(6))
        w_in_l = w_in[l]
        w_mix = jnp.concatenate([w_in_l[:, :MIX_IN], jnp.zeros((D, 1), F32)], axis=1)[:, perm].astype(BF16)
        w_gate = w_in_l[:, MIX_IN:].reshape(D, N_BRANCH, D).transpose(1, 0, 2).astype(BF16)

        h = _normmod(xs, norm1_g[l], sc1, sh1, **geo)
        p = _matmul(h, w_mix, tm=tm, tn=512, out_dtype=F32)

        o_f = _gla_dir(p, cos, sin, gla_a2[l], gla_ab[l], gla_norm_g[l], None, rev=False, nbat=nbat, nblk=nblk)
        ya = _gla_dir(p, cos, sin, gla_a2[l], gla_ab[l], gla_norm_g[l], o_f, rev=True, nbat=nbat, nblk=nblk)

        yb = _na(p, _na_bias_tiles(na_rpb[l]), nbat=nbat, nblk=nblk)

        qkv = _conv_silu(p, P_GDN_QKV, gdn_conv[l], jnp.zeros((3 * BRANCH_W,), F32), nblk=nblk)
        o_f = _gdn_dir(p, qkv, gdn_a_log[l], gdn_dt_bias[l], gdn_norm_g[l], None, rev=False, nbat=nbat, nblk=nblk)
        yc = _gdn_dir(p, qkv, gdn_a_log[l], gdn_dt_bias[l], gdn_norm_g[l], o_f, rev=True, nbat=nbat, nblk=nblk)

        xbc = _conv_silu(p, P_M2_XBC, m2_conv[l], m2_conv_b[l], nblk=nblk)
        o_f = _ssd_dir(p, xbc, m2_a_log[l], m2_dt_bias[l], m2_d[l], m2_norm_g[l], None, rev=False,
                       nbat=nbat, nblk=nblk)
        yd = _ssd_dir(p, xbc, m2_a_log[l], m2_dt_bias[l], m2_d[l], m2_norm_g[l], o_f, rev=True,
                      nbat=nbat, nblk=nblk)

        merged = _merge(h, w_gate, b_merge[l], (ya, yb, yc, yd), w_branch[l].astype(BF16), tm=tm, tn=256)
        xs = _matmul_residual(merged, w_out[l].astype(BF16), xs, g1, tn=512, name="out_proj", **geo)

        h2 = _normmod(xs, norm2_g[l], sc2, sh2, **geo)
        u = _ffn_up(h2, w_ffn1[l].astype(BF16), w_ffn3[l].astype(BF16), tm=tm, tn=512)
        xs = _matmul_residual(u, w_ffn2[l].astype(BF16), xs, g2, tn=256, name="ffn_down", **geo)

    return _final_norm(xs, final_norm_g, nbat=nbat, nblk=nblk, lc=lc)
```

```python
import functools

import numpy as np
import jax
import jax.numpy as jnp
from jax import lax
from jax.experimental import pallas as pl
from jax.experimental.pallas import tpu as pltpu

D_MODEL = 2048
GRID_W = 64
N_BRANCH = 4
BRANCH_W = D_MODEL // 4
CHUNK = 64
CONV_W = 5
RMS_EPS = 1e-6
NEG_INF = -1e30
ROPE_BASE = 10000.0
GLA_H = 4
GLA_DV = BRANCH_W // GLA_H
GLA_DK = GLA_DV // 2
GLA_LR = 16
GLA_TAU = 16.0
NA_H = 4
NA_D = BRANCH_W // NA_H
NA_WIN_R = 8
NA_WIN_C = 16
GDN_H = 4
GDN_D = BRANCH_W // GDN_H
M2_P = 64
M2_H = BRANCH_W // M2_P
M2_N = 128
M2_G = 2
M2_CONV_CH = BRANCH_W + 2 * M2_G * M2_N
D_FF = ((8 * D_MODEL + 3 * 256 - 1) // (3 * 256)) * 256
GLA_IN = 2 * GLA_H * GLA_DK + 2 * BRANCH_W + 2 * GLA_LR
NA_IN = 3 * BRANCH_W
GDN_IN = 4 * BRANCH_W + 4 * GDN_H
M2_IN = BRANCH_W + M2_CONV_CH + 2 * M2_H
MIX_IN = GLA_IN + NA_IN + GDN_IN + M2_IN

F32 = jnp.float32
BF16 = jnp.bfloat16

LANE = 128
SUBLANE = 8
V7X_VMEM_BYTES = 64 * 1024 * 1024
VMEM_LIMIT = V7X_VMEM_BYTES - 8 * 1024 * 1024

TOK_BLK = 4 * CHUNK
ROWS_PER_BLK = TOK_BLK // GRID_W
HALO = SUBLANE
MOD_ROWS = 8

P_GLA_V, P_GLA_G = 0, 512
P_NA_Q, P_NA_K, P_NA_V = 1024, 1536, 2048
P_GDN_QKV, P_GDN_Z = 2560, 4096
P_M2_Z, P_M2_XBC = 4608, 5120
P_GLA_Q, P_GLA_K = 6144, 6400
P_GLA_LR, P_GDN_SC, P_M2_DT = 6656, 6784, 6912
P_COLS = 7168


def _mix_weight(w_in_l):
    gla, na, gdn, m2 = 0, GLA_IN, GLA_IN + NA_IN, GLA_IN + NA_IN + GDN_IN
    fields = sorted([
        (P_GLA_V, gla + 512, 512), (P_GLA_G, gla + 1024, 512),
        (P_NA_Q, na, 512), (P_NA_K, na + 512, 512), (P_NA_V, na + 1024, 512),
        (P_GDN_QKV, gdn, 1536), (P_GDN_Z, gdn + 1536, 512),
        (P_M2_Z, m2, 512), (P_M2_XBC, m2 + 512, M2_CONV_CH),
        (P_GLA_Q, gla, 256), (P_GLA_K, gla + 256, 256),
        (P_GLA_LR, gla + 1536, 2 * GLA_LR), (P_GDN_SC, gdn + 2048, 4 * GDN_H),
        (P_M2_DT, m2 + 512 + M2_CONV_CH, 2 * M2_H)])
    d = w_in_l.shape[0]
    parts, col = [], 0
    for dst, src, n in fields:
        if dst > col:
            parts.append(jnp.zeros((d, dst - col), BF16))
        parts.append(w_in_l[:, src:src + n].astype(BF16))
        col = dst + n
    parts.append(jnp.zeros((d, P_COLS - col), BF16))
    return jnp.concatenate(parts, axis=1)


def _cparams(sem):
    return pltpu.CompilerParams(dimension_semantics=sem, vmem_limit_bytes=VMEM_LIMIT)


def _sigmoid(x):
    return 1.0 / (1.0 + jnp.exp(-x))


def _silu(x):
    return x * _sigmoid(x)


def _softplus(x):
    return jnp.maximum(x, 0.0) + jnp.log1p(jnp.exp(-jnp.abs(x)))


def _dot(a, b):
    return jnp.dot(a.astype(BF16), b.astype(BF16), preferred_element_type=F32)


def _dot_nt(a, b):
    return lax.dot_general(a.astype(BF16), b.astype(BF16), (((1,), (1,)), ((), ())),
                           preferred_element_type=F32)


def _dot_tn(a, b):
    return lax.dot_general(a.astype(BF16), b.astype(BF16), (((0,), (0,)), ((), ())),
                           preferred_element_type=F32)


def _dot_hi(a, b):
    return jnp.dot(a, b, precision=lax.Precision.HIGHEST, preferred_element_type=F32)


def _split(a):
    hi = a.astype(BF16)
    return hi, (a - hi.astype(F32)).astype(BF16)


def _dot_split(a, b):
    (ah, al), (bh, bl) = a, b
    d = lambda x, y: jnp.dot(x, y, preferred_element_type=F32)
    return d(ah, bh) + (d(ah, bl) + d(al, bh))


def _iota2(shape, dim):
    return lax.broadcasted_iota(jnp.int32, shape, dim)


def _idiv(x, n):
    assert n & (n - 1) == 0
    return lax.shift_right_logical(x, jnp.int32(n.bit_length() - 1))


def _imod(x, n):
    assert n & (n - 1) == 0
    return x & (n - 1)


def _chunk_mask(rev, strict=False):
    r = _iota2((TOK_BLK, TOK_BLK), 0)
    c = _iota2((TOK_BLK, TOK_BLK), 1)
    same = _idiv(r, CHUNK) == _idiv(c, CHUNK)
    if rev:
        tri = (c > r) if strict else (c >= r)
    else:
        tri = (c < r) if strict else (c <= r)
    return same & tri


def _chunk_order(rev):
    n = TOK_BLK // CHUNK
    return list(range(n - 1, -1, -1)) if rev else list(range(n))


def _row_split(i, tm, s1):
    r0 = i * tm
    return lax.div(r0, jnp.int32(s1)), lax.rem(r0, jnp.int32(s1))


def _pick_mod(ref, b, nb, is_ctx):
    return jnp.where(is_ctx, ref[nb:nb + 1, :], ref[pl.ds(b, 1), :])


def _ada_kernel(c_ref, w_ref, b_ref, o_ref):
    a = _silu(c_ref[...])
    o_ref[...] = _dot(a, w_ref[...]) + b_ref[...]


def _ada(cvec, w_ada, b_ada):
    L, D, N = w_ada.shape
    tn = 1024
    return pl.pallas_call(
        _ada_kernel,
        grid=(L, N // tn),
        in_specs=[pl.BlockSpec((MOD_ROWS, D), lambda l, j: (0, 0)),
                  pl.BlockSpec((None, D, tn), lambda l, j: (l, 0, j)),
                  pl.BlockSpec((None, 1, tn), lambda l, j: (l, 0, j))],
        out_specs=pl.BlockSpec((None, MOD_ROWS, tn), lambda l, j: (l, 0, j)),
        out_shape=jax.ShapeDtypeStruct((L, MOD_ROWS, N), F32),
        compiler_params=_cparams(("arbitrary", "arbitrary")),
        name="ada_mod",
    )(cvec, w_ada, b_ada.reshape(L, 1, N))


def _normmod_kernel(x_ref, g_ref, sc_ref, sh_ref, o_ref, *, tm, s1, lc, nb):
    b, pos0 = _row_split(pl.program_id(0), tm, s1)
    is_ctx = (pos0 + _iota2((tm, 1), 0)) < lc
    x = x_ref[...]
    y = x * lax.rsqrt(jnp.mean(x * x, axis=-1, keepdims=True) + RMS_EPS) * g_ref[...]
    sc = _pick_mod(sc_ref, b, nb, is_ctx)
    sh = _pick_mod(sh_ref, b, nb, is_ctx)
    o_ref[...] = (y * (1.0 + sc) + sh).astype(o_ref.dtype)


def _normmod(xs, g, sc, sh, *, tm, s1, lc, nb):
    R, D = xs.shape
    return pl.pallas_call(
        functools.partial(_normmod_kernel, tm=tm, s1=s1, lc=lc, nb=nb),
        grid=(R // tm,),
        in_specs=[pl.BlockSpec((tm, D), lambda i: (i, 0)),
                  pl.BlockSpec((1, D), lambda i: (0, 0)),
                  pl.BlockSpec((MOD_ROWS, D), lambda i: (0, 0)),
                  pl.BlockSpec((MOD_ROWS, D), lambda i: (0, 0))],
        out_specs=pl.BlockSpec((tm, D), lambda i: (i, 0)),
        out_shape=jax.ShapeDtypeStruct((R, D), BF16),
        compiler_params=_cparams(("arbitrary",)),
        name="norm_mod",
    )(xs, g.reshape(1, D), sc, sh)


def _mm_kernel(a_ref, w_ref, o_ref):
    o_ref[...] = jnp.dot(a_ref[...], w_ref[...], preferred_element_type=F32).astype(o_ref.dtype)


def _matmul(a, w, *, tm, tn, out_dtype):
    R, K = a.shape
    N = w.shape[1]
    return pl.pallas_call(
        _mm_kernel,
        grid=(R // tm, N // tn),
        in_specs=[pl.BlockSpec((tm, K), lambda i, j: (i, 0)),
                  pl.BlockSpec((K, tn), lambda i, j: (0, j))],
        out_specs=pl.BlockSpec((tm, tn), lambda i, j: (i, j)),
        out_shape=jax.ShapeDtypeStruct((R, N), out_dtype),
        compiler_params=_cparams(("arbitrary", "arbitrary")),
        name="mix_proj",
    )(a, w)


def _mm_res_kernel(a_ref, w_ref, x_ref, gt_ref, o_ref, *, tm, s1, lc, nb):
    b, pos0 = _row_split(pl.program_id(0), tm, s1)
    is_ctx = (pos0 + _iota2((tm, 1), 0)) < lc
    acc = jnp.dot(a_ref[...], w_ref[...], preferred_element_type=F32)
    o_ref[...] = x_ref[...] + _pick_mod(gt_ref, b, nb, is_ctx) * acc


def _matmul_residual(a, w, xs, gate, *, tm, tn, s1, lc, nb, name):
    R, K = a.shape
    N = w.shape[1]
    return pl.pallas_call(
        functools.partial(_mm_res_kernel, tm=tm, s1=s1, lc=lc, nb=nb),
        grid=(R // tm, N // tn),
        in_specs=[pl.BlockSpec((tm, K), lambda i, j: (i, 0)),
                  pl.BlockSpec((K, tn), lambda i, j: (0, j)),
                  pl.BlockSpec((tm, tn), lambda i, j: (i, j)),
                  pl.BlockSpec((MOD_ROWS, tn), lambda i, j: (0, j))],
        out_specs=pl.BlockSpec((tm, tn), lambda i, j: (i, j)),
        out_shape=jax.ShapeDtypeStruct((R, N), F32),
        compiler_params=_cparams(("arbitrary", "arbitrary")),
        name=name,
    )(a, w, xs, gate)


def _ffn_up_kernel(a_ref, w1_ref, w3_ref, o_ref):
    a = a_ref[...]
    u = jnp.dot(a, w1_ref[...], preferred_element_type=F32)
    v = jnp.dot(a, w3_ref[...], preferred_element_type=F32)
    o_ref[...] = (_silu(u) * v).astype(o_ref.dtype)


def _ffn_up(a, w1, w3, *, tm, tn):
    R, K = a.shape
    N = w1.shape[1]
    return pl.pallas_call(
        _ffn_up_kernel,
        grid=(R // tm, N // tn),
        in_specs=[pl.BlockSpec((tm, K), lambda i, j: (i, 0)),
                  pl.BlockSpec((K, tn), lambda i, j: (0, j)),
                  pl.BlockSpec((K, tn), lambda i, j: (0, j))],
        out_specs=pl.BlockSpec((tm, tn), lambda i, j: (i, j)),
        out_shape=jax.ShapeDtypeStruct((R, N), BF16),
        compiler_params=_cparams(("arbitrary", "arbitrary")),
        name="ffn_up",
    )(a, w1, w3)


def _merge_kernel(h_ref, wg_ref, bm_ref, ya_ref, yb_ref, yc_ref, yd_ref, wb_ref, o_ref):
    h = h_ref[...]
    acc = None
    for i, y_ref in enumerate((ya_ref, yb_ref, yc_ref, yd_ref)):
        gate = _sigmoid(jnp.dot(h, wg_ref[i], preferred_element_type=F32) + bm_ref[i])
        term = gate * jnp.dot(y_ref[...], wb_ref[i], preferred_element_type=F32)
        acc = term if acc is None else acc + term
    o_ref[...] = acc.astype(o_ref.dtype)


def _merge(h, wg, b_merge, ys, wb, *, tm, tn):
    R, D = h.shape
    N = wg.shape[2]
    yspec = pl.BlockSpec((tm, BRANCH_W), lambda i, j: (i, 0))
    return pl.pallas_call(
        _merge_kernel,
        grid=(R // tm, N // tn),
        in_specs=[pl.BlockSpec((tm, D), lambda i, j: (i, 0)),
                  pl.BlockSpec((N_BRANCH, D, tn), lambda i, j: (0, 0, j)),
                  pl.BlockSpec((N_BRANCH, 1, tn), lambda i, j: (0, 0, j)),
                  yspec, yspec, yspec, yspec,
                  pl.BlockSpec((N_BRANCH, BRANCH_W, tn), lambda i, j: (0, 0, j))],
        out_specs=pl.BlockSpec((tm, tn), lambda i, j: (i, j)),
        out_shape=jax.ShapeDtypeStruct((R, N), BF16),
        compiler_params=_cparams(("arbitrary", "arbitrary")),
        name="merge",
    )(h, wg, b_merge.reshape(N_BRANCH, 1, N), *ys, wb)


def _conv_kernel(xm_ref, xp_ref, xn_ref, w_ref, b_ref, o_ref, *, nblk):
    blk = lax.rem(pl.program_id(0), jnp.int32(nblk))
    prev_ok = blk >= 2
    next_ok = (blk >= 1) & (blk < nblk - 1)
    xp = jnp.where(prev_ok, xp_ref[...], 0.0)
    xn = jnp.where(next_ok, xn_ref[...], 0.0)
    xpad = jnp.concatenate([xp, xm_ref[...], xn], axis=0)
    w = w_ref[...]
    acc = b_ref[...] + w[0:1, :] * xpad[HALO - 2:HALO - 2 + TOK_BLK, :]
    for j in range(1, CONV_W):
        acc = acc + w[j:j + 1, :] * xpad[HALO - 2 + j:HALO - 2 + j + TOK_BLK, :]
    o_ref[...] = _silu(acc)


def _conv_silu(p, col0, w, bias, *, nblk):
    R = p.shape[0]
    C = w.shape[1]
    cw = 512
    cb0 = col0 // cw
    hb = TOK_BLK // HALO
    nhalo = R // HALO
    wpad = jnp.concatenate([w, jnp.zeros((SUBLANE - CONV_W, C), F32)], axis=0)
    return pl.pallas_call(
        functools.partial(_conv_kernel, nblk=nblk),
        grid=(R // TOK_BLK, C // cw),
        in_specs=[pl.BlockSpec((TOK_BLK, cw), lambda r, c: (r, cb0 + c)),
                  pl.BlockSpec((HALO, cw), lambda r, c: (jnp.maximum(r * hb - 1, 0), cb0 + c)),
                  pl.BlockSpec((HALO, cw), lambda r, c: (jnp.minimum((r + 1) * hb, nhalo - 1), cb0 + c)),
                  pl.BlockSpec((SUBLANE, cw), lambda r, c: (0, c)),
                  pl.BlockSpec((1, cw), lambda r, c: (0, c))],
        out_specs=pl.BlockSpec((TOK_BLK, cw), lambda r, c: (r, c)),
        out_shape=jax.ShapeDtypeStruct((R, C), F32),
        compiler_params=_cparams(("arbitrary", "arbitrary")),
        name="short_conv",
    )(p, p, p, wpad, bias.reshape(1, C))


def _scan_block(rev, nblk):
    if rev:
        return lambda s: jnp.where(s == 0, 0, nblk - s)
    return lambda s: s


def _head_rmsnorm_gate(o, g, gate, n_head, width):
    outs = []
    for h in range(n_head):
        oh = o[:, h * width:(h + 1) * width]
        yh = oh * lax.rsqrt(jnp.mean(oh * oh, axis=-1, keepdims=True) + RMS_EPS) * g
        outs.append(yh * _silu(gate[:, h * width:(h + 1) * width]))
    return jnp.concatenate(outs, axis=1)


def _gla_kernel(q_ref, k_ref, v_ref, g_ref, lr_ref, cos_ref, sin_ref, a2_ref, ab_ref, lm_ref, ng_ref,
                *rest, rev):
    if rev:
        of_ref, y_ref, st_ref = rest
    else:
        o_ref, st_ref = rest

    @pl.when(pl.program_id(1) == 0)
    def _():
        st_ref[...] = jnp.zeros_like(st_ref)

    hk = GLA_H * GLA_DK
    lane = _iota2((TOK_BLK, hk), 1)
    first_half = _imod(lane, GLA_DK) < (GLA_DK // 2)
    cos = cos_ref[...]
    sin = sin_ref[...]

    def rope(x):
        partner = jnp.where(first_half, pltpu.roll(x, hk - GLA_DK // 2, 1), pltpu.roll(x, GLA_DK // 2, 1))
        return x * cos + partner * sin

    q = rope(q_ref[...]) * GLA_DK ** -0.5
    k = rope(k_ref[...])
    v = v_ref[...]
    loga = -_softplus(-(_dot(lr_ref[...], a2_ref[...]) + ab_ref[...])) / GLA_TAU
    b = _dot_hi(lm_ref[...], loga)
    q_in = q * jnp.exp(b)
    k_in = k * jnp.exp(-b)

    mask = _chunk_mask(rev)
    head_of_lane = _idiv(lane, GLA_DK)
    o_heads = []
    for h in range(GLA_H):
        att = _dot_nt(jnp.where(head_of_lane == h, q_in, 0.0), k_in)
        att = jnp.where(mask, att, 0.0)
        o_heads.append(_dot(att, v[:, h * GLA_DV:(h + 1) * GLA_DV]))
    o_intra = jnp.concatenate(o_heads, axis=1)

    st = st_ref[...]
    diag = _idiv(_iota2(st.shape, 0), GLA_DV) == _idiv(_iota2(st.shape, 1), GLA_DK)
    o_inter = [None] * (TOK_BLK // CHUNK)
    for c in _chunk_order(rev):
        rows = slice(c * CHUNK, (c + 1) * CHUNK)
        b_c = b[rows]
        b_last = b_c[0:1] if rev else b_c[CHUNK - 1:CHUNK]
        o_inter[c] = _dot_nt(q_in[rows], st)
        k_end = k[rows] * jnp.exp(b_last - b_c)
        ds = _dot_tn(v[rows], k_end)
        st = st * jnp.exp(b_last) + jnp.where(diag, ds, 0.0)
    st_ref[...] = st
    o = o_intra + jnp.concatenate(o_inter, axis=0)

    if rev:
        y_ref[...] = _head_rmsnorm_gate(of_ref[...] + o, ng_ref[...], g_ref[...], GLA_H, GLA_DV).astype(y_ref.dtype)
    else:
        o_ref[...] = o


def _cumsum_matrix(rev):
    r = np.arange(TOK_BLK)[:, None]
    c = np.arange(TOK_BLK)[None, :]
    same = (r // CHUNK) == (c // CHUNK)
    tri = (c >= r) if rev else (c <= r)
    return jnp.asarray((same & tri).astype(np.float32))


def _gla_dir(p, cos, sin, a2, ab, ng, o_f, *, rev, nbat, nblk):
    R = p.shape[0]
    blk = _scan_block(rev, nblk)
    d = 1 if rev else 0

    def pspec(width, col):
        return pl.BlockSpec((TOK_BLK, width), lambda b, s: (b * nblk + blk(s), col // width))

    a2d = jnp.zeros((LANE, GLA_H * GLA_DK), F32).at[d * GLA_LR:(d + 1) * GLA_LR].set(a2[d])
    const = lambda shape: pl.BlockSpec(shape, lambda b, s: (0, 0))
    in_specs = [pspec(256, P_GLA_Q), pspec(256, P_GLA_K), pspec(512, P_GLA_V), pspec(512, P_GLA_G),
                pspec(LANE, P_GLA_LR),
                pl.BlockSpec((TOK_BLK, 256), lambda b, s: (blk(s), 0)),
                pl.BlockSpec((TOK_BLK, 256), lambda b, s: (blk(s), 0)),
                const((LANE, 256)), const((1, 256)), const((TOK_BLK, TOK_BLK)), const((1, GLA_DV))]
    args = [p, p, p, p, p, cos, sin, a2d, ab[d].reshape(1, -1), _cumsum_matrix(rev), ng.reshape(1, -1)]
    ospec = pl.BlockSpec((TOK_BLK, BRANCH_W), lambda b, s: (b * nblk + blk(s), 0))
    if rev:
        in_specs.append(ospec)
        args.append(o_f)
    return pl.pallas_call(
        functools.partial(_gla_kernel, rev=rev),
        grid=(nbat, nblk),
        in_specs=in_specs,
        out_specs=ospec,
        out_shape=jax.ShapeDtypeStruct((R, BRANCH_W), BF16 if rev else F32),
        scratch_shapes=[pltpu.VMEM((GLA_H * GLA_DV, GLA_H * GLA_DK), F32)],
        compiler_params=_cparams(("arbitrary", "arbitrary")),
        name="gla_bwd" if rev else "gla_fwd",
    )(*args)


def _rope_tables(lc, t):
    n_freq = GLA_DK // 4
    freqs = ROPE_BASE ** (-jnp.arange(n_freq, dtype=F32) / n_freq)
    tt = jnp.arange(t)
    row = (tt // GRID_W).astype(F32)
    col = (tt % GRID_W).astype(F32)
    ang = jnp.concatenate([row[:, None] * freqs, col[:, None] * freqs], axis=-1)
    cos, sin = jnp.cos(ang), jnp.sin(ang)
    cos = jnp.concatenate([jnp.ones((lc, GLA_DK // 2), F32), cos], axis=0)
    sin = jnp.concatenate([jnp.zeros((lc, GLA_DK // 2), F32), sin], axis=0)
    cos_h = jnp.concatenate([cos, cos], axis=1)
    sin_h = jnp.concatenate([-sin, sin], axis=1)
    return jnp.tile(cos_h, (1, GLA_H)), jnp.tile(sin_h, (1, GLA_H))


def _na_kernel(q_ref, kp_ref, kc_ref, kn_ref, kx_ref, vp_ref, vc_ref, vn_ref, vx_ref, bias_ref, y_ref):
    scale = NA_D ** -0.5
    outs = []
    for h in range(NA_H):
        hs = slice(h * NA_D, (h + 1) * NA_D)
        qh = (q_ref[:, hs] * scale).astype(BF16)
        s = jnp.concatenate(
            [_dot_nt(qh, kp_ref[:, hs]) + bias_ref[0, h],
             _dot_nt(qh, kc_ref[:, hs]) + bias_ref[1, h],
             _dot_nt(qh, kn_ref[:, hs]) + bias_ref[2, h],
             _dot_nt(qh, kx_ref[:, hs])], axis=1)
        m = jnp.max(s, axis=-1, keepdims=True)
        e = jnp.exp(s - m)
        p = e / jnp.sum(e, axis=-1, keepdims=True)
        o = (_dot(p[:, 0:TOK_BLK], vp_ref[:, hs]) + _dot(p[:, TOK_BLK:2 * TOK_BLK], vc_ref[:, hs])
             + _dot(p[:, 2 * TOK_BLK:3 * TOK_BLK], vn_ref[:, hs]) + _dot(p[:, 3 * TOK_BLK:], vx_ref[:, hs]))
        outs.append(o)
    y_ref[...] = jnp.concatenate(outs, axis=1).astype(y_ref.dtype)


def _na_bias_tiles(rpb):
    edge = GRID_W - NA_WIN_C
    ext = jnp.concatenate([jnp.repeat(rpb[..., :1], edge, axis=-1), rpb,
                           jnp.repeat(rpb[..., -1:], edge, axis=-1)], axis=-1)
    toep = jnp.stack([ext[..., GRID_W - 1 - qc:2 * GRID_W - 1 - qc] for qc in range(GRID_W)], axis=-2)
    qc = np.arange(GRID_W)[:, None]
    kc = np.arange(GRID_W)[None, :]
    c0 = np.clip(qc - NA_WIN_C // 2, 0, GRID_W - NA_WIN_C)
    col_ok = (kc >= c0) & (kc < c0 + NA_WIN_C)
    toep = jnp.where(jnp.asarray(col_ok), toep, NEG_INF).astype(F32)
    masked = jnp.full((NA_H, GRID_W, GRID_W), NEG_INF, F32)
    kinds = []
    for kind in range(3):
        offs = []
        for off in (-1, 0, 1):
            rows = []
            for qr in range(ROWS_PER_BLK):
                start = (0, qr - NA_WIN_R // 2, ROWS_PER_BLK - NA_WIN_R)[kind]
                cols = []
                for kb in range(ROWS_PER_BLK):
                    kr = kb + ROWS_PER_BLK * off
                    ok = start <= kr < start + NA_WIN_R
                    cols.append(toep[:, kr - qr + NA_WIN_R - 1] if ok else masked)
                rows.append(jnp.concatenate(cols, axis=-1))
            offs.append(jnp.concatenate(rows, axis=-2))
        kinds.append(jnp.stack(offs))
    kinds.append(jnp.full_like(kinds[0], NEG_INF))
    return jnp.stack(kinds)


def _na(p, bias, *, nbat, nblk):
    R = p.shape[0]

    def kind(s):
        return jnp.where(s == 0, 3, jnp.where(s == 1, 0, jnp.where(s == nblk - 1, 2, 1)))

    def spec(col, blk):
        return pl.BlockSpec((TOK_BLK, BRANCH_W), lambda b, s: (b * nblk + blk(s), col // BRANCH_W))

    prev = lambda s: jnp.maximum(s - 1, 1)
    cur = lambda s: s
    nxt = lambda s: jnp.minimum(s + 1, nblk - 1)
    ctx = lambda s: 0
    return pl.pallas_call(
        _na_kernel,
        grid=(nbat, nblk),
        in_specs=[spec(P_NA_Q, cur),
                  spec(P_NA_K, prev), spec(P_NA_K, cur), spec(P_NA_K, nxt), spec(P_NA_K, ctx),
                  spec(P_NA_V, prev), spec(P_NA_V, cur), spec(P_NA_V, nxt), spec(P_NA_V, ctx),
                  pl.BlockSpec((None, 3, NA_H, TOK_BLK, TOK_BLK), lambda b, s: (kind(s), 0, 0, 0, 0))],
        out_specs=pl.BlockSpec((TOK_BLK, BRANCH_W), lambda b, s: (b * nblk + s, 0)),
        out_shape=jax.ShapeDtypeStruct((R, BRANCH_W), BF16),
        compiler_params=_cparams(("arbitrary", "arbitrary")),
        name="nbr_attn",
    )(p, p, p, p, p, p, p, p, p, bias)


def _l2norm(x):
    return x * lax.rsqrt(jnp.sum(x * x, axis=-1, keepdims=True) + RMS_EPS)


def _gdn_kernel(q_ref, k_ref, v_ref, z_ref, sc_ref, alog_ref, dtb_ref, lm_ref, ng_ref, *rest, rev):
    if rev:
        of_ref, y_ref, st_ref = rest
    else:
        o_ref, st_ref = rest

    @pl.when(pl.program_id(1) == 0)
    def _():
        st_ref[...] = jnp.zeros_like(st_ref)

    d = 1 if rev else 0
    sc = sc_ref[...]
    beta_all = _sigmoid(sc)
    g_all = -jnp.exp(alog_ref[...]) * _softplus(sc + dtb_ref[...])
    b_all = _dot_hi(lm_ref[...], g_all)
    b_all_t = b_all.T
    eb_all = jnp.exp(b_all)

    m_incl = _chunk_mask(rev)
    m_strict = _chunk_mask(rev, strict=True)
    eye = (_iota2((TOK_BLK, TOK_BLK), 0) == _iota2((TOK_BLK, TOK_BLK), 1)).astype(F32)
    n_chunk = TOK_BLK // CHUNK
    o_heads = []
    heads = range(GDN_H)
    qs, ks, bcols, ebcs, attns, nmats, rhss = [], [], [], [], [], [], []
    for h in heads:
        hs = slice(h * GDN_D, (h + 1) * GDN_D)
        qh = _l2norm(q_ref[:, hs]) * GDN_D ** -0.5
        kh = _l2norm(k_ref[:, hs])
        vh = v_ref[:, hs]
        lb, lg = GDN_H * d + h, 2 * GDN_H + GDN_H * d + h
        beta = beta_all[:, lb:lb + 1]
        bcol = b_all[:, lg:lg + 1]
        brow = b_all_t[lg:lg + 1, :]
        ebc = eb_all[:, lg:lg + 1]
        diff = bcol - brow
        dec_incl = jnp.where(m_incl, jnp.exp(jnp.where(m_incl, diff, 0.0)), 0.0)
        dec_strict = jnp.where(m_strict, dec_incl, 0.0)
        kk = _dot_nt(kh, kh)
        qs.append(qh)
        ks.append(kh)
        bcols.append(bcol)
        ebcs.append(ebc)
        attns.append(_dot_nt(qh, kh) * dec_incl)
        nmats.append(-(beta * kk * dec_strict))
        rhss.append(_split(jnp.concatenate([kh * (beta * ebc), vh * beta], axis=1)))

    tinvs = [eye + n for n in nmats]
    nss = [_split(n) for n in nmats]
    for _ in range(5):
        nss = [_split(_dot_split(ns, ns)) for ns in nss]
        tinvs = [t + _dot_split(_split(t), ns) for t, ns in zip(tinvs, nss)]
    sols = [_dot_split(_split(t), r) for t, r in zip(tinvs, rhss)]
    ws = [s[:, :GDN_D] for s in sols]
    u0s = [s[:, GDN_D:] for s in sols]

    sts = [st_ref[h] for h in heads]
    u_parts = [[None] * n_chunk for _ in heads]
    o_parts = [[None] * n_chunk for _ in heads]
    for c in _chunk_order(rev):
        rows = slice(c * CHUNK, (c + 1) * CHUNK)
        last = c * CHUNK if rev else (c + 1) * CHUNK - 1
        for h in heads:
            st = sts[h]
            b_last = bcols[h][last:last + 1]
            u = u0s[h][rows] - _dot(ws[h][rows], st)
            u_parts[h][c] = u
            o_parts[h][c] = ebcs[h][rows] * _dot(qs[h][rows], st)
            k_end = ks[h][rows] * jnp.exp(b_last - bcols[h][rows])
            sts[h] = jnp.exp(b_last) * st + _dot_tn(k_end, u)
    for h in heads:
        st_ref[h] = sts[h]
    o_heads = [jnp.concatenate(o_parts[h], axis=0) + _dot(attns[h], jnp.concatenate(u_parts[h], axis=0))
               for h in heads]
    o = jnp.concatenate(o_heads, axis=1)

    if rev:
        y_ref[...] = _head_rmsnorm_gate(of_ref[...] + o, ng_ref[...], z_ref[...], GDN_H, GDN_D).astype(y_ref.dtype)
    else:
        o_ref[...] = o


def _gdn_dir(p, qkv, a_log, dt_bias, ng, o_f, *, rev, nbat, nblk):
    R = p.shape[0]
    blk = _scan_block(rev, nblk)
    d = 1 if rev else 0
    lane0 = 2 * GDN_H + GDN_H * d
    alog_row = jnp.zeros((1, LANE), F32).at[0, lane0:lane0 + GDN_H].set(a_log[d])
    dtb_row = jnp.zeros((1, LANE), F32).at[0, lane0:lane0 + GDN_H].set(dt_bias[d])

    def bspec(width, col):
        return pl.BlockSpec((TOK_BLK, width), lambda b, s: (b * nblk + blk(s), col // width))

    const = lambda shape: pl.BlockSpec(shape, lambda b, s: (0, 0))
    in_specs = [bspec(512, 0), bspec(512, 512), bspec(512, 1024), bspec(512, P_GDN_Z), bspec(LANE, P_GDN_SC),
                const((1, LANE)), const((1, LANE)), const((TOK_BLK, TOK_BLK)), const((1, GDN_D))]
    args = [qkv, qkv, qkv, p, p, alog_row, dtb_row, _cumsum_matrix(rev), ng.reshape(1, -1)]
    ospec = pl.BlockSpec((TOK_BLK, BRANCH_W), lambda b, s: (b * nblk + blk(s), 0))
    if rev:
        in_specs.append(ospec)
        args.append(o_f)
    return pl.pallas_call(
        functools.partial(_gdn_kernel, rev=rev),
        grid=(nbat, nblk),
        in_specs=in_specs,
        out_specs=ospec,
        out_shape=jax.ShapeDtypeStruct((R, BRANCH_W), BF16 if rev else F32),
        scratch_shapes=[pltpu.VMEM((GDN_H, GDN_D, GDN_D), F32)],
        compiler_params=_cparams(("arbitrary", "arbitrary")),
        name="gdn_bwd" if rev else "gdn_fwd",
    )(*args)


def _ssd_kernel(xs_ref, bm_ref, cm_ref, dt_ref, z_ref, alog_ref, dtb_ref, ex_ref, lm_ref, dsk_ref, ng_ref,
                *rest, rev):
    if rev:
        of_ref, y_ref, st_ref = rest
    else:
        o_ref, st_ref = rest

    @pl.when(pl.program_id(1) == 0)
    def _():
        st_ref[...] = jnp.zeros_like(st_ref)

    d = 1 if rev else 0
    heads_per_g = M2_H // M2_G
    gw = heads_per_g * M2_P
    xs = xs_ref[...]
    dt = _softplus(dt_ref[...] + dtb_ref[...])
    loga = -jnp.exp(alog_ref[...]) * dt
    b8 = _dot_hi(lm_ref[...], loga)
    b8_t = b8.T
    ex = ex_ref[...]
    xv = xs * _dot_hi(dt, ex)
    b_e = _dot_hi(b8, ex)
    eb_e = jnp.exp(b_e)

    mask = _chunk_mask(rev)
    lane = _iota2((TOK_BLK, gw), 1)
    o_groups = []
    for g in range(M2_G):
        gs = slice(g * M2_N, (g + 1) * M2_N)
        xs_g = slice(g * gw, (g + 1) * gw)
        cq = cm_ref[:, gs]
        bk = bm_ref[:, gs]
        scores = _dot_nt(cq, bk)
        xv_g = xv[:, xs_g]
        acc = None
        for hh in range(heads_per_g):
            lh = M2_H * d + heads_per_g * g + hh
            diff = b8[:, lh:lh + 1] - b8_t[lh:lh + 1, :]
            dec = jnp.where(mask, jnp.exp(jnp.where(mask, diff, 0.0)), 0.0)
            term = _dot(scores * dec, jnp.where(_idiv(lane, M2_P) == hh, xv_g, 0.0))
            acc = term if acc is None else acc + term

        st = st_ref[g]
        o_parts = [None] * (TOK_BLK // CHUNK)
        for c in _chunk_order(rev):
            rows = slice(c * CHUNK, (c + 1) * CHUNK)
            last = c * CHUNK if rev else (c + 1) * CHUNK - 1
            b_c = b_e[rows, xs_g]
            b_last = b_e[last:last + 1, xs_g]
            o_parts[c] = eb_e[rows, xs_g] * _dot(cq[rows], st)
            ds = _dot_tn(bk[rows], xv_g[rows] * jnp.exp(b_last - b_c))
            st = jnp.exp(b_last) * st + ds
        st_ref[g] = st
        o_groups.append(acc + jnp.concatenate(o_parts, axis=0))
    o = jnp.concatenate(o_groups, axis=1)

    if rev:
        y = (of_ref[...] + o + dsk_ref[...] * xs) * _silu(z_ref[...])
        y = y * lax.rsqrt(jnp.mean(y * y, axis=-1, keepdims=True) + RMS_EPS) * ng_ref[...]
        y_ref[...] = y.astype(y_ref.dtype)
    else:
        o_ref[...] = o


def _ssd_dir(p, xbc, a_log, dt_bias, d_skip, ng, o_f, *, rev, nbat, nblk):
    R = p.shape[0]
    blk = _scan_block(rev, nblk)
    d = 1 if rev else 0
    lane0 = M2_H * d
    alog_row = jnp.zeros((1, LANE), F32).at[0, lane0:lane0 + M2_H].set(a_log[d])
    dtb_row = jnp.zeros((1, LANE), F32).at[0, lane0:lane0 + M2_H].set(dt_bias[d])
    ex = np.zeros((LANE, BRANCH_W), np.float32)
    for h in range(M2_H):
        ex[lane0 + h, h * M2_P:(h + 1) * M2_P] = 1.0
    dsk_row = jnp.repeat(d_skip, M2_P).reshape(1, BRANCH_W)

    def bspec(width, col):
        return pl.BlockSpec((TOK_BLK, width), lambda b, s: (b * nblk + blk(s), col // width))

    const = lambda shape: pl.BlockSpec(shape, lambda b, s: (0, 0))
    in_specs = [bspec(512, 0), bspec(256, 512), bspec(256, 768), bspec(LANE, P_M2_DT), bspec(512, P_M2_Z),
                const((1, LANE)), const((1, LANE)), const((LANE, BRANCH_W)), const((TOK_BLK, TOK_BLK)),
                const((1, BRANCH_W)), const((1, BRANCH_W))]
    args = [xbc, xbc, xbc, p, p, alog_row, dtb_row, jnp.asarray(ex), _cumsum_matrix(rev), dsk_row,
            ng.reshape(1, -1)]
    ospec = pl.BlockSpec((TOK_BLK, BRANCH_W), lambda b, s: (b * nblk + blk(s), 0))
    if rev:
        in_specs.append(ospec)
        args.append(o_f)
    return pl.pallas_call(
        functools.partial(_ssd_kernel, rev=rev),
        grid=(nbat, nblk),
        in_specs=in_specs,
        out_specs=ospec,
        out_shape=jax.ShapeDtypeStruct((R, BRANCH_W), BF16 if rev else F32),
        scratch_shapes=[pltpu.VMEM((M2_G, M2_N, (M2_H // M2_G) * M2_P), F32)],
        compiler_params=_cparams(("arbitrary", "arbitrary")),
        name="ssd_bwd" if rev else "ssd_fwd",
    )(*args)


def _final_norm_kernel(x_ref, g_ref, o_ref):
    x = x_ref[...]
    o_ref[...] = x * lax.rsqrt(jnp.mean(x * x, axis=-1, keepdims=True) + RMS_EPS) * g_ref[...]


def _final_norm(xs, g, *, nbat, nblk, lc):
    D = xs.shape[1]
    cb = lc // TOK_BLK
    nlat = nblk - cb
    out = pl.pallas_call(
        _final_norm_kernel,
        grid=(nbat, nlat),
        in_specs=[pl.BlockSpec((TOK_BLK, D), lambda b, s: (b * nblk + cb + s, 0)),
                  pl.BlockSpec((1, D), lambda b, s: (0, 0))],
        out_specs=pl.BlockSpec((TOK_BLK, D), lambda b, s: (b * nlat + s, 0)),
        out_shape=jax.ShapeDtypeStruct((nbat * nlat * TOK_BLK, D), F32),
        compiler_params=_cparams(("arbitrary", "arbitrary")),
        name="final_norm",
    )(xs, g.reshape(1, D))
    return out.reshape(nbat, nlat * TOK_BLK, D)


def _row_tile(s1):
    for tm in range(1056, 0, -16):
        if s1 % tm == 0:
            return tm
    raise ValueError(f"no row tile for sequence length {s1}")


def kernel(x, c, ctx, c_ctx, norm1_g, norm2_g, w_ada, b_ada, w_in, b_merge, gla_a2, gla_ab, gla_norm_g, na_rpb, gdn_conv, gdn_a_log, gdn_dt_bias, gdn_norm_g, m2_conv, m2_conv_b, m2_a_log, m2_dt_bias, m2_d, m2_norm_g, w_branch, w_out, w_ffn1, w_ffn3, w_ffn2, final_norm_g):
    nbat, t, D = x.shape
    lc = ctx.shape[1]
    depth = w_in.shape[0]
    assert D == D_MODEL and lc == TOK_BLK and t % TOK_BLK == 0 and t // GRID_W >= 3 * ROWS_PER_BLK
    assert nbat + 1 <= MOD_ROWS
    s1 = lc + t
    nblk = s1 // TOK_BLK
    tm = _row_tile(s1)
    geo = dict(tm=tm, s1=s1, lc=lc, nb=nbat)

    xs = jnp.concatenate([ctx, x], axis=1).reshape(nbat * s1, D)
    cvec = jnp.concatenate([c, c_ctx[None], jnp.zeros((MOD_ROWS - nbat - 1, D), F32)], axis=0)
    mods = _ada(cvec, w_ada, b_ada).reshape(depth, MOD_ROWS, 6, D).transpose(0, 2, 1, 3)
    cos, sin = _rope_tables(lc, t)

    for l in range(depth):
        sh1, sc1, g1, sh2, sc2, g2 = (mods[l, i] for i in range(6))
        w_in_l = w_in[l]
        w_mix = _mix_weight(w_in_l)
        w_gate = w_in_l[:, MIX_IN:].reshape(D, N_BRANCH, D).transpose(1, 0, 2).astype(BF16)

        h = _normmod(xs, norm1_g[l], sc1, sh1, **geo)
        p = _matmul(h, w_mix, tm=tm, tn=512, out_dtype=F32)

        o_f = _gla_dir(p, cos, sin, gla_a2[l], gla_ab[l], gla_norm_g[l], None, rev=False, nbat=nbat, nblk=nblk)
        ya = _gla_dir(p, cos, sin, gla_a2[l], gla_ab[l], gla_norm_g[l], o_f, rev=True, nbat=nbat, nblk=nblk)

        yb = _na(p, _na_bias_tiles(na_rpb[l]), nbat=nbat, nblk=nblk)

        qkv = _conv_silu(p, P_GDN_QKV, gdn_conv[l], jnp.zeros((3 * BRANCH_W,), F32), nblk=nblk)
        o_f = _gdn_dir(p, qkv, gdn_a_log[l], gdn_dt_bias[l], gdn_norm_g[l], None, rev=False, nbat=nbat, nblk=nblk)
        yc = _gdn_dir(p, qkv, gdn_a_log[l], gdn_dt_bias[l], gdn_norm_g[l], o_f, rev=True, nbat=nbat, nblk=nblk)

        xbc = _conv_silu(p, P_M2_XBC, m2_conv[l], m2_conv_b[l], nblk=nblk)
        o_f = _ssd_dir(p, xbc, m2_a_log[l], m2_dt_bias[l], m2_d[l], m2_norm_g[l], None, rev=False,
                       nbat=nbat, nblk=nblk)
        yd = _ssd_dir(p, xbc, m2_a_log[l], m2_dt_bias[l], m2_d[l], m2_norm_g[l], o_f, rev=True,
                      nbat=nbat, nblk=nblk)

        merged = _merge(h, w_gate, b_merge[l], (ya, yb, yc, yd), w_branch[l].astype(BF16), tm=tm, tn=256)
        xs = _matmul_residual(merged, w_out[l].astype(BF16), xs, g1, tn=512, name="out_proj", **geo)

        h2 = _normmod(xs, norm2_g[l], sc2, sh2, **geo)
        u = _ffn_up(h2, w_ffn1[l].astype(BF16), w_ffn3[l].astype(BF16), tm=tm, tn=512)
        xs = _matmul_residual(u, w_ffn2[l].astype(BF16), xs, g2, tn=256, name="ffn_down", **geo)

    return _final_norm(xs, final_norm_g, nbat=nbat, nblk=nblk, lc=lc)
```

```python
import functools

import numpy as np
import jax
import jax.numpy as jnp
from jax import lax
from jax.experimental import pallas as pl
from jax.experimental.pallas import tpu as pltpu

D_MODEL = 2048
GRID_W = 64
N_BRANCH = 4
BRANCH_W = D_MODEL // 4
CHUNK = 64
CONV_W = 5
RMS_EPS = 1e-6
NEG_INF = -1e30
ROPE_BASE = 10000.0
GLA_H = 4
GLA_DV = BRANCH_W // GLA_H
GLA_DK = GLA_DV // 2
GLA_LR = 16
GLA_TAU = 16.0
NA_H = 4
NA_D = BRANCH_W // NA_H
NA_WIN_R = 8
NA_WIN_C = 16
GDN_H = 4
GDN_D = BRANCH_W // GDN_H
M2_P = 64
M2_H = BRANCH_W // M2_P
M2_N = 128
M2_G = 2
M2_CONV_CH = BRANCH_W + 2 * M2_G * M2_N
D_FF = ((8 * D_MODEL + 3 * 256 - 1) // (3 * 256)) * 256
GLA_IN = 2 * GLA_H * GLA_DK + 2 * BRANCH_W + 2 * GLA_LR
NA_IN = 3 * BRANCH_W
GDN_IN = 4 * BRANCH_W + 4 * GDN_H
M2_IN = BRANCH_W + M2_CONV_CH + 2 * M2_H
MIX_IN = GLA_IN + NA_IN + GDN_IN + M2_IN

F32 = jnp.float32
BF16 = jnp.bfloat16

LANE = 128
SUBLANE = 8
V7X_VMEM_BYTES = 64 * 1024 * 1024
VMEM_LIMIT = V7X_VMEM_BYTES - 8 * 1024 * 1024

TOK_BLK = 4 * CHUNK
ROWS_PER_BLK = TOK_BLK // GRID_W
HALO = SUBLANE
MOD_ROWS = 8

P_GDN_QKV = 0
P_GLA_V = 1536
P_M2_XBC = 2048
P_GLA_G = 3072
P_NA_Q, P_NA_K, P_NA_V = 3584, 4096, 4608
P_GDN_Z, P_M2_Z = 5120, 5632
P_GLA_Q, P_GLA_K = 6144, 6400
P_GLA_LR, P_GDN_SC, P_M2_DT = 6656, 6784, 6912
P_COLS = 7168


def _mix_weight(w_in_l):
    gla, na, gdn, m2 = 0, GLA_IN, GLA_IN + NA_IN, GLA_IN + NA_IN + GDN_IN
    fields = sorted([
        (P_GLA_V, gla + 512, 512), (P_GLA_G, gla + 1024, 512),
        (P_NA_Q, na, 512), (P_NA_K, na + 512, 512), (P_NA_V, na + 1024, 512),
        (P_GDN_QKV, gdn, 1536), (P_GDN_Z, gdn + 1536, 512),
        (P_M2_Z, m2, 512), (P_M2_XBC, m2 + 512, M2_CONV_CH),
        (P_GLA_Q, gla, 256), (P_GLA_K, gla + 256, 256),
        (P_GLA_LR, gla + 1536, 2 * GLA_LR), (P_GDN_SC, gdn + 2048, 4 * GDN_H),
        (P_M2_DT, m2 + 512 + M2_CONV_CH, 2 * M2_H)])
    d = w_in_l.shape[0]
    parts, col = [], 0
    for dst, src, n in fields:
        if dst > col:
            parts.append(jnp.zeros((d, dst - col), BF16))
        parts.append(w_in_l[:, src:src + n].astype(BF16))
        col = dst + n
    parts.append(jnp.zeros((d, P_COLS - col), BF16))
    return jnp.concatenate(parts, axis=1)


def _cparams(sem):
    return pltpu.CompilerParams(dimension_semantics=sem, vmem_limit_bytes=VMEM_LIMIT)


def _sigmoid(x):
    return 1.0 / (1.0 + jnp.exp(-x))


def _silu(x):
    return x * _sigmoid(x)


def _softplus(x):
    return jnp.maximum(x, 0.0) + jnp.log1p(jnp.exp(-jnp.abs(x)))


def _dot(a, b):
    return jnp.dot(a.astype(BF16), b.astype(BF16), preferred_element_type=F32)


def _dot_nt(a, b):
    return lax.dot_general(a.astype(BF16), b.astype(BF16), (((1,), (1,)), ((), ())),
                           preferred_element_type=F32)


def _dot_tn(a, b):
    return lax.dot_general(a.astype(BF16), b.astype(BF16), (((0,), (0,)), ((), ())),
                           preferred_element_type=F32)


def _dot_hi(a, b):
    return jnp.dot(a, b, precision=lax.Precision.HIGHEST, preferred_element_type=F32)


def _split(a):
    hi = a.astype(BF16)
    return hi, (a - hi.astype(F32)).astype(BF16)


def _dot_split(a, b):
    (ah, al), (bh, bl) = a, b
    d = lambda x, y: jnp.dot(x, y, preferred_element_type=F32)
    return d(ah, bh) + (d(ah, bl) + d(al, bh))


def _iota2(shape, dim):
    return lax.broadcasted_iota(jnp.int32, shape, dim)


def _idiv(x, n):
    assert n & (n - 1) == 0
    return lax.shift_right_logical(x, jnp.int32(n.bit_length() - 1))


def _imod(x, n):
    assert n & (n - 1) == 0
    return x & (n - 1)


def _chunk_mask(rev, strict=False):
    r = _iota2((TOK_BLK, TOK_BLK), 0)
    c = _iota2((TOK_BLK, TOK_BLK), 1)
    same = _idiv(r, CHUNK) == _idiv(c, CHUNK)
    if rev:
        tri = (c > r) if strict else (c >= r)
    else:
        tri = (c < r) if strict else (c <= r)
    return same & tri


def _chunk_order(rev):
    n = TOK_BLK // CHUNK
    return list(range(n - 1, -1, -1)) if rev else list(range(n))


def _row_split(i, tm, s1):
    r0 = i * tm
    return lax.div(r0, jnp.int32(s1)), lax.rem(r0, jnp.int32(s1))


def _pick_mod(ref, b, nb, is_ctx):
    return jnp.where(is_ctx, ref[nb:nb + 1, :], ref[pl.ds(b, 1), :])


def _ada_kernel(c_ref, w_ref, b_ref, o_ref):
    a = _silu(c_ref[...])
    o_ref[...] = _dot(a, w_ref[...]) + b_ref[...]


def _ada(cvec, w_ada, b_ada):
    L, D, N = w_ada.shape
    tn = 1024
    return pl.pallas_call(
        _ada_kernel,
        grid=(L, N // tn),
        in_specs=[pl.BlockSpec((MOD_ROWS, D), lambda l, j: (0, 0)),
                  pl.BlockSpec((None, D, tn), lambda l, j: (l, 0, j)),
                  pl.BlockSpec((None, 1, tn), lambda l, j: (l, 0, j))],
        out_specs=pl.BlockSpec((None, MOD_ROWS, tn), lambda l, j: (l, 0, j)),
        out_shape=jax.ShapeDtypeStruct((L, MOD_ROWS, N), F32),
        compiler_params=_cparams(("arbitrary", "arbitrary")),
        name="ada_mod",
    )(cvec, w_ada, b_ada.reshape(L, 1, N))


def _normmod_kernel(x_ref, g_ref, sc_ref, sh_ref, o_ref, *, tm, s1, lc, nb):
    b, pos0 = _row_split(pl.program_id(0), tm, s1)
    is_ctx = (pos0 + _iota2((tm, 1), 0)) < lc
    x = x_ref[...]
    y = x * lax.rsqrt(jnp.mean(x * x, axis=-1, keepdims=True) + RMS_EPS) * g_ref[...]
    sc = _pick_mod(sc_ref, b, nb, is_ctx)
    sh = _pick_mod(sh_ref, b, nb, is_ctx)
    o_ref[...] = (y * (1.0 + sc) + sh).astype(o_ref.dtype)


def _normmod(xs, g, sc, sh, *, tm, s1, lc, nb):
    R, D = xs.shape
    return pl.pallas_call(
        functools.partial(_normmod_kernel, tm=tm, s1=s1, lc=lc, nb=nb),
        grid=(R // tm,),
        in_specs=[pl.BlockSpec((tm, D), lambda i: (i, 0)),
                  pl.BlockSpec((1, D), lambda i: (0, 0)),
                  pl.BlockSpec((MOD_ROWS, D), lambda i: (0, 0)),
                  pl.BlockSpec((MOD_ROWS, D), lambda i: (0, 0))],
        out_specs=pl.BlockSpec((tm, D), lambda i: (i, 0)),
        out_shape=jax.ShapeDtypeStruct((R, D), BF16),
        compiler_params=_cparams(("arbitrary",)),
        name="norm_mod",
    )(xs, g.reshape(1, D), sc, sh)


def _mm_kernel(a_ref, w_ref, o_ref):
    o_ref[...] = jnp.dot(a_ref[...], w_ref[...], preferred_element_type=F32).astype(o_ref.dtype)


def _matmul(a, w, *, tm, tn, out_dtype):
    R, K = a.shape
    N = w.shape[1]
    assert R % tm == 0 and N % tn == 0
    return pl.pallas_call(
        _mm_kernel,
        grid=(R // tm, N // tn),
        in_specs=[pl.BlockSpec((tm, K), lambda i, j: (i, 0)),
                  pl.BlockSpec((K, tn), lambda i, j: (0, j))],
        out_specs=pl.BlockSpec((tm, tn), lambda i, j: (i, j)),
        out_shape=jax.ShapeDtypeStruct((R, N), out_dtype),
        compiler_params=_cparams(("arbitrary", "arbitrary")),
        name="mix_proj",
    )(a, w)


def _mm_res_kernel(a_ref, w_ref, x_ref, gt_ref, o_ref, *, tm, s1, lc, nb):
    b, pos0 = _row_split(pl.program_id(0), tm, s1)
    is_ctx = (pos0 + _iota2((tm, 1), 0)) < lc
    acc = jnp.dot(a_ref[...], w_ref[...], preferred_element_type=F32)
    o_ref[...] = x_ref[...] + _pick_mod(gt_ref, b, nb, is_ctx) * acc


def _matmul_residual(a, w, xs, gate, *, tm, tn, s1, lc, nb, name):
    R, K = a.shape
    N = w.shape[1]
    assert R % tm == 0 and N % tn == 0
    return pl.pallas_call(
        functools.partial(_mm_res_kernel, tm=tm, s1=s1, lc=lc, nb=nb),
        grid=(R // tm, N // tn),
        in_specs=[pl.BlockSpec((tm, K), lambda i, j: (i, 0)),
                  pl.BlockSpec((K, tn), lambda i, j: (0, j)),
                  pl.BlockSpec((tm, tn), lambda i, j: (i, j)),
                  pl.BlockSpec((MOD_ROWS, tn), lambda i, j: (0, j))],
        out_specs=pl.BlockSpec((tm, tn), lambda i, j: (i, j)),
        out_shape=jax.ShapeDtypeStruct((R, N), F32),
        compiler_params=_cparams(("arbitrary", "arbitrary")),
        name=name,
    )(a, w, xs, gate)


def _outproj_norm_kernel(a_ref, w_ref, x_ref, gt_ref, g_ref, sc_ref, sh_ref, xo_ref, h_ref, *, tm, s1, lc, nb):
    b, pos0 = _row_split(pl.program_id(0), tm, s1)
    is_ctx = (pos0 + _iota2((tm, 1), 0)) < lc
    acc = jnp.dot(a_ref[...], w_ref[...], preferred_element_type=F32)
    x = x_ref[...] + _pick_mod(gt_ref, b, nb, is_ctx) * acc
    xo_ref[...] = x
    y = x * lax.rsqrt(jnp.mean(x * x, axis=-1, keepdims=True) + RMS_EPS) * g_ref[...]
    h_ref[...] = (y * (1.0 + _pick_mod(sc_ref, b, nb, is_ctx)) + _pick_mod(sh_ref, b, nb, is_ctx)).astype(h_ref.dtype)


def _outproj_norm(a, w, xs, gate, g, sc, sh, *, tm, s1, lc, nb):
    R, K = a.shape
    D = w.shape[1]
    row = lambda i: (i, 0)
    const = lambda i: (0, 0)
    return pl.pallas_call(
        functools.partial(_outproj_norm_kernel, tm=tm, s1=s1, lc=lc, nb=nb),
        grid=(R // tm,),
        in_specs=[pl.BlockSpec((tm, K), row), pl.BlockSpec((K, D), const), pl.BlockSpec((tm, D), row),
                  pl.BlockSpec((MOD_ROWS, D), const), pl.BlockSpec((1, D), const),
                  pl.BlockSpec((MOD_ROWS, D), const), pl.BlockSpec((MOD_ROWS, D), const)],
        out_specs=[pl.BlockSpec((tm, D), row), pl.BlockSpec((tm, D), row)],
        out_shape=[jax.ShapeDtypeStruct((R, D), F32), jax.ShapeDtypeStruct((R, D), BF16)],
        compiler_params=_cparams(("arbitrary",)),
        name="out_proj",
    )(a, w, xs, gate, g.reshape(1, D), sc, sh)


def _ffn_up_kernel(a_ref, w1_ref, w3_ref, o_ref):
    a = a_ref[...]
    u = jnp.dot(a, w1_ref[...], preferred_element_type=F32)
    v = jnp.dot(a, w3_ref[...], preferred_element_type=F32)
    o_ref[...] = (_silu(u) * v).astype(o_ref.dtype)


def _ffn_up(a, w1, w3, *, tm, tn):
    R, K = a.shape
    N = w1.shape[1]
    assert R % tm == 0 and N % tn == 0
    return pl.pallas_call(
        _ffn_up_kernel,
        grid=(R // tm, N // tn),
        in_specs=[pl.BlockSpec((tm, K), lambda i, j: (i, 0)),
                  pl.BlockSpec((K, tn), lambda i, j: (0, j)),
                  pl.BlockSpec((K, tn), lambda i, j: (0, j))],
        out_specs=pl.BlockSpec((tm, tn), lambda i, j: (i, j)),
        out_shape=jax.ShapeDtypeStruct((R, N), BF16),
        compiler_params=_cparams(("arbitrary", "arbitrary")),
        name="ffn_up",
    )(a, w1, w3)


def _merge_kernel(h_ref, g0_ref, g1_ref, g2_ref, g3_ref, bm_ref, ya_ref, yb_ref, yc_ref, yd_ref, wb_ref, o_ref):
    h = h_ref[...]
    acc = None
    branches = zip((g0_ref, g1_ref, g2_ref, g3_ref), (ya_ref, yb_ref, yc_ref, yd_ref))
    for i, (wg_ref, y_ref) in enumerate(branches):
        gate = _sigmoid(jnp.dot(h, wg_ref[...], preferred_element_type=F32) + bm_ref[i])
        term = gate * jnp.dot(y_ref[...], wb_ref[i], preferred_element_type=F32)
        acc = term if acc is None else acc + term
    o_ref[...] = acc.astype(o_ref.dtype)


def _merge(h, wg, b_merge, ys, wb, *, tm, tn):
    R, D = h.shape
    N = wb.shape[2]
    assert R % tm == 0 and N % tn == 0
    nj = N // tn
    yspec = pl.BlockSpec((tm, BRANCH_W), lambda i, j: (i, 0))
    gspec = lambda br: pl.BlockSpec((D, tn), lambda i, j: (0, br * nj + j))
    return pl.pallas_call(
        _merge_kernel,
        grid=(R // tm, nj),
        in_specs=[pl.BlockSpec((tm, D), lambda i, j: (i, 0)),
                  gspec(0), gspec(1), gspec(2), gspec(3),
                  pl.BlockSpec((N_BRANCH, 1, tn), lambda i, j: (0, 0, j)),
                  yspec, yspec, yspec, yspec,
                  pl.BlockSpec((N_BRANCH, BRANCH_W, tn), lambda i, j: (0, 0, j))],
        out_specs=pl.BlockSpec((tm, tn), lambda i, j: (i, j)),
        out_shape=jax.ShapeDtypeStruct((R, N), BF16),
        compiler_params=_cparams(("arbitrary", "arbitrary")),
        name="merge",
    )(h, wg, wg, wg, wg, b_merge.reshape(N_BRANCH, 1, N), *ys, wb)


def _conv_kernel(xm_ref, xp_ref, xn_ref, w_ref, b_ref, o_ref, *, nblk, nbat):
    blk = pl.program_id(0)
    prev_ok = blk >= 2
    next_ok = (blk >= 1) & (blk < nblk - 1)
    w = w_ref[...]
    for bi in range(nbat):
        xp = jnp.where(prev_ok, xp_ref[bi], 0.0)
        xn = jnp.where(next_ok, xn_ref[bi], 0.0)
        xpad = jnp.concatenate([xp, xm_ref[bi], xn], axis=0)
        acc = b_ref[...] + w[0:1, :] * xpad[HALO - 2:HALO - 2 + TOK_BLK, :]
        for j in range(1, CONV_W):
            acc = acc + w[j:j + 1, :] * xpad[HALO - 2 + j:HALO - 2 + j + TOK_BLK, :]
        o_ref[bi] = _silu(acc)


def _conv_silu(p, col0, w, bias, *, nblk):
    nbat, s1, _ = p.shape
    C = w.shape[1]
    assert col0 % C == 0
    cb = col0 // C
    hb = TOK_BLK // HALO
    nhalo = s1 // HALO
    wpad = jnp.concatenate([w, jnp.zeros((SUBLANE - CONV_W, C), F32)], axis=0)
    return pl.pallas_call(
        functools.partial(_conv_kernel, nblk=nblk, nbat=nbat),
        grid=(nblk,),
        in_specs=[pl.BlockSpec((nbat, TOK_BLK, C), lambda r: (0, r, cb)),
                  pl.BlockSpec((nbat, HALO, C), lambda r: (0, jnp.maximum(r * hb - 1, 0), cb)),
                  pl.BlockSpec((nbat, HALO, C), lambda r: (0, jnp.minimum((r + 1) * hb, nhalo - 1), cb)),
                  pl.BlockSpec((SUBLANE, C), lambda r: (0, 0)),
                  pl.BlockSpec((1, C), lambda r: (0, 0))],
        out_specs=pl.BlockSpec((nbat, TOK_BLK, C), lambda r: (0, r, 0)),
        out_shape=jax.ShapeDtypeStruct((nbat, s1, C), F32),
        compiler_params=_cparams(("arbitrary",)),
        name="short_conv",
    )(p, p, p, wpad, bias.reshape(1, C))


def _scan_block(rev, nblk):
    if rev:
        return lambda s: jnp.where(s == 0, 0, nblk - s)
    return lambda s: s


def _blk_spec(nbat, width, col, blk):
    assert col % width == 0
    return pl.BlockSpec((nbat, TOK_BLK, width), lambda s: (0, blk(s), col // width))


def _const_spec(shape):
    return pl.BlockSpec(shape, lambda s: (0,) * len(shape))


def _head_rmsnorm_gate(o, g, gate, n_head, width):
    outs = []
    for h in range(n_head):
        oh = o[:, h * width:(h + 1) * width]
        yh = oh * lax.rsqrt(jnp.mean(oh * oh, axis=-1, keepdims=True) + RMS_EPS) * g
        outs.append(yh * _silu(gate[:, h * width:(h + 1) * width]))
    return jnp.concatenate(outs, axis=1)


def _scan_call(kern, name, in_specs, args, o_f, *, rev, nbat, s1, nblk, state_shape):
    blk = _scan_block(rev, nblk)
    ospec = pl.BlockSpec((nbat, TOK_BLK, BRANCH_W), lambda s: (0, blk(s), 0))
    if rev:
        in_specs = in_specs + [ospec]
        args = args + [o_f]
    return pl.pallas_call(
        functools.partial(kern, rev=rev, nbat=nbat),
        grid=(nblk,),
        in_specs=in_specs,
        out_specs=ospec,
        out_shape=jax.ShapeDtypeStruct((nbat, s1, BRANCH_W), BF16 if rev else F32),
        scratch_shapes=[pltpu.VMEM((nbat,) + state_shape, F32)],
        compiler_params=_cparams(("arbitrary",)),
        name=name + ("_bwd" if rev else "_fwd"),
    )(*args)


def _gla_kernel(q_ref, k_ref, v_ref, g_ref, lr_ref, cos_ref, sin_ref, a2_ref, ab_ref, lm_ref, ng_ref,
                *rest, rev, nbat):
    if rev:
        of_ref, y_ref, st_ref = rest
    else:
        o_ref, st_ref = rest

    @pl.when(pl.program_id(0) == 0)
    def _():
        st_ref[...] = jnp.zeros_like(st_ref)

    hk = GLA_H * GLA_DK
    lane = _iota2((TOK_BLK, hk), 1)
    first_half = _imod(lane, GLA_DK) < (GLA_DK // 2)
    head_of_lane = _idiv(lane, GLA_DK)
    cos = cos_ref[...]
    sin = sin_ref[...]
    mask = _chunk_mask(rev)
    n_chunk = TOK_BLK // CHUNK
    batch = range(nbat)

    def rope(x):
        partner = jnp.where(first_half, pltpu.roll(x, hk - GLA_DK // 2, 1), pltpu.roll(x, GLA_DK // 2, 1))
        return x * cos + partner * sin

    ks, vs, bs, q_ins, o_intras = [], [], [], [], []
    for bi in batch:
        q = rope(q_ref[bi]) * GLA_DK ** -0.5
        k = rope(k_ref[bi])
        v = v_ref[bi]
        loga = -_softplus(-(_dot(lr_ref[bi], a2_ref[...]) + ab_ref[...])) / GLA_TAU
        b = _dot_hi(lm_ref[...], loga)
        q_in = q * jnp.exp(b)
        k_in = k * jnp.exp(-b)
        o_heads = []
        for h in range(GLA_H):
            att = _dot_nt(jnp.where(head_of_lane == h, q_in, 0.0), k_in)
            att = jnp.where(mask, att, 0.0)
            o_heads.append(_dot(att, v[:, h * GLA_DV:(h + 1) * GLA_DV]))
        ks.append(k)
        vs.append(v)
        bs.append(b)
        q_ins.append(q_in)
        o_intras.append(jnp.concatenate(o_heads, axis=1))

    sts = [st_ref[bi] for bi in batch]
    diag = _idiv(_iota2(sts[0].shape, 0), GLA_DV) == _idiv(_iota2(sts[0].shape, 1), GLA_DK)
    o_inter = [[None] * n_chunk for _ in batch]
    for c in _chunk_order(rev):
        rows = slice(c * CHUNK, (c + 1) * CHUNK)
        for bi in batch:
            b_c = bs[bi][rows]
            b_last = b_c[0:1] if rev else b_c[CHUNK - 1:CHUNK]
            o_inter[bi][c] = _dot_nt(q_ins[bi][rows], sts[bi])
            k_end = ks[bi][rows] * jnp.exp(b_last - b_c)
            ds = _dot_tn(vs[bi][rows], k_end)
            sts[bi] = sts[bi] * jnp.exp(b_last) + jnp.where(diag, ds, 0.0)
    for bi in batch:
        st_ref[bi] = sts[bi]
        o = o_intras[bi] + jnp.concatenate(o_inter[bi], axis=0)
        if rev:
            y = _head_rmsnorm_gate(of_ref[bi] + o, ng_ref[...], g_ref[bi], GLA_H, GLA_DV)
            y_ref[bi] = y.astype(y_ref.dtype)
        else:
            o_ref[bi] = o


def _cumsum_matrix(rev):
    r = np.arange(TOK_BLK)[:, None]
    c = np.arange(TOK_BLK)[None, :]
    same = (r // CHUNK) == (c // CHUNK)
    tri = (c >= r) if rev else (c <= r)
    return jnp.asarray((same & tri).astype(np.float32))


def _gla_dir(p, cos, sin, a2, ab, ng, o_f, *, rev, nblk):
    nbat, s1, _ = p.shape
    blk = _scan_block(rev, nblk)
    d = 1 if rev else 0
    a2d = jnp.zeros((LANE, GLA_H * GLA_DK), F32).at[d * GLA_LR:(d + 1) * GLA_LR].set(a2[d])
    tab = pl.BlockSpec((TOK_BLK, GLA_H * GLA_DK), lambda s: (blk(s), 0))
    in_specs = [_blk_spec(nbat, 256, P_GLA_Q, blk), _blk_spec(nbat, 256, P_GLA_K, blk),
                _blk_spec(nbat, 512, P_GLA_V, blk), _blk_spec(nbat, 512, P_GLA_G, blk),
                _blk_spec(nbat, LANE, P_GLA_LR, blk), tab, tab,
                _const_spec((LANE, 256)), _const_spec((1, 256)), _const_spec((TOK_BLK, TOK_BLK)),
                _const_spec((1, GLA_DV))]
    args = [p, p, p, p, p, cos, sin, a2d, ab[d].reshape(1, -1), _cumsum_matrix(rev), ng.reshape(1, -1)]
    return _scan_call(_gla_kernel, "gla", in_specs, args, o_f, rev=rev, nbat=nbat, s1=s1, nblk=nblk,
                      state_shape=(GLA_H * GLA_DV, GLA_H * GLA_DK))


def _rope_tables(lc, t):
    n_freq = GLA_DK // 4
    freqs = ROPE_BASE ** (-jnp.arange(n_freq, dtype=F32) / n_freq)
    tt = jnp.arange(t)
    row = (tt // GRID_W).astype(F32)
    col = (tt % GRID_W).astype(F32)
    ang = jnp.concatenate([row[:, None] * freqs, col[:, None] * freqs], axis=-1)
    cos, sin = jnp.cos(ang), jnp.sin(ang)
    cos = jnp.concatenate([jnp.ones((lc, GLA_DK // 2), F32), cos], axis=0)
    sin = jnp.concatenate([jnp.zeros((lc, GLA_DK // 2), F32), sin], axis=0)
    cos_h = jnp.concatenate([cos, cos], axis=1)
    sin_h = jnp.concatenate([-sin, sin], axis=1)
    return jnp.tile(cos_h, (1, GLA_H)), jnp.tile(sin_h, (1, GLA_H))


def _na_kernel(q_ref, kp_ref, kc_ref, kn_ref, kx_ref, vp_ref, vc_ref, vn_ref, vx_ref, bias_ref, y_ref, *, nbat):
    scale = NA_D ** -0.5
    for bi in range(nbat):
        outs = []
        for h in range(NA_H):
            hs = slice(h * NA_D, (h + 1) * NA_D)
            qh = (q_ref[bi, :, hs] * scale).astype(BF16)
            s = jnp.concatenate(
                [_dot_nt(qh, kp_ref[bi, :, hs]) + bias_ref[0, h],
                 _dot_nt(qh, kc_ref[bi, :, hs]) + bias_ref[1, h],
                 _dot_nt(qh, kn_ref[bi, :, hs]) + bias_ref[2, h],
                 _dot_nt(qh, kx_ref[bi, :, hs])], axis=1)
            m = jnp.max(s, axis=-1, keepdims=True)
            e = jnp.exp(s - m)
            p = e / jnp.sum(e, axis=-1, keepdims=True)
            o = (_dot(p[:, 0:TOK_BLK], vp_ref[bi, :, hs]) + _dot(p[:, TOK_BLK:2 * TOK_BLK], vc_ref[bi, :, hs])
                 + _dot(p[:, 2 * TOK_BLK:3 * TOK_BLK], vn_ref[bi, :, hs])
                 + _dot(p[:, 3 * TOK_BLK:], vx_ref[bi, :, hs]))
            outs.append(o)
        y_ref[bi] = jnp.concatenate(outs, axis=1).astype(y_ref.dtype)


def _na_bias_tiles(rpb):
    edge = GRID_W - NA_WIN_C
    ext = jnp.concatenate([jnp.repeat(rpb[..., :1], edge, axis=-1), rpb,
                           jnp.repeat(rpb[..., -1:], edge, axis=-1)], axis=-1)
    toep = jnp.stack([ext[..., GRID_W - 1 - qc:2 * GRID_W - 1 - qc] for qc in range(GRID_W)], axis=-2)
    qc = np.arange(GRID_W)[:, None]
    kc = np.arange(GRID_W)[None, :]
    c0 = np.clip(qc - NA_WIN_C // 2, 0, GRID_W - NA_WIN_C)
    col_ok = (kc >= c0) & (kc < c0 + NA_WIN_C)
    toep = jnp.where(jnp.asarray(col_ok), toep, NEG_INF).astype(F32)
    masked = jnp.full((NA_H, GRID_W, GRID_W), NEG_INF, F32)
    kinds = []
    for kind in range(3):
        offs = []
        for off in (-1, 0, 1):
            rows = []
            for qr in range(ROWS_PER_BLK):
                start = (0, qr - NA_WIN_R // 2, ROWS_PER_BLK - NA_WIN_R)[kind]
                cols = []
                for kb in range(ROWS_PER_BLK):
                    kr = kb + ROWS_PER_BLK * off
                    ok = start <= kr < start + NA_WIN_R
                    cols.append(toep[:, kr - qr + NA_WIN_R - 1] if ok else masked)
                rows.append(jnp.concatenate(cols, axis=-1))
            offs.append(jnp.concatenate(rows, axis=-2))
        kinds.append(jnp.stack(offs))
    kinds.append(jnp.full_like(kinds[0], NEG_INF))
    return jnp.stack(kinds)


def _na(p, bias, *, nblk):
    nbat, s1, _ = p.shape

    def kind(s):
        return jnp.where(s == 0, 3, jnp.where(s == 1, 0, jnp.where(s == nblk - 1, 2, 1)))

    prev = lambda s: jnp.maximum(s - 1, 1)
    cur = lambda s: s
    nxt = lambda s: jnp.minimum(s + 1, nblk - 1)
    ctx = lambda s: 0
    spec = lambda col, blk: _blk_spec(nbat, BRANCH_W, col, blk)
    return pl.pallas_call(
        functools.partial(_na_kernel, nbat=nbat),
        grid=(nblk,),
        in_specs=[spec(P_NA_Q, cur),
                  spec(P_NA_K, prev), spec(P_NA_K, cur), spec(P_NA_K, nxt), spec(P_NA_K, ctx),
                  spec(P_NA_V, prev), spec(P_NA_V, cur), spec(P_NA_V, nxt), spec(P_NA_V, ctx),
                  pl.BlockSpec((None, 3, NA_H, TOK_BLK, TOK_BLK), lambda s: (kind(s), 0, 0, 0, 0))],
        out_specs=pl.BlockSpec((nbat, TOK_BLK, BRANCH_W), lambda s: (0, s, 0)),
        out_shape=jax.ShapeDtypeStruct((nbat, s1, BRANCH_W), BF16),
        compiler_params=_cparams(("arbitrary",)),
        name="nbr_attn",
    )(p, p, p, p, p, p, p, p, p, bias)


def _l2norm(x):
    return x * lax.rsqrt(jnp.sum(x * x, axis=-1, keepdims=True) + RMS_EPS)


def _gdn_kernel(q_ref, k_ref, v_ref, z_ref, sc_ref, alog_ref, dtb_ref, lm_ref, ng_ref, *rest, rev, nbat):
    if rev:
        of_ref, y_ref, st_ref = rest
    else:
        o_ref, st_ref = rest

    @pl.when(pl.program_id(0) == 0)
    def _():
        st_ref[...] = jnp.zeros_like(st_ref)

    d = 1 if rev else 0
    m_incl = _chunk_mask(rev)
    m_strict = _chunk_mask(rev, strict=True)
    eye = (_iota2((TOK_BLK, TOK_BLK), 0) == _iota2((TOK_BLK, TOK_BLK), 1)).astype(F32)
    n_chunk = TOK_BLK // CHUNK

    units = [(bi, h) for bi in range(nbat) for h in range(GDN_H)]
    qs, ks, bcols, ebcs, attns, nmats, rhss = [], [], [], [], [], [], []
    for bi in range(nbat):
        sc = sc_ref[bi]
        beta_all = _sigmoid(sc)
        g_all = -jnp.exp(alog_ref[...]) * _softplus(sc + dtb_ref[...])
        b_all = _dot_hi(lm_ref[...], g_all)
        b_all_t = b_all.T
        eb_all = jnp.exp(b_all)
        for h in range(GDN_H):
            hs = slice(h * GDN_D, (h + 1) * GDN_D)
            qh = _l2norm(q_ref[bi, :, hs]) * GDN_D ** -0.5
            kh = _l2norm(k_ref[bi, :, hs])
            vh = v_ref[bi, :, hs]
            lb, lg = GDN_H * d + h, 2 * GDN_H + GDN_H * d + h
            beta = beta_all[:, lb:lb + 1]
            bcol = b_all[:, lg:lg + 1]
            brow = b_all_t[lg:lg + 1, :]
            ebc = eb_all[:, lg:lg + 1]
            diff = bcol - brow
            dec_incl = jnp.where(m_incl, jnp.exp(jnp.where(m_incl, diff, 0.0)), 0.0)
            dec_strict = jnp.where(m_strict, dec_incl, 0.0)
            kk = _dot_nt(kh, kh)
            qs.append(qh)
            ks.append(kh)
            bcols.append(bcol)
            ebcs.append(ebc)
            attns.append(_dot_nt(qh, kh) * dec_incl)
            nmats.append(-(beta * kk * dec_strict))
            rhss.append(_split(jnp.concatenate([kh * (beta * ebc), vh * beta], axis=1)))

    tinvs = [eye + n for n in nmats]
    nss = [_split(n) for n in nmats]
    for _ in range(5):
        nss = [_split(_dot_split(ns, ns)) for ns in nss]
        tinvs = [t + _dot_split(_split(t), ns) for t, ns in zip(tinvs, nss)]
    sols = [_dot_split(_split(t), r) for t, r in zip(tinvs, rhss)]
    ws = [s[:, :GDN_D] for s in sols]
    u0s = [s[:, GDN_D:] for s in sols]

    sts = [st_ref[bi, h] for bi, h in units]
    u_parts = [[None] * n_chunk for _ in units]
    o_parts = [[None] * n_chunk for _ in units]
    for c in _chunk_order(rev):
        rows = slice(c * CHUNK, (c + 1) * CHUNK)
        last = c * CHUNK if rev else (c + 1) * CHUNK - 1
        for i in range(len(units)):
            st = sts[i]
            b_last = bcols[i][last:last + 1]
            u = u0s[i][rows] - _dot(ws[i][rows], st)
            u_parts[i][c] = u
            o_parts[i][c] = ebcs[i][rows] * _dot(qs[i][rows], st)
            k_end = ks[i][rows] * jnp.exp(b_last - bcols[i][rows])
            sts[i] = jnp.exp(b_last) * st + _dot_tn(k_end, u)
    o_units = []
    for i, (bi, h) in enumerate(units):
        st_ref[bi, h] = sts[i]
        o_units.append(jnp.concatenate(o_parts[i], axis=0) + _dot(attns[i], jnp.concatenate(u_parts[i], axis=0)))
    for bi in range(nbat):
        o = jnp.concatenate(o_units[bi * GDN_H:(bi + 1) * GDN_H], axis=1)
        if rev:
            y = _head_rmsnorm_gate(of_ref[bi] + o, ng_ref[...], z_ref[bi], GDN_H, GDN_D)
            y_ref[bi] = y.astype(y_ref.dtype)
        else:
            o_ref[bi] = o


def _gdn_dir(p, qkv, a_log, dt_bias, ng, o_f, *, rev, nblk):
    nbat, s1, _ = p.shape
    blk = _scan_block(rev, nblk)
    d = 1 if rev else 0
    lane0 = 2 * GDN_H + GDN_H * d
    alog_row = jnp.zeros((1, LANE), F32).at[0, lane0:lane0 + GDN_H].set(a_log[d])
    dtb_row = jnp.zeros((1, LANE), F32).at[0, lane0:lane0 + GDN_H].set(dt_bias[d])
    in_specs = [_blk_spec(nbat, 512, 0, blk), _blk_spec(nbat, 512, 512, blk), _blk_spec(nbat, 512, 1024, blk),
                _blk_spec(nbat, 512, P_GDN_Z, blk), _blk_spec(nbat, LANE, P_GDN_SC, blk),
                _const_spec((1, LANE)), _const_spec((1, LANE)), _const_spec((TOK_BLK, TOK_BLK)),
                _const_spec((1, GDN_D))]
    args = [qkv, qkv, qkv, p, p, alog_row, dtb_row, _cumsum_matrix(rev), ng.reshape(1, -1)]
    return _scan_call(_gdn_kernel, "gdn", in_specs, args, o_f, rev=rev, nbat=nbat, s1=s1, nblk=nblk,
                      state_shape=(GDN_H, GDN_D, GDN_D))


def _ssd_kernel(xs_ref, bm_ref, cm_ref, dt_ref, z_ref, alog_ref, dtb_ref, ex_ref, lm_ref, dsk_ref, ng_ref,
                *rest, rev, nbat):
    if rev:
        of_ref, y_ref, st_ref = rest
    else:
        o_ref, st_ref = rest

    @pl.when(pl.program_id(0) == 0)
    def _():
        st_ref[...] = jnp.zeros_like(st_ref)

    d = 1 if rev else 0
    heads_per_g = M2_H // M2_G
    gw = heads_per_g * M2_P
    n_chunk = TOK_BLK // CHUNK
    mask = _chunk_mask(rev)
    lane = _iota2((TOK_BLK, gw), 1)
    ex = ex_ref[...]

    units = [(bi, g) for bi in range(nbat) for g in range(M2_G)]
    cqs, bks, xvs, b_es, eb_es, accs = [], [], [], [], [], []
    for bi in range(nbat):
        dt = _softplus(dt_ref[bi] + dtb_ref[...])
        loga = -jnp.exp(alog_ref[...]) * dt
        b8 = _dot_hi(lm_ref[...], loga)
        b8_t = b8.T
        xv = xs_ref[bi] * _dot_hi(dt, ex)
        b_e = _dot_hi(b8, ex)
        eb_e = jnp.exp(b_e)
        for g in range(M2_G):
            gs = slice(g * M2_N, (g + 1) * M2_N)
            xs_g = slice(g * gw, (g + 1) * gw)
            cq = cm_ref[bi, :, gs]
            bk = bm_ref[bi, :, gs]
            scores = _dot_nt(cq, bk)
            xv_g = xv[:, xs_g]
            acc = None
            for hh in range(heads_per_g):
                lh = M2_H * d + heads_per_g * g + hh
                diff = b8[:, lh:lh + 1] - b8_t[lh:lh + 1, :]
                dec = jnp.where(mask, jnp.exp(jnp.where(mask, diff, 0.0)), 0.0)
                term = _dot(scores * dec, jnp.where(_idiv(lane, M2_P) == hh, xv_g, 0.0))
                acc = term if acc is None else acc + term
            cqs.append(cq)
            bks.append(bk)
            xvs.append(xv_g)
            b_es.append(b_e[:, xs_g])
            eb_es.append(eb_e[:, xs_g])
            accs.append(acc)

    sts = [st_ref[bi, g] for bi, g in units]
    o_parts = [[None] * n_chunk for _ in units]
    for c in _chunk_order(rev):
        rows = slice(c * CHUNK, (c + 1) * CHUNK)
        last = c * CHUNK if rev else (c + 1) * CHUNK - 1
        for i in range(len(units)):
            b_c = b_es[i][rows]
            b_last = b_es[i][last:last + 1]
            o_parts[i][c] = eb_es[i][rows] * _dot(cqs[i][rows], sts[i])
            ds = _dot_tn(bks[i][rows], xvs[i][rows] * jnp.exp(b_last - b_c))
            sts[i] = jnp.exp(b_last) * sts[i] + ds
    o_units = []
    for i, (bi, g) in enumerate(units):
        st_ref[bi, g] = sts[i]
        o_units.append(accs[i] + jnp.concatenate(o_parts[i], axis=0))
    for bi in range(nbat):
        o = jnp.concatenate(o_units[bi * M2_G:(bi + 1) * M2_G], axis=1)
        if rev:
            y = (of_ref[bi] + o + dsk_ref[...] * xs_ref[bi]) * _silu(z_ref[bi])
            y = y * lax.rsqrt(jnp.mean(y * y, axis=-1, keepdims=True) + RMS_EPS) * ng_ref[...]
            y_ref[bi] = y.astype(y_ref.dtype)
        else:
            o_ref[bi] = o


def _ssd_dir(p, xbc, a_log, dt_bias, d_skip, ng, o_f, *, rev, nblk):
    nbat, s1, _ = p.shape
    blk = _scan_block(rev, nblk)
    d = 1 if rev else 0
    lane0 = M2_H * d
    alog_row = jnp.zeros((1, LANE), F32).at[0, lane0:lane0 + M2_H].set(a_log[d])
    dtb_row = jnp.zeros((1, LANE), F32).at[0, lane0:lane0 + M2_H].set(dt_bias[d])
    ex = np.zeros((LANE, BRANCH_W), np.float32)
    for h in range(M2_H):
        ex[lane0 + h, h * M2_P:(h + 1) * M2_P] = 1.0
    dsk_row = jnp.repeat(d_skip, M2_P).reshape(1, BRANCH_W)
    in_specs = [_blk_spec(nbat, 512, 0, blk), _blk_spec(nbat, 256, 512, blk), _blk_spec(nbat, 256, 768, blk),
                _blk_spec(nbat, LANE, P_M2_DT, blk), _blk_spec(nbat, 512, P_M2_Z, blk),
                _const_spec((1, LANE)), _const_spec((1, LANE)), _const_spec((LANE, BRANCH_W)),
                _const_spec((TOK_BLK, TOK_BLK)), _const_spec((1, BRANCH_W)), _const_spec((1, BRANCH_W))]
    args = [xbc, xbc, xbc, p, p, alog_row, dtb_row, jnp.asarray(ex), _cumsum_matrix(rev), dsk_row,
            ng.reshape(1, -1)]
    return _scan_call(_ssd_kernel, "ssd", in_specs, args, o_f, rev=rev, nbat=nbat, s1=s1, nblk=nblk,
                      state_shape=(M2_G, M2_N, (M2_H // M2_G) * M2_P))


def _final_norm_kernel(x_ref, g_ref, o_ref):
    x = x_ref[...]
    o_ref[...] = x * lax.rsqrt(jnp.mean(x * x, axis=-1, keepdims=True) + RMS_EPS) * g_ref[...]


def _final_norm(xs, g, *, nbat, nblk, lc):
    D = xs.shape[1]
    cb = lc // TOK_BLK
    nlat = nblk - cb
    out = pl.pallas_call(
        _final_norm_kernel,
        grid=(nbat, nlat),
        in_specs=[pl.BlockSpec((TOK_BLK, D), lambda b, s: (b * nblk + cb + s, 0)),
                  pl.BlockSpec((1, D), lambda b, s: (0, 0))],
        out_specs=pl.BlockSpec((TOK_BLK, D), lambda b, s: (b * nlat + s, 0)),
        out_shape=jax.ShapeDtypeStruct((nbat * nlat * TOK_BLK, D), F32),
        compiler_params=_cparams(("arbitrary", "arbitrary")),
        name="final_norm",
    )(xs, g.reshape(1, D))
    return out.reshape(nbat, nlat * TOK_BLK, D)


def _row_tile(s1, limit):
    for tm in range(limit - limit % 16, 0, -16):
        if s1 % tm == 0:
            return tm
    raise ValueError(f"no row tile for sequence length {s1}")


def kernel(x, c, ctx, c_ctx, norm1_g, norm2_g, w_ada, b_ada, w_in, b_merge, gla_a2, gla_ab, gla_norm_g, na_rpb, gdn_conv, gdn_a_log, gdn_dt_bias, gdn_norm_g, m2_conv, m2_conv_b, m2_a_log, m2_dt_bias, m2_d, m2_norm_g, w_branch, w_out, w_ffn1, w_ffn3, w_ffn2, final_norm_g):
    nbat, t, D = x.shape
    lc = ctx.shape[1]
    depth = w_in.shape[0]
    assert D == D_MODEL and lc == TOK_BLK and t % TOK_BLK == 0 and t // GRID_W >= 3 * ROWS_PER_BLK
    assert nbat + 1 <= MOD_ROWS
    s1 = lc + t
    nblk = s1 // TOK_BLK
    tm = _row_tile(s1, 1056)
    geo = dict(s1=s1, lc=lc, nb=nbat)

    xs = jnp.concatenate([ctx, x], axis=1).reshape(nbat * s1, D)
    cvec = jnp.concatenate([c, c_ctx[None], jnp.zeros((MOD_ROWS - nbat - 1, D), F32)], axis=0)
    mods = _ada(cvec, w_ada, b_ada).reshape(depth, MOD_ROWS, 6, D).transpose(0, 2, 1, 3)
    cos, sin = _rope_tables(lc, t)
    flat = lambda y: y.reshape(nbat * s1, BRANCH_W)

    for l in range(depth):
        sh1, sc1, g1, sh2, sc2, g2 = (mods[l, i] for i in range(6))
        w_in_l = w_in[l]
        w_mix = _mix_weight(w_in_l)
        w_gate = w_in_l[:, MIX_IN:].astype(BF16)

        h = _normmod(xs, norm1_g[l], sc1, sh1, tm=tm, **geo)
        p = _matmul(h, w_mix, tm=tm, tn=1024, out_dtype=F32).reshape(nbat, s1, P_COLS)

        o_f = _gla_dir(p, cos, sin, gla_a2[l], gla_ab[l], gla_norm_g[l], None, rev=False, nblk=nblk)
        ya = _gla_dir(p, cos, sin, gla_a2[l], gla_ab[l], gla_norm_g[l], o_f, rev=True, nblk=nblk)

        yb = _na(p, _na_bias_tiles(na_rpb[l]), nblk=nblk)

        qkv = _conv_silu(p, P_GDN_QKV, gdn_conv[l], jnp.zeros((3 * BRANCH_W,), F32), nblk=nblk)
        o_f = _gdn_dir(p, qkv, gdn_a_log[l], gdn_dt_bias[l], gdn_norm_g[l], None, rev=False, nblk=nblk)
        yc = _gdn_dir(p, qkv, gdn_a_log[l], gdn_dt_bias[l], gdn_norm_g[l], o_f, rev=True, nblk=nblk)

        xbc = _conv_silu(p, P_M2_XBC, m2_conv[l], m2_conv_b[l], nblk=nblk)
        o_f = _ssd_dir(p, xbc, m2_a_log[l], m2_dt_bias[l], m2_d[l], m2_norm_g[l], None, rev=False, nblk=nblk)
        yd = _ssd_dir(p, xbc, m2_a_log[l], m2_dt_bias[l], m2_d[l], m2_norm_g[l], o_f, rev=True, nblk=nblk)

        ys = (flat(ya), flat(yb), flat(yc), flat(yd))
        merged = _merge(h, w_gate, b_merge[l], ys, w_branch[l].astype(BF16), tm=tm, tn=512)
        xs, h2 = _outproj_norm(merged, w_out[l].astype(BF16), xs, g1, norm2_g[l], sc2, sh2,
                               tm=_row_tile(s1, 384), **geo)

        u = _ffn_up(h2, w_ffn1[l].astype(BF16), w_ffn3[l].astype(BF16), tm=_row_tile(s1, 2112), tn=512)
        xs = _matmul_residual(u, w_ffn2[l].astype(BF16), xs, g2, tm=tm, tn=512, name="ffn_down", **geo)

    return _final_norm(xs, final_norm_g, nbat=nbat, nblk=nblk, lc=lc)
```

```python
import functools

import numpy as np
import jax
import jax.numpy as jnp
from jax import lax
from jax.experimental import pallas as pl
from jax.experimental.pallas import tpu as pltpu

D_MODEL = 2048
GRID_W = 64
N_BRANCH = 4
BRANCH_W = D_MODEL // 4
CHUNK = 64
CONV_W = 5
RMS_EPS = 1e-6
NEG_INF = -1e30
ROPE_BASE = 10000.0
GLA_H = 4
GLA_DV = BRANCH_W // GLA_H
GLA_DK = GLA_DV // 2
GLA_LR = 16
GLA_TAU = 16.0
NA_H = 4
NA_D = BRANCH_W // NA_H
NA_WIN_R = 8
NA_WIN_C = 16
GDN_H = 4
GDN_D = BRANCH_W // GDN_H
M2_P = 64
M2_H = BRANCH_W // M2_P
M2_N = 128
M2_G = 2
M2_CONV_CH = BRANCH_W + 2 * M2_G * M2_N
D_FF = ((8 * D_MODEL + 3 * 256 - 1) // (3 * 256)) * 256
GLA_IN = 2 * GLA_H * GLA_DK + 2 * BRANCH_W + 2 * GLA_LR
NA_IN = 3 * BRANCH_W
GDN_IN = 4 * BRANCH_W + 4 * GDN_H
M2_IN = BRANCH_W + M2_CONV_CH + 2 * M2_H
MIX_IN = GLA_IN + NA_IN + GDN_IN + M2_IN

F32 = jnp.float32
BF16 = jnp.bfloat16

LANE = 128
SUBLANE = 8
V7X_VMEM_BYTES = 64 * 1024 * 1024
VMEM_LIMIT = V7X_VMEM_BYTES - 8 * 1024 * 1024

TOK_BLK = 4 * CHUNK
ROWS_PER_BLK = TOK_BLK // GRID_W
HALO = SUBLANE
MOD_ROWS = 8

P_GDN_QKV = 0
P_GLA_V = 1536
P_M2_XBC = 2048
P_GLA_G = 3072
P_NA_Q, P_NA_K, P_NA_V = 3584, 4096, 4608
P_GDN_Z, P_M2_Z = 5120, 5632
P_GLA_Q, P_GLA_K = 6144, 6400
P_GLA_LR, P_GDN_SC, P_M2_DT = 6656, 6784, 6912
P_COLS = 7168


def _mix_weight(w_in_l):
    gla, na, gdn, m2 = 0, GLA_IN, GLA_IN + NA_IN, GLA_IN + NA_IN + GDN_IN
    fields = sorted([
        (P_GLA_V, gla + 512, 512), (P_GLA_G, gla + 1024, 512),
        (P_NA_Q, na, 512), (P_NA_K, na + 512, 512), (P_NA_V, na + 1024, 512),
        (P_GDN_QKV, gdn, 1536), (P_GDN_Z, gdn + 1536, 512),
        (P_M2_Z, m2, 512), (P_M2_XBC, m2 + 512, M2_CONV_CH),
        (P_GLA_Q, gla, 256), (P_GLA_K, gla + 256, 256),
        (P_GLA_LR, gla + 1536, 2 * GLA_LR), (P_GDN_SC, gdn + 2048, 4 * GDN_H),
        (P_M2_DT, m2 + 512 + M2_CONV_CH, 2 * M2_H)])
    d = w_in_l.shape[0]
    parts, col = [], 0
    for dst, src, n in fields:
        if dst > col:
            parts.append(jnp.zeros((d, dst - col), BF16))
        parts.append(w_in_l[:, src:src + n].astype(BF16))
        col = dst + n
    parts.append(jnp.zeros((d, P_COLS - col), BF16))
    return jnp.concatenate(parts, axis=1)


def _cparams(sem):
    return pltpu.CompilerParams(dimension_semantics=sem, vmem_limit_bytes=VMEM_LIMIT)


def _sigmoid(x):
    return 1.0 / (1.0 + jnp.exp(-x))


def _silu(x):
    return x * _sigmoid(x)


def _softplus(x):
    return jnp.maximum(x, 0.0) + jnp.log1p(jnp.exp(-jnp.abs(x)))


def _dot(a, b):
    return jnp.dot(a.astype(BF16), b.astype(BF16), preferred_element_type=F32)


def _dot_nt(a, b):
    return lax.dot_general(a.astype(BF16), b.astype(BF16), (((1,), (1,)), ((), ())),
                           preferred_element_type=F32)


def _dot_tn(a, b):
    return lax.dot_general(a.astype(BF16), b.astype(BF16), (((0,), (0,)), ((), ())),
                           preferred_element_type=F32)


def _split3(a):
    hi = a.astype(BF16)
    r1 = a - hi.astype(F32)
    mid = r1.astype(BF16)
    return hi, mid, (r1 - mid.astype(F32)).astype(BF16)


def _select_rows(m01, a):
    d = lambda y: jnp.dot(m01, y, preferred_element_type=F32)
    hi, mid, lo = _split3(a)
    return d(hi) + (d(mid) + d(lo))


def _select_cols(a, m01):
    d = lambda y: jnp.dot(y, m01, preferred_element_type=F32)
    hi, mid, lo = _split3(a)
    return d(hi) + (d(mid) + d(lo))


def _split(a):
    hi = a.astype(BF16)
    return hi, (a - hi.astype(F32)).astype(BF16)


def _dot_split(a, b):
    (ah, al), (bh, bl) = a, b
    d = lambda x, y: jnp.dot(x, y, preferred_element_type=F32)
    return d(ah, bh) + (d(ah, bl) + d(al, bh))


def _iota2(shape, dim):
    return lax.broadcasted_iota(jnp.int32, shape, dim)


def _idiv(x, n):
    assert n & (n - 1) == 0
    return lax.shift_right_logical(x, jnp.int32(n.bit_length() - 1))


def _imod(x, n):
    assert n & (n - 1) == 0
    return x & (n - 1)


def _chunk_mask(rev, strict=False):
    r = _iota2((TOK_BLK, TOK_BLK), 0)
    c = _iota2((TOK_BLK, TOK_BLK), 1)
    same = _idiv(r, CHUNK) == _idiv(c, CHUNK)
    if rev:
        tri = (c > r) if strict else (c >= r)
    else:
        tri = (c < r) if strict else (c <= r)
    return same & tri


def _chunk_order(rev):
    n = TOK_BLK // CHUNK
    return list(range(n - 1, -1, -1)) if rev else list(range(n))


def _row_split(i, tm, s1):
    r0 = i * tm
    return lax.div(r0, jnp.int32(s1)), lax.rem(r0, jnp.int32(s1))


def _pick_mod(ref, b, nb, is_ctx):
    return jnp.where(is_ctx, ref[nb:nb + 1, :], ref[pl.ds(b, 1), :])


def _ada_kernel(c_ref, w_ref, b_ref, o_ref):
    a = _silu(c_ref[...])
    o_ref[...] = _dot(a, w_ref[...]) + b_ref[...]


def _ada(cvec, w_ada, b_ada):
    L, D, N = w_ada.shape
    tn = 1024
    return pl.pallas_call(
        _ada_kernel,
        grid=(L, N // tn),
        in_specs=[pl.BlockSpec((MOD_ROWS, D), lambda l, j: (0, 0)),
                  pl.BlockSpec((None, D, tn), lambda l, j: (l, 0, j)),
                  pl.BlockSpec((None, 1, tn), lambda l, j: (l, 0, j))],
        out_specs=pl.BlockSpec((None, MOD_ROWS, tn), lambda l, j: (l, 0, j)),
        out_shape=jax.ShapeDtypeStruct((L, MOD_ROWS, N), F32),
        compiler_params=_cparams(("arbitrary", "arbitrary")),
        name="ada_mod",
    )(cvec, w_ada, b_ada.reshape(L, 1, N))


def _normmod_kernel(x_ref, g_ref, sc_ref, sh_ref, o_ref, *, tm, s1, lc, nb):
    b, pos0 = _row_split(pl.program_id(0), tm, s1)
    is_ctx = (pos0 + _iota2((tm, 1), 0)) < lc
    x = x_ref[...]
    y = x * lax.rsqrt(jnp.mean(x * x, axis=-1, keepdims=True) + RMS_EPS) * g_ref[...]
    sc = _pick_mod(sc_ref, b, nb, is_ctx)
    sh = _pick_mod(sh_ref, b, nb, is_ctx)
    o_ref[...] = (y * (1.0 + sc) + sh).astype(o_ref.dtype)


def _normmod(xs, g, sc, sh, *, tm, s1, lc, nb):
    R, D = xs.shape
    return pl.pallas_call(
        functools.partial(_normmod_kernel, tm=tm, s1=s1, lc=lc, nb=nb),
        grid=(R // tm,),
        in_specs=[pl.BlockSpec((tm, D), lambda i: (i, 0)),
                  pl.BlockSpec((1, D), lambda i: (0, 0)),
                  pl.BlockSpec((MOD_ROWS, D), lambda i: (0, 0)),
                  pl.BlockSpec((MOD_ROWS, D), lambda i: (0, 0))],
        out_specs=pl.BlockSpec((tm, D), lambda i: (i, 0)),
        out_shape=jax.ShapeDtypeStruct((R, D), BF16),
        compiler_params=_cparams(("arbitrary",)),
        name="norm_mod",
    )(xs, g.reshape(1, D), sc, sh)


def _mm_kernel(a_ref, w_ref, o_ref):
    o_ref[...] = jnp.dot(a_ref[...], w_ref[...], preferred_element_type=F32).astype(o_ref.dtype)


def _matmul(a, w, *, tm, tn, out_dtype):
    R, K = a.shape
    N = w.shape[1]
    assert R % tm == 0 and N % tn == 0
    return pl.pallas_call(
        _mm_kernel,
        grid=(R // tm, N // tn),
        in_specs=[pl.BlockSpec((tm, K), lambda i, j: (i, 0)),
                  pl.BlockSpec((K, tn), lambda i, j: (0, j))],
        out_specs=pl.BlockSpec((tm, tn), lambda i, j: (i, j)),
        out_shape=jax.ShapeDtypeStruct((R, N), out_dtype),
        compiler_params=_cparams(("arbitrary", "arbitrary")),
        name="mix_proj",
    )(a, w)


def _mm_res_kernel(a_ref, w_ref, x_ref, gt_ref, o_ref, *, tm, s1, lc, nb):
    b, pos0 = _row_split(pl.program_id(0), tm, s1)
    is_ctx = (pos0 + _iota2((tm, 1), 0)) < lc
    acc = jnp.dot(a_ref[...], w_ref[...], preferred_element_type=F32)
    o_ref[...] = x_ref[...] + _pick_mod(gt_ref, b, nb, is_ctx) * acc


def _matmul_residual(a, w, xs, gate, *, tm, tn, s1, lc, nb, name):
    R, K = a.shape
    N = w.shape[1]
    assert R % tm == 0 and N % tn == 0
    return pl.pallas_call(
        functools.partial(_mm_res_kernel, tm=tm, s1=s1, lc=lc, nb=nb),
        grid=(R // tm, N // tn),
        in_specs=[pl.BlockSpec((tm, K), lambda i, j: (i, 0)),
                  pl.BlockSpec((K, tn), lambda i, j: (0, j)),
                  pl.BlockSpec((tm, tn), lambda i, j: (i, j)),
                  pl.BlockSpec((MOD_ROWS, tn), lambda i, j: (0, j))],
        out_specs=pl.BlockSpec((tm, tn), lambda i, j: (i, j)),
        out_shape=jax.ShapeDtypeStruct((R, N), F32),
        compiler_params=_cparams(("arbitrary", "arbitrary")),
        name=name,
    )(a, w, xs, gate)


def _outproj_norm_kernel(a_ref, w_ref, x_ref, gt_ref, g_ref, sc_ref, sh_ref, xo_ref, h_ref, *, tm, s1, lc, nb):
    b, pos0 = _row_split(pl.program_id(0), tm, s1)
    is_ctx = (pos0 + _iota2((tm, 1), 0)) < lc
    acc = jnp.dot(a_ref[...], w_ref[...], preferred_element_type=F32)
    x = x_ref[...] + _pick_mod(gt_ref, b, nb, is_ctx) * acc
    xo_ref[...] = x
    y = x * lax.rsqrt(jnp.mean(x * x, axis=-1, keepdims=True) + RMS_EPS) * g_ref[...]
    h_ref[...] = (y * (1.0 + _pick_mod(sc_ref, b, nb, is_ctx)) + _pick_mod(sh_ref, b, nb, is_ctx)).astype(h_ref.dtype)


def _outproj_norm(a, w, xs, gate, g, sc, sh, *, tm, s1, lc, nb):
    R, K = a.shape
    D = w.shape[1]
    row = lambda i: (i, 0)
    const = lambda i: (0, 0)
    return pl.pallas_call(
        functools.partial(_outproj_norm_kernel, tm=tm, s1=s1, lc=lc, nb=nb),
        grid=(R // tm,),
        in_specs=[pl.BlockSpec((tm, K), row), pl.BlockSpec((K, D), const), pl.BlockSpec((tm, D), row),
                  pl.BlockSpec((MOD_ROWS, D), const), pl.BlockSpec((1, D), const),
                  pl.BlockSpec((MOD_ROWS, D), const), pl.BlockSpec((MOD_ROWS, D), const)],
        out_specs=[pl.BlockSpec((tm, D), row), pl.BlockSpec((tm, D), row)],
        out_shape=[jax.ShapeDtypeStruct((R, D), F32), jax.ShapeDtypeStruct((R, D), BF16)],
        compiler_params=_cparams(("arbitrary",)),
        name="out_proj",
    )(a, w, xs, gate, g.reshape(1, D), sc, sh)


def _ffn_up_kernel(a_ref, w1_ref, w3_ref, o_ref):
    a = a_ref[...]
    u = jnp.dot(a, w1_ref[...], preferred_element_type=F32)
    v = jnp.dot(a, w3_ref[...], preferred_element_type=F32)
    o_ref[...] = (_silu(u) * v).astype(o_ref.dtype)


def _ffn_up(a, w1, w3, *, tm, tn):
    R, K = a.shape
    N = w1.shape[1]
    assert R % tm == 0 and N % tn == 0
    return pl.pallas_call(
        _ffn_up_kernel,
        grid=(R // tm, N // tn),
        in_specs=[pl.BlockSpec((tm, K), lambda i, j: (i, 0)),
                  pl.BlockSpec((K, tn), lambda i, j: (0, j)),
                  pl.BlockSpec((K, tn), lambda i, j: (0, j))],
        out_specs=pl.BlockSpec((tm, tn), lambda i, j: (i, j)),
        out_shape=jax.ShapeDtypeStruct((R, N), BF16),
        compiler_params=_cparams(("arbitrary", "arbitrary")),
        name="ffn_up",
    )(a, w1, w3)


def _merge_kernel(h_ref, g0_ref, g1_ref, g2_ref, g3_ref, bm_ref, ya_ref, yb_ref, yc_ref, yd_ref, wb_ref, o_ref):
    h = h_ref[...]
    acc = None
    branches = zip((g0_ref, g1_ref, g2_ref, g3_ref), (ya_ref, yb_ref, yc_ref, yd_ref))
    for i, (wg_ref, y_ref) in enumerate(branches):
        gate = _sigmoid(jnp.dot(h, wg_ref[...], preferred_element_type=F32) + bm_ref[i])
        term = gate * jnp.dot(y_ref[...], wb_ref[i], preferred_element_type=F32)
        acc = term if acc is None else acc + term
    o_ref[...] = acc.astype(o_ref.dtype)


def _merge(h, wg, b_merge, ys, wb, *, tm, tn):
    R, D = h.shape
    N = wb.shape[2]
    assert R % tm == 0 and N % tn == 0
    nj = N // tn
    yspec = pl.BlockSpec((tm, BRANCH_W), lambda i, j: (i, 0))
    gspec = lambda br: pl.BlockSpec((D, tn), lambda i, j: (0, br * nj + j))
    return pl.pallas_call(
        _merge_kernel,
        grid=(R // tm, nj),
        in_specs=[pl.BlockSpec((tm, D), lambda i, j: (i, 0)),
                  gspec(0), gspec(1), gspec(2), gspec(3),
                  pl.BlockSpec((N_BRANCH, 1, tn), lambda i, j: (0, 0, j)),
                  yspec, yspec, yspec, yspec,
                  pl.BlockSpec((N_BRANCH, BRANCH_W, tn), lambda i, j: (0, 0, j))],
        out_specs=pl.BlockSpec((tm, tn), lambda i, j: (i, j)),
        out_shape=jax.ShapeDtypeStruct((R, N), BF16),
        compiler_params=_cparams(("arbitrary", "arbitrary")),
        name="merge",
    )(h, wg, wg, wg, wg, b_merge.reshape(N_BRANCH, 1, N), *ys, wb)


def _conv_kernel(xm_ref, xp_ref, xn_ref, w_ref, b_ref, o_ref, *, nblk, nbat):
    blk = pl.program_id(0)
    prev_ok = blk >= 2
    next_ok = (blk >= 1) & (blk < nblk - 1)
    w = w_ref[...]
    rows = TOK_BLK + 2 * HALO
    half = CONV_W // 2
    for bi in range(nbat):
        xp = jnp.where(prev_ok, xp_ref[bi], 0.0)
        xn = jnp.where(next_ok, xn_ref[bi], 0.0)
        xpad = jnp.concatenate([xp, xm_ref[bi], xn], axis=0)
        acc = b_ref[...] + w[half:half + 1, :] * xm_ref[bi]
        for j in range(CONV_W):
            if j != half:
                tap = pltpu.roll(xpad, (half - j) % rows, 0)[HALO:HALO + TOK_BLK]
                acc = acc + w[j:j + 1, :] * tap
        o_ref[bi] = _silu(acc)


def _conv_silu(p, col0, w, bias, *, nblk):
    nbat, s1, _ = p.shape
    C = w.shape[1]
    assert col0 % C == 0
    cb = col0 // C
    hb = TOK_BLK // HALO
    nhalo = s1 // HALO
    wpad = jnp.concatenate([w, jnp.zeros((SUBLANE - CONV_W, C), F32)], axis=0)
    return pl.pallas_call(
        functools.partial(_conv_kernel, nblk=nblk, nbat=nbat),
        grid=(nblk,),
        in_specs=[pl.BlockSpec((nbat, TOK_BLK, C), lambda r: (0, r, cb)),
                  pl.BlockSpec((nbat, HALO, C), lambda r: (0, jnp.maximum(r * hb - 1, 0), cb)),
                  pl.BlockSpec((nbat, HALO, C), lambda r: (0, jnp.minimum((r + 1) * hb, nhalo - 1), cb)),
                  pl.BlockSpec((SUBLANE, C), lambda r: (0, 0)),
                  pl.BlockSpec((1, C), lambda r: (0, 0))],
        out_specs=pl.BlockSpec((nbat, TOK_BLK, C), lambda r: (0, r, 0)),
        out_shape=jax.ShapeDtypeStruct((nbat, s1, C), F32),
        compiler_params=_cparams(("arbitrary",)),
        name="short_conv",
    )(p, p, p, wpad, bias.reshape(1, C))


def _scan_block(rev, nblk):
    if rev:
        return lambda s: jnp.where(s == 0, 0, nblk - s)
    return lambda s: s


def _blk_spec(nbat, width, col, blk):
    assert col % width == 0
    return pl.BlockSpec((nbat, TOK_BLK, width), lambda s: (0, blk(s), col // width))


def _const_spec(shape):
    return pl.BlockSpec(shape, lambda s: (0,) * len(shape))


def _head_rmsnorm_gate(o, g, gate, n_head, width):
    outs = []
    for h in range(n_head):
        oh = o[:, h * width:(h + 1) * width]
        yh = oh * lax.rsqrt(jnp.mean(oh * oh, axis=-1, keepdims=True) + RMS_EPS) * g
        outs.append(yh * _silu(gate[:, h * width:(h + 1) * width]))
    return jnp.concatenate(outs, axis=1)


def _scan_call(kern, name, in_specs, args, o_f, *, rev, nbat, s1, nblk, state_shape):
    blk = _scan_block(rev, nblk)
    ospec = pl.BlockSpec((nbat, TOK_BLK, BRANCH_W), lambda s: (0, blk(s), 0))
    if rev:
        in_specs = in_specs + [ospec]
        args = args + [o_f]
    return pl.pallas_call(
        functools.partial(kern, rev=rev, nbat=nbat),
        grid=(nblk,),
        in_specs=in_specs,
        out_specs=ospec,
        out_shape=jax.ShapeDtypeStruct((nbat, s1, BRANCH_W), BF16 if rev else F32),
        scratch_shapes=[pltpu.VMEM((nbat,) + state_shape, F32)],
        compiler_params=_cparams(("arbitrary",)),
        name=name + ("_bwd" if rev else "_fwd"),
    )(*args)


def _gla_kernel(q_ref, k_ref, v_ref, g_ref, lr_ref, cos_ref, sin_ref, a2_ref, ab_ref, lm_ref, ng_ref,
                *rest, rev, nbat):
    if rev:
        of_ref, y_ref, st_ref = rest
    else:
        o_ref, st_ref = rest

    @pl.when(pl.program_id(0) == 0)
    def _():
        st_ref[...] = jnp.zeros_like(st_ref)

    hk = GLA_H * GLA_DK
    lane = _iota2((TOK_BLK, hk), 1)
    first_half = _imod(lane, GLA_DK) < (GLA_DK // 2)
    head_of_lane = _idiv(lane, GLA_DK)
    cos = cos_ref[...]
    sin = sin_ref[...]
    mask = _chunk_mask(rev)
    n_chunk = TOK_BLK // CHUNK
    batch = range(nbat)

    def rope(x):
        partner = jnp.where(first_half, pltpu.roll(x, hk - GLA_DK // 2, 1), pltpu.roll(x, GLA_DK // 2, 1))
        return x * cos + partner * sin

    ks, vs, bs, q_ins, o_intras = [], [], [], [], []
    for bi in batch:
        q = rope(q_ref[bi]) * GLA_DK ** -0.5
        k = rope(k_ref[bi])
        v = v_ref[bi]
        loga = -_softplus(-(_dot(lr_ref[bi], a2_ref[...]) + ab_ref[...])) / GLA_TAU
        b = _select_rows(lm_ref[...], loga)
        q_in = q * jnp.exp(b)
        k_in = k * jnp.exp(-b)
        o_heads = []
        for h in range(GLA_H):
            att = _dot_nt(jnp.where(head_of_lane == h, q_in, 0.0), k_in)
            att = jnp.where(mask, att, 0.0)
            o_heads.append(_dot(att, v[:, h * GLA_DV:(h + 1) * GLA_DV]))
        ks.append(k)
        vs.append(v)
        bs.append(b)
        q_ins.append(q_in)
        o_intras.append(jnp.concatenate(o_heads, axis=1))

    sts = [st_ref[bi] for bi in batch]
    diag = _idiv(_iota2(sts[0].shape, 0), GLA_DV) == _idiv(_iota2(sts[0].shape, 1), GLA_DK)
    o_inter = [[None] * n_chunk for _ in batch]
    for c in _chunk_order(rev):
        rows = slice(c * CHUNK, (c + 1) * CHUNK)
        for bi in batch:
            b_c = bs[bi][rows]
            b_last = b_c[0:1] if rev else b_c[CHUNK - 1:CHUNK]
            o_inter[bi][c] = _dot_nt(q_ins[bi][rows], sts[bi])
            k_end = ks[bi][rows] * jnp.exp(b_last - b_c)
            ds = _dot_tn(vs[bi][rows], k_end)
            sts[bi] = sts[bi] * jnp.exp(b_last) + jnp.where(diag, ds, 0.0)
    for bi in batch:
        st_ref[bi] = sts[bi]
        o = o_intras[bi] + jnp.concatenate(o_inter[bi], axis=0)
        if rev:
            y = _head_rmsnorm_gate(of_ref[bi] + o, ng_ref[...], g_ref[bi], GLA_H, GLA_DV)
            y_ref[bi] = y.astype(y_ref.dtype)
        else:
            o_ref[bi] = o


def _cumsum_matrix(rev):
    r = np.arange(TOK_BLK)[:, None]
    c = np.arange(TOK_BLK)[None, :]
    same = (r // CHUNK) == (c // CHUNK)
    tri = (c >= r) if rev else (c <= r)
    return jnp.asarray((same & tri).astype(np.float32), dtype=BF16)


def _gla_dir(p, cos, sin, a2, ab, ng, o_f, *, rev, nblk):
    nbat, s1, _ = p.shape
    blk = _scan_block(rev, nblk)
    d = 1 if rev else 0
    a2d = jnp.zeros((LANE, GLA_H * GLA_DK), F32).at[d * GLA_LR:(d + 1) * GLA_LR].set(a2[d])
    tab = pl.BlockSpec((TOK_BLK, GLA_H * GLA_DK), lambda s: (blk(s), 0))
    in_specs = [_blk_spec(nbat, 256, P_GLA_Q, blk), _blk_spec(nbat, 256, P_GLA_K, blk),
                _blk_spec(nbat, 512, P_GLA_V, blk), _blk_spec(nbat, 512, P_GLA_G, blk),
                _blk_spec(nbat, LANE, P_GLA_LR, blk), tab, tab,
                _const_spec((LANE, 256)), _const_spec((1, 256)), _const_spec((TOK_BLK, TOK_BLK)),
                _const_spec((1, GLA_DV))]
    args = [p, p, p, p, p, cos, sin, a2d, ab[d].reshape(1, -1), _cumsum_matrix(rev), ng.reshape(1, -1)]
    return _scan_call(_gla_kernel, "gla", in_specs, args, o_f, rev=rev, nbat=nbat, s1=s1, nblk=nblk,
                      state_shape=(GLA_H * GLA_DV, GLA_H * GLA_DK))


def _rope_tables(lc, t):
    n_freq = GLA_DK // 4
    freqs = ROPE_BASE ** (-jnp.arange(n_freq, dtype=F32) / n_freq)
    tt = jnp.arange(t)
    row = (tt // GRID_W).astype(F32)
    col = (tt % GRID_W).astype(F32)
    ang = jnp.concatenate([row[:, None] * freqs, col[:, None] * freqs], axis=-1)
    cos, sin = jnp.cos(ang), jnp.sin(ang)
    cos = jnp.concatenate([jnp.ones((lc, GLA_DK // 2), F32), cos], axis=0)
    sin = jnp.concatenate([jnp.zeros((lc, GLA_DK // 2), F32), sin], axis=0)
    cos_h = jnp.concatenate([cos, cos], axis=1)
    sin_h = jnp.concatenate([-sin, sin], axis=1)
    return jnp.tile(cos_h, (1, GLA_H)), jnp.tile(sin_h, (1, GLA_H))


def _na_kernel(q_ref, kp_ref, kc_ref, kn_ref, kx_ref, vp_ref, vc_ref, vn_ref, vx_ref, bias_ref, y_ref, *, nbat):
    scale = NA_D ** -0.5
    for bi in range(nbat):
        outs = []
        for h in range(NA_H):
            hs = slice(h * NA_D, (h + 1) * NA_D)
            qh = (q_ref[bi, :, hs] * scale).astype(BF16)
            s = jnp.concatenate(
                [_dot_nt(qh, kp_ref[bi, :, hs]) + bias_ref[0, h],
                 _dot_nt(qh, kc_ref[bi, :, hs]) + bias_ref[1, h],
                 _dot_nt(qh, kn_ref[bi, :, hs]) + bias_ref[2, h],
                 _dot_nt(qh, kx_ref[bi, :, hs])], axis=1)
            m = jnp.max(s, axis=-1, keepdims=True)
            e = jnp.exp(s - m)
            p = e / jnp.sum(e, axis=-1, keepdims=True)
            o = (_dot(p[:, 0:TOK_BLK], vp_ref[bi, :, hs]) + _dot(p[:, TOK_BLK:2 * TOK_BLK], vc_ref[bi, :, hs])
                 + _dot(p[:, 2 * TOK_BLK:3 * TOK_BLK], vn_ref[bi, :, hs])
                 + _dot(p[:, 3 * TOK_BLK:], vx_ref[bi, :, hs]))
            outs.append(o)
        y_ref[bi] = jnp.concatenate(outs, axis=1).astype(y_ref.dtype)


def _na_bias_tiles(rpb):
    edge = GRID_W - NA_WIN_C
    ext = jnp.concatenate([jnp.repeat(rpb[..., :1], edge, axis=-1), rpb,
                           jnp.repeat(rpb[..., -1:], edge, axis=-1)], axis=-1)
    toep = jnp.stack([ext[..., GRID_W - 1 - qc:2 * GRID_W - 1 - qc] for qc in range(GRID_W)], axis=-2)
    qc = np.arange(GRID_W)[:, None]
    kc = np.arange(GRID_W)[None, :]
    c0 = np.clip(qc - NA_WIN_C // 2, 0, GRID_W - NA_WIN_C)
    col_ok = (kc >= c0) & (kc < c0 + NA_WIN_C)
    toep = jnp.where(jnp.asarray(col_ok), toep, NEG_INF).astype(F32)
    masked = jnp.full((NA_H, GRID_W, GRID_W), NEG_INF, F32)
    kinds = []
    for kind in range(3):
        offs = []
        for off in (-1, 0, 1):
            rows = []
            for qr in range(ROWS_PER_BLK):
                start = (0, qr - NA_WIN_R // 2, ROWS_PER_BLK - NA_WIN_R)[kind]
                cols = []
                for kb in range(ROWS_PER_BLK):
                    kr = kb + ROWS_PER_BLK * off
                    ok = start <= kr < start + NA_WIN_R
                    cols.append(toep[:, kr - qr + NA_WIN_R - 1] if ok else masked)
                rows.append(jnp.concatenate(cols, axis=-1))
            offs.append(jnp.concatenate(rows, axis=-2))
        kinds.append(jnp.stack(offs))
    kinds.append(jnp.full_like(kinds[0], NEG_INF))
    return jnp.stack(kinds)


def _na(p, bias, *, nblk):
    nbat, s1, _ = p.shape

    def kind(s):
        return jnp.where(s == 0, 3, jnp.where(s == 1, 0, jnp.where(s == nblk - 1, 2, 1)))

    prev = lambda s: jnp.maximum(s - 1, 1)
    cur = lambda s: s
    nxt = lambda s: jnp.minimum(s + 1, nblk - 1)
    ctx = lambda s: 0
    spec = lambda col, blk: _blk_spec(nbat, BRANCH_W, col, blk)
    return pl.pallas_call(
        functools.partial(_na_kernel, nbat=nbat),
        grid=(nblk,),
        in_specs=[spec(P_NA_Q, cur),
                  spec(P_NA_K, prev), spec(P_NA_K, cur), spec(P_NA_K, nxt), spec(P_NA_K, ctx),
                  spec(P_NA_V, prev), spec(P_NA_V, cur), spec(P_NA_V, nxt), spec(P_NA_V, ctx),
                  pl.BlockSpec((None, 3, NA_H, TOK_BLK, TOK_BLK), lambda s: (kind(s), 0, 0, 0, 0))],
        out_specs=pl.BlockSpec((nbat, TOK_BLK, BRANCH_W), lambda s: (0, s, 0)),
        out_shape=jax.ShapeDtypeStruct((nbat, s1, BRANCH_W), BF16),
        compiler_params=_cparams(("arbitrary",)),
        name="nbr_attn",
    )(p, p, p, p, p, p, p, p, p, bias)


def _l2norm(x):
    return x * lax.rsqrt(jnp.sum(x * x, axis=-1, keepdims=True) + RMS_EPS)


def _gdn_kernel(q_ref, k_ref, v_ref, z_ref, sc_ref, alog_ref, dtb_ref, lm_ref, ng_ref, *rest, rev, nbat):
    if rev:
        of_ref, y_ref, st_ref = rest
    else:
        o_ref, st_ref = rest

    @pl.when(pl.program_id(0) == 0)
    def _():
        st_ref[...] = jnp.zeros_like(st_ref)

    d = 1 if rev else 0
    m_incl = _chunk_mask(rev)
    m_strict = _chunk_mask(rev, strict=True)
    eye = (_iota2((TOK_BLK, TOK_BLK), 0) == _iota2((TOK_BLK, TOK_BLK), 1)).astype(F32)
    n_chunk = TOK_BLK // CHUNK

    units = [(bi, h) for bi in range(nbat) for h in range(GDN_H)]
    qs, ks, bcols, ebcs, attns, nmats, rhss = [], [], [], [], [], [], []
    for bi in range(nbat):
        sc = sc_ref[bi]
        beta_all = _sigmoid(sc)
        g_all = -jnp.exp(alog_ref[...]) * _softplus(sc + dtb_ref[...])
        b_all = _select_rows(lm_ref[...], g_all)
        b_all_t = b_all.T
        eb_all = jnp.exp(b_all)
        for h in range(GDN_H):
            hs = slice(h * GDN_D, (h + 1) * GDN_D)
            qh = _l2norm(q_ref[bi, :, hs]) * GDN_D ** -0.5
            kh = _l2norm(k_ref[bi, :, hs])
            vh = v_ref[bi, :, hs]
            lb, lg = GDN_H * d + h, 2 * GDN_H + GDN_H * d + h
            beta = beta_all[:, lb:lb + 1]
            bcol = b_all[:, lg:lg + 1]
            brow = b_all_t[lg:lg + 1, :]
            ebc = eb_all[:, lg:lg + 1]
            diff = bcol - brow
            dec_incl = jnp.where(m_incl, jnp.exp(jnp.where(m_incl, diff, 0.0)), 0.0)
            dec_strict = jnp.where(m_strict, dec_incl, 0.0)
            kk = _dot_nt(kh, kh)
            qs.append(qh)
            ks.append(kh)
            bcols.append(bcol)
            ebcs.append(ebc)
            attns.append(_dot_nt(qh, kh) * dec_incl)
            nmats.append(-(beta * kk * dec_strict))
            rhss.append(_split(jnp.concatenate([kh * (beta * ebc), vh * beta], axis=1)))

    bdot = lambda x, y: jnp.dot(x, y, preferred_element_type=F32)
    nsplit = [_split(n) for n in nmats]
    t0s = [eye + n for n in nmats]
    powers = [ns[0] for ns in nsplit]
    for _ in range(5):
        powers = [bdot(m, m).astype(BF16) for m in powers]
        t0s = [t + bdot(t.astype(BF16), m) for t, m in zip(t0s, powers)]
    t0split = [_split(t) for t in t0s]
    resid = [(eye - t) + _dot_split(ns, ts) for t, ns, ts in zip(t0s, nsplit, t0split)]
    tinvs = [t + bdot(ts[0], r.astype(BF16)) for t, ts, r in zip(t0s, t0split, resid)]
    sols = [_dot_split(_split(t), r) for t, r in zip(tinvs, rhss)]
    ws = [s[:, :GDN_D] for s in sols]
    u0s = [s[:, GDN_D:] for s in sols]

    sts = [st_ref[bi, h] for bi, h in units]
    u_parts = [[None] * n_chunk for _ in units]
    o_parts = [[None] * n_chunk for _ in units]
    for c in _chunk_order(rev):
        rows = slice(c * CHUNK, (c + 1) * CHUNK)
        last = c * CHUNK if rev else (c + 1) * CHUNK - 1
        for i in range(len(units)):
            st = sts[i]
            b_last = bcols[i][last:last + 1]
            u = u0s[i][rows] - _dot(ws[i][rows], st)
            u_parts[i][c] = u
            o_parts[i][c] = ebcs[i][rows] * _dot(qs[i][rows], st)
            k_end = ks[i][rows] * jnp.exp(b_last - bcols[i][rows])
            sts[i] = jnp.exp(b_last) * st + _dot_tn(k_end, u)
    o_units = []
    for i, (bi, h) in enumerate(units):
        st_ref[bi, h] = sts[i]
        o_units.append(jnp.concatenate(o_parts[i], axis=0) + _dot(attns[i], jnp.concatenate(u_parts[i], axis=0)))
    for bi in range(nbat):
        o = jnp.concatenate(o_units[bi * GDN_H:(bi + 1) * GDN_H], axis=1)
        if rev:
            y = _head_rmsnorm_gate(of_ref[bi] + o, ng_ref[...], z_ref[bi], GDN_H, GDN_D)
            y_ref[bi] = y.astype(y_ref.dtype)
        else:
            o_ref[bi] = o


def _gdn_dir(p, qkv, a_log, dt_bias, ng, o_f, *, rev, nblk):
    nbat, s1, _ = p.shape
    blk = _scan_block(rev, nblk)
    d = 1 if rev else 0
    lane0 = 2 * GDN_H + GDN_H * d
    alog_row = jnp.zeros((1, LANE), F32).at[0, lane0:lane0 + GDN_H].set(a_log[d])
    dtb_row = jnp.zeros((1, LANE), F32).at[0, lane0:lane0 + GDN_H].set(dt_bias[d])
    in_specs = [_blk_spec(nbat, 512, 0, blk), _blk_spec(nbat, 512, 512, blk), _blk_spec(nbat, 512, 1024, blk),
                _blk_spec(nbat, 512, P_GDN_Z, blk), _blk_spec(nbat, LANE, P_GDN_SC, blk),
                _const_spec((1, LANE)), _const_spec((1, LANE)), _const_spec((TOK_BLK, TOK_BLK)),
                _const_spec((1, GDN_D))]
    args = [qkv, qkv, qkv, p, p, alog_row, dtb_row, _cumsum_matrix(rev), ng.reshape(1, -1)]
    return _scan_call(_gdn_kernel, "gdn", in_specs, args, o_f, rev=rev, nbat=nbat, s1=s1, nblk=nblk,
                      state_shape=(GDN_H, GDN_D, GDN_D))


def _ssd_kernel(xs_ref, bm_ref, cm_ref, dt_ref, z_ref, alog_ref, dtb_ref, ex_ref, lm_ref, dsk_ref, ng_ref,
                *rest, rev, nbat):
    if rev:
        of_ref, y_ref, st_ref = rest
    else:
        o_ref, st_ref = rest

    @pl.when(pl.program_id(0) == 0)
    def _():
        st_ref[...] = jnp.zeros_like(st_ref)

    d = 1 if rev else 0
    heads_per_g = M2_H // M2_G
    gw = heads_per_g * M2_P
    n_chunk = TOK_BLK // CHUNK
    mask = _chunk_mask(rev)
    lane = _iota2((TOK_BLK, gw), 1)
    ex = ex_ref[...]

    units = [(bi, g) for bi in range(nbat) for g in range(M2_G)]
    cqs, bks, xvs, b_es, eb_es, accs = [], [], [], [], [], []
    for bi in range(nbat):
        dt = _softplus(dt_ref[bi] + dtb_ref[...])
        loga = -jnp.exp(alog_ref[...]) * dt
        b8 = _select_rows(lm_ref[...], loga)
        b8_t = b8.T
        xv = xs_ref[bi] * _select_cols(dt, ex)
        b_e = _select_cols(b8, ex)
        eb_e = jnp.exp(b_e)
        for g in range(M2_G):
            gs = slice(g * M2_N, (g + 1) * M2_N)
            xs_g = slice(g * gw, (g + 1) * gw)
            cq = cm_ref[bi, :, gs]
            bk = bm_ref[bi, :, gs]
            scores = _dot_nt(cq, bk)
            xv_g = xv[:, xs_g]
            acc = None
            for hh in range(heads_per_g):
                lh = M2_H * d + heads_per_g * g + hh
                diff = b8[:, lh:lh + 1] - b8_t[lh:lh + 1, :]
                dec = jnp.where(mask, jnp.exp(jnp.where(mask, diff, 0.0)), 0.0)
                term = _dot(scores * dec, jnp.where(_idiv(lane, M2_P) == hh, xv_g, 0.0))
                acc = term if acc is None else acc + term
            cqs.append(cq)
            bks.append(bk)
            xvs.append(xv_g)
            b_es.append(b_e[:, xs_g])
            eb_es.append(eb_e[:, xs_g])
            accs.append(acc)

    sts = [st_ref[bi, g] for bi, g in units]
    o_parts = [[None] * n_chunk for _ in units]
    for c in _chunk_order(rev):
        rows = slice(c * CHUNK, (c + 1) * CHUNK)
        last = c * CHUNK if rev else (c + 1) * CHUNK - 1
        for i in range(len(units)):
            b_c = b_es[i][rows]
            b_last = b_es[i][last:last + 1]
            o_parts[i][c] = eb_es[i][rows] * _dot(cqs[i][rows], sts[i])
            ds = _dot_tn(bks[i][rows], xvs[i][rows] * jnp.exp(b_last - b_c))
            sts[i] = jnp.exp(b_last) * sts[i] + ds
    o_units = []
    for i, (bi, g) in enumerate(units):
        st_ref[bi, g] = sts[i]
        o_units.append(accs[i] + jnp.concatenate(o_parts[i], axis=0))
    for bi in range(nbat):
        o = jnp.concatenate(o_units[bi * M2_G:(bi + 1) * M2_G], axis=1)
        if rev:
            y = (of_ref[bi] + o + dsk_ref[...] * xs_ref[bi]) * _silu(z_ref[bi])
            y = y * lax.rsqrt(jnp.mean(y * y, axis=-1, keepdims=True) + RMS_EPS) * ng_ref[...]
            y_ref[bi] = y.astype(y_ref.dtype)
        else:
            o_ref[bi] = o


def _ssd_dir(p, xbc, a_log, dt_bias, d_skip, ng, o_f, *, rev, nblk):
    nbat, s1, _ = p.shape
    blk = _scan_block(rev, nblk)
    d = 1 if rev else 0
    lane0 = M2_H * d
    alog_row = jnp.zeros((1, LANE), F32).at[0, lane0:lane0 + M2_H].set(a_log[d])
    dtb_row = jnp.zeros((1, LANE), F32).at[0, lane0:lane0 + M2_H].set(dt_bias[d])
    ex = np.zeros((LANE, BRANCH_W), np.float32)
    for h in range(M2_H):
        ex[lane0 + h, h * M2_P:(h + 1) * M2_P] = 1.0
    dsk_row = jnp.repeat(d_skip, M2_P).reshape(1, BRANCH_W)
    in_specs = [_blk_spec(nbat, 512, 0, blk), _blk_spec(nbat, 256, 512, blk), _blk_spec(nbat, 256, 768, blk),
                _blk_spec(nbat, LANE, P_M2_DT, blk), _blk_spec(nbat, 512, P_M2_Z, blk),
                _const_spec((1, LANE)), _const_spec((1, LANE)), _const_spec((LANE, BRANCH_W)),
                _const_spec((TOK_BLK, TOK_BLK)), _const_spec((1, BRANCH_W)), _const_spec((1, BRANCH_W))]
    args = [xbc, xbc, xbc, p, p, alog_row, dtb_row, jnp.asarray(ex, dtype=BF16), _cumsum_matrix(rev), dsk_row,
            ng.reshape(1, -1)]
    return _scan_call(_ssd_kernel, "ssd", in_specs, args, o_f, rev=rev, nbat=nbat, s1=s1, nblk=nblk,
                      state_shape=(M2_G, M2_N, (M2_H // M2_G) * M2_P))


def _final_norm_kernel(x_ref, g_ref, o_ref):
    x = x_ref[...]
    o_ref[...] = x * lax.rsqrt(jnp.mean(x * x, axis=-1, keepdims=True) + RMS_EPS) * g_ref[...]


def _final_norm(xs, g, *, nbat, nblk, lc):
    D = xs.shape[1]
    cb = lc // TOK_BLK
    nlat = nblk - cb
    out = pl.pallas_call(
        _final_norm_kernel,
        grid=(nbat, nlat),
        in_specs=[pl.BlockSpec((TOK_BLK, D), lambda b, s: (b * nblk + cb + s, 0)),
                  pl.BlockSpec((1, D), lambda b, s: (0, 0))],
        out_specs=pl.BlockSpec((TOK_BLK, D), lambda b, s: (b * nlat + s, 0)),
        out_shape=jax.ShapeDtypeStruct((nbat * nlat * TOK_BLK, D), F32),
        compiler_params=_cparams(("arbitrary", "arbitrary")),
        name="final_norm",
    )(xs, g.reshape(1, D))
    return out.reshape(nbat, nlat * TOK_BLK, D)


def _row_tile(s1, limit):
    for tm in range(limit - limit % 16, 0, -16):
        if s1 % tm == 0:
            return tm
    raise ValueError(f"no row tile for sequence length {s1}")


def kernel(x, c, ctx, c_ctx, norm1_g, norm2_g, w_ada, b_ada, w_in, b_merge, gla_a2, gla_ab, gla_norm_g, na_rpb, gdn_conv, gdn_a_log, gdn_dt_bias, gdn_norm_g, m2_conv, m2_conv_b, m2_a_log, m2_dt_bias, m2_d, m2_norm_g, w_branch, w_out, w_ffn1, w_ffn3, w_ffn2, final_norm_g):
    nbat, t, D = x.shape
    lc = ctx.shape[1]
    depth = w_in.shape[0]
    assert D == D_MODEL and lc == TOK_BLK and t % TOK_BLK == 0 and t // GRID_W >= 3 * ROWS_PER_BLK
    assert nbat + 1 <= MOD_ROWS
    s1 = lc + t
    nblk = s1 // TOK_BLK
    tm = _row_tile(s1, 1056)
    geo = dict(s1=s1, lc=lc, nb=nbat)

    xs = jnp.concatenate([ctx, x], axis=1).reshape(nbat * s1, D)
    cvec = jnp.concatenate([c, c_ctx[None], jnp.zeros((MOD_ROWS - nbat - 1, D), F32)], axis=0)
    mods = _ada(cvec, w_ada, b_ada).reshape(depth, MOD_ROWS, 6, D).transpose(0, 2, 1, 3)
    cos, sin = _rope_tables(lc, t)
    flat = lambda y: y.reshape(nbat * s1, BRANCH_W)

    for l in range(depth):
        sh1, sc1, g1, sh2, sc2, g2 = (mods[l, i] for i in range(6))
        w_in_l = w_in[l]
        w_mix = _mix_weight(w_in_l)
        w_gate = w_in_l[:, MIX_IN:].astype(BF16)

        h = _normmod(xs, norm1_g[l], sc1, sh1, tm=_row_tile(s1, 384), **geo)
        p = _matmul(h, w_mix, tm=tm, tn=1024, out_dtype=F32).reshape(nbat, s1, P_COLS)

        o_f = _gla_dir(p, cos, sin, gla_a2[l], gla_ab[l], gla_norm_g[l], None, rev=False, nblk=nblk)
        ya = _gla_dir(p, cos, sin, gla_a2[l], gla_ab[l], gla_norm_g[l], o_f, rev=True, nblk=nblk)

        yb = _na(p, _na_bias_tiles(na_rpb[l]), nblk=nblk)

        qkv = _conv_silu(p, P_GDN_QKV, gdn_conv[l], jnp.zeros((3 * BRANCH_W,), F32), nblk=nblk)
        o_f = _gdn_dir(p, qkv, gdn_a_log[l], gdn_dt_bias[l], gdn_norm_g[l], None, rev=False, nblk=nblk)
        yc = _gdn_dir(p, qkv, gdn_a_log[l], gdn_dt_bias[l], gdn_norm_g[l], o_f, rev=True, nblk=nblk)

        xbc = _conv_silu(p, P_M2_XBC, m2_conv[l], m2_conv_b[l], nblk=nblk)
        o_f = _ssd_dir(p, xbc, m2_a_log[l], m2_dt_bias[l], m2_d[l], m2_norm_g[l], None, rev=False, nblk=nblk)
        yd = _ssd_dir(p, xbc, m2_a_log[l], m2_dt_bias[l], m2_d[l], m2_norm_g[l], o_f, rev=True, nblk=nblk)

        ys = (flat(ya), flat(yb), flat(yc), flat(yd))
        merged = _merge(h, w_gate, b_merge[l], ys, w_branch[l].astype(BF16), tm=tm, tn=512)
        xs, h2 = _outproj_norm(merged, w_out[l].astype(BF16), xs, g1, norm2_g[l], sc2, sh2,
                               tm=_row_tile(s1, 384), **geo)

        u = _ffn_up(h2, w_ffn1[l].astype(BF16), w_ffn3[l].astype(BF16), tm=_row_tile(s1, 2112), tn=512)
        xs = _matmul_residual(u, w_ffn2[l].astype(BF16), xs, g2, tm=tm, tn=512, name="ffn_down", **geo)

    return _final_norm(xs, final_norm_g, nbat=nbat, nblk=nblk, lc=lc)
```

```python
import functools

import numpy as np
import jax
import jax.numpy as jnp
from jax import lax
from jax.experimental import pallas as pl
from jax.experimental.pallas import tpu as pltpu

D_MODEL = 2048
GRID_W = 64
N_BRANCH = 4
BRANCH_W = D_MODEL // 4
CHUNK = 64
CONV_W = 5
RMS_EPS = 1e-6
NEG_INF = -1e30
ROPE_BASE = 10000.0
GLA_H = 4
GLA_DV = BRANCH_W // GLA_H
GLA_DK = GLA_DV // 2
GLA_LR = 16
GLA_TAU = 16.0
NA_H = 4
NA_D = BRANCH_W // NA_H
NA_WIN_R = 8
NA_WIN_C = 16
GDN_H = 4
GDN_D = BRANCH_W // GDN_H
M2_P = 64
M2_H = BRANCH_W // M2_P
M2_N = 128
M2_G = 2
M2_CONV_CH = BRANCH_W + 2 * M2_G * M2_N
D_FF = ((8 * D_MODEL + 3 * 256 - 1) // (3 * 256)) * 256
GLA_IN = 2 * GLA_H * GLA_DK + 2 * BRANCH_W + 2 * GLA_LR
NA_IN = 3 * BRANCH_W
GDN_IN = 4 * BRANCH_W + 4 * GDN_H
M2_IN = BRANCH_W + M2_CONV_CH + 2 * M2_H
MIX_IN = GLA_IN + NA_IN + GDN_IN + M2_IN

F32 = jnp.float32
BF16 = jnp.bfloat16

LANE = 128
SUBLANE = 8
V7X_VMEM_BYTES = 64 * 1024 * 1024
VMEM_LIMIT = V7X_VMEM_BYTES - 8 * 1024 * 1024

TOK_BLK = 4 * CHUNK
ROWS_PER_BLK = TOK_BLK // GRID_W
HALO = SUBLANE
MOD_ROWS = 8

P_GDN_QKV = 0
P_GLA_V = 1536
P_M2_XBC = 2048
P_GLA_G = 3072
P_NA_Q, P_NA_K, P_NA_V = 3584, 4096, 4608
P_GDN_Z, P_M2_Z = 5120, 5632
P_GLA_Q, P_GLA_K = 6144, 6400
P_GLA_LR, P_GDN_SC, P_M2_DT = 6656, 6784, 6912
P_COLS = 7168
MIX_TILE = 1024


def _mix_fields():
    gla, na, gdn, m2 = 0, GLA_IN, GLA_IN + NA_IN, GLA_IN + NA_IN + GDN_IN
    return sorted([
        (P_GLA_V, gla + 512, 512), (P_GLA_G, gla + 1024, 512),
        (P_NA_Q, na, 512), (P_NA_K, na + 512, 512), (P_NA_V, na + 1024, 512),
        (P_GDN_QKV, gdn, 1536), (P_GDN_Z, gdn + 1536, 512),
        (P_M2_Z, m2, 512), (P_M2_XBC, m2 + 512, M2_CONV_CH),
        (P_GLA_Q, gla, 256), (P_GLA_K, gla + 256, 256),
        (P_GLA_LR, gla + 1536, 2 * GLA_LR), (P_GDN_SC, gdn + 2048, 4 * GDN_H),
        (P_M2_DT, m2 + 512 + M2_CONV_CH, 2 * M2_H)])


def _cparams(sem):
    return pltpu.CompilerParams(dimension_semantics=sem, vmem_limit_bytes=VMEM_LIMIT)


def _wprep_kernel(w_ref, mix_ref, gate_ref):
    col = 0
    for dst, src, n in _mix_fields():
        if dst > col:
            mix_ref[:, col:dst] = jnp.zeros((mix_ref.shape[0], dst - col), mix_ref.dtype)
        mix_ref[:, dst:dst + n] = w_ref[:, src:src + n].astype(mix_ref.dtype)
        col = dst + n
    mix_ref[:, col:] = jnp.zeros((mix_ref.shape[0], P_COLS - col), mix_ref.dtype)
    gate_ref[...] = w_ref[:, MIX_IN:].astype(gate_ref.dtype)


def _cast_kernel(w_ref, o_ref):
    o_ref[...] = w_ref[...].astype(o_ref.dtype)


def _to_bf16(w):
    shape = w.shape
    w2 = w.reshape(-1, shape[-1])
    rb = 256
    assert w2.shape[0] % rb == 0
    out = pl.pallas_call(
        _cast_kernel,
        grid=(w2.shape[0] // rb,),
        in_specs=[pl.BlockSpec((rb, shape[-1]), lambda r: (r, 0))],
        out_specs=pl.BlockSpec((rb, shape[-1]), lambda r: (r, 0)),
        out_shape=jax.ShapeDtypeStruct(w2.shape, BF16),
        compiler_params=_cparams(("arbitrary",)),
        name="cast_weights",
    )(w2)
    return out.reshape(shape)


def _split_in_weights(w_in):
    L, D, N = w_in.shape
    rb = 256
    return pl.pallas_call(
        _wprep_kernel,
        grid=(L, D // rb),
        in_specs=[pl.BlockSpec((None, rb, N), lambda l, r: (l, r, 0))],
        out_specs=[pl.BlockSpec((None, rb, P_COLS), lambda l, r: (l, r, 0)),
                   pl.BlockSpec((None, rb, N - MIX_IN), lambda l, r: (l, r, 0))],
        out_shape=[jax.ShapeDtypeStruct((L, D, P_COLS), BF16), jax.ShapeDtypeStruct((L, D, N - MIX_IN), BF16)],
        compiler_params=_cparams(("arbitrary", "arbitrary")),
        name="split_in_weights",
    )(w_in)


def _sigmoid(x):
    return 1.0 / (1.0 + jnp.exp(-x))


def _silu(x):
    return x * _sigmoid(x)


def _softplus(x):
    return jnp.maximum(x, 0.0) + jnp.log1p(jnp.exp(-jnp.abs(x)))


def _dot(a, b):
    return jnp.dot(a.astype(BF16), b.astype(BF16), preferred_element_type=F32)


def _dot_nt(a, b):
    return lax.dot_general(a.astype(BF16), b.astype(BF16), (((1,), (1,)), ((), ())),
                           preferred_element_type=F32)


def _dot_tn(a, b):
    return lax.dot_general(a.astype(BF16), b.astype(BF16), (((0,), (0,)), ((), ())),
                           preferred_element_type=F32)


def _split3(a):
    hi = a.astype(BF16)
    r1 = a - hi.astype(F32)
    mid = r1.astype(BF16)
    return hi, mid, (r1 - mid.astype(F32)).astype(BF16)


def _select_rows(m01, a):
    d = lambda y: jnp.dot(m01, y, preferred_element_type=F32)
    hi, mid, lo = _split3(a)
    return d(hi) + (d(mid) + d(lo))


def _select_cols(a, m01):
    d = lambda y: jnp.dot(y, m01, preferred_element_type=F32)
    hi, mid, lo = _split3(a)
    return d(hi) + (d(mid) + d(lo))


def _split(a):
    hi = a.astype(BF16)
    return hi, (a - hi.astype(F32)).astype(BF16)


def _dot_split(a, b):
    (ah, al), (bh, bl) = a, b
    d = lambda x, y: jnp.dot(x, y, preferred_element_type=F32)
    return d(ah, bh) + (d(ah, bl) + d(al, bh))


def _iota2(shape, dim):
    return lax.broadcasted_iota(jnp.int32, shape, dim)


def _idiv(x, n):
    assert n & (n - 1) == 0
    return lax.shift_right_logical(x, jnp.int32(n.bit_length() - 1))


def _imod(x, n):
    assert n & (n - 1) == 0
    return x & (n - 1)


def _chunk_mask(rev, strict=False):
    r = _iota2((TOK_BLK, TOK_BLK), 0)
    c = _iota2((TOK_BLK, TOK_BLK), 1)
    same = _idiv(r, CHUNK) == _idiv(c, CHUNK)
    if rev:
        tri = (c > r) if strict else (c >= r)
    else:
        tri = (c < r) if strict else (c <= r)
    return same & tri


def _chunk_order(rev):
    n = TOK_BLK // CHUNK
    return list(range(n - 1, -1, -1)) if rev else list(range(n))


def _row_split(i, tm, s1):
    r0 = i * tm
    return lax.div(r0, jnp.int32(s1)), lax.rem(r0, jnp.int32(s1))


def _pick_mod(ref, b, nb, is_ctx):
    return jnp.where(is_ctx, ref[nb:nb + 1, :], ref[pl.ds(b, 1), :])


def _ada_kernel(c_ref, w_ref, b_ref, o_ref):
    a = _silu(c_ref[...])
    o_ref[...] = _dot(a, w_ref[...]) + b_ref[...]


def _ada(cvec, w_ada, b_ada):
    L, D, N = w_ada.shape
    tn = 1024
    return pl.pallas_call(
        _ada_kernel,
        grid=(L, N // tn),
        in_specs=[pl.BlockSpec((MOD_ROWS, D), lambda l, j: (0, 0)),
                  pl.BlockSpec((None, D, tn), lambda l, j: (l, 0, j)),
                  pl.BlockSpec((None, 1, tn), lambda l, j: (l, 0, j))],
        out_specs=pl.BlockSpec((None, MOD_ROWS, tn), lambda l, j: (l, 0, j)),
        out_shape=jax.ShapeDtypeStruct((L, MOD_ROWS, N), F32),
        compiler_params=_cparams(("arbitrary", "arbitrary")),
        name="ada_mod",
    )(cvec, w_ada, b_ada.reshape(L, 1, N))


def _normmod_kernel(x_ref, g_ref, sc_ref, sh_ref, o_ref, *, tm, s1, lc, nb):
    b, pos0 = _row_split(pl.program_id(0), tm, s1)
    is_ctx = (pos0 + _iota2((tm, 1), 0)) < lc
    x = x_ref[...]
    y = x * lax.rsqrt(jnp.mean(x * x, axis=-1, keepdims=True) + RMS_EPS) * g_ref[...]
    sc = _pick_mod(sc_ref, b, nb, is_ctx)
    sh = _pick_mod(sh_ref, b, nb, is_ctx)
    o_ref[...] = (y * (1.0 + sc) + sh).astype(o_ref.dtype)


def _normmod(xs, g, sc, sh, *, tm, s1, lc, nb):
    R, D = xs.shape
    return pl.pallas_call(
        functools.partial(_normmod_kernel, tm=tm, s1=s1, lc=lc, nb=nb),
        grid=(R // tm,),
        in_specs=[pl.BlockSpec((tm, D), lambda i: (i, 0)),
                  pl.BlockSpec((1, D), lambda i: (0, 0)),
                  pl.BlockSpec((MOD_ROWS, D), lambda i: (0, 0)),
                  pl.BlockSpec((MOD_ROWS, D), lambda i: (0, 0))],
        out_specs=pl.BlockSpec((tm, D), lambda i: (i, 0)),
        out_shape=jax.ShapeDtypeStruct((R, D), BF16),
        compiler_params=_cparams(("arbitrary",)),
        name="norm_mod",
    )(xs, g.reshape(1, D), sc, sh)


def _mm_kernel(a_ref, w_ref, o_ref):
    o_ref[...] = jnp.dot(a_ref[...], w_ref[...], preferred_element_type=F32).astype(o_ref.dtype)


def _matmul(a, w, *, tm, tn, out_dtype):
    R, K = a.shape
    N = w.shape[1]
    assert R % tm == 0 and N % tn == 0
    return pl.pallas_call(
        _mm_kernel,
        grid=(R // tm, N // tn),
        in_specs=[pl.BlockSpec((tm, K), lambda i, j: (i, 0)),
                  pl.BlockSpec((K, tn), lambda i, j: (0, j))],
        out_specs=pl.BlockSpec((tm, tn), lambda i, j: (i, j)),
        out_shape=jax.ShapeDtypeStruct((R, N), out_dtype),
        compiler_params=_cparams(("arbitrary", "arbitrary")),
        name="mix_proj",
    )(a, w)


def _mm_res_kernel(a_ref, w_ref, x_ref, gt_ref, o_ref, *, tm, s1, lc, nb):
    b, pos0 = _row_split(pl.program_id(0), tm, s1)
    is_ctx = (pos0 + _iota2((tm, 1), 0)) < lc
    acc = jnp.dot(a_ref[...], w_ref[...], preferred_element_type=F32)
    o_ref[...] = x_ref[...] + _pick_mod(gt_ref, b, nb, is_ctx) * acc


def _matmul_residual(a, w, xs, gate, *, tm, tn, s1, lc, nb, name):
    R, K = a.shape
    N = w.shape[1]
    assert R % tm == 0 and N % tn == 0
    return pl.pallas_call(
        functools.partial(_mm_res_kernel, tm=tm, s1=s1, lc=lc, nb=nb),
        grid=(R // tm, N // tn),
        in_specs=[pl.BlockSpec((tm, K), lambda i, j: (i, 0)),
                  pl.BlockSpec((K, tn), lambda i, j: (0, j)),
                  pl.BlockSpec((tm, tn), lambda i, j: (i, j)),
                  pl.BlockSpec((MOD_ROWS, tn), lambda i, j: (0, j))],
        out_specs=pl.BlockSpec((tm, tn), lambda i, j: (i, j)),
        out_shape=jax.ShapeDtypeStruct((R, N), F32),
        compiler_params=_cparams(("arbitrary", "arbitrary")),
        name=name,
    )(a, w, xs, gate)


def _outproj_norm_kernel(a_ref, w_ref, x_ref, gt_ref, g_ref, sc_ref, sh_ref, xo_ref, h_ref, *, tm, s1, lc, nb):
    b, pos0 = _row_split(pl.program_id(0), tm, s1)
    is_ctx = (pos0 + _iota2((tm, 1), 0)) < lc
    acc = jnp.dot(a_ref[...], w_ref[...], preferred_element_type=F32)
    x = x_ref[...] + _pick_mod(gt_ref, b, nb, is_ctx) * acc
    xo_ref[...] = x
    y = x * lax.rsqrt(jnp.mean(x * x, axis=-1, keepdims=True) + RMS_EPS) * g_ref[...]
    h_ref[...] = (y * (1.0 + _pick_mod(sc_ref, b, nb, is_ctx)) + _pick_mod(sh_ref, b, nb, is_ctx)).astype(h_ref.dtype)


def _outproj_norm(a, w, xs, gate, g, sc, sh, *, tm, s1, lc, nb):
    R, K = a.shape
    D = w.shape[1]
    row = lambda i: (i, 0)
    const = lambda i: (0, 0)
    return pl.pallas_call(
        functools.partial(_outproj_norm_kernel, tm=tm, s1=s1, lc=lc, nb=nb),
        grid=(R // tm,),
        in_specs=[pl.BlockSpec((tm, K), row), pl.BlockSpec((K, D), const), pl.BlockSpec((tm, D), row),
                  pl.BlockSpec((MOD_ROWS, D), const), pl.BlockSpec((1, D), const),
                  pl.BlockSpec((MOD_ROWS, D), const), pl.BlockSpec((MOD_ROWS, D), const)],
        out_specs=[pl.BlockSpec((tm, D), row), pl.BlockSpec((tm, D), row)],
        out_shape=[jax.ShapeDtypeStruct((R, D), F32), jax.ShapeDtypeStruct((R, D), BF16)],
        compiler_params=_cparams(("arbitrary",)),
        name="out_proj",
    )(a, w, xs, gate, g.reshape(1, D), sc, sh)


def _ffn_up_kernel(a_ref, w1_ref, w3_ref, o_ref):
    a = a_ref[...]
    u = jnp.dot(a, w1_ref[...], preferred_element_type=F32)
    v = jnp.dot(a, w3_ref[...], preferred_element_type=F32)
    o_ref[...] = (_silu(u) * v).astype(o_ref.dtype)


def _ffn_up(a, w1, w3, *, tm, tn):
    R, K = a.shape
    N = w1.shape[1]
    assert R % tm == 0 and N % tn == 0
    return pl.pallas_call(
        _ffn_up_kernel,
        grid=(R // tm, N // tn),
        in_specs=[pl.BlockSpec((tm, K), lambda i, j: (i, 0)),
                  pl.BlockSpec((K, tn), lambda i, j: (0, j)),
                  pl.BlockSpec((K, tn), lambda i, j: (0, j))],
        out_specs=pl.BlockSpec((tm, tn), lambda i, j: (i, j)),
        out_shape=jax.ShapeDtypeStruct((R, N), BF16),
        compiler_params=_cparams(("arbitrary", "arbitrary")),
        name="ffn_up",
    )(a, w1, w3)


def _merge_kernel(h_ref, g0_ref, g1_ref, g2_ref, g3_ref, bm_ref, ya_ref, yb_ref, yc_ref, yd_ref, wb_ref, o_ref):
    h = h_ref[...]
    acc = None
    branches = zip((g0_ref, g1_ref, g2_ref, g3_ref), (ya_ref, yb_ref, yc_ref, yd_ref))
    for i, (wg_ref, y_ref) in enumerate(branches):
        gate = _sigmoid(jnp.dot(h, wg_ref[...], preferred_element_type=F32) + bm_ref[i])
        term = gate * jnp.dot(y_ref[...], wb_ref[i], preferred_element_type=F32)
        acc = term if acc is None else acc + term
    o_ref[...] = acc.astype(o_ref.dtype)


def _merge(h, wg, b_merge, ys, wb, *, tm, tn):
    R, D = h.shape
    N = wb.shape[2]
    assert R % tm == 0 and N % tn == 0
    nj = N // tn
    yspec = pl.BlockSpec((tm, BRANCH_W), lambda i, j: (i, 0))
    gspec = lambda br: pl.BlockSpec((D, tn), lambda i, j: (0, br * nj + j))
    return pl.pallas_call(
        _merge_kernel,
        grid=(R // tm, nj),
        in_specs=[pl.BlockSpec((tm, D), lambda i, j: (i, 0)),
                  gspec(0), gspec(1), gspec(2), gspec(3),
                  pl.BlockSpec((N_BRANCH, 1, tn), lambda i, j: (0, 0, j)),
                  yspec, yspec, yspec, yspec,
                  pl.BlockSpec((N_BRANCH, BRANCH_W, tn), lambda i, j: (0, 0, j))],
        out_specs=pl.BlockSpec((tm, tn), lambda i, j: (i, j)),
        out_shape=jax.ShapeDtypeStruct((R, N), BF16),
        compiler_params=_cparams(("arbitrary", "arbitrary")),
        name="merge",
    )(h, wg, wg, wg, wg, b_merge.reshape(N_BRANCH, 1, N), *ys, wb)


def _conv_silu_block(xm, xp, xn, w, bias, blk, nblk):
    prev_ok = blk >= 2
    next_ok = (blk >= 1) & (blk < nblk - 1)
    rows = TOK_BLK + 2 * HALO
    half = CONV_W // 2
    xpad = jnp.concatenate([jnp.where(prev_ok, xp, 0.0), xm, jnp.where(next_ok, xn, 0.0)], axis=0)
    acc = bias + w[half:half + 1, :] * xm
    for j in range(CONV_W):
        if j != half:
            tap = pltpu.roll(xpad, (half - j) % rows, 0)[HALO:HALO + TOK_BLK]
            acc = acc + w[j:j + 1, :] * tap
    return _silu(acc)


def _conv_inputs(p, col0, w, bias, nblk):
    nbat, s1, _ = p.shape
    C = w.shape[1]
    assert col0 % C == 0
    cb = col0 // C
    hb = TOK_BLK // HALO
    nhalo = s1 // HALO
    wpad = jnp.concatenate([w, jnp.zeros((SUBLANE - CONV_W, C), F32)], axis=0)
    specs = [pl.BlockSpec((nbat, TOK_BLK, C), lambda r: (0, r, cb)),
             pl.BlockSpec((nbat, HALO, C), lambda r: (0, jnp.maximum(r * hb - 1, 0), cb)),
             pl.BlockSpec((nbat, HALO, C), lambda r: (0, jnp.minimum((r + 1) * hb, nhalo - 1), cb)),
             _const_spec((SUBLANE, C)), _const_spec((1, C))]
    return specs, [p, p, p, wpad, bias.reshape(1, C)]


def _scan_block(rev, nblk):
    if rev:
        return lambda s: jnp.where(s == 0, 0, nblk - s)
    return lambda s: s


def _blk_spec(nbat, width, col, blk):
    assert col % width == 0
    return pl.BlockSpec((nbat, TOK_BLK, width), lambda s: (0, blk(s), col // width))


def _const_spec(shape):
    return pl.BlockSpec(shape, lambda s: (0,) * len(shape))


def _head_rmsnorm_gate(o, g, gate, n_head, width):
    outs = []
    for h in range(n_head):
        oh = o[:, h * width:(h + 1) * width]
        yh = oh * lax.rsqrt(jnp.mean(oh * oh, axis=-1, keepdims=True) + RMS_EPS) * g
        outs.append(yh * _silu(gate[:, h * width:(h + 1) * width]))
    return jnp.concatenate(outs, axis=1)


def _scan_call(kern, name, in_specs, args, o_f, *, rev, nbat, s1, nblk, state_shape, conv_width=0):
    blk = _scan_block(rev, nblk)
    ospec = pl.BlockSpec((nbat, TOK_BLK, BRANCH_W), lambda s: (0, blk(s), 0))
    out_specs = [ospec]
    out_shape = [jax.ShapeDtypeStruct((nbat, s1, BRANCH_W), BF16 if rev else F32)]
    if rev:
        in_specs = in_specs + [ospec]
        args = args + [o_f]
    elif conv_width:
        out_specs.append(pl.BlockSpec((nbat, TOK_BLK, conv_width), lambda s: (0, s, 0)))
        out_shape.append(jax.ShapeDtypeStruct((nbat, s1, conv_width), F32))
    return pl.pallas_call(
        functools.partial(kern, rev=rev, nbat=nbat, nblk=nblk),
        grid=(nblk,),
        in_specs=in_specs,
        out_specs=out_specs,
        out_shape=out_shape,
        scratch_shapes=[pltpu.VMEM((nbat,) + state_shape, F32)],
        compiler_params=_cparams(("arbitrary",)),
        name=name + ("_bwd" if rev else "_fwd"),
    )(*args)


def _gla_kernel(q_ref, k_ref, v_ref, g_ref, lr_ref, cos_ref, sin_ref, a2_ref, ab_ref, lm_ref, ng_ref,
                *rest, rev, nbat, nblk):
    if rev:
        of_ref, y_ref, st_ref = rest
    else:
        o_ref, st_ref = rest

    @pl.when(pl.program_id(0) == 0)
    def _():
        st_ref[...] = jnp.zeros_like(st_ref)

    hk = GLA_H * GLA_DK
    lane = _iota2((TOK_BLK, hk), 1)
    first_half = _imod(lane, GLA_DK) < (GLA_DK // 2)
    head_of_lane = _idiv(lane, GLA_DK)
    cos = cos_ref[...]
    sin = sin_ref[...]
    mask = _chunk_mask(rev)
    n_chunk = TOK_BLK // CHUNK
    batch = range(nbat)

    def rope(x):
        partner = jnp.where(first_half, pltpu.roll(x, hk - GLA_DK // 2, 1), pltpu.roll(x, GLA_DK // 2, 1))
        return x * cos + partner * sin

    ks, vs, bs, q_ins, o_intras = [], [], [], [], []
    for bi in batch:
        q = rope(q_ref[bi]) * GLA_DK ** -0.5
        k = rope(k_ref[bi])
        v = v_ref[bi]
        loga = -_softplus(-(_dot(lr_ref[bi], a2_ref[...]) + ab_ref[...])) / GLA_TAU
        b = _select_rows(lm_ref[...], loga)
        q_in = q * jnp.exp(b)
        k_in = k * jnp.exp(-b)
        o_heads = []
        for h in range(GLA_H):
            att = _dot_nt(jnp.where(head_of_lane == h, q_in, 0.0), k_in)
            att = jnp.where(mask, att, 0.0)
            o_heads.append(_dot(att, v[:, h * GLA_DV:(h + 1) * GLA_DV]))
        ks.append(k)
        vs.append(v)
        bs.append(b)
        q_ins.append(q_in)
        o_intras.append(jnp.concatenate(o_heads, axis=1))

    sts = [st_ref[bi] for bi in batch]
    diag = _idiv(_iota2(sts[0].shape, 0), GLA_DV) == _idiv(_iota2(sts[0].shape, 1), GLA_DK)
    o_inter = [[None] * n_chunk for _ in batch]
    for c in _chunk_order(rev):
        rows = slice(c * CHUNK, (c + 1) * CHUNK)
        for bi in batch:
            b_c = bs[bi][rows]
            b_last = b_c[0:1] if rev else b_c[CHUNK - 1:CHUNK]
            o_inter[bi][c] = _dot_nt(q_ins[bi][rows], sts[bi])
            k_end = ks[bi][rows] * jnp.exp(b_last - b_c)
            ds = _dot_tn(vs[bi][rows], k_end)
            sts[bi] = sts[bi] * jnp.exp(b_last) + jnp.where(diag, ds, 0.0)
    for bi in batch:
        st_ref[bi] = sts[bi]
        o = o_intras[bi] + jnp.concatenate(o_inter[bi], axis=0)
        if rev:
            y = _head_rmsnorm_gate(of_ref[bi] + o, ng_ref[...], g_ref[bi], GLA_H, GLA_DV)
            y_ref[bi] = y.astype(y_ref.dtype)
        else:
            o_ref[bi] = o


def _cumsum_matrix(rev):
    r = np.arange(TOK_BLK)[:, None]
    c = np.arange(TOK_BLK)[None, :]
    same = (r // CHUNK) == (c // CHUNK)
    tri = (c >= r) if rev else (c <= r)
    return jnp.asarray((same & tri).astype(np.float32), dtype=BF16)


def _gla_dir(p, cos, sin, a2, ab, ng, o_f, *, rev, nblk):
    nbat, s1, _ = p.shape
    blk = _scan_block(rev, nblk)
    d = 1 if rev else 0
    a2d = jnp.zeros((LANE, GLA_H * GLA_DK), F32).at[d * GLA_LR:(d + 1) * GLA_LR].set(a2[d])
    tab = pl.BlockSpec((TOK_BLK, GLA_H * GLA_DK), lambda s: (blk(s), 0))
    in_specs = [_blk_spec(nbat, 256, P_GLA_Q, blk), _blk_spec(nbat, 256, P_GLA_K, blk),
                _blk_spec(nbat, 512, P_GLA_V, blk), _blk_spec(nbat, 512, P_GLA_G, blk),
                _blk_spec(nbat, LANE, P_GLA_LR, blk), tab, tab,
                _const_spec((LANE, 256)), _const_spec((1, 256)), _const_spec((TOK_BLK, TOK_BLK)),
                _const_spec((1, GLA_DV))]
    args = [p, p, p, p, p, cos, sin, a2d, ab[d].reshape(1, -1), _cumsum_matrix(rev), ng.reshape(1, -1)]
    return _scan_call(_gla_kernel, "gla", in_specs, args, o_f, rev=rev, nbat=nbat, s1=s1, nblk=nblk,
                      state_shape=(GLA_H * GLA_DV, GLA_H * GLA_DK))[0]


def _rope_tables(lc, t):
    n_freq = GLA_DK // 4
    freqs = ROPE_BASE ** (-jnp.arange(n_freq, dtype=F32) / n_freq)
    tt = jnp.arange(t)
    row = (tt // GRID_W).astype(F32)
    col = (tt % GRID_W).astype(F32)
    ang = jnp.concatenate([row[:, None] * freqs, col[:, None] * freqs], axis=-1)
    cos, sin = jnp.cos(ang), jnp.sin(ang)
    cos = jnp.concatenate([jnp.ones((lc, GLA_DK // 2), F32), cos], axis=0)
    sin = jnp.concatenate([jnp.zeros((lc, GLA_DK // 2), F32), sin], axis=0)
    cos_h = jnp.concatenate([cos, cos], axis=1)
    sin_h = jnp.concatenate([-sin, sin], axis=1)
    return jnp.tile(cos_h, (1, GLA_H)), jnp.tile(sin_h, (1, GLA_H))


def _na_kernel(q_ref, kp_ref, kc_ref, kn_ref, kx_ref, vp_ref, vc_ref, vn_ref, vx_ref, bias_ref, y_ref, *, nbat):
    scale = NA_D ** -0.5
    for bi in range(nbat):
        outs = []
        for h in range(NA_H):
            hs = slice(h * NA_D, (h + 1) * NA_D)
            qh = (q_ref[bi, :, hs] * scale).astype(BF16)
            s = jnp.concatenate(
                [_dot_nt(qh, kp_ref[bi, :, hs]) + bias_ref[0, h],
                 _dot_nt(qh, kc_ref[bi, :, hs]) + bias_ref[1, h],
                 _dot_nt(qh, kn_ref[bi, :, hs]) + bias_ref[2, h],
                 _dot_nt(qh, kx_ref[bi, :, hs])], axis=1)
            m = jnp.max(s, axis=-1, keepdims=True)
            e = jnp.exp(s - m)
            p = e / jnp.sum(e, axis=-1, keepdims=True)
            o = (_dot(p[:, 0:TOK_BLK], vp_ref[bi, :, hs]) + _dot(p[:, TOK_BLK:2 * TOK_BLK], vc_ref[bi, :, hs])
                 + _dot(p[:, 2 * TOK_BLK:3 * TOK_BLK], vn_ref[bi, :, hs])
                 + _dot(p[:, 3 * TOK_BLK:], vx_ref[bi, :, hs]))
            outs.append(o)
        y_ref[bi] = jnp.concatenate(outs, axis=1).astype(y_ref.dtype)


def _na_bias_tiles(rpb):
    edge = GRID_W - NA_WIN_C
    ext = jnp.concatenate([jnp.repeat(rpb[..., :1], edge, axis=-1), rpb,
                           jnp.repeat(rpb[..., -1:], edge, axis=-1)], axis=-1)
    toep = jnp.stack([ext[..., GRID_W - 1 - qc:2 * GRID_W - 1 - qc] for qc in range(GRID_W)], axis=-2)
    qc = np.arange(GRID_W)[:, None]
    kc = np.arange(GRID_W)[None, :]
    c0 = np.clip(qc - NA_WIN_C // 2, 0, GRID_W - NA_WIN_C)
    col_ok = (kc >= c0) & (kc < c0 + NA_WIN_C)
    toep = jnp.where(jnp.asarray(col_ok), toep, NEG_INF).astype(F32)
    masked = jnp.full((NA_H, GRID_W, GRID_W), NEG_INF, F32)
    kinds = []
    for kind in range(3):
        offs = []
        for off in (-1, 0, 1):
            rows = []
            for qr in range(ROWS_PER_BLK):
                start = (0, qr - NA_WIN_R // 2, ROWS_PER_BLK - NA_WIN_R)[kind]
                cols = []
                for kb in range(ROWS_PER_BLK):
                    kr = kb + ROWS_PER_BLK * off
                    ok = start <= kr < start + NA_WIN_R
                    cols.append(toep[:, kr - qr + NA_WIN_R - 1] if ok else masked)
                rows.append(jnp.concatenate(cols, axis=-1))
            offs.append(jnp.concatenate(rows, axis=-2))
        kinds.append(jnp.stack(offs))
    kinds.append(jnp.full_like(kinds[0], NEG_INF))
    return jnp.stack(kinds)


def _na(p, bias, *, nblk):
    nbat, s1, _ = p.shape

    def kind(s):
        return jnp.where(s == 0, 3, jnp.where(s == 1, 0, jnp.where(s == nblk - 1, 2, 1)))

    prev = lambda s: jnp.maximum(s - 1, 1)
    cur = lambda s: s
    nxt = lambda s: jnp.minimum(s + 1, nblk - 1)
    ctx = lambda s: 0
    spec = lambda col, blk: _blk_spec(nbat, BRANCH_W, col, blk)
    return pl.pallas_call(
        functools.partial(_na_kernel, nbat=nbat),
        grid=(nblk,),
        in_specs=[spec(P_NA_Q, cur),
                  spec(P_NA_K, prev), spec(P_NA_K, cur), spec(P_NA_K, nxt), spec(P_NA_K, ctx),
                  spec(P_NA_V, prev), spec(P_NA_V, cur), spec(P_NA_V, nxt), spec(P_NA_V, ctx),
                  pl.BlockSpec((None, 3, NA_H, TOK_BLK, TOK_BLK), lambda s: (kind(s), 0, 0, 0, 0))],
        out_specs=pl.BlockSpec((nbat, TOK_BLK, BRANCH_W), lambda s: (0, s, 0)),
        out_shape=jax.ShapeDtypeStruct((nbat, s1, BRANCH_W), BF16),
        compiler_params=_cparams(("arbitrary",)),
        name="nbr_attn",
    )(p, p, p, p, p, p, p, p, p, bias)


def _l2norm(x):
    return x * lax.rsqrt(jnp.sum(x * x, axis=-1, keepdims=True) + RMS_EPS)


def _gdn_kernel(*refs, rev, nbat, nblk):
    if rev:
        qkv_ref, z_ref, sc_ref, alog_ref, dtb_ref, lm_ref, ng_ref, of_ref, y_ref, st_ref = refs
        qkv = [qkv_ref[bi] for bi in range(nbat)]
    else:
        (xm_ref, xp_ref, xn_ref, cw_ref, cb_ref, z_ref, sc_ref, alog_ref, dtb_ref, lm_ref, ng_ref,
         o_ref, qkvc_ref, st_ref) = refs
        qkv = []
        for bi in range(nbat):
            qkv.append(_conv_silu_block(xm_ref[bi], xp_ref[bi], xn_ref[bi], cw_ref[...], cb_ref[...],
                                        pl.program_id(0), nblk))
            qkvc_ref[bi] = qkv[bi]

    @pl.when(pl.program_id(0) == 0)
    def _():
        st_ref[...] = jnp.zeros_like(st_ref)

    d = 1 if rev else 0
    m_incl = _chunk_mask(rev)
    m_strict = _chunk_mask(rev, strict=True)
    eye = (_iota2((TOK_BLK, TOK_BLK), 0) == _iota2((TOK_BLK, TOK_BLK), 1)).astype(F32)
    n_chunk = TOK_BLK // CHUNK

    units = [(bi, h) for bi in range(nbat) for h in range(GDN_H)]
    qs, ks, bcols, ebcs, attns, nmats, rhss = [], [], [], [], [], [], []
    for bi in range(nbat):
        sc = sc_ref[bi]
        beta_all = _sigmoid(sc)
        g_all = -jnp.exp(alog_ref[...]) * _softplus(sc + dtb_ref[...])
        b_all = _select_rows(lm_ref[...], g_all)
        b_all_t = b_all.T
        eb_all = jnp.exp(b_all)
        for h in range(GDN_H):
            head = lambda part: qkv[bi][:, part * BRANCH_W + h * GDN_D:part * BRANCH_W + (h + 1) * GDN_D]
            qh = _l2norm(head(0)) * GDN_D ** -0.5
            kh = _l2norm(head(1))
            vh = head(2)
            lb, lg = GDN_H * d + h, 2 * GDN_H + GDN_H * d + h
            beta = beta_all[:, lb:lb + 1]
            bcol = b_all[:, lg:lg + 1]
            brow = b_all_t[lg:lg + 1, :]
            ebc = eb_all[:, lg:lg + 1]
            diff = bcol - brow
            dec_incl = jnp.where(m_incl, jnp.exp(jnp.where(m_incl, diff, 0.0)), 0.0)
            dec_strict = jnp.where(m_strict, dec_incl, 0.0)
            kk = _dot_nt(kh, kh)
            qs.append(qh)
            ks.append(kh)
            bcols.append(bcol)
            ebcs.append(ebc)
            attns.append(_dot_nt(qh, kh) * dec_incl)
            nmats.append(-(beta * kk * dec_strict))
            rhss.append(_split(jnp.concatenate([kh * (beta * ebc), vh * beta], axis=1)))

    bdot = lambda x, y: jnp.dot(x, y, preferred_element_type=F32)
    nsplit = [_split(n) for n in nmats]
    t0s = [eye + n for n in nmats]
    powers = [ns[0] for ns in nsplit]
    for _ in range(5):
        powers = [bdot(m, m).astype(BF16) for m in powers]
        t0s = [t + bdot(t.astype(BF16), m) for t, m in zip(t0s, powers)]
    t0split = [_split(t) for t in t0s]
    resid = [(eye - t) + _dot_split(ns, ts) for t, ns, ts in zip(t0s, nsplit, t0split)]
    tinvs = [t + bdot(ts[0], r.astype(BF16)) for t, ts, r in zip(t0s, t0split, resid)]
    sols = [_dot_split(_split(t), r) for t, r in zip(tinvs, rhss)]
    ws = [s[:, :GDN_D] for s in sols]
    u0s = [s[:, GDN_D:] for s in sols]

    sts = [st_ref[bi, h] for bi, h in units]
    u_parts = [[None] * n_chunk for _ in units]
    o_parts = [[None] * n_chunk for _ in units]
    for c in _chunk_order(rev):
        rows = slice(c * CHUNK, (c + 1) * CHUNK)
        last = c * CHUNK if rev else (c + 1) * CHUNK - 1
        for i in range(len(units)):
            st = sts[i]
            b_last = bcols[i][last:last + 1]
            u = u0s[i][rows] - _dot(ws[i][rows], st)
            u_parts[i][c] = u
            o_parts[i][c] = ebcs[i][rows] * _dot(qs[i][rows], st)
            k_end = ks[i][rows] * jnp.exp(b_last - bcols[i][rows])
            sts[i] = jnp.exp(b_last) * st + _dot_tn(k_end, u)
    o_units = []
    for i, (bi, h) in enumerate(units):
        st_ref[bi, h] = sts[i]
        o_units.append(jnp.concatenate(o_parts[i], axis=0) + _dot(attns[i], jnp.concatenate(u_parts[i], axis=0)))
    for bi in range(nbat):
        o = jnp.concatenate(o_units[bi * GDN_H:(bi + 1) * GDN_H], axis=1)
        if rev:
            y = _head_rmsnorm_gate(of_ref[bi] + o, ng_ref[...], z_ref[bi], GDN_H, GDN_D)
            y_ref[bi] = y.astype(y_ref.dtype)
        else:
            o_ref[bi] = o


def _gdn_dir(p, qkv, conv_w, a_log, dt_bias, ng, o_f, *, rev, nblk):
    nbat, s1, _ = p.shape
    blk = _scan_block(rev, nblk)
    d = 1 if rev else 0
    width = 3 * BRANCH_W
    lane0 = 2 * GDN_H + GDN_H * d
    alog_row = jnp.zeros((1, LANE), F32).at[0, lane0:lane0 + GDN_H].set(a_log[d])
    dtb_row = jnp.zeros((1, LANE), F32).at[0, lane0:lane0 + GDN_H].set(dt_bias[d])
    if rev:
        in_specs, args = [_blk_spec(nbat, width, 0, blk)], [qkv]
    else:
        in_specs, args = _conv_inputs(p, P_GDN_QKV, conv_w, jnp.zeros((width,), F32), nblk)
    in_specs += [_blk_spec(nbat, 512, P_GDN_Z, blk), _blk_spec(nbat, LANE, P_GDN_SC, blk),
                 _const_spec((1, LANE)), _const_spec((1, LANE)), _const_spec((TOK_BLK, TOK_BLK)),
                 _const_spec((1, GDN_D))]
    args += [p, p, alog_row, dtb_row, _cumsum_matrix(rev), ng.reshape(1, -1)]
    return _scan_call(_gdn_kernel, "gdn", in_specs, args, o_f, rev=rev, nbat=nbat, s1=s1, nblk=nblk,
                      state_shape=(GDN_H, GDN_D, GDN_D), conv_width=width)


def _ssd_kernel(*refs, rev, nbat, nblk):
    if rev:
        (xbc_ref, dt_ref, z_ref, alog_ref, dtb_ref, ex_ref, lm_ref, dsk_ref, ng_ref, of_ref,
         y_ref, st_ref) = refs
        xbc = [xbc_ref[bi] for bi in range(nbat)]
    else:
        (xm_ref, xp_ref, xn_ref, cw_ref, cb_ref, dt_ref, z_ref, alog_ref, dtb_ref, ex_ref, lm_ref, dsk_ref, ng_ref,
         o_ref, xbcc_ref, st_ref) = refs
        xbc = []
        for bi in range(nbat):
            xbc.append(_conv_silu_block(xm_ref[bi], xp_ref[bi], xn_ref[bi], cw_ref[...], cb_ref[...],
                                        pl.program_id(0), nblk))
            xbcc_ref[bi] = xbc[bi]

    @pl.when(pl.program_id(0) == 0)
    def _():
        st_ref[...] = jnp.zeros_like(st_ref)

    d = 1 if rev else 0
    heads_per_g = M2_H // M2_G
    gw = heads_per_g * M2_P
    n_chunk = TOK_BLK // CHUNK
    mask = _chunk_mask(rev)
    lane = _iota2((TOK_BLK, gw), 1)
    ex = ex_ref[...]

    units = [(bi, g) for bi in range(nbat) for g in range(M2_G)]
    cqs, bks, xvs, b_es, eb_es, accs = [], [], [], [], [], []
    for bi in range(nbat):
        dt = _softplus(dt_ref[bi] + dtb_ref[...])
        loga = -jnp.exp(alog_ref[...]) * dt
        b8 = _select_rows(lm_ref[...], loga)
        b8_t = b8.T
        xv = xbc[bi][:, :BRANCH_W] * _select_cols(dt, ex)
        b_e = _select_cols(b8, ex)
        eb_e = jnp.exp(b_e)
        for g in range(M2_G):
            gs = slice(g * M2_N, (g + 1) * M2_N)
            xs_g = slice(g * gw, (g + 1) * gw)
            bk = xbc[bi][:, BRANCH_W + g * M2_N:BRANCH_W + (g + 1) * M2_N]
            cq = xbc[bi][:, BRANCH_W + (M2_G + g) * M2_N:BRANCH_W + (M2_G + g + 1) * M2_N]
            scores = _dot_nt(cq, bk)
            xv_g = xv[:, xs_g]
            acc = None
            for hh in range(heads_per_g):
                lh = M2_H * d + heads_per_g * g + hh
                diff = b8[:, lh:lh + 1] - b8_t[lh:lh + 1, :]
                dec = jnp.where(mask, jnp.exp(jnp.where(mask, diff, 0.0)), 0.0)
                term = _dot(scores * dec, jnp.where(_idiv(lane, M2_P) == hh, xv_g, 0.0))
                acc = term if acc is None else acc + term
            cqs.append(cq)
            bks.append(bk)
            xvs.append(xv_g)
            b_es.append(b_e[:, xs_g])
            eb_es.append(eb_e[:, xs_g])
            accs.append(acc)

    sts = [st_ref[bi, g] for bi, g in units]
    o_parts = [[None] * n_chunk for _ in units]
    for c in _chunk_order(rev):
        rows = slice(c * CHUNK, (c + 1) * CHUNK)
        last = c * CHUNK if rev else (c + 1) * CHUNK - 1
        for i in range(len(units)):
            b_c = b_es[i][rows]
            b_last = b_es[i][last:last + 1]
            o_parts[i][c] = eb_es[i][rows] * _dot(cqs[i][rows], sts[i])
            ds = _dot_tn(bks[i][rows], xvs[i][rows] * jnp.exp(b_last - b_c))
            sts[i] = jnp.exp(b_last) * sts[i] + ds
    o_units = []
    for i, (bi, g) in enumerate(units):
        st_ref[bi, g] = sts[i]
        o_units.append(accs[i] + jnp.concatenate(o_parts[i], axis=0))
    for bi in range(nbat):
        o = jnp.concatenate(o_units[bi * M2_G:(bi + 1) * M2_G], axis=1)
        if rev:
            y = (of_ref[bi] + o + dsk_ref[...] * xbc[bi][:, :BRANCH_W]) * _silu(z_ref[bi])
            y = y * lax.rsqrt(jnp.mean(y * y, axis=-1, keepdims=True) + RMS_EPS) * ng_ref[...]
            y_ref[bi] = y.astype(y_ref.dtype)
        else:
            o_ref[bi] = o


def _ssd_dir(p, xbc, conv_w, conv_b, a_log, dt_bias, d_skip, ng, o_f, *, rev, nblk):
    nbat, s1, _ = p.shape
    blk = _scan_block(rev, nblk)
    d = 1 if rev else 0
    lane0 = M2_H * d
    alog_row = jnp.zeros((1, LANE), F32).at[0, lane0:lane0 + M2_H].set(a_log[d])
    dtb_row = jnp.zeros((1, LANE), F32).at[0, lane0:lane0 + M2_H].set(dt_bias[d])
    ex = np.zeros((LANE, BRANCH_W), np.float32)
    for h in range(M2_H):
        ex[lane0 + h, h * M2_P:(h + 1) * M2_P] = 1.0
    dsk_row = jnp.repeat(d_skip, M2_P).reshape(1, BRANCH_W)
    if rev:
        in_specs, args = [_blk_spec(nbat, M2_CONV_CH, 0, blk)], [xbc]
    else:
        in_specs, args = _conv_inputs(p, P_M2_XBC, conv_w, conv_b, nblk)
    in_specs += [_blk_spec(nbat, LANE, P_M2_DT, blk), _blk_spec(nbat, 512, P_M2_Z, blk),
                 _const_spec((1, LANE)), _const_spec((1, LANE)), _const_spec((LANE, BRANCH_W)),
                 _const_spec((TOK_BLK, TOK_BLK)), _const_spec((1, BRANCH_W)), _const_spec((1, BRANCH_W))]
    args += [p, p, alog_row, dtb_row, jnp.asarray(ex, dtype=BF16), _cumsum_matrix(rev), dsk_row, ng.reshape(1, -1)]
    return _scan_call(_ssd_kernel, "ssd", in_specs, args, o_f, rev=rev, nbat=nbat, s1=s1, nblk=nblk,
                      state_shape=(M2_G, M2_N, (M2_H // M2_G) * M2_P), conv_width=M2_CONV_CH)


def _final_norm_kernel(x_ref, g_ref, o_ref):
    x = x_ref[...]
    o_ref[...] = x * lax.rsqrt(jnp.mean(x * x, axis=-1, keepdims=True) + RMS_EPS) * g_ref[...]


def _final_norm(xs, g, *, nbat, nblk, lc):
    D = xs.shape[1]
    cb = lc // TOK_BLK
    nlat = nblk - cb
    out = pl.pallas_call(
        _final_norm_kernel,
        grid=(nbat, nlat),
        in_specs=[pl.BlockSpec((TOK_BLK, D), lambda b, s: (b * nblk + cb + s, 0)),
                  pl.BlockSpec((1, D), lambda b, s: (0, 0))],
        out_specs=pl.BlockSpec((TOK_BLK, D), lambda b, s: (b * nlat + s, 0)),
        out_shape=jax.ShapeDtypeStruct((nbat * nlat * TOK_BLK, D), F32),
        compiler_params=_cparams(("arbitrary", "arbitrary")),
        name="final_norm",
    )(xs, g.reshape(1, D))
    return out.reshape(nbat, nlat * TOK_BLK, D)


def _row_tile(s1, limit):
    for tm in range(limit - limit % 16, 0, -16):
        if s1 % tm == 0:
            return tm
    raise ValueError(f"no row tile for sequence length {s1}")


def kernel(x, c, ctx, c_ctx, norm1_g, norm2_g, w_ada, b_ada, w_in, b_merge, gla_a2, gla_ab, gla_norm_g, na_rpb, gdn_conv, gdn_a_log, gdn_dt_bias, gdn_norm_g, m2_conv, m2_conv_b, m2_a_log, m2_dt_bias, m2_d, m2_norm_g, w_branch, w_out, w_ffn1, w_ffn3, w_ffn2, final_norm_g):
    nbat, t, D = x.shape
    lc = ctx.shape[1]
    depth = w_in.shape[0]
    assert D == D_MODEL and lc == TOK_BLK and t % TOK_BLK == 0 and t // GRID_W >= 3 * ROWS_PER_BLK
    assert nbat + 1 <= MOD_ROWS
    s1 = lc + t
    nblk = s1 // TOK_BLK
    tm = _row_tile(s1, 1056)
    geo = dict(s1=s1, lc=lc, nb=nbat)

    xs = jnp.concatenate([ctx, x], axis=1).reshape(nbat * s1, D)
    cvec = jnp.concatenate([c, c_ctx[None], jnp.zeros((MOD_ROWS - nbat - 1, D), F32)], axis=0)
    mods = _ada(cvec, w_ada, b_ada).reshape(depth, MOD_ROWS, 6, D).transpose(0, 2, 1, 3)
    cos, sin = _rope_tables(lc, t)
    w_mix, w_gate = _split_in_weights(w_in)
    wb_branch, wb_out, wb_ffn1, wb_ffn3, wb_ffn2 = map(_to_bf16, (w_branch, w_out, w_ffn1, w_ffn3, w_ffn2))
    flat = lambda y: y.reshape(nbat * s1, BRANCH_W)

    for l in range(depth):
        sh1, sc1, g1, sh2, sc2, g2 = (mods[l, i] for i in range(6))

        h = _normmod(xs, norm1_g[l], sc1, sh1, tm=_row_tile(s1, 384), **geo)
        p = _matmul(h, w_mix[l], tm=tm, tn=MIX_TILE, out_dtype=F32).reshape(nbat, s1, P_COLS)

        o_f = _gla_dir(p, cos, sin, gla_a2[l], gla_ab[l], gla_norm_g[l], None, rev=False, nblk=nblk)
        ya = _gla_dir(p, cos, sin, gla_a2[l], gla_ab[l], gla_norm_g[l], o_f, rev=True, nblk=nblk)

        yb = _na(p, _na_bias_tiles(na_rpb[l]), nblk=nblk)

        gdn_par = (gdn_conv[l], gdn_a_log[l], gdn_dt_bias[l], gdn_norm_g[l])
        o_f, qkv = _gdn_dir(p, None, *gdn_par, None, rev=False, nblk=nblk)
        yc, = _gdn_dir(p, qkv, *gdn_par, o_f, rev=True, nblk=nblk)

        ssd_par = (m2_conv[l], m2_conv_b[l], m2_a_log[l], m2_dt_bias[l], m2_d[l], m2_norm_g[l])
        o_f, xbc = _ssd_dir(p, None, *ssd_par, None, rev=False, nblk=nblk)
        yd, = _ssd_dir(p, xbc, *ssd_par, o_f, rev=True, nblk=nblk)

        ys = (flat(ya), flat(yb), flat(yc), flat(yd))
        merged = _merge(h, w_gate[l], b_merge[l], ys, wb_branch[l], tm=tm, tn=512)
        xs, h2 = _outproj_norm(merged, wb_out[l], xs, g1, norm2_g[l], sc2, sh2,
                               tm=_row_tile(s1, 384), **geo)

        u = _ffn_up(h2, wb_ffn1[l], wb_ffn3[l], tm=_row_tile(s1, 2112), tn=512)
        xs = _matmul_residual(u, wb_ffn2[l], xs, g2, tm=tm, tn=512, name="ffn_down", **geo)

    return _final_norm(xs, final_norm_g, nbat=nbat, nblk=nblk, lc=lc)
```

```python
import functools

import numpy as np
import jax
import jax.numpy as jnp
from jax import lax
from jax.experimental import pallas as pl
from jax.experimental.pallas import tpu as pltpu

D_MODEL = 2048
GRID_W = 64
N_BRANCH = 4
BRANCH_W = D_MODEL // 4
CHUNK = 64
CONV_W = 5
RMS_EPS = 1e-6
NEG_INF = -1e30
ROPE_BASE = 10000.0
GLA_H = 4
GLA_DV = BRANCH_W // GLA_H
GLA_DK = GLA_DV // 2
GLA_LR = 16
GLA_TAU = 16.0
NA_H = 4
NA_D = BRANCH_W // NA_H
NA_WIN_R = 8
NA_WIN_C = 16
GDN_H = 4
GDN_D = BRANCH_W // GDN_H
M2_P = 64
M2_H = BRANCH_W // M2_P
M2_N = 128
M2_G = 2
M2_CONV_CH = BRANCH_W + 2 * M2_G * M2_N
D_FF = ((8 * D_MODEL + 3 * 256 - 1) // (3 * 256)) * 256
GLA_IN = 2 * GLA_H * GLA_DK + 2 * BRANCH_W + 2 * GLA_LR
NA_IN = 3 * BRANCH_W
GDN_IN = 4 * BRANCH_W + 4 * GDN_H
M2_IN = BRANCH_W + M2_CONV_CH + 2 * M2_H
MIX_IN = GLA_IN + NA_IN + GDN_IN + M2_IN

F32 = jnp.float32
BF16 = jnp.bfloat16

LANE = 128
SUBLANE = 8
V7X_VMEM_BYTES = 64 * 1024 * 1024
VMEM_LIMIT = V7X_VMEM_BYTES - 8 * 1024 * 1024

TOK_BLK = 4 * CHUNK
ROWS_PER_BLK = TOK_BLK // GRID_W
HALO = SUBLANE
MOD_ROWS = 8

P_GDN_QKV = 0
P_GLA_V = 1536
P_M2_XBC = 2048
P_GLA_G = 3072
P_NA_Q, P_NA_K, P_NA_V = 3584, 4096, 4608
P_GDN_Z, P_M2_Z = 5120, 5632
P_GLA_Q, P_GLA_K = 6144, 6400
P_GLA_LR, P_GDN_SC, P_M2_DT = 6656, 6784, 6912
P_COLS = 7168
MIX_TILE = 1024


def _mix_fields():
    gla, na, gdn, m2 = 0, GLA_IN, GLA_IN + NA_IN, GLA_IN + NA_IN + GDN_IN
    return sorted([
        (P_GLA_V, gla + 512, 512), (P_GLA_G, gla + 1024, 512),
        (P_NA_Q, na, 512), (P_NA_K, na + 512, 512), (P_NA_V, na + 1024, 512),
        (P_GDN_QKV, gdn, 1536), (P_GDN_Z, gdn + 1536, 512),
        (P_M2_Z, m2, 512), (P_M2_XBC, m2 + 512, M2_CONV_CH),
        (P_GLA_Q, gla, 256), (P_GLA_K, gla + 256, 256),
        (P_GLA_LR, gla + 1536, 2 * GLA_LR), (P_GDN_SC, gdn + 2048, 4 * GDN_H),
        (P_M2_DT, m2 + 512 + M2_CONV_CH, 2 * M2_H)])


def _cparams(sem):
    return pltpu.CompilerParams(dimension_semantics=sem, vmem_limit_bytes=VMEM_LIMIT)


def _wprep_kernel(w_ref, mix_ref, gate_ref):
    col = 0
    for dst, src, n in _mix_fields():
        if dst > col:
            mix_ref[:, col:dst] = jnp.zeros((mix_ref.shape[0], dst - col), mix_ref.dtype)
        mix_ref[:, dst:dst + n] = w_ref[:, src:src + n].astype(mix_ref.dtype)
        col = dst + n
    mix_ref[:, col:] = jnp.zeros((mix_ref.shape[0], P_COLS - col), mix_ref.dtype)
    gate_ref[...] = w_ref[:, MIX_IN:].astype(gate_ref.dtype)


def _split_in_weights(w_in_l):
    D, N = w_in_l.shape
    rb = 256
    return pl.pallas_call(
        _wprep_kernel,
        grid=(D // rb,),
        in_specs=[pl.BlockSpec((rb, N), lambda r: (r, 0))],
        out_specs=[pl.BlockSpec((rb, P_COLS), lambda r: (r, 0)),
                   pl.BlockSpec((rb, N - MIX_IN), lambda r: (r, 0))],
        out_shape=[jax.ShapeDtypeStruct((D, P_COLS), BF16), jax.ShapeDtypeStruct((D, N - MIX_IN), BF16)],
        compiler_params=_cparams(("arbitrary",)),
        name="split_in_weights",
    )(w_in_l)


def _sigmoid(x):
    return 1.0 / (1.0 + jnp.exp(-x))


def _silu(x):
    return x * _sigmoid(x)


def _softplus(x):
    return jnp.maximum(x, 0.0) + jnp.log1p(jnp.exp(-jnp.abs(x)))


def _dot(a, b):
    return jnp.dot(a.astype(BF16), b.astype(BF16), preferred_element_type=F32)


def _dot_nt(a, b):
    return lax.dot_general(a.astype(BF16), b.astype(BF16), (((1,), (1,)), ((), ())),
                           preferred_element_type=F32)


def _dot_tn(a, b):
    return lax.dot_general(a.astype(BF16), b.astype(BF16), (((0,), (0,)), ((), ())),
                           preferred_element_type=F32)


def _split3(a):
    hi = a.astype(BF16)
    r1 = a - hi.astype(F32)
    mid = r1.astype(BF16)
    return hi, mid, (r1 - mid.astype(F32)).astype(BF16)


def _select_rows(m01, a):
    d = lambda y: jnp.dot(m01, y, preferred_element_type=F32)
    hi, mid, lo = _split3(a)
    return d(hi) + (d(mid) + d(lo))


def _select_cols(a, m01):
    d = lambda y: jnp.dot(y, m01, preferred_element_type=F32)
    hi, mid, lo = _split3(a)
    return d(hi) + (d(mid) + d(lo))


def _split(a):
    hi = a.astype(BF16)
    return hi, (a - hi.astype(F32)).astype(BF16)


def _dot_split(a, b):
    (ah, al), (bh, bl) = a, b
    d = lambda x, y: jnp.dot(x, y, preferred_element_type=F32)
    return d(ah, bh) + (d(ah, bl) + d(al, bh))


def _iota2(shape, dim):
    return lax.broadcasted_iota(jnp.int32, shape, dim)


def _idiv(x, n):
    assert n & (n - 1) == 0
    return lax.shift_right_logical(x, jnp.int32(n.bit_length() - 1))


def _imod(x, n):
    assert n & (n - 1) == 0
    return x & (n - 1)


def _chunk_mask(rev, strict=False):
    r = _iota2((TOK_BLK, TOK_BLK), 0)
    c = _iota2((TOK_BLK, TOK_BLK), 1)
    same = _idiv(r, CHUNK) == _idiv(c, CHUNK)
    if rev:
        tri = (c > r) if strict else (c >= r)
    else:
        tri = (c < r) if strict else (c <= r)
    return same & tri


def _chunk_order(rev):
    n = TOK_BLK // CHUNK
    return list(range(n - 1, -1, -1)) if rev else list(range(n))


def _row_split(i, tm, s1):
    r0 = i * tm
    return lax.div(r0, jnp.int32(s1)), lax.rem(r0, jnp.int32(s1))


def _pick_mod(ref, b, nb, is_ctx):
    return jnp.where(is_ctx, ref[nb:nb + 1, :], ref[pl.ds(b, 1), :])


def _ada_kernel(c_ref, w_ref, b_ref, o_ref):
    a = _silu(c_ref[...])
    o_ref[...] = _dot(a, w_ref[...]) + b_ref[...]


def _ada(cvec, w_ada, b_ada):
    L, D, N = w_ada.shape
    tn = 1024
    return pl.pallas_call(
        _ada_kernel,
        grid=(L, N // tn),
        in_specs=[pl.BlockSpec((MOD_ROWS, D), lambda l, j: (0, 0)),
                  pl.BlockSpec((None, D, tn), lambda l, j: (l, 0, j)),
                  pl.BlockSpec((None, 1, tn), lambda l, j: (l, 0, j))],
        out_specs=pl.BlockSpec((None, MOD_ROWS, tn), lambda l, j: (l, 0, j)),
        out_shape=jax.ShapeDtypeStruct((L, MOD_ROWS, N), F32),
        compiler_params=_cparams(("arbitrary", "arbitrary")),
        name="ada_mod",
    )(cvec, w_ada, b_ada.reshape(L, 1, N))


def _normmod_kernel(x_ref, g_ref, sc_ref, sh_ref, o_ref, *, tm, s1, lc, nb):
    b, pos0 = _row_split(pl.program_id(0), tm, s1)
    is_ctx = (pos0 + _iota2((tm, 1), 0)) < lc
    x = x_ref[...]
    y = x * lax.rsqrt(jnp.mean(x * x, axis=-1, keepdims=True) + RMS_EPS) * g_ref[...]
    sc = _pick_mod(sc_ref, b, nb, is_ctx)
    sh = _pick_mod(sh_ref, b, nb, is_ctx)
    o_ref[...] = (y * (1.0 + sc) + sh).astype(o_ref.dtype)


def _normmod(xs, g, sc, sh, *, tm, s1, lc, nb):
    R, D = xs.shape
    return pl.pallas_call(
        functools.partial(_normmod_kernel, tm=tm, s1=s1, lc=lc, nb=nb),
        grid=(R // tm,),
        in_specs=[pl.BlockSpec((tm, D), lambda i: (i, 0)),
                  pl.BlockSpec((1, D), lambda i: (0, 0)),
                  pl.BlockSpec((MOD_ROWS, D), lambda i: (0, 0)),
                  pl.BlockSpec((MOD_ROWS, D), lambda i: (0, 0))],
        out_specs=pl.BlockSpec((tm, D), lambda i: (i, 0)),
        out_shape=jax.ShapeDtypeStruct((R, D), BF16),
        compiler_params=_cparams(("arbitrary",)),
        name="norm_mod",
    )(xs, g.reshape(1, D), sc, sh)


def _mm_kernel(a_ref, w_ref, o_ref):
    o_ref[...] = jnp.dot(a_ref[...], w_ref[...], preferred_element_type=F32).astype(o_ref.dtype)


def _matmul(a, w, *, tm, tn, out_dtype):
    R, K = a.shape
    N = w.shape[1]
    assert R % tm == 0 and N % tn == 0
    return pl.pallas_call(
        _mm_kernel,
        grid=(R // tm, N // tn),
        in_specs=[pl.BlockSpec((tm, K), lambda i, j: (i, 0)),
                  pl.BlockSpec((K, tn), lambda i, j: (0, j))],
        out_specs=pl.BlockSpec((tm, tn), lambda i, j: (i, j)),
        out_shape=jax.ShapeDtypeStruct((R, N), out_dtype),
        compiler_params=_cparams(("arbitrary", "arbitrary")),
        name="mix_proj",
    )(a, w)


def _mm_res_kernel(a_ref, w_ref, x_ref, gt_ref, o_ref, *, tm, s1, lc, nb):
    b, pos0 = _row_split(pl.program_id(0), tm, s1)
    is_ctx = (pos0 + _iota2((tm, 1), 0)) < lc
    acc = jnp.dot(a_ref[...], w_ref[...], preferred_element_type=F32)
    o_ref[...] = x_ref[...] + _pick_mod(gt_ref, b, nb, is_ctx) * acc


def _matmul_residual(a, w, xs, gate, *, tm, tn, s1, lc, nb, name):
    R, K = a.shape
    N = w.shape[1]
    assert R % tm == 0 and N % tn == 0
    return pl.pallas_call(
        functools.partial(_mm_res_kernel, tm=tm, s1=s1, lc=lc, nb=nb),
        grid=(R // tm, N // tn),
        in_specs=[pl.BlockSpec((tm, K), lambda i, j: (i, 0)),
                  pl.BlockSpec((K, tn), lambda i, j: (0, j)),
                  pl.BlockSpec((tm, tn), lambda i, j: (i, j)),
                  pl.BlockSpec((MOD_ROWS, tn), lambda i, j: (0, j))],
        out_specs=pl.BlockSpec((tm, tn), lambda i, j: (i, j)),
        out_shape=jax.ShapeDtypeStruct((R, N), F32),
        compiler_params=_cparams(("arbitrary", "arbitrary")),
        name=name,
    )(a, w, xs, gate)


def _outproj_norm_kernel(a_ref, w_ref, x_ref, gt_ref, g_ref, sc_ref, sh_ref, xo_ref, h_ref, *, tm, s1, lc, nb):
    b, pos0 = _row_split(pl.program_id(0), tm, s1)
    is_ctx = (pos0 + _iota2((tm, 1), 0)) < lc
    acc = jnp.dot(a_ref[...], w_ref[...], preferred_element_type=F32)
    x = x_ref[...] + _pick_mod(gt_ref, b, nb, is_ctx) * acc
    xo_ref[...] = x
    y = x * lax.rsqrt(jnp.mean(x * x, axis=-1, keepdims=True) + RMS_EPS) * g_ref[...]
    h_ref[...] = (y * (1.0 + _pick_mod(sc_ref, b, nb, is_ctx)) + _pick_mod(sh_ref, b, nb, is_ctx)).astype(h_ref.dtype)


def _outproj_norm(a, w, xs, gate, g, sc, sh, *, tm, s1, lc, nb):
    R, K = a.shape
    D = w.shape[1]
    row = lambda i: (i, 0)
    const = lambda i: (0, 0)
    return pl.pallas_call(
        functools.partial(_outproj_norm_kernel, tm=tm, s1=s1, lc=lc, nb=nb),
        grid=(R // tm,),
        in_specs=[pl.BlockSpec((tm, K), row), pl.BlockSpec((K, D), const), pl.BlockSpec((tm, D), row),
                  pl.BlockSpec((MOD_ROWS, D), const), pl.BlockSpec((1, D), const),
                  pl.BlockSpec((MOD_ROWS, D), const), pl.BlockSpec((MOD_ROWS, D), const)],
        out_specs=[pl.BlockSpec((tm, D), row), pl.BlockSpec((tm, D), row)],
        out_shape=[jax.ShapeDtypeStruct((R, D), F32), jax.ShapeDtypeStruct((R, D), BF16)],
        compiler_params=_cparams(("arbitrary",)),
        name="out_proj",
    )(a, w, xs, gate, g.reshape(1, D), sc, sh)


def _ffn_up_kernel(a_ref, w1_ref, w3_ref, o_ref):
    a = a_ref[...]
    u = jnp.dot(a, w1_ref[...], preferred_element_type=F32)
    v = jnp.dot(a, w3_ref[...], preferred_element_type=F32)
    o_ref[...] = (_silu(u) * v).astype(o_ref.dtype)


def _ffn_up(a, w1, w3, *, tm, tn):
    R, K = a.shape
    N = w1.shape[1]
    assert R % tm == 0 and N % tn == 0
    return pl.pallas_call(
        _ffn_up_kernel,
        grid=(R // tm, N // tn),
        in_specs=[pl.BlockSpec((tm, K), lambda i, j: (i, 0)),
                  pl.BlockSpec((K, tn), lambda i, j: (0, j)),
                  pl.BlockSpec((K, tn), lambda i, j: (0, j))],
        out_specs=pl.BlockSpec((tm, tn), lambda i, j: (i, j)),
        out_shape=jax.ShapeDtypeStruct((R, N), BF16),
        compiler_params=_cparams(("arbitrary", "arbitrary")),
        name="ffn_up",
    )(a, w1, w3)


def _merge_kernel(h_ref, g0_ref, g1_ref, g2_ref, g3_ref, bm_ref, ya_ref, yb_ref, yc_ref, yd_ref, wb_ref, o_ref):
    h = h_ref[...]
    acc = None
    branches = zip((g0_ref, g1_ref, g2_ref, g3_ref), (ya_ref, yb_ref, yc_ref, yd_ref))
    for i, (wg_ref, y_ref) in enumerate(branches):
        gate = _sigmoid(jnp.dot(h, wg_ref[...], preferred_element_type=F32) + bm_ref[i])
        term = gate * jnp.dot(y_ref[...], wb_ref[i], preferred_element_type=F32)
        acc = term if acc is None else acc + term
    o_ref[...] = acc.astype(o_ref.dtype)


def _merge(h, wg, b_merge, ys, wb, *, tm, tn):
    R, D = h.shape
    N = wb.shape[2]
    assert R % tm == 0 and N % tn == 0
    nj = N // tn
    yspec = pl.BlockSpec((tm, BRANCH_W), lambda i, j: (i, 0))
    gspec = lambda br: pl.BlockSpec((D, tn), lambda i, j: (0, br * nj + j))
    return pl.pallas_call(
        _merge_kernel,
        grid=(R // tm, nj),
        in_specs=[pl.BlockSpec((tm, D), lambda i, j: (i, 0)),
                  gspec(0), gspec(1), gspec(2), gspec(3),
                  pl.BlockSpec((N_BRANCH, 1, tn), lambda i, j: (0, 0, j)),
                  yspec, yspec, yspec, yspec,
                  pl.BlockSpec((N_BRANCH, BRANCH_W, tn), lambda i, j: (0, 0, j))],
        out_specs=pl.BlockSpec((tm, tn), lambda i, j: (i, j)),
        out_shape=jax.ShapeDtypeStruct((R, N), BF16),
        compiler_params=_cparams(("arbitrary", "arbitrary")),
        name="merge",
    )(h, wg, wg, wg, wg, b_merge.reshape(N_BRANCH, 1, N), *ys, wb)


def _conv_silu_block(xm, xp, xn, w, bias, blk, nblk):
    prev_ok = blk >= 2
    next_ok = (blk >= 1) & (blk < nblk - 1)
    rows = TOK_BLK + 2 * HALO
    half = CONV_W // 2
    xpad = jnp.concatenate([jnp.where(prev_ok, xp, 0.0), xm, jnp.where(next_ok, xn, 0.0)], axis=0)
    acc = bias + w[half:half + 1, :] * xm
    for j in range(CONV_W):
        if j != half:
            tap = pltpu.roll(xpad, (half - j) % rows, 0)[HALO:HALO + TOK_BLK]
            acc = acc + w[j:j + 1, :] * tap
    return _silu(acc)


def _conv_inputs(p, col0, w, bias, nblk):
    nbat, s1, _ = p.shape
    C = w.shape[1]
    assert col0 % C == 0
    cb = col0 // C
    hb = TOK_BLK // HALO
    nhalo = s1 // HALO
    wpad = jnp.concatenate([w, jnp.zeros((SUBLANE - CONV_W, C), F32)], axis=0)
    specs = [pl.BlockSpec((nbat, TOK_BLK, C), lambda r: (0, r, cb)),
             pl.BlockSpec((nbat, HALO, C), lambda r: (0, jnp.maximum(r * hb - 1, 0), cb)),
             pl.BlockSpec((nbat, HALO, C), lambda r: (0, jnp.minimum((r + 1) * hb, nhalo - 1), cb)),
             _const_spec((SUBLANE, C)), _const_spec((1, C))]
    return specs, [p, p, p, wpad, bias.reshape(1, C)]


def _scan_block(rev, nblk):
    if rev:
        return lambda s: jnp.where(s == 0, 0, nblk - s)
    return lambda s: s


def _blk_spec(nbat, width, col, blk):
    assert col % width == 0
    return pl.BlockSpec((nbat, TOK_BLK, width), lambda s: (0, blk(s), col // width))


def _const_spec(shape):
    return pl.BlockSpec(shape, lambda s: (0,) * len(shape))


def _head_rmsnorm_gate(o, g, gate, n_head, width):
    outs = []
    for h in range(n_head):
        oh = o[:, h * width:(h + 1) * width]
        yh = oh * lax.rsqrt(jnp.mean(oh * oh, axis=-1, keepdims=True) + RMS_EPS) * g
        outs.append(yh * _silu(gate[:, h * width:(h + 1) * width]))
    return jnp.concatenate(outs, axis=1)


def _scan_call(kern, name, in_specs, args, o_f, *, rev, nbat, s1, nblk, state_shape, conv_width=0):
    blk = _scan_block(rev, nblk)
    ospec = pl.BlockSpec((nbat, TOK_BLK, BRANCH_W), lambda s: (0, blk(s), 0))
    out_specs = [ospec]
    out_shape = [jax.ShapeDtypeStruct((nbat, s1, BRANCH_W), BF16 if rev else F32)]
    if rev:
        in_specs = in_specs + [ospec]
        args = args + [o_f]
    elif conv_width:
        out_specs.append(pl.BlockSpec((nbat, TOK_BLK, conv_width), lambda s: (0, s, 0)))
        out_shape.append(jax.ShapeDtypeStruct((nbat, s1, conv_width), F32))
    return pl.pallas_call(
        functools.partial(kern, rev=rev, nbat=nbat, nblk=nblk),
        grid=(nblk,),
        in_specs=in_specs,
        out_specs=out_specs,
        out_shape=out_shape,
        scratch_shapes=[pltpu.VMEM((nbat,) + state_shape, F32)],
        compiler_params=_cparams(("arbitrary",)),
        name=name + ("_bwd" if rev else "_fwd"),
    )(*args)


def _gla_kernel(q_ref, k_ref, v_ref, g_ref, lr_ref, cos_ref, sin_ref, a2_ref, ab_ref, lm_ref, ng_ref,
                *rest, rev, nbat, nblk):
    if rev:
        of_ref, y_ref, st_ref = rest
    else:
        o_ref, st_ref = rest

    @pl.when(pl.program_id(0) == 0)
    def _():
        st_ref[...] = jnp.zeros_like(st_ref)

    hk = GLA_H * GLA_DK
    lane = _iota2((TOK_BLK, hk), 1)
    first_half = _imod(lane, GLA_DK) < (GLA_DK // 2)
    head_of_lane = _idiv(lane, GLA_DK)
    cos = cos_ref[...]
    sin = sin_ref[...]
    mask = _chunk_mask(rev)
    n_chunk = TOK_BLK // CHUNK
    batch = range(nbat)

    def rope(x):
        partner = jnp.where(first_half, pltpu.roll(x, hk - GLA_DK // 2, 1), pltpu.roll(x, GLA_DK // 2, 1))
        return x * cos + partner * sin

    ks, vs, bs, q_ins, o_intras = [], [], [], [], []
    for bi in batch:
        q = rope(q_ref[bi]) * GLA_DK ** -0.5
        k = rope(k_ref[bi])
        v = v_ref[bi]
        loga = -_softplus(-(_dot(lr_ref[bi], a2_ref[...]) + ab_ref[...])) / GLA_TAU
        b = _select_rows(lm_ref[...], loga)
        q_in = q * jnp.exp(b)
        k_in = k * jnp.exp(-b)
        o_heads = []
        for h in range(GLA_H):
            att = _dot_nt(jnp.where(head_of_lane == h, q_in, 0.0), k_in)
            att = jnp.where(mask, att, 0.0)
            o_heads.append(_dot(att, v[:, h * GLA_DV:(h + 1) * GLA_DV]))
        ks.append(k)
        vs.append(v)
        bs.append(b)
        q_ins.append(q_in)
        o_intras.append(jnp.concatenate(o_heads, axis=1))

    sts = [st_ref[bi] for bi in batch]
    diag = _idiv(_iota2(sts[0].shape, 0), GLA_DV) == _idiv(_iota2(sts[0].shape, 1), GLA_DK)
    o_inter = [[None] * n_chunk for _ in batch]
    for c in _chunk_order(rev):
        rows = slice(c * CHUNK, (c + 1) * CHUNK)
        for bi in batch:
            b_c = bs[bi][rows]
            b_last = b_c[0:1] if rev else b_c[CHUNK - 1:CHUNK]
            o_inter[bi][c] = _dot_nt(q_ins[bi][rows], sts[bi])
            k_end = ks[bi][rows] * jnp.exp(b_last - b_c)
            ds = _dot_tn(vs[bi][rows], k_end)
            sts[bi] = sts[bi] * jnp.exp(b_last) + jnp.where(diag, ds, 0.0)
    for bi in batch:
        st_ref[bi] = sts[bi]
        o = o_intras[bi] + jnp.concatenate(o_inter[bi], axis=0)
        if rev:
            y = _head_rmsnorm_gate(of_ref[bi] + o, ng_ref[...], g_ref[bi], GLA_H, GLA_DV)
            y_ref[bi] = y.astype(y_ref.dtype)
        else:
            o_ref[bi] = o


def _cumsum_matrix(rev):
    r = np.arange(TOK_BLK)[:, None]
    c = np.arange(TOK_BLK)[None, :]
    same = (r // CHUNK) == (c // CHUNK)
    tri = (c >= r) if rev else (c <= r)
    return jnp.asarray((same & tri).astype(np.float32), dtype=BF16)


def _gla_dir(p, cos, sin, a2, ab, ng, o_f, *, rev, nblk):
    nbat, s1, _ = p.shape
    blk = _scan_block(rev, nblk)
    d = 1 if rev else 0
    a2d = jnp.zeros((LANE, GLA_H * GLA_DK), F32).at[d * GLA_LR:(d + 1) * GLA_LR].set(a2[d])
    tab = pl.BlockSpec((TOK_BLK, GLA_H * GLA_DK), lambda s: (blk(s), 0))
    in_specs = [_blk_spec(nbat, 256, P_GLA_Q, blk), _blk_spec(nbat, 256, P_GLA_K, blk),
                _blk_spec(nbat, 512, P_GLA_V, blk), _blk_spec(nbat, 512, P_GLA_G, blk),
                _blk_spec(nbat, LANE, P_GLA_LR, blk), tab, tab,
                _const_spec((LANE, 256)), _const_spec((1, 256)), _const_spec((TOK_BLK, TOK_BLK)),
                _const_spec((1, GLA_DV))]
    args = [p, p, p, p, p, cos, sin, a2d, ab[d].reshape(1, -1), _cumsum_matrix(rev), ng.reshape(1, -1)]
    return _scan_call(_gla_kernel, "gla", in_specs, args, o_f, rev=rev, nbat=nbat, s1=s1, nblk=nblk,
                      state_shape=(GLA_H * GLA_DV, GLA_H * GLA_DK))[0]


def _rope_tables(lc, t):
    n_freq = GLA_DK // 4
    freqs = ROPE_BASE ** (-jnp.arange(n_freq, dtype=F32) / n_freq)
    tt = jnp.arange(t)
    row = (tt // GRID_W).astype(F32)
    col = (tt % GRID_W).astype(F32)
    ang = jnp.concatenate([row[:, None] * freqs, col[:, None] * freqs], axis=-1)
    cos, sin = jnp.cos(ang), jnp.sin(ang)
    cos = jnp.concatenate([jnp.ones((lc, GLA_DK // 2), F32), cos], axis=0)
    sin = jnp.concatenate([jnp.zeros((lc, GLA_DK // 2), F32), sin], axis=0)
    cos_h = jnp.concatenate([cos, cos], axis=1)
    sin_h = jnp.concatenate([-sin, sin], axis=1)
    return jnp.tile(cos_h, (1, GLA_H)), jnp.tile(sin_h, (1, GLA_H))


def _na_kernel(q_ref, kp_ref, kc_ref, kn_ref, kx_ref, vp_ref, vc_ref, vn_ref, vx_ref, bias_ref, y_ref, *, nbat):
    scale = NA_D ** -0.5
    for bi in range(nbat):
        outs = []
        for h in range(NA_H):
            hs = slice(h * NA_D, (h + 1) * NA_D)
            qh = (q_ref[bi, :, hs] * scale).astype(BF16)
            s = jnp.concatenate(
                [_dot_nt(qh, kp_ref[bi, :, hs]) + bias_ref[0, h],
                 _dot_nt(qh, kc_ref[bi, :, hs]) + bias_ref[1, h],
                 _dot_nt(qh, kn_ref[bi, :, hs]) + bias_ref[2, h],
                 _dot_nt(qh, kx_ref[bi, :, hs])], axis=1)
            m = jnp.max(s, axis=-1, keepdims=True)
            e = jnp.exp(s - m)
            p = e / jnp.sum(e, axis=-1, keepdims=True)
            o = (_dot(p[:, 0:TOK_BLK], vp_ref[bi, :, hs]) + _dot(p[:, TOK_BLK:2 * TOK_BLK], vc_ref[bi, :, hs])
                 + _dot(p[:, 2 * TOK_BLK:3 * TOK_BLK], vn_ref[bi, :, hs])
                 + _dot(p[:, 3 * TOK_BLK:], vx_ref[bi, :, hs]))
            outs.append(o)
        y_ref[bi] = jnp.concatenate(outs, axis=1).astype(y_ref.dtype)


def _na_bias_tiles(rpb):
    edge = GRID_W - NA_WIN_C
    ext = jnp.concatenate([jnp.repeat(rpb[..., :1], edge, axis=-1), rpb,
                           jnp.repeat(rpb[..., -1:], edge, axis=-1)], axis=-1)
    toep = jnp.stack([ext[..., GRID_W - 1 - qc:2 * GRID_W - 1 - qc] for qc in range(GRID_W)], axis=-2)
    qc = np.arange(GRID_W)[:, None]
    kc = np.arange(GRID_W)[None, :]
    c0 = np.clip(qc - NA_WIN_C // 2, 0, GRID_W - NA_WIN_C)
    col_ok = (kc >= c0) & (kc < c0 + NA_WIN_C)
    toep = jnp.where(jnp.asarray(col_ok), toep, NEG_INF).astype(F32)
    masked = jnp.full((NA_H, GRID_W, GRID_W), NEG_INF, F32)
    kinds = []
    for kind in range(3):
        offs = []
        for off in (-1, 0, 1):
            rows = []
            for qr in range(ROWS_PER_BLK):
                start = (0, qr - NA_WIN_R // 2, ROWS_PER_BLK - NA_WIN_R)[kind]
                cols = []
                for kb in range(ROWS_PER_BLK):
                    kr = kb + ROWS_PER_BLK * off
                    ok = start <= kr < start + NA_WIN_R
                    cols.append(toep[:, kr - qr + NA_WIN_R - 1] if ok else masked)
                rows.append(jnp.concatenate(cols, axis=-1))
            offs.append(jnp.concatenate(rows, axis=-2))
        kinds.append(jnp.stack(offs))
    kinds.append(jnp.full_like(kinds[0], NEG_INF))
    return jnp.stack(kinds)


def _na(p, bias, *, nblk):
    nbat, s1, _ = p.shape

    def kind(s):
        return jnp.where(s == 0, 3, jnp.where(s == 1, 0, jnp.where(s == nblk - 1, 2, 1)))

    prev = lambda s: jnp.maximum(s - 1, 1)
    cur = lambda s: s
    nxt = lambda s: jnp.minimum(s + 1, nblk - 1)
    ctx = lambda s: 0
    spec = lambda col, blk: _blk_spec(nbat, BRANCH_W, col, blk)
    return pl.pallas_call(
        functools.partial(_na_kernel, nbat=nbat),
        grid=(nblk,),
        in_specs=[spec(P_NA_Q, cur),
                  spec(P_NA_K, prev), spec(P_NA_K, cur), spec(P_NA_K, nxt), spec(P_NA_K, ctx),
                  spec(P_NA_V, prev), spec(P_NA_V, cur), spec(P_NA_V, nxt), spec(P_NA_V, ctx),
                  pl.BlockSpec((None, 3, NA_H, TOK_BLK, TOK_BLK), lambda s: (kind(s), 0, 0, 0, 0))],
        out_specs=pl.BlockSpec((nbat, TOK_BLK, BRANCH_W), lambda s: (0, s, 0)),
        out_shape=jax.ShapeDtypeStruct((nbat, s1, BRANCH_W), BF16),
        compiler_params=_cparams(("arbitrary",)),
        name="nbr_attn",
    )(p, p, p, p, p, p, p, p, p, bias)


def _l2norm(x):
    return x * lax.rsqrt(jnp.sum(x * x, axis=-1, keepdims=True) + RMS_EPS)


def _gdn_kernel(*refs, rev, nbat, nblk):
    if rev:
        qkv_ref, z_ref, sc_ref, alog_ref, dtb_ref, lm_ref, ng_ref, of_ref, y_ref, st_ref = refs
        qkv = [qkv_ref[bi] for bi in range(nbat)]
    else:
        (xm_ref, xp_ref, xn_ref, cw_ref, cb_ref, z_ref, sc_ref, alog_ref, dtb_ref, lm_ref, ng_ref,
         o_ref, qkvc_ref, st_ref) = refs
        qkv = []
        for bi in range(nbat):
            qkv.append(_conv_silu_block(xm_ref[bi], xp_ref[bi], xn_ref[bi], cw_ref[...], cb_ref[...],
                                        pl.program_id(0), nblk))
            qkvc_ref[bi] = qkv[bi]

    @pl.when(pl.program_id(0) == 0)
    def _():
        st_ref[...] = jnp.zeros_like(st_ref)

    d = 1 if rev else 0
    m_incl = _chunk_mask(rev)
    m_strict = _chunk_mask(rev, strict=True)
    eye = (_iota2((TOK_BLK, TOK_BLK), 0) == _iota2((TOK_BLK, TOK_BLK), 1)).astype(F32)
    n_chunk = TOK_BLK // CHUNK

    units = [(bi, h) for bi in range(nbat) for h in range(GDN_H)]
    qs, ks, bcols, ebcs, attns, nmats, rhss = [], [], [], [], [], [], []
    for bi in range(nbat):
        sc = sc_ref[bi]
        beta_all = _sigmoid(sc)
        g_all = -jnp.exp(alog_ref[...]) * _softplus(sc + dtb_ref[...])
        b_all = _select_rows(lm_ref[...], g_all)
        b_all_t = b_all.T
        eb_all = jnp.exp(b_all)
        for h in range(GDN_H):
            head = lambda part: qkv[bi][:, part * BRANCH_W + h * GDN_D:part * BRANCH_W + (h + 1) * GDN_D]
            qh = _l2norm(head(0)) * GDN_D ** -0.5
            kh = _l2norm(head(1))
            vh = head(2)
            lb, lg = GDN_H * d + h, 2 * GDN_H + GDN_H * d + h
            beta = beta_all[:, lb:lb + 1]
            bcol = b_all[:, lg:lg + 1]
            brow = b_all_t[lg:lg + 1, :]
            ebc = eb_all[:, lg:lg + 1]
            diff = bcol - brow
            dec_incl = jnp.where(m_incl, jnp.exp(jnp.where(m_incl, diff, 0.0)), 0.0)
            dec_strict = jnp.where(m_strict, dec_incl, 0.0)
            kk = _dot_nt(kh, kh)
            qs.append(qh)
            ks.append(kh)
            bcols.append(bcol)
            ebcs.append(ebc)
            attns.append(_dot_nt(qh, kh) * dec_incl)
            nmats.append(-(beta * kk * dec_strict))
            rhss.append(_split(jnp.concatenate([kh * (beta * ebc), vh * beta], axis=1)))

    bdot = lambda x, y: jnp.dot(x, y, preferred_element_type=F32)
    nsplit = [_split(n) for n in nmats]
    t0s = [eye + n for n in nmats]
    powers = [ns[0] for ns in nsplit]
    for _ in range(5):
        powers = [bdot(m, m).astype(BF16) for m in powers]
        t0s = [t + bdot(t.astype(BF16), m) for t, m in zip(t0s, powers)]
    t0split = [_split(t) for t in t0s]
    resid = [(eye - t) + _dot_split(ns, ts) for t, ns, ts in zip(t0s, nsplit, t0split)]
    tinvs = [t + bdot(ts[0], r.astype(BF16)) for t, ts, r in zip(t0s, t0split, resid)]
    sols = [_dot_split(_split(t), r) for t, r in zip(tinvs, rhss)]
    ws = [s[:, :GDN_D] for s in sols]
    u0s = [s[:, GDN_D:] for s in sols]

    sts = [st_ref[bi, h] for bi, h in units]
    u_parts = [[None] * n_chunk for _ in units]
    o_parts = [[None] * n_chunk for _ in units]
    for c in _chunk_order(rev):
        rows = slice(c * CHUNK, (c + 1) * CHUNK)
        last = c * CHUNK if rev else (c + 1) * CHUNK - 1
        for i in range(len(units)):
            st = sts[i]
            b_last = bcols[i][last:last + 1]
            u = u0s[i][rows] - _dot(ws[i][rows], st)
            u_parts[i][c] = u
            o_parts[i][c] = ebcs[i][rows] * _dot(qs[i][rows], st)
            k_end = ks[i][rows] * jnp.exp(b_last - bcols[i][rows])
            sts[i] = jnp.exp(b_last) * st + _dot_tn(k_end, u)
    o_units = []
    for i, (bi, h) in enumerate(units):
        st_ref[bi, h] = sts[i]
        o_units.append(jnp.concatenate(o_parts[i], axis=0) + _dot(attns[i], jnp.concatenate(u_parts[i], axis=0)))
    for bi in range(nbat):
        o = jnp.concatenate(o_units[bi * GDN_H:(bi + 1) * GDN_H], axis=1)
        if rev:
            y = _head_rmsnorm_gate(of_ref[bi] + o, ng_ref[...], z_ref[bi], GDN_H, GDN_D)
            y_ref[bi] = y.astype(y_ref.dtype)
        else:
            o_ref[bi] = o


def _gdn_dir(p, qkv, conv_w, a_log, dt_bias, ng, o_f, *, rev, nblk):
    nbat, s1, _ = p.shape
    blk = _scan_block(rev, nblk)
    d = 1 if rev else 0
    width = 3 * BRANCH_W
    lane0 = 2 * GDN_H + GDN_H * d
    alog_row = jnp.zeros((1, LANE), F32).at[0, lane0:lane0 + GDN_H].set(a_log[d])
    dtb_row = jnp.zeros((1, LANE), F32).at[0, lane0:lane0 + GDN_H].set(dt_bias[d])
    if rev:
        in_specs, args = [_blk_spec(nbat, width, 0, blk)], [qkv]
    else:
        in_specs, args = _conv_inputs(p, P_GDN_QKV, conv_w, jnp.zeros((width,), F32), nblk)
    in_specs += [_blk_spec(nbat, 512, P_GDN_Z, blk), _blk_spec(nbat, LANE, P_GDN_SC, blk),
                 _const_spec((1, LANE)), _const_spec((1, LANE)), _const_spec((TOK_BLK, TOK_BLK)),
                 _const_spec((1, GDN_D))]
    args += [p, p, alog_row, dtb_row, _cumsum_matrix(rev), ng.reshape(1, -1)]
    return _scan_call(_gdn_kernel, "gdn", in_specs, args, o_f, rev=rev, nbat=nbat, s1=s1, nblk=nblk,
                      state_shape=(GDN_H, GDN_D, GDN_D), conv_width=width)


def _ssd_kernel(*refs, rev, nbat, nblk):
    if rev:
        (xbc_ref, dt_ref, z_ref, alog_ref, dtb_ref, ex_ref, lm_ref, dsk_ref, ng_ref, of_ref,
         y_ref, st_ref) = refs
        xbc = [xbc_ref[bi] for bi in range(nbat)]
    else:
        (xm_ref, xp_ref, xn_ref, cw_ref, cb_ref, dt_ref, z_ref, alog_ref, dtb_ref, ex_ref, lm_ref, dsk_ref, ng_ref,
         o_ref, xbcc_ref, st_ref) = refs
        xbc = []
        for bi in range(nbat):
            xbc.append(_conv_silu_block(xm_ref[bi], xp_ref[bi], xn_ref[bi], cw_ref[...], cb_ref[...],
                                        pl.program_id(0), nblk))
            xbcc_ref[bi] = xbc[bi]

    @pl.when(pl.program_id(0) == 0)
    def _():
        st_ref[...] = jnp.zeros_like(st_ref)

    d = 1 if rev else 0
    heads_per_g = M2_H // M2_G
    gw = heads_per_g * M2_P
    n_chunk = TOK_BLK // CHUNK
    mask = _chunk_mask(rev)
    lane = _iota2((TOK_BLK, gw), 1)
    ex = ex_ref[...]

    units = [(bi, g) for bi in range(nbat) for g in range(M2_G)]
    cqs, bks, xvs, b_es, eb_es, accs = [], [], [], [], [], []
    for bi in range(nbat):
        dt = _softplus(dt_ref[bi] + dtb_ref[...])
        loga = -jnp.exp(alog_ref[...]) * dt
        b8 = _select_rows(lm_ref[...], loga)
        b8_t = b8.T
        xv = xbc[bi][:, :BRANCH_W] * _select_cols(dt, ex)
        b_e = _select_cols(b8, ex)
        eb_e = jnp.exp(b_e)
        for g in range(M2_G):
            gs = slice(g * M2_N, (g + 1) * M2_N)
            xs_g = slice(g * gw, (g + 1) * gw)
            bk = xbc[bi][:, BRANCH_W + g * M2_N:BRANCH_W + (g + 1) * M2_N]
            cq = xbc[bi][:, BRANCH_W + (M2_G + g) * M2_N:BRANCH_W + (M2_G + g + 1) * M2_N]
            scores = _dot_nt(cq, bk)
            xv_g = xv[:, xs_g]
            acc = None
            for hh in range(heads_per_g):
                lh = M2_H * d + heads_per_g * g + hh
                diff = b8[:, lh:lh + 1] - b8_t[lh:lh + 1, :]
                dec = jnp.where(mask, jnp.exp(jnp.where(mask, diff, 0.0)), 0.0)
                term = _dot(scores * dec, jnp.where(_idiv(lane, M2_P) == hh, xv_g, 0.0))
                acc = term if acc is None else acc + term
            cqs.append(cq)
            bks.append(bk)
            xvs.append(xv_g)
            b_es.append(b_e[:, xs_g])
            eb_es.append(eb_e[:, xs_g])
            accs.append(acc)

    sts = [st_ref[bi, g] for bi, g in units]
    o_parts = [[None] * n_chunk for _ in units]
    for c in _chunk_order(rev):
        rows = slice(c * CHUNK, (c + 1) * CHUNK)
        last = c * CHUNK if rev else (c + 1) * CHUNK - 1
        for i in range(len(units)):
            b_c = b_es[i][rows]
            b_last = b_es[i][last:last + 1]
            o_parts[i][c] = eb_es[i][rows] * _dot(cqs[i][rows], sts[i])
            ds = _dot_tn(bks[i][rows], xvs[i][rows] * jnp.exp(b_last - b_c))
            sts[i] = jnp.exp(b_last) * sts[i] + ds
    o_units = []
    for i, (bi, g) in enumerate(units):
        st_ref[bi, g] = sts[i]
        o_units.append(accs[i] + jnp.concatenate(o_parts[i], axis=0))
    for bi in range(nbat):
        o = jnp.concatenate(o_units[bi * M2_G:(bi + 1) * M2_G], axis=1)
        if rev:
            y = (of_ref[bi] + o + dsk_ref[...] * xbc[bi][:, :BRANCH_W]) * _silu(z_ref[bi])
            y = y * lax.rsqrt(jnp.mean(y * y, axis=-1, keepdims=True) + RMS_EPS) * ng_ref[...]
            y_ref[bi] = y.astype(y_ref.dtype)
        else:
            o_ref[bi] = o


def _ssd_dir(p, xbc, conv_w, conv_b, a_log, dt_bias, d_skip, ng, o_f, *, rev, nblk):
    nbat, s1, _ = p.shape
    blk = _scan_block(rev, nblk)
    d = 1 if rev else 0
    lane0 = M2_H * d
    alog_row = jnp.zeros((1, LANE), F32).at[0, lane0:lane0 + M2_H].set(a_log[d])
    dtb_row = jnp.zeros((1, LANE), F32).at[0, lane0:lane0 + M2_H].set(dt_bias[d])
    ex = np.zeros((LANE, BRANCH_W), np.float32)
    for h in range(M2_H):
        ex[lane0 + h, h * M2_P:(h + 1) * M2_P] = 1.0
    dsk_row = jnp.repeat(d_skip, M2_P).reshape(1, BRANCH_W)
    if rev:
        in_specs, args = [_blk_spec(nbat, M2_CONV_CH, 0, blk)], [xbc]
    else:
        in_specs, args = _conv_inputs(p, P_M2_XBC, conv_w, conv_b, nblk)
    in_specs += [_blk_spec(nbat, LANE, P_M2_DT, blk), _blk_spec(nbat, 512, P_M2_Z, blk),
                 _const_spec((1, LANE)), _const_spec((1, LANE)), _const_spec((LANE, BRANCH_W)),
                 _const_spec((TOK_BLK, TOK_BLK)), _const_spec((1, BRANCH_W)), _const_spec((1, BRANCH_W))]
    args += [p, p, alog_row, dtb_row, jnp.asarray(ex, dtype=BF16), _cumsum_matrix(rev), dsk_row, ng.reshape(1, -1)]
    return _scan_call(_ssd_kernel, "ssd", in_specs, args, o_f, rev=rev, nbat=nbat, s1=s1, nblk=nblk,
                      state_shape=(M2_G, M2_N, (M2_H // M2_G) * M2_P), conv_width=M2_CONV_CH)


def _final_norm_kernel(x_ref, g_ref, o_ref):
    x = x_ref[...]
    o_ref[...] = x * lax.rsqrt(jnp.mean(x * x, axis=-1, keepdims=True) + RMS_EPS) * g_ref[...]


def _final_norm(xs, g, *, nbat, nblk, lc):
    D = xs.shape[1]
    cb = lc // TOK_BLK
    nlat = nblk - cb
    out = pl.pallas_call(
        _final_norm_kernel,
        grid=(nbat, nlat),
        in_specs=[pl.BlockSpec((TOK_BLK, D), lambda b, s: (b * nblk + cb + s, 0)),
                  pl.BlockSpec((1, D), lambda b, s: (0, 0))],
        out_specs=pl.BlockSpec((TOK_BLK, D), lambda b, s: (b * nlat + s, 0)),
        out_shape=jax.ShapeDtypeStruct((nbat * nlat * TOK_BLK, D), F32),
        compiler_params=_cparams(("arbitrary", "arbitrary")),
        name="final_norm",
    )(xs, g.reshape(1, D))
    return out.reshape(nbat, nlat * TOK_BLK, D)


def _row_tile(s1, limit):
    for tm in range(limit - limit % 16, 0, -16):
        if s1 % tm == 0:
            return tm
    raise ValueError(f"no row tile for sequence length {s1}")


def kernel(x, c, ctx, c_ctx, norm1_g, norm2_g, w_ada, b_ada, w_in, b_merge, gla_a2, gla_ab, gla_norm_g, na_rpb, gdn_conv, gdn_a_log, gdn_dt_bias, gdn_norm_g, m2_conv, m2_conv_b, m2_a_log, m2_dt_bias, m2_d, m2_norm_g, w_branch, w_out, w_ffn1, w_ffn3, w_ffn2, final_norm_g):
    nbat, t, D = x.shape
    lc = ctx.shape[1]
    depth = w_in.shape[0]
    assert D == D_MODEL and lc == TOK_BLK and t % TOK_BLK == 0 and t // GRID_W >= 3 * ROWS_PER_BLK
    assert nbat + 1 <= MOD_ROWS
    s1 = lc + t
    nblk = s1 // TOK_BLK
    tm = _row_tile(s1, 1056)
    geo = dict(s1=s1, lc=lc, nb=nbat)

    xs = jnp.concatenate([ctx, x], axis=1).reshape(nbat * s1, D)
    cvec = jnp.concatenate([c, c_ctx[None], jnp.zeros((MOD_ROWS - nbat - 1, D), F32)], axis=0)
    mods = _ada(cvec, w_ada, b_ada).reshape(depth, MOD_ROWS, 6, D).transpose(0, 2, 1, 3)
    cos, sin = _rope_tables(lc, t)
    flat = lambda y: y.reshape(nbat * s1, BRANCH_W)

    for l in range(depth):
        sh1, sc1, g1, sh2, sc2, g2 = (mods[l, i] for i in range(6))

        h = _normmod(xs, norm1_g[l], sc1, sh1, tm=_row_tile(s1, 384), **geo)
        w_mix, w_gate = _split_in_weights(w_in[l].astype(BF16))
        p = _matmul(h, w_mix, tm=tm, tn=MIX_TILE, out_dtype=F32).reshape(nbat, s1, P_COLS)

        o_f = _gla_dir(p, cos, sin, gla_a2[l], gla_ab[l], gla_norm_g[l], None, rev=False, nblk=nblk)
        ya = _gla_dir(p, cos, sin, gla_a2[l], gla_ab[l], gla_norm_g[l], o_f, rev=True, nblk=nblk)

        yb = _na(p, _na_bias_tiles(na_rpb[l]), nblk=nblk)

        gdn_par = (gdn_conv[l], gdn_a_log[l], gdn_dt_bias[l], gdn_norm_g[l])
        o_f, qkv = _gdn_dir(p, None, *gdn_par, None, rev=False, nblk=nblk)
        yc, = _gdn_dir(p, qkv, *gdn_par, o_f, rev=True, nblk=nblk)

        ssd_par = (m2_conv[l], m2_conv_b[l], m2_a_log[l], m2_dt_bias[l], m2_d[l], m2_norm_g[l])
        o_f, xbc = _ssd_dir(p, None, *ssd_par, None, rev=False, nblk=nblk)
        yd, = _ssd_dir(p, xbc, *ssd_par, o_f, rev=True, nblk=nblk)

        ys = (flat(ya), flat(yb), flat(yc), flat(yd))
        merged = _merge(h, w_gate, b_merge[l], ys, w_branch[l].astype(BF16), tm=tm, tn=512)
        xs, h2 = _outproj_norm(merged, w_out[l].astype(BF16), xs, g1, norm2_g[l], sc2, sh2,
                               tm=_row_tile(s1, 384), **geo)

        u = _ffn_up(h2, w_ffn1[l].astype(BF16), w_ffn3[l].astype(BF16), tm=_row_tile(s1, 2112), tn=512)
        xs = _matmul_residual(u, w_ffn2[l].astype(BF16), xs, g2, tm=tm, tn=512, name="ffn_down", **geo)

    return _final_norm(xs, final_norm_g, nbat=nbat, nblk=nblk, lc=lc)
```

```python
import functools

import numpy as np
import jax
import jax.numpy as jnp
from jax import lax
from jax.experimental import pallas as pl
from jax.experimental.pallas import tpu as pltpu

D_MODEL = 2048
GRID_W = 64
N_BRANCH = 4
BRANCH_W = D_MODEL // 4
CHUNK = 64
CONV_W = 5
RMS_EPS = 1e-6
NEG_INF = -1e30
ROPE_BASE = 10000.0
GLA_H = 4
GLA_DV = BRANCH_W // GLA_H
GLA_DK = GLA_DV // 2
GLA_LR = 16
GLA_TAU = 16.0
NA_H = 4
NA_D = BRANCH_W // NA_H
NA_WIN_R = 8
NA_WIN_C = 16
GDN_H = 4
GDN_D = BRANCH_W // GDN_H
M2_P = 64
M2_H = BRANCH_W // M2_P
M2_N = 128
M2_G = 2
M2_CONV_CH = BRANCH_W + 2 * M2_G * M2_N
D_FF = ((8 * D_MODEL + 3 * 256 - 1) // (3 * 256)) * 256
GLA_IN = 2 * GLA_H * GLA_DK + 2 * BRANCH_W + 2 * GLA_LR
NA_IN = 3 * BRANCH_W
GDN_IN = 4 * BRANCH_W + 4 * GDN_H
M2_IN = BRANCH_W + M2_CONV_CH + 2 * M2_H
MIX_IN = GLA_IN + NA_IN + GDN_IN + M2_IN

F32 = jnp.float32
BF16 = jnp.bfloat16

LANE = 128
SUBLANE = 8
V7X_VMEM_BYTES = 64 * 1024 * 1024
VMEM_LIMIT = V7X_VMEM_BYTES - 8 * 1024 * 1024

TOK_BLK = 4 * CHUNK
ROWS_PER_BLK = TOK_BLK // GRID_W
HALO = SUBLANE
MOD_ROWS = 8

P_GDN_QKV = 0
P_GLA_V = 1536
P_M2_XBC = 2048
P_GLA_G = 3072
P_NA_Q, P_NA_K, P_NA_V = 3584, 4096, 4608
P_GDN_Z, P_M2_Z = 5120, 5632
P_GLA_Q, P_GLA_K = 6144, 6400
P_GLA_LR, P_GDN_SC, P_M2_DT = 6656, 6784, 6912
P_COLS = 7168
MIX_TILE = 1024


def _mix_fields():
    gla, na, gdn, m2 = 0, GLA_IN, GLA_IN + NA_IN, GLA_IN + NA_IN + GDN_IN
    return sorted([
        (P_GLA_V, gla + 512, 512), (P_GLA_G, gla + 1024, 512),
        (P_NA_Q, na, 512), (P_NA_K, na + 512, 512), (P_NA_V, na + 1024, 512),
        (P_GDN_QKV, gdn, 1536), (P_GDN_Z, gdn + 1536, 512),
        (P_M2_Z, m2, 512), (P_M2_XBC, m2 + 512, M2_CONV_CH),
        (P_GLA_Q, gla, 256), (P_GLA_K, gla + 256, 256),
        (P_GLA_LR, gla + 1536, 2 * GLA_LR), (P_GDN_SC, gdn + 2048, 4 * GDN_H),
        (P_M2_DT, m2 + 512 + M2_CONV_CH, 2 * M2_H)])


def _cparams(sem):
    return pltpu.CompilerParams(dimension_semantics=sem, vmem_limit_bytes=VMEM_LIMIT)


def _wprep_kernel(w_ref, mix_ref, gate_ref):
    col = 0
    for dst, src, n in _mix_fields():
        if dst > col:
            mix_ref[:, col:dst] = jnp.zeros((mix_ref.shape[0], dst - col), mix_ref.dtype)
        mix_ref[:, dst:dst + n] = w_ref[:, src:src + n].astype(mix_ref.dtype)
        col = dst + n
    mix_ref[:, col:] = jnp.zeros((mix_ref.shape[0], P_COLS - col), mix_ref.dtype)
    gate_ref[...] = w_ref[:, MIX_IN:].astype(gate_ref.dtype)


def _split_in_weights(w_in, layer):
    _, D, N = w_in.shape
    rb = 256
    return pl.pallas_call(
        _wprep_kernel,
        grid=(D // rb,),
        in_specs=[pl.BlockSpec((None, rb, N), lambda r: (layer, r, 0))],
        out_specs=[pl.BlockSpec((rb, P_COLS), lambda r: (r, 0)),
                   pl.BlockSpec((rb, N - MIX_IN), lambda r: (r, 0))],
        out_shape=[jax.ShapeDtypeStruct((D, P_COLS), BF16), jax.ShapeDtypeStruct((D, N - MIX_IN), BF16)],
        compiler_params=_cparams(("arbitrary",)),
        name="split_in_weights",
    )(w_in)


def _sigmoid(x):
    return 1.0 / (1.0 + jnp.exp(-x))


def _silu(x):
    return x * _sigmoid(x)


def _softplus(x):
    return jnp.maximum(x, 0.0) + jnp.log1p(jnp.exp(-jnp.abs(x)))


def _dot(a, b):
    return jnp.dot(a.astype(BF16), b.astype(BF16), preferred_element_type=F32)


def _dot_nt(a, b):
    return lax.dot_general(a.astype(BF16), b.astype(BF16), (((1,), (1,)), ((), ())),
                           preferred_element_type=F32)


def _dot_tn(a, b):
    return lax.dot_general(a.astype(BF16), b.astype(BF16), (((0,), (0,)), ((), ())),
                           preferred_element_type=F32)


def _split3(a):
    hi = a.astype(BF16)
    r1 = a - hi.astype(F32)
    mid = r1.astype(BF16)
    return hi, mid, (r1 - mid.astype(F32)).astype(BF16)


def _select_rows(m01, a):
    d = lambda y: jnp.dot(m01, y, preferred_element_type=F32)
    hi, mid, lo = _split3(a)
    return d(hi) + (d(mid) + d(lo))


def _select_cols(a, m01):
    d = lambda y: jnp.dot(y, m01, preferred_element_type=F32)
    hi, mid, lo = _split3(a)
    return d(hi) + (d(mid) + d(lo))


def _split(a):
    hi = a.astype(BF16)
    return hi, (a - hi.astype(F32)).astype(BF16)


def _dot_split(a, b):
    (ah, al), (bh, bl) = a, b
    d = lambda x, y: jnp.dot(x, y, preferred_element_type=F32)
    return d(ah, bh) + (d(ah, bl) + d(al, bh))


def _iota2(shape, dim):
    return lax.broadcasted_iota(jnp.int32, shape, dim)


def _idiv(x, n):
    assert n & (n - 1) == 0
    return lax.shift_right_logical(x, jnp.int32(n.bit_length() - 1))


def _imod(x, n):
    assert n & (n - 1) == 0
    return x & (n - 1)


def _chunk_mask(rev, strict=False):
    r = _iota2((TOK_BLK, TOK_BLK), 0)
    c = _iota2((TOK_BLK, TOK_BLK), 1)
    same = _idiv(r, CHUNK) == _idiv(c, CHUNK)
    if rev:
        tri = (c > r) if strict else (c >= r)
    else:
        tri = (c < r) if strict else (c <= r)
    return same & tri


def _chunk_order(rev):
    n = TOK_BLK // CHUNK
    return list(range(n - 1, -1, -1)) if rev else list(range(n))


def _row_split(i, tm, s1):
    r0 = i * tm
    return lax.div(r0, jnp.int32(s1)), lax.rem(r0, jnp.int32(s1))


def _pick_mod(ref, b, nb, is_ctx):
    return jnp.where(is_ctx, ref[nb:nb + 1, :], ref[pl.ds(b, 1), :])


def _ada_kernel(c_ref, w_ref, b_ref, o_ref):
    a = _silu(c_ref[...])
    o_ref[...] = _dot(a, w_ref[...]) + b_ref[...]


def _ada(cvec, w_ada, b_ada):
    L, D, N = w_ada.shape
    tn = 1024
    return pl.pallas_call(
        _ada_kernel,
        grid=(L, N // tn),
        in_specs=[pl.BlockSpec((MOD_ROWS, D), lambda l, j: (0, 0)),
                  pl.BlockSpec((None, D, tn), lambda l, j: (l, 0, j)),
                  pl.BlockSpec((None, 1, tn), lambda l, j: (l, 0, j))],
        out_specs=pl.BlockSpec((None, MOD_ROWS, tn), lambda l, j: (l, 0, j)),
        out_shape=jax.ShapeDtypeStruct((L, MOD_ROWS, N), F32),
        compiler_params=_cparams(("arbitrary", "arbitrary")),
        name="ada_mod",
    )(cvec, w_ada, b_ada.reshape(L, 1, N))


def _normmod_kernel(x_ref, g_ref, sc_ref, sh_ref, o_ref, *, tm, s1, lc, nb):
    b, pos0 = _row_split(pl.program_id(0), tm, s1)
    is_ctx = (pos0 + _iota2((tm, 1), 0)) < lc
    x = x_ref[...]
    y = x * lax.rsqrt(jnp.mean(x * x, axis=-1, keepdims=True) + RMS_EPS) * g_ref[...]
    sc = _pick_mod(sc_ref, b, nb, is_ctx)
    sh = _pick_mod(sh_ref, b, nb, is_ctx)
    o_ref[...] = (y * (1.0 + sc) + sh).astype(o_ref.dtype)


def _normmod(xs, g, sc, sh, *, tm, s1, lc, nb):
    R, D = xs.shape
    return pl.pallas_call(
        functools.partial(_normmod_kernel, tm=tm, s1=s1, lc=lc, nb=nb),
        grid=(R // tm,),
        in_specs=[pl.BlockSpec((tm, D), lambda i: (i, 0)),
                  pl.BlockSpec((1, D), lambda i: (0, 0)),
                  pl.BlockSpec((MOD_ROWS, D), lambda i: (0, 0)),
                  pl.BlockSpec((MOD_ROWS, D), lambda i: (0, 0))],
        out_specs=pl.BlockSpec((tm, D), lambda i: (i, 0)),
        out_shape=jax.ShapeDtypeStruct((R, D), BF16),
        compiler_params=_cparams(("arbitrary",)),
        name="norm_mod",
    )(xs, g.reshape(1, D), sc, sh)


def _mm_kernel(a_ref, w_ref, o_ref):
    o_ref[...] = jnp.dot(a_ref[...], w_ref[...], preferred_element_type=F32).astype(o_ref.dtype)


def _matmul(a, w, *, tm, tn, out_dtype):
    R, K = a.shape
    N = w.shape[1]
    assert R % tm == 0 and N % tn == 0
    return pl.pallas_call(
        _mm_kernel,
        grid=(R // tm, N // tn),
        in_specs=[pl.BlockSpec((tm, K), lambda i, j: (i, 0)),
                  pl.BlockSpec((K, tn), lambda i, j: (0, j))],
        out_specs=pl.BlockSpec((tm, tn), lambda i, j: (i, j)),
        out_shape=jax.ShapeDtypeStruct((R, N), out_dtype),
        compiler_params=_cparams(("arbitrary", "arbitrary")),
        name="mix_proj",
    )(a, w)


def _mm_res_kernel(a_ref, w_ref, x_ref, gt_ref, o_ref, *, tm, s1, lc, nb):
    b, pos0 = _row_split(pl.program_id(0), tm, s1)
    is_ctx = (pos0 + _iota2((tm, 1), 0)) < lc
    acc = jnp.dot(a_ref[...], w_ref[...], preferred_element_type=F32)
    o_ref[...] = x_ref[...] + _pick_mod(gt_ref, b, nb, is_ctx) * acc


def _matmul_residual(a, w, layer, xs, gate, *, tm, tn, s1, lc, nb, name):
    R, K = a.shape
    N = w.shape[2]
    assert R % tm == 0 and N % tn == 0
    return pl.pallas_call(
        functools.partial(_mm_res_kernel, tm=tm, s1=s1, lc=lc, nb=nb),
        grid=(R // tm, N // tn),
        in_specs=[pl.BlockSpec((tm, K), lambda i, j: (i, 0)),
                  pl.BlockSpec((None, K, tn), lambda i, j: (layer, 0, j)),
                  pl.BlockSpec((tm, tn), lambda i, j: (i, j)),
                  pl.BlockSpec((MOD_ROWS, tn), lambda i, j: (0, j))],
        out_specs=pl.BlockSpec((tm, tn), lambda i, j: (i, j)),
        out_shape=jax.ShapeDtypeStruct((R, N), F32),
        compiler_params=_cparams(("arbitrary", "arbitrary")),
        name=name,
    )(a, w, xs, gate)


def _outproj_norm_kernel(a_ref, w_ref, x_ref, gt_ref, g_ref, sc_ref, sh_ref, xo_ref, h_ref, *, tm, s1, lc, nb):
    b, pos0 = _row_split(pl.program_id(0), tm, s1)
    is_ctx = (pos0 + _iota2((tm, 1), 0)) < lc
    acc = jnp.dot(a_ref[...], w_ref[...], preferred_element_type=F32)
    x = x_ref[...] + _pick_mod(gt_ref, b, nb, is_ctx) * acc
    xo_ref[...] = x
    y = x * lax.rsqrt(jnp.mean(x * x, axis=-1, keepdims=True) + RMS_EPS) * g_ref[...]
    h_ref[...] = (y * (1.0 + _pick_mod(sc_ref, b, nb, is_ctx)) + _pick_mod(sh_ref, b, nb, is_ctx)).astype(h_ref.dtype)


def _outproj_norm(a, w, layer, xs, gate, g, sc, sh, *, tm, s1, lc, nb):
    R, K = a.shape
    D = w.shape[2]
    row = lambda i: (i, 0)
    const = lambda i: (0, 0)
    return pl.pallas_call(
        functools.partial(_outproj_norm_kernel, tm=tm, s1=s1, lc=lc, nb=nb),
        grid=(R // tm,),
        in_specs=[pl.BlockSpec((tm, K), row), pl.BlockSpec((None, K, D), lambda i: (layer, 0, 0)),
                  pl.BlockSpec((tm, D), row),
                  pl.BlockSpec((MOD_ROWS, D), const), pl.BlockSpec((1, D), const),
                  pl.BlockSpec((MOD_ROWS, D), const), pl.BlockSpec((MOD_ROWS, D), const)],
        out_specs=[pl.BlockSpec((tm, D), row), pl.BlockSpec((tm, D), row)],
        out_shape=[jax.ShapeDtypeStruct((R, D), F32), jax.ShapeDtypeStruct((R, D), BF16)],
        compiler_params=_cparams(("arbitrary",)),
        name="out_proj",
    )(a, w, xs, gate, g.reshape(1, D), sc, sh)


def _ffn_up_kernel(a_ref, w1_ref, w3_ref, o_ref):
    a = a_ref[...]
    u = jnp.dot(a, w1_ref[...], preferred_element_type=F32)
    v = jnp.dot(a, w3_ref[...], preferred_element_type=F32)
    o_ref[...] = (_silu(u) * v).astype(o_ref.dtype)


def _ffn_up(a, w1, w3, layer, *, tm, tn):
    R, K = a.shape
    N = w1.shape[2]
    assert R % tm == 0 and N % tn == 0
    wspec = pl.BlockSpec((None, K, tn), lambda i, j: (layer, 0, j))
    return pl.pallas_call(
        _ffn_up_kernel,
        grid=(R // tm, N // tn),
        in_specs=[pl.BlockSpec((tm, K), lambda i, j: (i, 0)), wspec, wspec],
        out_specs=pl.BlockSpec((tm, tn), lambda i, j: (i, j)),
        out_shape=jax.ShapeDtypeStruct((R, N), BF16),
        compiler_params=_cparams(("arbitrary", "arbitrary")),
        name="ffn_up",
    )(a, w1, w3)


def _merge_kernel(h_ref, g0_ref, g1_ref, g2_ref, g3_ref, bm_ref, ya_ref, yb_ref, yc_ref, yd_ref, wb_ref, o_ref):
    h = h_ref[...]
    acc = None
    branches = zip((g0_ref, g1_ref, g2_ref, g3_ref), (ya_ref, yb_ref, yc_ref, yd_ref))
    for i, (wg_ref, y_ref) in enumerate(branches):
        gate = _sigmoid(jnp.dot(h, wg_ref[...], preferred_element_type=F32) + bm_ref[i])
        term = gate * jnp.dot(y_ref[...], wb_ref[i], preferred_element_type=F32)
        acc = term if acc is None else acc + term
    o_ref[...] = acc.astype(o_ref.dtype)


def _merge(h, wg, b_merge, ys, wb, layer, *, tm, tn):
    R, D = h.shape
    N = wb.shape[3]
    assert R % tm == 0 and N % tn == 0
    nj = N // tn
    yspec = pl.BlockSpec((tm, BRANCH_W), lambda i, j: (i, 0))
    gspec = lambda br: pl.BlockSpec((D, tn), lambda i, j: (0, br * nj + j))
    return pl.pallas_call(
        _merge_kernel,
        grid=(R // tm, nj),
        in_specs=[pl.BlockSpec((tm, D), lambda i, j: (i, 0)),
                  gspec(0), gspec(1), gspec(2), gspec(3),
                  pl.BlockSpec((N_BRANCH, 1, tn), lambda i, j: (0, 0, j)),
                  yspec, yspec, yspec, yspec,
                  pl.BlockSpec((None, N_BRANCH, BRANCH_W, tn), lambda i, j: (layer, 0, 0, j))],
        out_specs=pl.BlockSpec((tm, tn), lambda i, j: (i, j)),
        out_shape=jax.ShapeDtypeStruct((R, N), BF16),
        compiler_params=_cparams(("arbitrary", "arbitrary")),
        name="merge",
    )(h, wg, wg, wg, wg, b_merge.reshape(N_BRANCH, 1, N), *ys, wb)


def _conv_silu_block(xm, xp, xn, w, bias, blk, nblk):
    prev_ok = blk >= 2
    next_ok = (blk >= 1) & (blk < nblk - 1)
    rows = TOK_BLK + 2 * HALO
    half = CONV_W // 2
    xpad = jnp.concatenate([jnp.where(prev_ok, xp, 0.0), xm, jnp.where(next_ok, xn, 0.0)], axis=0)
    acc = bias + w[half:half + 1, :] * xm
    for j in range(CONV_W):
        if j != half:
            tap = pltpu.roll(xpad, (half - j) % rows, 0)[HALO:HALO + TOK_BLK]
            acc = acc + w[j:j + 1, :] * tap
    return _silu(acc)


def _conv_inputs(p, col0, w, bias, nblk):
    nbat, s1, _ = p.shape
    C = w.shape[1]
    assert col0 % C == 0
    cb = col0 // C
    hb = TOK_BLK // HALO
    nhalo = s1 // HALO
    wpad = jnp.concatenate([w, jnp.zeros((SUBLANE - CONV_W, C), F32)], axis=0)
    specs = [pl.BlockSpec((nbat, TOK_BLK, C), lambda r: (0, r, cb)),
             pl.BlockSpec((nbat, HALO, C), lambda r: (0, jnp.maximum(r * hb - 1, 0), cb)),
             pl.BlockSpec((nbat, HALO, C), lambda r: (0, jnp.minimum((r + 1) * hb, nhalo - 1), cb)),
             _const_spec((SUBLANE, C)), _const_spec((1, C))]
    return specs, [p, p, p, wpad, bias.reshape(1, C)]


def _scan_block(rev, nblk):
    if rev:
        return lambda s: jnp.where(s == 0, 0, nblk - s)
    return lambda s: s


def _blk_spec(nbat, width, col, blk):
    assert col % width == 0
    return pl.BlockSpec((nbat, TOK_BLK, width), lambda s: (0, blk(s), col // width))


def _const_spec(shape):
    return pl.BlockSpec(shape, lambda s: (0,) * len(shape))


def _head_rmsnorm_gate(o, g, gate, n_head, width):
    outs = []
    for h in range(n_head):
        oh = o[:, h * width:(h + 1) * width]
        yh = oh * lax.rsqrt(jnp.mean(oh * oh, axis=-1, keepdims=True) + RMS_EPS) * g
        outs.append(yh * _silu(gate[:, h * width:(h + 1) * width]))
    return jnp.concatenate(outs, axis=1)


def _scan_call(kern, name, in_specs, args, o_f, *, rev, nbat, s1, nblk, state_shape, conv_width=0):
    blk = _scan_block(rev, nblk)
    ospec = pl.BlockSpec((nbat, TOK_BLK, BRANCH_W), lambda s: (0, blk(s), 0))
    out_specs = [ospec]
    out_shape = [jax.ShapeDtypeStruct((nbat, s1, BRANCH_W), BF16 if rev else F32)]
    if rev:
        in_specs = in_specs + [ospec]
        args = args + [o_f]
    elif conv_width:
        out_specs.append(pl.BlockSpec((nbat, TOK_BLK, conv_width), lambda s: (0, s, 0)))
        out_shape.append(jax.ShapeDtypeStruct((nbat, s1, conv_width), F32))
    return pl.pallas_call(
        functools.partial(kern, rev=rev, nbat=nbat, nblk=nblk),
        grid=(nblk,),
        in_specs=in_specs,
        out_specs=out_specs,
        out_shape=out_shape,
        scratch_shapes=[pltpu.VMEM((nbat,) + state_shape, F32)],
        compiler_params=_cparams(("arbitrary",)),
        name=name + ("_bwd" if rev else "_fwd"),
    )(*args)


def _gla_kernel(q_ref, k_ref, v_ref, g_ref, lr_ref, cos_ref, sin_ref, a2_ref, ab_ref, lm_ref, ng_ref,
                *rest, rev, nbat, nblk):
    if rev:
        of_ref, y_ref, st_ref = rest
    else:
        o_ref, st_ref = rest

    @pl.when(pl.program_id(0) == 0)
    def _():
        st_ref[...] = jnp.zeros_like(st_ref)

    hk = GLA_H * GLA_DK
    lane = _iota2((TOK_BLK, hk), 1)
    first_half = _imod(lane, GLA_DK) < (GLA_DK // 2)
    head_of_lane = _idiv(lane, GLA_DK)
    cos = cos_ref[...]
    sin = sin_ref[...]
    mask = _chunk_mask(rev)
    n_chunk = TOK_BLK // CHUNK
    batch = range(nbat)

    def rope(x):
        partner = jnp.where(first_half, pltpu.roll(x, hk - GLA_DK // 2, 1), pltpu.roll(x, GLA_DK // 2, 1))
        return x * cos + partner * sin

    ks, vs, bs, q_ins, o_intras = [], [], [], [], []
    for bi in batch:
        q = rope(q_ref[bi]) * GLA_DK ** -0.5
        k = rope(k_ref[bi])
        v = v_ref[bi]
        loga = -_softplus(-(_dot(lr_ref[bi], a2_ref[...]) + ab_ref[...])) / GLA_TAU
        b = _select_rows(lm_ref[...], loga)
        q_in = q * jnp.exp(b)
        k_in = k * jnp.exp(-b)
        o_heads = []
        for h in range(GLA_H):
            att = _dot_nt(jnp.where(head_of_lane == h, q_in, 0.0), k_in)
            att = jnp.where(mask, att, 0.0)
            o_heads.append(_dot(att, v[:, h * GLA_DV:(h + 1) * GLA_DV]))
        ks.append(k)
        vs.append(v)
        bs.append(b)
        q_ins.append(q_in)
        o_intras.append(jnp.concatenate(o_heads, axis=1))

    sts = [st_ref[bi] for bi in batch]
    diag = _idiv(_iota2(sts[0].shape, 0), GLA_DV) == _idiv(_iota2(sts[0].shape, 1), GLA_DK)
    o_inter = [[None] * n_chunk for _ in batch]
    for c in _chunk_order(rev):
        rows = slice(c * CHUNK, (c + 1) * CHUNK)
        for bi in batch:
            b_c = bs[bi][rows]
            b_last = b_c[0:1] if rev else b_c[CHUNK - 1:CHUNK]
            o_inter[bi][c] = _dot_nt(q_ins[bi][rows], sts[bi])
            k_end = ks[bi][rows] * jnp.exp(b_last - b_c)
            ds = _dot_tn(vs[bi][rows], k_end)
            sts[bi] = sts[bi] * jnp.exp(b_last) + jnp.where(diag, ds, 0.0)
    for bi in batch:
        st_ref[bi] = sts[bi]
        o = o_intras[bi] + jnp.concatenate(o_inter[bi], axis=0)
        if rev:
            y = _head_rmsnorm_gate(of_ref[bi] + o, ng_ref[...], g_ref[bi], GLA_H, GLA_DV)
            y_ref[bi] = y.astype(y_ref.dtype)
        else:
            o_ref[bi] = o


def _cumsum_matrix(rev):
    r = np.arange(TOK_BLK)[:, None]
    c = np.arange(TOK_BLK)[None, :]
    same = (r // CHUNK) == (c // CHUNK)
    tri = (c >= r) if rev else (c <= r)
    return jnp.asarray((same & tri).astype(np.float32), dtype=BF16)


def _gla_dir(p, cos, sin, a2, ab, ng, o_f, *, rev, nblk):
    nbat, s1, _ = p.shape
    blk = _scan_block(rev, nblk)
    d = 1 if rev else 0
    a2d = jnp.zeros((LANE, GLA_H * GLA_DK), F32).at[d * GLA_LR:(d + 1) * GLA_LR].set(a2[d])
    tab = pl.BlockSpec((TOK_BLK, GLA_H * GLA_DK), lambda s: (blk(s), 0))
    in_specs = [_blk_spec(nbat, 256, P_GLA_Q, blk), _blk_spec(nbat, 256, P_GLA_K, blk),
                _blk_spec(nbat, 512, P_GLA_V, blk), _blk_spec(nbat, 512, P_GLA_G, blk),
                _blk_spec(nbat, LANE, P_GLA_LR, blk), tab, tab,
                _const_spec((LANE, 256)), _const_spec((1, 256)), _const_spec((TOK_BLK, TOK_BLK)),
                _const_spec((1, GLA_DV))]
    args = [p, p, p, p, p, cos, sin, a2d, ab[d].reshape(1, -1), _cumsum_matrix(rev), ng.reshape(1, -1)]
    return _scan_call(_gla_kernel, "gla", in_specs, args, o_f, rev=rev, nbat=nbat, s1=s1, nblk=nblk,
                      state_shape=(GLA_H * GLA_DV, GLA_H * GLA_DK))[0]


def _rope_tables(lc, t):
    n_freq = GLA_DK // 4
    freqs = ROPE_BASE ** (-jnp.arange(n_freq, dtype=F32) / n_freq)
    tt = jnp.arange(t)
    row = (tt // GRID_W).astype(F32)
    col = (tt % GRID_W).astype(F32)
    ang = jnp.concatenate([row[:, None] * freqs, col[:, None] * freqs], axis=-1)
    cos, sin = jnp.cos(ang), jnp.sin(ang)
    cos = jnp.concatenate([jnp.ones((lc, GLA_DK // 2), F32), cos], axis=0)
    sin = jnp.concatenate([jnp.zeros((lc, GLA_DK // 2), F32), sin], axis=0)
    cos_h = jnp.concatenate([cos, cos], axis=1)
    sin_h = jnp.concatenate([-sin, sin], axis=1)
    return jnp.tile(cos_h, (1, GLA_H)), jnp.tile(sin_h, (1, GLA_H))


def _na_kernel(q_ref, kp_ref, kc_ref, kn_ref, kx_ref, vp_ref, vc_ref, vn_ref, vx_ref, bias_ref, y_ref, *, nbat):
    scale = NA_D ** -0.5
    for bi in range(nbat):
        outs = []
        for h in range(NA_H):
            hs = slice(h * NA_D, (h + 1) * NA_D)
            qh = (q_ref[bi, :, hs] * scale).astype(BF16)
            s = jnp.concatenate(
                [_dot_nt(qh, kp_ref[bi, :, hs]) + bias_ref[0, h],
                 _dot_nt(qh, kc_ref[bi, :, hs]) + bias_ref[1, h],
                 _dot_nt(qh, kn_ref[bi, :, hs]) + bias_ref[2, h],
                 _dot_nt(qh, kx_ref[bi, :, hs])], axis=1)
            m = jnp.max(s, axis=-1, keepdims=True)
            e = jnp.exp(s - m)
            p = e / jnp.sum(e, axis=-1, keepdims=True)
            o = (_dot(p[:, 0:TOK_BLK], vp_ref[bi, :, hs]) + _dot(p[:, TOK_BLK:2 * TOK_BLK], vc_ref[bi, :, hs])
                 + _dot(p[:, 2 * TOK_BLK:3 * TOK_BLK], vn_ref[bi, :, hs])
                 + _dot(p[:, 3 * TOK_BLK:], vx_ref[bi, :, hs]))
            outs.append(o)
        y_ref[bi] = jnp.concatenate(outs, axis=1).astype(y_ref.dtype)


def _na_bias_tiles(rpb):
    edge = GRID_W - NA_WIN_C
    ext = jnp.concatenate([jnp.repeat(rpb[..., :1], edge, axis=-1), rpb,
                           jnp.repeat(rpb[..., -1:], edge, axis=-1)], axis=-1)
    toep = jnp.stack([ext[..., GRID_W - 1 - qc:2 * GRID_W - 1 - qc] for qc in range(GRID_W)], axis=-2)
    qc = np.arange(GRID_W)[:, None]
    kc = np.arange(GRID_W)[None, :]
    c0 = np.clip(qc - NA_WIN_C // 2, 0, GRID_W - NA_WIN_C)
    col_ok = (kc >= c0) & (kc < c0 + NA_WIN_C)
    toep = jnp.where(jnp.asarray(col_ok), toep, NEG_INF).astype(F32)
    masked = jnp.full((NA_H, GRID_W, GRID_W), NEG_INF, F32)
    kinds = []
    for kind in range(3):
        offs = []
        for off in (-1, 0, 1):
            rows = []
            for qr in range(ROWS_PER_BLK):
                start = (0, qr - NA_WIN_R // 2, ROWS_PER_BLK - NA_WIN_R)[kind]
                cols = []
                for kb in range(ROWS_PER_BLK):
                    kr = kb + ROWS_PER_BLK * off
                    ok = start <= kr < start + NA_WIN_R
                    cols.append(toep[:, kr - qr + NA_WIN_R - 1] if ok else masked)
                rows.append(jnp.concatenate(cols, axis=-1))
            offs.append(jnp.concatenate(rows, axis=-2))
        kinds.append(jnp.stack(offs))
    kinds.append(jnp.full_like(kinds[0], NEG_INF))
    return jnp.stack(kinds)


def _na(p, bias, *, nblk):
    nbat, s1, _ = p.shape

    def kind(s):
        return jnp.where(s == 0, 3, jnp.where(s == 1, 0, jnp.where(s == nblk - 1, 2, 1)))

    prev = lambda s: jnp.maximum(s - 1, 1)
    cur = lambda s: s
    nxt = lambda s: jnp.minimum(s + 1, nblk - 1)
    ctx = lambda s: 0
    spec = lambda col, blk: _blk_spec(nbat, BRANCH_W, col, blk)
    return pl.pallas_call(
        functools.partial(_na_kernel, nbat=nbat),
        grid=(nblk,),
        in_specs=[spec(P_NA_Q, cur),
                  spec(P_NA_K, prev), spec(P_NA_K, cur), spec(P_NA_K, nxt), spec(P_NA_K, ctx),
                  spec(P_NA_V, prev), spec(P_NA_V, cur), spec(P_NA_V, nxt), spec(P_NA_V, ctx),
                  pl.BlockSpec((None, 3, NA_H, TOK_BLK, TOK_BLK), lambda s: (kind(s), 0, 0, 0, 0))],
        out_specs=pl.BlockSpec((nbat, TOK_BLK, BRANCH_W), lambda s: (0, s, 0)),
        out_shape=jax.ShapeDtypeStruct((nbat, s1, BRANCH_W), BF16),
        compiler_params=_cparams(("arbitrary",)),
        name="nbr_attn",
    )(p, p, p, p, p, p, p, p, p, bias)


def _l2norm(x):
    return x * lax.rsqrt(jnp.sum(x * x, axis=-1, keepdims=True) + RMS_EPS)


def _gdn_kernel(*refs, rev, nbat, nblk):
    if rev:
        qkv_ref, z_ref, sc_ref, alog_ref, dtb_ref, lm_ref, ng_ref, of_ref, y_ref, st_ref = refs
        qkv = [qkv_ref[bi] for bi in range(nbat)]
    else:
        (xm_ref, xp_ref, xn_ref, cw_ref, cb_ref, z_ref, sc_ref, alog_ref, dtb_ref, lm_ref, ng_ref,
         o_ref, qkvc_ref, st_ref) = refs
        qkv = []
        for bi in range(nbat):
            qkv.append(_conv_silu_block(xm_ref[bi], xp_ref[bi], xn_ref[bi], cw_ref[...], cb_ref[...],
                                        pl.program_id(0), nblk))
            qkvc_ref[bi] = qkv[bi]

    @pl.when(pl.program_id(0) == 0)
    def _():
        st_ref[...] = jnp.zeros_like(st_ref)

    d = 1 if rev else 0
    m_incl = _chunk_mask(rev)
    m_strict = _chunk_mask(rev, strict=True)
    eye = (_iota2((TOK_BLK, TOK_BLK), 0) == _iota2((TOK_BLK, TOK_BLK), 1)).astype(F32)
    n_chunk = TOK_BLK // CHUNK

    units = [(bi, h) for bi in range(nbat) for h in range(GDN_H)]
    qs, ks, bcols, ebcs, attns, nmats, rhss = [], [], [], [], [], [], []
    for bi in range(nbat):
        sc = sc_ref[bi]
        beta_all = _sigmoid(sc)
        g_all = -jnp.exp(alog_ref[...]) * _softplus(sc + dtb_ref[...])
        b_all = _select_rows(lm_ref[...], g_all)
        b_all_t = b_all.T
        eb_all = jnp.exp(b_all)
        for h in range(GDN_H):
            head = lambda part: qkv[bi][:, part * BRANCH_W + h * GDN_D:part * BRANCH_W + (h + 1) * GDN_D]
            qh = _l2norm(head(0)) * GDN_D ** -0.5
            kh = _l2norm(head(1))
            vh = head(2)
            lb, lg = GDN_H * d + h, 2 * GDN_H + GDN_H * d + h
            beta = beta_all[:, lb:lb + 1]
            bcol = b_all[:, lg:lg + 1]
            brow = b_all_t[lg:lg + 1, :]
            ebc = eb_all[:, lg:lg + 1]
            diff = bcol - brow
            dec_incl = jnp.where(m_incl, jnp.exp(jnp.where(m_incl, diff, 0.0)), 0.0)
            dec_strict = jnp.where(m_strict, dec_incl, 0.0)
            kk = _dot_nt(kh, kh)
            qs.append(qh)
            ks.append(kh)
            bcols.append(bcol)
            ebcs.append(ebc)
            attns.append(_dot_nt(qh, kh) * dec_incl)
            nmats.append(-(beta * kk * dec_strict))
            rhss.append(_split(jnp.concatenate([kh * (beta * ebc), vh * beta], axis=1)))

    bdot = lambda x, y: jnp.dot(x, y, preferred_element_type=F32)
    nsplit = [_split(n) for n in nmats]
    t0s = [eye + n for n in nmats]
    powers = [ns[0] for ns in nsplit]
    for _ in range(5):
        powers = [bdot(m, m).astype(BF16) for m in powers]
        t0s = [t + bdot(t.astype(BF16), m) for t, m in zip(t0s, powers)]
    t0split = [_split(t) for t in t0s]
    resid = [(eye - t) + _dot_split(ns, ts) for t, ns, ts in zip(t0s, nsplit, t0split)]
    tinvs = [t + bdot(ts[0], r.astype(BF16)) for t, ts, r in zip(t0s, t0split, resid)]
    sols = [_dot_split(_split(t), r) for t, r in zip(tinvs, rhss)]
    ws = [s[:, :GDN_D] for s in sols]
    u0s = [s[:, GDN_D:] for s in sols]

    sts = [st_ref[bi, h] for bi, h in units]
    u_parts = [[None] * n_chunk for _ in units]
    o_parts = [[None] * n_chunk for _ in units]
    for c in _chunk_order(rev):
        rows = slice(c * CHUNK, (c + 1) * CHUNK)
        last = c * CHUNK if rev else (c + 1) * CHUNK - 1
        for i in range(len(units)):
            st = sts[i]
            b_last = bcols[i][last:last + 1]
            u = u0s[i][rows] - _dot(ws[i][rows], st)
            u_parts[i][c] = u
            o_parts[i][c] = ebcs[i][rows] * _dot(qs[i][rows], st)
            k_end = ks[i][rows] * jnp.exp(b_last - bcols[i][rows])
            sts[i] = jnp.exp(b_last) * st + _dot_tn(k_end, u)
    o_units = []
    for i, (bi, h) in enumerate(units):
        st_ref[bi, h] = sts[i]
        o_units.append(jnp.concatenate(o_parts[i], axis=0) + _dot(attns[i], jnp.concatenate(u_parts[i], axis=0)))
    for bi in range(nbat):
        o = jnp.concatenate(o_units[bi * GDN_H:(bi + 1) * GDN_H], axis=1)
        if rev:
            y = _head_rmsnorm_gate(of_ref[bi] + o, ng_ref[...], z_ref[bi], GDN_H, GDN_D)
            y_ref[bi] = y.astype(y_ref.dtype)
        else:
            o_ref[bi] = o


def _gdn_dir(p, qkv, conv_w, a_log, dt_bias, ng, o_f, *, rev, nblk):
    nbat, s1, _ = p.shape
    blk = _scan_block(rev, nblk)
    d = 1 if rev else 0
    width = 3 * BRANCH_W
    lane0 = 2 * GDN_H + GDN_H * d
    alog_row = jnp.zeros((1, LANE), F32).at[0, lane0:lane0 + GDN_H].set(a_log[d])
    dtb_row = jnp.zeros((1, LANE), F32).at[0, lane0:lane0 + GDN_H].set(dt_bias[d])
    if rev:
        in_specs, args = [_blk_spec(nbat, width, 0, blk)], [qkv]
    else:
        in_specs, args = _conv_inputs(p, P_GDN_QKV, conv_w, jnp.zeros((width,), F32), nblk)
    in_specs += [_blk_spec(nbat, 512, P_GDN_Z, blk), _blk_spec(nbat, LANE, P_GDN_SC, blk),
                 _const_spec((1, LANE)), _const_spec((1, LANE)), _const_spec((TOK_BLK, TOK_BLK)),
                 _const_spec((1, GDN_D))]
    args += [p, p, alog_row, dtb_row, _cumsum_matrix(rev), ng.reshape(1, -1)]
    return _scan_call(_gdn_kernel, "gdn", in_specs, args, o_f, rev=rev, nbat=nbat, s1=s1, nblk=nblk,
                      state_shape=(GDN_H, GDN_D, GDN_D), conv_width=width)


def _ssd_kernel(*refs, rev, nbat, nblk):
    if rev:
        (xbc_ref, dt_ref, z_ref, alog_ref, dtb_ref, ex_ref, lm_ref, dsk_ref, ng_ref, of_ref,
         y_ref, st_ref) = refs
        xbc = [xbc_ref[bi] for bi in range(nbat)]
    else:
        (xm_ref, xp_ref, xn_ref, cw_ref, cb_ref, dt_ref, z_ref, alog_ref, dtb_ref, ex_ref, lm_ref, dsk_ref, ng_ref,
         o_ref, xbcc_ref, st_ref) = refs
        xbc = []
        for bi in range(nbat):
            xbc.append(_conv_silu_block(xm_ref[bi], xp_ref[bi], xn_ref[bi], cw_ref[...], cb_ref[...],
                                        pl.program_id(0), nblk))
            xbcc_ref[bi] = xbc[bi]

    @pl.when(pl.program_id(0) == 0)
    def _():
        st_ref[...] = jnp.zeros_like(st_ref)

    d = 1 if rev else 0
    heads_per_g = M2_H // M2_G
    gw = heads_per_g * M2_P
    n_chunk = TOK_BLK // CHUNK
    mask = _chunk_mask(rev)
    lane = _iota2((TOK_BLK, gw), 1)
    ex = ex_ref[...]

    units = [(bi, g) for bi in range(nbat) for g in range(M2_G)]
    cqs, bks, xvs, b_es, eb_es, accs = [], [], [], [], [], []
    for bi in range(nbat):
        dt = _softplus(dt_ref[bi] + dtb_ref[...])
        loga = -jnp.exp(alog_ref[...]) * dt
        b8 = _select_rows(lm_ref[...], loga)
        b8_t = b8.T
        xv = xbc[bi][:, :BRANCH_W] * _select_cols(dt, ex)
        b_e = _select_cols(b8, ex)
        eb_e = jnp.exp(b_e)
        for g in range(M2_G):
            gs = slice(g * M2_N, (g + 1) * M2_N)
            xs_g = slice(g * gw, (g + 1) * gw)
            bk = xbc[bi][:, BRANCH_W + g * M2_N:BRANCH_W + (g + 1) * M2_N]
            cq = xbc[bi][:, BRANCH_W + (M2_G + g) * M2_N:BRANCH_W + (M2_G + g + 1) * M2_N]
            scores = _dot_nt(cq, bk)
            xv_g = xv[:, xs_g]
            acc = None
            for hh in range(heads_per_g):
                lh = M2_H * d + heads_per_g * g + hh
                diff = b8[:, lh:lh + 1] - b8_t[lh:lh + 1, :]
                dec = jnp.where(mask, jnp.exp(jnp.where(mask, diff, 0.0)), 0.0)
                term = _dot(scores * dec, jnp.where(_idiv(lane, M2_P) == hh, xv_g, 0.0))
                acc = term if acc is None else acc + term
            cqs.append(cq)
            bks.append(bk)
            xvs.append(xv_g)
            b_es.append(b_e[:, xs_g])
            eb_es.append(eb_e[:, xs_g])
            accs.append(acc)

    sts = [st_ref[bi, g] for bi, g in units]
    o_parts = [[None] * n_chunk for _ in units]
    for c in _chunk_order(rev):
        rows = slice(c * CHUNK, (c + 1) * CHUNK)
        last = c * CHUNK if rev else (c + 1) * CHUNK - 1
        for i in range(len(units)):
            b_c = b_es[i][rows]
            b_last = b_es[i][last:last + 1]
            o_parts[i][c] = eb_es[i][rows] * _dot(cqs[i][rows], sts[i])
            ds = _dot_tn(bks[i][rows], xvs[i][rows] * jnp.exp(b_last - b_c))
            sts[i] = jnp.exp(b_last) * sts[i] + ds
    o_units = []
    for i, (bi, g) in enumerate(units):
        st_ref[bi, g] = sts[i]
        o_units.append(accs[i] + jnp.concatenate(o_parts[i], axis=0))
    for bi in range(nbat):
        o = jnp.concatenate(o_units[bi * M2_G:(bi + 1) * M2_G], axis=1)
        if rev:
            y = (of_ref[bi] + o + dsk_ref[...] * xbc[bi][:, :BRANCH_W]) * _silu(z_ref[bi])
            y = y * lax.rsqrt(jnp.mean(y * y, axis=-1, keepdims=True) + RMS_EPS) * ng_ref[...]
            y_ref[bi] = y.astype(y_ref.dtype)
        else:
            o_ref[bi] = o


def _ssd_dir(p, xbc, conv_w, conv_b, a_log, dt_bias, d_skip, ng, o_f, *, rev, nblk):
    nbat, s1, _ = p.shape
    blk = _scan_block(rev, nblk)
    d = 1 if rev else 0
    lane0 = M2_H * d
    alog_row = jnp.zeros((1, LANE), F32).at[0, lane0:lane0 + M2_H].set(a_log[d])
    dtb_row = jnp.zeros((1, LANE), F32).at[0, lane0:lane0 + M2_H].set(dt_bias[d])
    ex = np.zeros((LANE, BRANCH_W), np.float32)
    for h in range(M2_H):
        ex[lane0 + h, h * M2_P:(h + 1) * M2_P] = 1.0
    dsk_row = jnp.repeat(d_skip, M2_P).reshape(1, BRANCH_W)
    if rev:
        in_specs, args = [_blk_spec(nbat, M2_CONV_CH, 0, blk)], [xbc]
    else:
        in_specs, args = _conv_inputs(p, P_M2_XBC, conv_w, conv_b, nblk)
    in_specs += [_blk_spec(nbat, LANE, P_M2_DT, blk), _blk_spec(nbat, 512, P_M2_Z, blk),
                 _const_spec((1, LANE)), _const_spec((1, LANE)), _const_spec((LANE, BRANCH_W)),
                 _const_spec((TOK_BLK, TOK_BLK)), _const_spec((1, BRANCH_W)), _const_spec((1, BRANCH_W))]
    args += [p, p, alog_row, dtb_row, jnp.asarray(ex, dtype=BF16), _cumsum_matrix(rev), dsk_row, ng.reshape(1, -1)]
    return _scan_call(_ssd_kernel, "ssd", in_specs, args, o_f, rev=rev, nbat=nbat, s1=s1, nblk=nblk,
                      state_shape=(M2_G, M2_N, (M2_H // M2_G) * M2_P), conv_width=M2_CONV_CH)


def _final_norm_kernel(x_ref, g_ref, o_ref):
    x = x_ref[...]
    o_ref[...] = x * lax.rsqrt(jnp.mean(x * x, axis=-1, keepdims=True) + RMS_EPS) * g_ref[...]


def _final_norm(xs, g, *, nbat, nblk, lc):
    D = xs.shape[1]
    cb = lc // TOK_BLK
    nlat = nblk - cb
    out = pl.pallas_call(
        _final_norm_kernel,
        grid=(nbat, nlat),
        in_specs=[pl.BlockSpec((TOK_BLK, D), lambda b, s: (b * nblk + cb + s, 0)),
                  pl.BlockSpec((1, D), lambda b, s: (0, 0))],
        out_specs=pl.BlockSpec((TOK_BLK, D), lambda b, s: (b * nlat + s, 0)),
        out_shape=jax.ShapeDtypeStruct((nbat * nlat * TOK_BLK, D), F32),
        compiler_params=_cparams(("arbitrary", "arbitrary")),
        name="final_norm",
    )(xs, g.reshape(1, D))
    return out.reshape(nbat, nlat * TOK_BLK, D)


def _row_tile(s1, limit):
    for tm in range(limit - limit % 16, 0, -16):
        if s1 % tm == 0:
            return tm
    raise ValueError(f"no row tile for sequence length {s1}")


def kernel(x, c, ctx, c_ctx, norm1_g, norm2_g, w_ada, b_ada, w_in, b_merge, gla_a2, gla_ab, gla_norm_g, na_rpb, gdn_conv, gdn_a_log, gdn_dt_bias, gdn_norm_g, m2_conv, m2_conv_b, m2_a_log, m2_dt_bias, m2_d, m2_norm_g, w_branch, w_out, w_ffn1, w_ffn3, w_ffn2, final_norm_g):
    nbat, t, D = x.shape
    lc = ctx.shape[1]
    depth = w_in.shape[0]
    assert D == D_MODEL and lc == TOK_BLK and t % TOK_BLK == 0 and t // GRID_W >= 3 * ROWS_PER_BLK
    assert nbat + 1 <= MOD_ROWS
    s1 = lc + t
    nblk = s1 // TOK_BLK
    tm = _row_tile(s1, 1056)
    geo = dict(s1=s1, lc=lc, nb=nbat)

    xs = jnp.concatenate([ctx, x], axis=1).reshape(nbat * s1, D)
    cvec = jnp.concatenate([c, c_ctx[None], jnp.zeros((MOD_ROWS - nbat - 1, D), F32)], axis=0)
    mods = _ada(cvec, w_ada, b_ada).reshape(depth, MOD_ROWS, 6, D).transpose(0, 2, 1, 3)
    cos, sin = _rope_tables(lc, t)
    wb_in, wb_branch, wb_out, wb_ffn1, wb_ffn3, wb_ffn2 = (
        w.astype(BF16) for w in (w_in, w_branch, w_out, w_ffn1, w_ffn3, w_ffn2))
    flat = lambda y: y.reshape(nbat * s1, BRANCH_W)

    for l in range(depth):
        sh1, sc1, g1, sh2, sc2, g2 = (mods[l, i] for i in range(6))

        h = _normmod(xs, norm1_g[l], sc1, sh1, tm=_row_tile(s1, 384), **geo)
        w_mix, w_gate = _split_in_weights(wb_in, l)
        p = _matmul(h, w_mix, tm=tm, tn=MIX_TILE, out_dtype=F32).reshape(nbat, s1, P_COLS)

        o_f = _gla_dir(p, cos, sin, gla_a2[l], gla_ab[l], gla_norm_g[l], None, rev=False, nblk=nblk)
        ya = _gla_dir(p, cos, sin, gla_a2[l], gla_ab[l], gla_norm_g[l], o_f, rev=True, nblk=nblk)

        yb = _na(p, _na_bias_tiles(na_rpb[l]), nblk=nblk)

        gdn_par = (gdn_conv[l], gdn_a_log[l], gdn_dt_bias[l], gdn_norm_g[l])
        o_f, qkv = _gdn_dir(p, None, *gdn_par, None, rev=False, nblk=nblk)
        yc, = _gdn_dir(p, qkv, *gdn_par, o_f, rev=True, nblk=nblk)

        ssd_par = (m2_conv[l], m2_conv_b[l], m2_a_log[l], m2_dt_bias[l], m2_d[l], m2_norm_g[l])
        o_f, xbc = _ssd_dir(p, None, *ssd_par, None, rev=False, nblk=nblk)
        yd, = _ssd_dir(p, xbc, *ssd_par, o_f, rev=True, nblk=nblk)

        ys = (flat(ya), flat(yb), flat(yc), flat(yd))
        merged = _merge(h, w_gate, b_merge[l], ys, wb_branch, l, tm=tm, tn=512)
        xs, h2 = _outproj_norm(merged, wb_out, l, xs, g1, norm2_g[l], sc2, sh2,
                               tm=_row_tile(s1, 384), **geo)

        u = _ffn_up(h2, wb_ffn1, wb_ffn3, l, tm=_row_tile(s1, 2112), tn=512)
        xs = _matmul_residual(u, wb_ffn2, l, xs, g2, tm=tm, tn=512, name="ffn_down", **geo)

    return _final_norm(xs, final_norm_g, nbat=nbat, nblk=nblk, lc=lc)
```

```python
import functools

import numpy as np
import jax
import jax.numpy as jnp
from jax import lax
from jax.experimental import pallas as pl
from jax.experimental.pallas import tpu as pltpu

D_MODEL = 2048
GRID_W = 64
N_BRANCH = 4
BRANCH_W = D_MODEL // 4
CHUNK = 64
CONV_W = 5
RMS_EPS = 1e-6
NEG_INF = -1e30
ROPE_BASE = 10000.0
GLA_H = 4
GLA_DV = BRANCH_W // GLA_H
GLA_DK = GLA_DV // 2
GLA_LR = 16
GLA_TAU = 16.0
NA_H = 4
NA_D = BRANCH_W // NA_H
NA_WIN_R = 8
NA_WIN_C = 16
GDN_H = 4
GDN_D = BRANCH_W // GDN_H
M2_P = 64
M2_H = BRANCH_W // M2_P
M2_N = 128
M2_G = 2
M2_CONV_CH = BRANCH_W + 2 * M2_G * M2_N
D_FF = ((8 * D_MODEL + 3 * 256 - 1) // (3 * 256)) * 256
GLA_IN = 2 * GLA_H * GLA_DK + 2 * BRANCH_W + 2 * GLA_LR
NA_IN = 3 * BRANCH_W
GDN_IN = 4 * BRANCH_W + 4 * GDN_H
M2_IN = BRANCH_W + M2_CONV_CH + 2 * M2_H
MIX_IN = GLA_IN + NA_IN + GDN_IN + M2_IN

F32 = jnp.float32
BF16 = jnp.bfloat16

LANE = 128
SUBLANE = 8
V7X_VMEM_BYTES = 64 * 1024 * 1024
VMEM_LIMIT = V7X_VMEM_BYTES - 8 * 1024 * 1024

TOK_BLK = 4 * CHUNK
ROWS_PER_BLK = TOK_BLK // GRID_W
HALO = SUBLANE
MOD_ROWS = 8

P_GDN_QKV = 0
P_GLA_V = 1536
P_M2_XBC = 2048
P_GLA_G = 3072
P_NA_Q, P_NA_K, P_NA_V = 3584, 4096, 4608
P_GDN_Z, P_M2_Z = 5120, 5632
P_GLA_Q, P_GLA_K = 6144, 6400
P_GLA_LR, P_GDN_SC, P_M2_DT = 6656, 6784, 6912
P_COLS = 7168
MIX_TILE = 1024


def _mix_fields():
    gla, na, gdn, m2 = 0, GLA_IN, GLA_IN + NA_IN, GLA_IN + NA_IN + GDN_IN
    return sorted([
        (P_GLA_V, gla + 512, 512), (P_GLA_G, gla + 1024, 512),
        (P_NA_Q, na, 512), (P_NA_K, na + 512, 512), (P_NA_V, na + 1024, 512),
        (P_GDN_QKV, gdn, 1536), (P_GDN_Z, gdn + 1536, 512),
        (P_M2_Z, m2, 512), (P_M2_XBC, m2 + 512, M2_CONV_CH),
        (P_GLA_Q, gla, 256), (P_GLA_K, gla + 256, 256),
        (P_GLA_LR, gla + 1536, 2 * GLA_LR), (P_GDN_SC, gdn + 2048, 4 * GDN_H),
        (P_M2_DT, m2 + 512 + M2_CONV_CH, 2 * M2_H)])


def _cparams(sem):
    return pltpu.CompilerParams(dimension_semantics=sem, vmem_limit_bytes=VMEM_LIMIT)


def _wprep_kernel(w_ref, mix_ref, gate_ref):
    col = 0
    for dst, src, n in _mix_fields():
        if dst > col:
            mix_ref[:, col:dst] = jnp.zeros((mix_ref.shape[0], dst - col), mix_ref.dtype)
        mix_ref[:, dst:dst + n] = w_ref[:, src:src + n].astype(mix_ref.dtype)
        col = dst + n
    mix_ref[:, col:] = jnp.zeros((mix_ref.shape[0], P_COLS - col), mix_ref.dtype)
    gate_ref[...] = w_ref[:, MIX_IN:].astype(gate_ref.dtype)


def _split_in_weights(w_in, layer):
    _, D, N = w_in.shape
    rb = 256
    return pl.pallas_call(
        _wprep_kernel,
        grid=(D // rb,),
        in_specs=[pl.BlockSpec((None, rb, N), lambda r: (layer, r, 0))],
        out_specs=[pl.BlockSpec((rb, P_COLS), lambda r: (r, 0)),
                   pl.BlockSpec((rb, N - MIX_IN), lambda r: (r, 0))],
        out_shape=[jax.ShapeDtypeStruct((D, P_COLS), BF16), jax.ShapeDtypeStruct((D, N - MIX_IN), BF16)],
        compiler_params=_cparams(("arbitrary",)),
        name="split_in_weights",
    )(w_in)


def _sigmoid(x):
    return 1.0 / (1.0 + jnp.exp(-x))


def _silu(x):
    return x * _sigmoid(x)


def _softplus(x):
    return jnp.maximum(x, 0.0) + jnp.log1p(jnp.exp(-jnp.abs(x)))


def _dot(a, b):
    return jnp.dot(a.astype(BF16), b.astype(BF16), preferred_element_type=F32)


def _dot_nt(a, b):
    return lax.dot_general(a.astype(BF16), b.astype(BF16), (((1,), (1,)), ((), ())),
                           preferred_element_type=F32)


def _dot_tn(a, b):
    return lax.dot_general(a.astype(BF16), b.astype(BF16), (((0,), (0,)), ((), ())),
                           preferred_element_type=F32)


def _split3(a):
    hi = a.astype(BF16)
    r1 = a - hi.astype(F32)
    mid = r1.astype(BF16)
    return hi, mid, (r1 - mid.astype(F32)).astype(BF16)


def _select_rows(m01, a):
    d = lambda y: jnp.dot(m01, y, preferred_element_type=F32)
    hi, mid, lo = _split3(a)
    return d(hi) + (d(mid) + d(lo))


def _select_cols(a, m01):
    d = lambda y: jnp.dot(y, m01, preferred_element_type=F32)
    hi, mid, lo = _split3(a)
    return d(hi) + (d(mid) + d(lo))


def _split(a):
    hi = a.astype(BF16)
    return hi, (a - hi.astype(F32)).astype(BF16)


def _dot_split(a, b):
    (ah, al), (bh, bl) = a, b
    d = lambda x, y: jnp.dot(x, y, preferred_element_type=F32)
    return d(ah, bh) + (d(ah, bl) + d(al, bh))


def _iota2(shape, dim):
    return lax.broadcasted_iota(jnp.int32, shape, dim)


def _idiv(x, n):
    assert n & (n - 1) == 0
    return lax.shift_right_logical(x, jnp.int32(n.bit_length() - 1))


def _imod(x, n):
    assert n & (n - 1) == 0
    return x & (n - 1)


def _chunk_mask(rev, strict=False):
    r = _iota2((TOK_BLK, TOK_BLK), 0)
    c = _iota2((TOK_BLK, TOK_BLK), 1)
    same = _idiv(r, CHUNK) == _idiv(c, CHUNK)
    if rev:
        tri = (c > r) if strict else (c >= r)
    else:
        tri = (c < r) if strict else (c <= r)
    return same & tri


def _chunk_order(rev):
    n = TOK_BLK // CHUNK
    return list(range(n - 1, -1, -1)) if rev else list(range(n))


def _row_split(i, tm, s1):
    r0 = i * tm
    return lax.div(r0, jnp.int32(s1)), lax.rem(r0, jnp.int32(s1))


def _pick_mod(ref, b, nb, is_ctx):
    return jnp.where(is_ctx, ref[nb:nb + 1, :], ref[pl.ds(b, 1), :])


def _ada_kernel(c_ref, w_ref, b_ref, o_ref):
    a = _silu(c_ref[...])
    o_ref[...] = _dot(a, w_ref[...]) + b_ref[...]


def _ada(cvec, w_ada, b_ada):
    L, D, N = w_ada.shape
    tn = 1024
    return pl.pallas_call(
        _ada_kernel,
        grid=(L, N // tn),
        in_specs=[pl.BlockSpec((MOD_ROWS, D), lambda l, j: (0, 0)),
                  pl.BlockSpec((None, D, tn), lambda l, j: (l, 0, j)),
                  pl.BlockSpec((None, 1, tn), lambda l, j: (l, 0, j))],
        out_specs=pl.BlockSpec((None, MOD_ROWS, tn), lambda l, j: (l, 0, j)),
        out_shape=jax.ShapeDtypeStruct((L, MOD_ROWS, N), F32),
        compiler_params=_cparams(("arbitrary", "arbitrary")),
        name="ada_mod",
    )(cvec, w_ada, b_ada.reshape(L, 1, N))


def _normmod_kernel(x_ref, g_ref, sc_ref, sh_ref, o_ref, *, tm, s1, lc, nb):
    b, pos0 = _row_split(pl.program_id(0), tm, s1)
    is_ctx = (pos0 + _iota2((tm, 1), 0)) < lc
    x = x_ref[...]
    y = x * lax.rsqrt(jnp.mean(x * x, axis=-1, keepdims=True) + RMS_EPS) * g_ref[...]
    sc = _pick_mod(sc_ref, b, nb, is_ctx)
    sh = _pick_mod(sh_ref, b, nb, is_ctx)
    o_ref[...] = (y * (1.0 + sc) + sh).astype(o_ref.dtype)


def _normmod(xs, g, sc, sh, *, tm, s1, lc, nb):
    R, D = xs.shape
    return pl.pallas_call(
        functools.partial(_normmod_kernel, tm=tm, s1=s1, lc=lc, nb=nb),
        grid=(R // tm,),
        in_specs=[pl.BlockSpec((tm, D), lambda i: (i, 0)),
                  pl.BlockSpec((1, D), lambda i: (0, 0)),
                  pl.BlockSpec((MOD_ROWS, D), lambda i: (0, 0)),
                  pl.BlockSpec((MOD_ROWS, D), lambda i: (0, 0))],
        out_specs=pl.BlockSpec((tm, D), lambda i: (i, 0)),
        out_shape=jax.ShapeDtypeStruct((R, D), BF16),
        compiler_params=_cparams(("arbitrary",)),
        name="norm_mod",
    )(xs, g.reshape(1, D), sc, sh)


def _concat_norm_kernel(ctx_ref, x_ref, g_ref, sc_ref, sh_ref, xs_ref, h_ref, *, nb):
    b, s = pl.program_id(0), pl.program_id(1)
    is_ctx = s == 0
    x = jnp.where(is_ctx, ctx_ref[...], x_ref[...])
    xs_ref[...] = x
    row = jnp.where(is_ctx, nb, b)
    y = x * lax.rsqrt(jnp.mean(x * x, axis=-1, keepdims=True) + RMS_EPS) * g_ref[...]
    h_ref[...] = (y * (1.0 + sc_ref[pl.ds(row, 1), :]) + sh_ref[pl.ds(row, 1), :]).astype(h_ref.dtype)


def _concat_norm(ctx, x, g, sc, sh, *, nblk):
    nbat, t, D = x.shape
    R = nbat * nblk * TOK_BLK
    const = lambda b, s: (0, 0)
    row = lambda b, s: (b * nblk + s, 0)
    return pl.pallas_call(
        functools.partial(_concat_norm_kernel, nb=nbat),
        grid=(nbat, nblk),
        in_specs=[pl.BlockSpec((None, TOK_BLK, D), lambda b, s: (b, 0, 0)),
                  pl.BlockSpec((None, TOK_BLK, D), lambda b, s: (b, jnp.maximum(s - 1, 0), 0)),
                  pl.BlockSpec((1, D), const), pl.BlockSpec((MOD_ROWS, D), const), pl.BlockSpec((MOD_ROWS, D), const)],
        out_specs=[pl.BlockSpec((TOK_BLK, D), row), pl.BlockSpec((TOK_BLK, D), row)],
        out_shape=[jax.ShapeDtypeStruct((R, D), F32), jax.ShapeDtypeStruct((R, D), BF16)],
        compiler_params=_cparams(("arbitrary", "arbitrary")),
        name="concat_norm",
    )(ctx, x, g.reshape(1, D), sc, sh)


def _mm_kernel(a_ref, w_ref, o_ref):
    o_ref[...] = jnp.dot(a_ref[...], w_ref[...], preferred_element_type=F32).astype(o_ref.dtype)


def _matmul(a, w, *, tm, tn, out_dtype):
    R, K = a.shape
    N = w.shape[1]
    assert R % tm == 0 and N % tn == 0
    return pl.pallas_call(
        _mm_kernel,
        grid=(R // tm, N // tn),
        in_specs=[pl.BlockSpec((tm, K), lambda i, j: (i, 0)),
                  pl.BlockSpec((K, tn), lambda i, j: (0, j))],
        out_specs=pl.BlockSpec((tm, tn), lambda i, j: (i, j)),
        out_shape=jax.ShapeDtypeStruct((R, N), out_dtype),
        compiler_params=_cparams(("arbitrary", "arbitrary")),
        name="mix_proj",
    )(a, w)


def _mm_res_kernel(a_ref, w_ref, x_ref, gt_ref, o_ref, *, tm, s1, lc, nb):
    b, pos0 = _row_split(pl.program_id(0), tm, s1)
    is_ctx = (pos0 + _iota2((tm, 1), 0)) < lc
    acc = jnp.dot(a_ref[...], w_ref[...], preferred_element_type=F32)
    o_ref[...] = x_ref[...] + _pick_mod(gt_ref, b, nb, is_ctx) * acc


def _matmul_residual(a, w, layer, xs, gate, *, tm, tn, s1, lc, nb, name):
    R, K = a.shape
    N = w.shape[2]
    assert R % tm == 0 and N % tn == 0
    return pl.pallas_call(
        functools.partial(_mm_res_kernel, tm=tm, s1=s1, lc=lc, nb=nb),
        grid=(R // tm, N // tn),
        in_specs=[pl.BlockSpec((tm, K), lambda i, j: (i, 0)),
                  pl.BlockSpec((None, K, tn), lambda i, j: (layer, 0, j)),
                  pl.BlockSpec((tm, tn), lambda i, j: (i, j)),
                  pl.BlockSpec((MOD_ROWS, tn), lambda i, j: (0, j))],
        out_specs=pl.BlockSpec((tm, tn), lambda i, j: (i, j)),
        out_shape=jax.ShapeDtypeStruct((R, N), F32),
        compiler_params=_cparams(("arbitrary", "arbitrary")),
        name=name,
    )(a, w, xs, gate)


def _outproj_norm_kernel(a_ref, w_ref, x_ref, gt_ref, g_ref, sc_ref, sh_ref, xo_ref, h_ref, *, tm, s1, lc, nb):
    b, pos0 = _row_split(pl.program_id(0), tm, s1)
    is_ctx = (pos0 + _iota2((tm, 1), 0)) < lc
    acc = jnp.dot(a_ref[...], w_ref[...], preferred_element_type=F32)
    x = x_ref[...] + _pick_mod(gt_ref, b, nb, is_ctx) * acc
    xo_ref[...] = x
    y = x * lax.rsqrt(jnp.mean(x * x, axis=-1, keepdims=True) + RMS_EPS) * g_ref[...]
    h_ref[...] = (y * (1.0 + _pick_mod(sc_ref, b, nb, is_ctx)) + _pick_mod(sh_ref, b, nb, is_ctx)).astype(h_ref.dtype)


def _outproj_norm(a, w, layer, xs, gate, g, sc, sh, *, tm, s1, lc, nb):
    R, K = a.shape
    D = w.shape[2]
    row = lambda i: (i, 0)
    const = lambda i: (0, 0)
    return pl.pallas_call(
        functools.partial(_outproj_norm_kernel, tm=tm, s1=s1, lc=lc, nb=nb),
        grid=(R // tm,),
        in_specs=[pl.BlockSpec((tm, K), row), pl.BlockSpec((None, K, D), lambda i: (layer, 0, 0)),
                  pl.BlockSpec((tm, D), row),
                  pl.BlockSpec((MOD_ROWS, D), const), pl.BlockSpec((1, D), const),
                  pl.BlockSpec((MOD_ROWS, D), const), pl.BlockSpec((MOD_ROWS, D), const)],
        out_specs=[pl.BlockSpec((tm, D), row), pl.BlockSpec((tm, D), row)],
        out_shape=[jax.ShapeDtypeStruct((R, D), F32), jax.ShapeDtypeStruct((R, D), BF16)],
        compiler_params=_cparams(("arbitrary",)),
        name="out_proj",
    )(a, w, xs, gate, g.reshape(1, D), sc, sh)


def _ffn_up_kernel(a_ref, w1_ref, w3_ref, o_ref):
    a = a_ref[...]
    u = jnp.dot(a, w1_ref[...], preferred_element_type=F32)
    v = jnp.dot(a, w3_ref[...], preferred_element_type=F32)
    o_ref[...] = (_silu(u) * v).astype(o_ref.dtype)


def _ffn_up(a, w1, w3, layer, *, tm, tn):
    R, K = a.shape
    N = w1.shape[2]
    assert R % tm == 0 and N % tn == 0
    wspec = pl.BlockSpec((None, K, tn), lambda i, j: (layer, 0, j))
    return pl.pallas_call(
        _ffn_up_kernel,
        grid=(R // tm, N // tn),
        in_specs=[pl.BlockSpec((tm, K), lambda i, j: (i, 0)), wspec, wspec],
        out_specs=pl.BlockSpec((tm, tn), lambda i, j: (i, j)),
        out_shape=jax.ShapeDtypeStruct((R, N), BF16),
        compiler_params=_cparams(("arbitrary", "arbitrary")),
        name="ffn_up",
    )(a, w1, w3)


def _merge_kernel(h_ref, g0_ref, g1_ref, g2_ref, g3_ref, bm_ref, ya_ref, yb_ref, yc_ref, yd_ref, wb_ref, o_ref):
    h = h_ref[...]
    acc = None
    branches = zip((g0_ref, g1_ref, g2_ref, g3_ref), (ya_ref, yb_ref, yc_ref, yd_ref))
    for i, (wg_ref, y_ref) in enumerate(branches):
        gate = _sigmoid(jnp.dot(h, wg_ref[...], preferred_element_type=F32) + bm_ref[i])
        term = gate * jnp.dot(y_ref[...], wb_ref[i], preferred_element_type=F32)
        acc = term if acc is None else acc + term
    o_ref[...] = acc.astype(o_ref.dtype)


def _merge(h, wg, b_merge, ys, wb, layer, *, tm, tn):
    R, D = h.shape
    N = wb.shape[3]
    assert R % tm == 0 and N % tn == 0
    nj = N // tn
    yspec = pl.BlockSpec((tm, BRANCH_W), lambda i, j: (i, 0))
    gspec = lambda br: pl.BlockSpec((D, tn), lambda i, j: (0, br * nj + j))
    return pl.pallas_call(
        _merge_kernel,
        grid=(R // tm, nj),
        in_specs=[pl.BlockSpec((tm, D), lambda i, j: (i, 0)),
                  gspec(0), gspec(1), gspec(2), gspec(3),
                  pl.BlockSpec((N_BRANCH, 1, tn), lambda i, j: (0, 0, j)),
                  yspec, yspec, yspec, yspec,
                  pl.BlockSpec((None, N_BRANCH, BRANCH_W, tn), lambda i, j: (layer, 0, 0, j))],
        out_specs=pl.BlockSpec((tm, tn), lambda i, j: (i, j)),
        out_shape=jax.ShapeDtypeStruct((R, N), BF16),
        compiler_params=_cparams(("arbitrary", "arbitrary")),
        name="merge",
    )(h, wg, wg, wg, wg, b_merge.reshape(N_BRANCH, 1, N), *ys, wb)


def _conv_silu_block(xm, xp, xn, w, bias, blk, nblk):
    prev_ok = blk >= 2
    next_ok = (blk >= 1) & (blk < nblk - 1)
    rows = TOK_BLK + 2 * HALO
    half = CONV_W // 2
    xpad = jnp.concatenate([jnp.where(prev_ok, xp, 0.0), xm, jnp.where(next_ok, xn, 0.0)], axis=0)
    acc = bias + w[half:half + 1, :] * xm
    for j in range(CONV_W):
        if j != half:
            tap = pltpu.roll(xpad, (half - j) % rows, 0)[HALO:HALO + TOK_BLK]
            acc = acc + w[j:j + 1, :] * tap
    return _silu(acc)


def _conv_inputs(p, col0, w, bias, nblk):
    nbat, s1, _ = p.shape
    C = w.shape[1]
    assert col0 % C == 0
    cb = col0 // C
    hb = TOK_BLK // HALO
    nhalo = s1 // HALO
    wpad = jnp.concatenate([w, jnp.zeros((SUBLANE - CONV_W, C), F32)], axis=0)
    specs = [pl.BlockSpec((nbat, TOK_BLK, C), lambda r: (0, r, cb)),
             pl.BlockSpec((nbat, HALO, C), lambda r: (0, jnp.maximum(r * hb - 1, 0), cb)),
             pl.BlockSpec((nbat, HALO, C), lambda r: (0, jnp.minimum((r + 1) * hb, nhalo - 1), cb)),
             _const_spec((SUBLANE, C)), _const_spec((1, C))]
    return specs, [p, p, p, wpad, bias.reshape(1, C)]


def _scan_block(rev, nblk):
    if rev:
        return lambda s: jnp.where(s == 0, 0, nblk - s)
    return lambda s: s


def _blk_spec(nbat, width, col, blk):
    assert col % width == 0
    return pl.BlockSpec((nbat, TOK_BLK, width), lambda s: (0, blk(s), col // width))


def _const_spec(shape):
    return pl.BlockSpec(shape, lambda s: (0,) * len(shape))


def _head_rmsnorm_gate(o, g, gate, n_head, width):
    outs = []
    for h in range(n_head):
        oh = o[:, h * width:(h + 1) * width]
        yh = oh * lax.rsqrt(jnp.mean(oh * oh, axis=-1, keepdims=True) + RMS_EPS) * g
        outs.append(yh * _silu(gate[:, h * width:(h + 1) * width]))
    return jnp.concatenate(outs, axis=1)


def _scan_call(kern, name, in_specs, args, o_f, *, rev, nbat, s1, nblk, state_shape, conv_width=0):
    blk = _scan_block(rev, nblk)
    ospec = pl.BlockSpec((nbat, TOK_BLK, BRANCH_W), lambda s: (0, blk(s), 0))
    out_specs = [ospec]
    out_shape = [jax.ShapeDtypeStruct((nbat, s1, BRANCH_W), BF16 if rev else F32)]
    if rev:
        in_specs = in_specs + [ospec]
        args = args + [o_f]
    elif conv_width:
        out_specs.append(pl.BlockSpec((nbat, TOK_BLK, conv_width), lambda s: (0, s, 0)))
        out_shape.append(jax.ShapeDtypeStruct((nbat, s1, conv_width), F32))
    return pl.pallas_call(
        functools.partial(kern, rev=rev, nbat=nbat, nblk=nblk),
        grid=(nblk,),
        in_specs=in_specs,
        out_specs=out_specs,
        out_shape=out_shape,
        scratch_shapes=[pltpu.VMEM((nbat,) + state_shape, F32)],
        compiler_params=_cparams(("arbitrary",)),
        name=name + ("_bwd" if rev else "_fwd"),
    )(*args)


def _gla_kernel(q_ref, k_ref, v_ref, g_ref, lr_ref, cos_ref, sin_ref, a2_ref, ab_ref, lm_ref, ng_ref,
                *rest, rev, nbat, nblk):
    if rev:
        of_ref, y_ref, st_ref = rest
    else:
        o_ref, st_ref = rest

    @pl.when(pl.program_id(0) == 0)
    def _():
        st_ref[...] = jnp.zeros_like(st_ref)

    hk = GLA_H * GLA_DK
    lane = _iota2((TOK_BLK, hk), 1)
    first_half = _imod(lane, GLA_DK) < (GLA_DK // 2)
    head_of_lane = _idiv(lane, GLA_DK)
    cos = cos_ref[...]
    sin = sin_ref[...]
    mask = _chunk_mask(rev)
    n_chunk = TOK_BLK // CHUNK
    batch = range(nbat)

    def rope(x):
        partner = jnp.where(first_half, pltpu.roll(x, hk - GLA_DK // 2, 1), pltpu.roll(x, GLA_DK // 2, 1))
        return x * cos + partner * sin

    ks, vs, bs, q_ins, o_intras = [], [], [], [], []
    for bi in batch:
        q = rope(q_ref[bi]) * GLA_DK ** -0.5
        k = rope(k_ref[bi])
        v = v_ref[bi]
        loga = -_softplus(-(_dot(lr_ref[bi], a2_ref[...]) + ab_ref[...])) / GLA_TAU
        b = _select_rows(lm_ref[...], loga)
        q_in = q * jnp.exp(b)
        k_in = k * jnp.exp(-b)
        o_heads = []
        for h in range(GLA_H):
            att = _dot_nt(jnp.where(head_of_lane == h, q_in, 0.0), k_in)
            att = jnp.where(mask, att, 0.0)
            o_heads.append(_dot(att, v[:, h * GLA_DV:(h + 1) * GLA_DV]))
        ks.append(k)
        vs.append(v)
        bs.append(b)
        q_ins.append(q_in)
        o_intras.append(jnp.concatenate(o_heads, axis=1))

    sts = [st_ref[bi] for bi in batch]
    diag = _idiv(_iota2(sts[0].shape, 0), GLA_DV) == _idiv(_iota2(sts[0].shape, 1), GLA_DK)
    o_inter = [[None] * n_chunk for _ in batch]
    for c in _chunk_order(rev):
        rows = slice(c * CHUNK, (c + 1) * CHUNK)
        for bi in batch:
            b_c = bs[bi][rows]
            b_last = b_c[0:1] if rev else b_c[CHUNK - 1:CHUNK]
            o_inter[bi][c] = _dot_nt(q_ins[bi][rows], sts[bi])
            k_end = ks[bi][rows] * jnp.exp(b_last - b_c)
            ds = _dot_tn(vs[bi][rows], k_end)
            sts[bi] = sts[bi] * jnp.exp(b_last) + jnp.where(diag, ds, 0.0)
    for bi in batch:
        st_ref[bi] = sts[bi]
        o = o_intras[bi] + jnp.concatenate(o_inter[bi], axis=0)
        if rev:
            y = _head_rmsnorm_gate(of_ref[bi] + o, ng_ref[...], g_ref[bi], GLA_H, GLA_DV)
            y_ref[bi] = y.astype(y_ref.dtype)
        else:
            o_ref[bi] = o


def _cumsum_matrix(rev):
    r = np.arange(TOK_BLK)[:, None]
    c = np.arange(TOK_BLK)[None, :]
    same = (r // CHUNK) == (c // CHUNK)
    tri = (c >= r) if rev else (c <= r)
    return jnp.asarray((same & tri).astype(np.float32), dtype=BF16)


def _gla_dir(p, cos, sin, a2, ab, ng, o_f, *, rev, nblk):
    nbat, s1, _ = p.shape
    blk = _scan_block(rev, nblk)
    d = 1 if rev else 0
    a2d = jnp.zeros((LANE, GLA_H * GLA_DK), F32).at[d * GLA_LR:(d + 1) * GLA_LR].set(a2[d])
    tab = pl.BlockSpec((TOK_BLK, GLA_H * GLA_DK), lambda s: (blk(s), 0))
    in_specs = [_blk_spec(nbat, 256, P_GLA_Q, blk), _blk_spec(nbat, 256, P_GLA_K, blk),
                _blk_spec(nbat, 512, P_GLA_V, blk), _blk_spec(nbat, 512, P_GLA_G, blk),
                _blk_spec(nbat, LANE, P_GLA_LR, blk), tab, tab,
                _const_spec((LANE, 256)), _const_spec((1, 256)), _const_spec((TOK_BLK, TOK_BLK)),
                _const_spec((1, GLA_DV))]
    args = [p, p, p, p, p, cos, sin, a2d, ab[d].reshape(1, -1), _cumsum_matrix(rev), ng.reshape(1, -1)]
    return _scan_call(_gla_kernel, "gla", in_specs, args, o_f, rev=rev, nbat=nbat, s1=s1, nblk=nblk,
                      state_shape=(GLA_H * GLA_DV, GLA_H * GLA_DK))[0]


def _rope_tables(lc, t):
    n_freq = GLA_DK // 4
    freqs = ROPE_BASE ** (-jnp.arange(n_freq, dtype=F32) / n_freq)
    tt = jnp.arange(t)
    row = (tt // GRID_W).astype(F32)
    col = (tt % GRID_W).astype(F32)
    ang = jnp.concatenate([row[:, None] * freqs, col[:, None] * freqs], axis=-1)
    cos, sin = jnp.cos(ang), jnp.sin(ang)
    cos = jnp.concatenate([jnp.ones((lc, GLA_DK // 2), F32), cos], axis=0)
    sin = jnp.concatenate([jnp.zeros((lc, GLA_DK // 2), F32), sin], axis=0)
    cos_h = jnp.concatenate([cos, cos], axis=1)
    sin_h = jnp.concatenate([-sin, sin], axis=1)
    return jnp.tile(cos_h, (1, GLA_H)), jnp.tile(sin_h, (1, GLA_H))


def _na_kernel(q_ref, kp_ref, kc_ref, kn_ref, kx_ref, vp_ref, vc_ref, vn_ref, vx_ref, bias_ref, y_ref, *, nbat):
    scale = NA_D ** -0.5
    for bi in range(nbat):
        outs = []
        for h in range(NA_H):
            hs = slice(h * NA_D, (h + 1) * NA_D)
            qh = (q_ref[bi, :, hs] * scale).astype(BF16)
            s = jnp.concatenate(
                [_dot_nt(qh, kp_ref[bi, :, hs]) + bias_ref[0, h],
                 _dot_nt(qh, kc_ref[bi, :, hs]) + bias_ref[1, h],
                 _dot_nt(qh, kn_ref[bi, :, hs]) + bias_ref[2, h],
                 _dot_nt(qh, kx_ref[bi, :, hs])], axis=1)
            m = jnp.max(s, axis=-1, keepdims=True)
            e = jnp.exp(s - m)
            p = e / jnp.sum(e, axis=-1, keepdims=True)
            o = (_dot(p[:, 0:TOK_BLK], vp_ref[bi, :, hs]) + _dot(p[:, TOK_BLK:2 * TOK_BLK], vc_ref[bi, :, hs])
                 + _dot(p[:, 2 * TOK_BLK:3 * TOK_BLK], vn_ref[bi, :, hs])
                 + _dot(p[:, 3 * TOK_BLK:], vx_ref[bi, :, hs]))
            outs.append(o)
        y_ref[bi] = jnp.concatenate(outs, axis=1).astype(y_ref.dtype)


def _na_bias_tiles(rpb):
    edge = GRID_W - NA_WIN_C
    ext = jnp.concatenate([jnp.repeat(rpb[..., :1], edge, axis=-1), rpb,
                           jnp.repeat(rpb[..., -1:], edge, axis=-1)], axis=-1)
    toep = jnp.stack([ext[..., GRID_W - 1 - qc:2 * GRID_W - 1 - qc] for qc in range(GRID_W)], axis=-2)
    qc = np.arange(GRID_W)[:, None]
    kc = np.arange(GRID_W)[None, :]
    c0 = np.clip(qc - NA_WIN_C // 2, 0, GRID_W - NA_WIN_C)
    col_ok = (kc >= c0) & (kc < c0 + NA_WIN_C)
    toep = jnp.where(jnp.asarray(col_ok), toep, NEG_INF).astype(F32)
    masked = jnp.full((NA_H, GRID_W, GRID_W), NEG_INF, F32)
    kinds = []
    for kind in range(3):
        offs = []
        for off in (-1, 0, 1):
            rows = []
            for qr in range(ROWS_PER_BLK):
                start = (0, qr - NA_WIN_R // 2, ROWS_PER_BLK - NA_WIN_R)[kind]
                cols = []
                for kb in range(ROWS_PER_BLK):
                    kr = kb + ROWS_PER_BLK * off
                    ok = start <= kr < start + NA_WIN_R
                    cols.append(toep[:, kr - qr + NA_WIN_R - 1] if ok else masked)
                rows.append(jnp.concatenate(cols, axis=-1))
            offs.append(jnp.concatenate(rows, axis=-2))
        kinds.append(jnp.stack(offs))
    kinds.append(jnp.full_like(kinds[0], NEG_INF))
    return jnp.stack(kinds)


def _na(p, bias, *, nblk):
    nbat, s1, _ = p.shape

    def kind(s):
        return jnp.where(s == 0, 3, jnp.where(s == 1, 0, jnp.where(s == nblk - 1, 2, 1)))

    prev = lambda s: jnp.maximum(s - 1, 1)
    cur = lambda s: s
    nxt = lambda s: jnp.minimum(s + 1, nblk - 1)
    ctx = lambda s: 0
    spec = lambda col, blk: _blk_spec(nbat, BRANCH_W, col, blk)
    return pl.pallas_call(
        functools.partial(_na_kernel, nbat=nbat),
        grid=(nblk,),
        in_specs=[spec(P_NA_Q, cur),
                  spec(P_NA_K, prev), spec(P_NA_K, cur), spec(P_NA_K, nxt), spec(P_NA_K, ctx),
                  spec(P_NA_V, prev), spec(P_NA_V, cur), spec(P_NA_V, nxt), spec(P_NA_V, ctx),
                  pl.BlockSpec((None, 3, NA_H, TOK_BLK, TOK_BLK), lambda s: (kind(s), 0, 0, 0, 0))],
        out_specs=pl.BlockSpec((nbat, TOK_BLK, BRANCH_W), lambda s: (0, s, 0)),
        out_shape=jax.ShapeDtypeStruct((nbat, s1, BRANCH_W), BF16),
        compiler_params=_cparams(("arbitrary",)),
        name="nbr_attn",
    )(p, p, p, p, p, p, p, p, p, bias)


def _l2norm(x):
    return x * lax.rsqrt(jnp.sum(x * x, axis=-1, keepdims=True) + RMS_EPS)


def _gdn_kernel(*refs, rev, nbat, nblk):
    if rev:
        qkv_ref, z_ref, sc_ref, alog_ref, dtb_ref, lm_ref, ng_ref, of_ref, y_ref, st_ref = refs
        qkv = [qkv_ref[bi] for bi in range(nbat)]
    else:
        (xm_ref, xp_ref, xn_ref, cw_ref, cb_ref, z_ref, sc_ref, alog_ref, dtb_ref, lm_ref, ng_ref,
         o_ref, qkvc_ref, st_ref) = refs
        qkv = []
        for bi in range(nbat):
            qkv.append(_conv_silu_block(xm_ref[bi], xp_ref[bi], xn_ref[bi], cw_ref[...], cb_ref[...],
                                        pl.program_id(0), nblk))
            qkvc_ref[bi] = qkv[bi]

    @pl.when(pl.program_id(0) == 0)
    def _():
        st_ref[...] = jnp.zeros_like(st_ref)

    d = 1 if rev else 0
    m_incl = _chunk_mask(rev)
    m_strict = _chunk_mask(rev, strict=True)
    eye = (_iota2((TOK_BLK, TOK_BLK), 0) == _iota2((TOK_BLK, TOK_BLK), 1)).astype(F32)
    n_chunk = TOK_BLK // CHUNK

    units = [(bi, h) for bi in range(nbat) for h in range(GDN_H)]
    qs, ks, bcols, ebcs, attns, nmats, rhss = [], [], [], [], [], [], []
    for bi in range(nbat):
        sc = sc_ref[bi]
        beta_all = _sigmoid(sc)
        g_all = -jnp.exp(alog_ref[...]) * _softplus(sc + dtb_ref[...])
        b_all = _select_rows(lm_ref[...], g_all)
        b_all_t = b_all.T
        eb_all = jnp.exp(b_all)
        for h in range(GDN_H):
            head = lambda part: qkv[bi][:, part * BRANCH_W + h * GDN_D:part * BRANCH_W + (h + 1) * GDN_D]
            qh = _l2norm(head(0)) * GDN_D ** -0.5
            kh = _l2norm(head(1))
            vh = head(2)
            lb, lg = GDN_H * d + h, 2 * GDN_H + GDN_H * d + h
            beta = beta_all[:, lb:lb + 1]
            bcol = b_all[:, lg:lg + 1]
            brow = b_all_t[lg:lg + 1, :]
            ebc = eb_all[:, lg:lg + 1]
            diff = bcol - brow
            dec_incl = jnp.where(m_incl, jnp.exp(jnp.where(m_incl, diff, 0.0)), 0.0)
            dec_strict = jnp.where(m_strict, dec_incl, 0.0)
            kk = _dot_nt(kh, kh)
            qs.append(qh)
            ks.append(kh)
            bcols.append(bcol)
            ebcs.append(ebc)
            attns.append(_dot_nt(qh, kh) * dec_incl)
            nmats.append(-(beta * kk * dec_strict))
            rhss.append(_split(jnp.concatenate([kh * (beta * ebc), vh * beta], axis=1)))

    bdot = lambda x, y: jnp.dot(x, y, preferred_element_type=F32)
    nsplit = [_split(n) for n in nmats]
    t0s = [eye + n for n in nmats]
    powers = [ns[0] for ns in nsplit]
    for _ in range(5):
        powers = [bdot(m, m).astype(BF16) for m in powers]
        t0s = [t + bdot(t.astype(BF16), m) for t, m in zip(t0s, powers)]
    t0split = [_split(t) for t in t0s]
    resid = [(eye - t) + _dot_split(ns, ts) for t, ns, ts in zip(t0s, nsplit, t0split)]
    tinvs = [t + bdot(ts[0], r.astype(BF16)) for t, ts, r in zip(t0s, t0split, resid)]
    sols = [_dot_split(_split(t), r) for t, r in zip(tinvs, rhss)]
    ws = [s[:, :GDN_D] for s in sols]
    u0s = [s[:, GDN_D:] for s in sols]

    sts = [st_ref[bi, h] for bi, h in units]
    u_parts = [[None] * n_chunk for _ in units]
    o_parts = [[None] * n_chunk for _ in units]
    for c in _chunk_order(rev):
        rows = slice(c * CHUNK, (c + 1) * CHUNK)
        last = c * CHUNK if rev else (c + 1) * CHUNK - 1
        for i in range(len(units)):
            st = sts[i]
            b_last = bcols[i][last:last + 1]
            u = u0s[i][rows] - _dot(ws[i][rows], st)
            u_parts[i][c] = u
            o_parts[i][c] = ebcs[i][rows] * _dot(qs[i][rows], st)
            k_end = ks[i][rows] * jnp.exp(b_last - bcols[i][rows])
            sts[i] = jnp.exp(b_last) * st + _dot_tn(k_end, u)
    o_units = []
    for i, (bi, h) in enumerate(units):
        st_ref[bi, h] = sts[i]
        o_units.append(jnp.concatenate(o_parts[i], axis=0) + _dot(attns[i], jnp.concatenate(u_parts[i], axis=0)))
    for bi in range(nbat):
        o = jnp.concatenate(o_units[bi * GDN_H:(bi + 1) * GDN_H], axis=1)
        if rev:
            y = _head_rmsnorm_gate(of_ref[bi] + o, ng_ref[...], z_ref[bi], GDN_H, GDN_D)
            y_ref[bi] = y.astype(y_ref.dtype)
        else:
            o_ref[bi] = o


def _gdn_dir(p, qkv, conv_w, a_log, dt_bias, ng, o_f, *, rev, nblk):
    nbat, s1, _ = p.shape
    blk = _scan_block(rev, nblk)
    d = 1 if rev else 0
    width = 3 * BRANCH_W
    lane0 = 2 * GDN_H + GDN_H * d
    alog_row = jnp.zeros((1, LANE), F32).at[0, lane0:lane0 + GDN_H].set(a_log[d])
    dtb_row = jnp.zeros((1, LANE), F32).at[0, lane0:lane0 + GDN_H].set(dt_bias[d])
    if rev:
        in_specs, args = [_blk_spec(nbat, width, 0, blk)], [qkv]
    else:
        in_specs, args = _conv_inputs(p, P_GDN_QKV, conv_w, jnp.zeros((width,), F32), nblk)
    in_specs += [_blk_spec(nbat, 512, P_GDN_Z, blk), _blk_spec(nbat, LANE, P_GDN_SC, blk),
                 _const_spec((1, LANE)), _const_spec((1, LANE)), _const_spec((TOK_BLK, TOK_BLK)),
                 _const_spec((1, GDN_D))]
    args += [p, p, alog_row, dtb_row, _cumsum_matrix(rev), ng.reshape(1, -1)]
    return _scan_call(_gdn_kernel, "gdn", in_specs, args, o_f, rev=rev, nbat=nbat, s1=s1, nblk=nblk,
                      state_shape=(GDN_H, GDN_D, GDN_D), conv_width=width)


def _ssd_kernel(*refs, rev, nbat, nblk):
    if rev:
        (xbc_ref, dt_ref, z_ref, alog_ref, dtb_ref, ex_ref, lm_ref, dsk_ref, ng_ref, of_ref,
         y_ref, st_ref) = refs
        xbc = [xbc_ref[bi] for bi in range(nbat)]
    else:
        (xm_ref, xp_ref, xn_ref, cw_ref, cb_ref, dt_ref, z_ref, alog_ref, dtb_ref, ex_ref, lm_ref, dsk_ref, ng_ref,
         o_ref, xbcc_ref, st_ref) = refs
        xbc = []
        for bi in range(nbat):
            xbc.append(_conv_silu_block(xm_ref[bi], xp_ref[bi], xn_ref[bi], cw_ref[...], cb_ref[...],
                                        pl.program_id(0), nblk))
            xbcc_ref[bi] = xbc[bi]

    @pl.when(pl.program_id(0) == 0)
    def _():
        st_ref[...] = jnp.zeros_like(st_ref)

    d = 1 if rev else 0
    heads_per_g = M2_H // M2_G
    gw = heads_per_g * M2_P
    n_chunk = TOK_BLK // CHUNK
    mask = _chunk_mask(rev)
    lane = _iota2((TOK_BLK, gw), 1)
    ex = ex_ref[...]

    units = [(bi, g) for bi in range(nbat) for g in range(M2_G)]
    cqs, bks, xvs, b_es, eb_es, accs = [], [], [], [], [], []
    for bi in range(nbat):
        dt = _softplus(dt_ref[bi] + dtb_ref[...])
        loga = -jnp.exp(alog_ref[...]) * dt
        b8 = _select_rows(lm_ref[...], loga)
        b8_t = b8.T
        xv = xbc[bi][:, :BRANCH_W] * _select_cols(dt, ex)
        b_e = _select_cols(b8, ex)
        eb_e = jnp.exp(b_e)
        for g in range(M2_G):
            gs = slice(g * M2_N, (g + 1) * M2_N)
            xs_g = slice(g * gw, (g + 1) * gw)
            bk = xbc[bi][:, BRANCH_W + g * M2_N:BRANCH_W + (g + 1) * M2_N]
            cq = xbc[bi][:, BRANCH_W + (M2_G + g) * M2_N:BRANCH_W + (M2_G + g + 1) * M2_N]
            scores = _dot_nt(cq, bk)
            xv_g = xv[:, xs_g]
            acc = None
            for hh in range(heads_per_g):
                lh = M2_H * d + heads_per_g * g + hh
                diff = b8[:, lh:lh + 1] - b8_t[lh:lh + 1, :]
                dec = jnp.where(mask, jnp.exp(jnp.where(mask, diff, 0.0)), 0.0)
                term = _dot(scores * dec, jnp.where(_idiv(lane, M2_P) == hh, xv_g, 0.0))
                acc = term if acc is None else acc + term
            cqs.append(cq)
            bks.append(bk)
            xvs.append(xv_g)
            b_es.append(b_e[:, xs_g])
            eb_es.append(eb_e[:, xs_g])
            accs.append(acc)

    sts = [st_ref[bi, g] for bi, g in units]
    o_parts = [[None] * n_chunk for _ in units]
    for c in _chunk_order(rev):
        rows = slice(c * CHUNK, (c + 1) * CHUNK)
        last = c * CHUNK if rev else (c + 1) * CHUNK - 1
        for i in range(len(units)):
            b_c = b_es[i][rows]
            b_last = b_es[i][last:last + 1]
            o_parts[i][c] = eb_es[i][rows] * _dot(cqs[i][rows], sts[i])
            ds = _dot_tn(bks[i][rows], xvs[i][rows] * jnp.exp(b_last - b_c))
            sts[i] = jnp.exp(b_last) * sts[i] + ds
    o_units = []
    for i, (bi, g) in enumerate(units):
        st_ref[bi, g] = sts[i]
        o_units.append(accs[i] + jnp.concatenate(o_parts[i], axis=0))
    for bi in range(nbat):
        o = jnp.concatenate(o_units[bi * M2_G:(bi + 1) * M2_G], axis=1)
        if rev:
            y = (of_ref[bi] + o + dsk_ref[...] * xbc[bi][:, :BRANCH_W]) * _silu(z_ref[bi])
            y = y * lax.rsqrt(jnp.mean(y * y, axis=-1, keepdims=True) + RMS_EPS) * ng_ref[...]
            y_ref[bi] = y.astype(y_ref.dtype)
        else:
            o_ref[bi] = o


def _ssd_dir(p, xbc, conv_w, conv_b, a_log, dt_bias, d_skip, ng, o_f, *, rev, nblk):
    nbat, s1, _ = p.shape
    blk = _scan_block(rev, nblk)
    d = 1 if rev else 0
    lane0 = M2_H * d
    alog_row = jnp.zeros((1, LANE), F32).at[0, lane0:lane0 + M2_H].set(a_log[d])
    dtb_row = jnp.zeros((1, LANE), F32).at[0, lane0:lane0 + M2_H].set(dt_bias[d])
    ex = np.zeros((LANE, BRANCH_W), np.float32)
    for h in range(M2_H):
        ex[lane0 + h, h * M2_P:(h + 1) * M2_P] = 1.0
    dsk_row = jnp.repeat(d_skip, M2_P).reshape(1, BRANCH_W)
    if rev:
        in_specs, args = [_blk_spec(nbat, M2_CONV_CH, 0, blk)], [xbc]
    else:
        in_specs, args = _conv_inputs(p, P_M2_XBC, conv_w, conv_b, nblk)
    in_specs += [_blk_spec(nbat, LANE, P_M2_DT, blk), _blk_spec(nbat, 512, P_M2_Z, blk),
                 _const_spec((1, LANE)), _const_spec((1, LANE)), _const_spec((LANE, BRANCH_W)),
                 _const_spec((TOK_BLK, TOK_BLK)), _const_spec((1, BRANCH_W)), _const_spec((1, BRANCH_W))]
    args += [p, p, alog_row, dtb_row, jnp.asarray(ex, dtype=BF16), _cumsum_matrix(rev), dsk_row, ng.reshape(1, -1)]
    return _scan_call(_ssd_kernel, "ssd", in_specs, args, o_f, rev=rev, nbat=nbat, s1=s1, nblk=nblk,
                      state_shape=(M2_G, M2_N, (M2_H // M2_G) * M2_P), conv_width=M2_CONV_CH)


def _final_norm_kernel(x_ref, g_ref, o_ref):
    x = x_ref[...]
    o_ref[...] = x * lax.rsqrt(jnp.mean(x * x, axis=-1, keepdims=True) + RMS_EPS) * g_ref[...]


def _final_norm(xs, g, *, nbat, nblk, lc):
    D = xs.shape[1]
    cb = lc // TOK_BLK
    nlat = nblk - cb
    out = pl.pallas_call(
        _final_norm_kernel,
        grid=(nbat, nlat),
        in_specs=[pl.BlockSpec((TOK_BLK, D), lambda b, s: (b * nblk + cb + s, 0)),
                  pl.BlockSpec((1, D), lambda b, s: (0, 0))],
        out_specs=pl.BlockSpec((TOK_BLK, D), lambda b, s: (b * nlat + s, 0)),
        out_shape=jax.ShapeDtypeStruct((nbat * nlat * TOK_BLK, D), F32),
        compiler_params=_cparams(("arbitrary", "arbitrary")),
        name="final_norm",
    )(xs, g.reshape(1, D))
    return out.reshape(nbat, nlat * TOK_BLK, D)


def _row_tile(s1, limit):
    for tm in range(limit - limit % 16, 0, -16):
        if s1 % tm == 0:
            return tm
    raise ValueError(f"no row tile for sequence length {s1}")


def kernel(x, c, ctx, c_ctx, norm1_g, norm2_g, w_ada, b_ada, w_in, b_merge, gla_a2, gla_ab, gla_norm_g, na_rpb, gdn_conv, gdn_a_log, gdn_dt_bias, gdn_norm_g, m2_conv, m2_conv_b, m2_a_log, m2_dt_bias, m2_d, m2_norm_g, w_branch, w_out, w_ffn1, w_ffn3, w_ffn2, final_norm_g):
    nbat, t, D = x.shape
    lc = ctx.shape[1]
    depth = w_in.shape[0]
    assert D == D_MODEL and lc == TOK_BLK and t % TOK_BLK == 0 and t // GRID_W >= 3 * ROWS_PER_BLK
    assert nbat + 1 <= MOD_ROWS
    s1 = lc + t
    nblk = s1 // TOK_BLK
    tm = _row_tile(s1, 1056)
    geo = dict(s1=s1, lc=lc, nb=nbat)

    cvec = jnp.concatenate([c, c_ctx[None], jnp.zeros((MOD_ROWS - nbat - 1, D), F32)], axis=0)
    mods = _ada(cvec, w_ada, b_ada).reshape(depth, MOD_ROWS, 6, D).transpose(0, 2, 1, 3)
    cos, sin = _rope_tables(lc, t)
    wb_in, wb_branch, wb_out, wb_ffn1, wb_ffn3, wb_ffn2 = (
        w.astype(BF16) for w in (w_in, w_branch, w_out, w_ffn1, w_ffn3, w_ffn2))
    flat = lambda y: y.reshape(nbat * s1, BRANCH_W)

    for l in range(depth):
        sh1, sc1, g1, sh2, sc2, g2 = (mods[l, i] for i in range(6))

        if l == 0:
            xs, h = _concat_norm(ctx, x, norm1_g[l], sc1, sh1, nblk=nblk)
        else:
            h = _normmod(xs, norm1_g[l], sc1, sh1, tm=_row_tile(s1, 384), **geo)
        w_mix, w_gate = _split_in_weights(wb_in, l)
        p = _matmul(h, w_mix, tm=tm, tn=MIX_TILE, out_dtype=F32).reshape(nbat, s1, P_COLS)

        o_f = _gla_dir(p, cos, sin, gla_a2[l], gla_ab[l], gla_norm_g[l], None, rev=False, nblk=nblk)
        ya = _gla_dir(p, cos, sin, gla_a2[l], gla_ab[l], gla_norm_g[l], o_f, rev=True, nblk=nblk)

        yb = _na(p, _na_bias_tiles(na_rpb[l]), nblk=nblk)

        gdn_par = (gdn_conv[l], gdn_a_log[l], gdn_dt_bias[l], gdn_norm_g[l])
        o_f, qkv = _gdn_dir(p, None, *gdn_par, None, rev=False, nblk=nblk)
        yc, = _gdn_dir(p, qkv, *gdn_par, o_f, rev=True, nblk=nblk)

        ssd_par = (m2_conv[l], m2_conv_b[l], m2_a_log[l], m2_dt_bias[l], m2_d[l], m2_norm_g[l])
        o_f, xbc = _ssd_dir(p, None, *ssd_par, None, rev=False, nblk=nblk)
        yd, = _ssd_dir(p, xbc, *ssd_par, o_f, rev=True, nblk=nblk)

        ys = (flat(ya), flat(yb), flat(yc), flat(yd))
        merged = _merge(h, w_gate, b_merge[l], ys, wb_branch, l, tm=tm, tn=512)
        xs, h2 = _outproj_norm(merged, wb_out, l, xs, g1, norm2_g[l], sc2, sh2,
                               tm=_row_tile(s1, 384), **geo)

        u = _ffn_up(h2, wb_ffn1, wb_ffn3, l, tm=_row_tile(s1, 2112), tn=512)
        xs = _matmul_residual(u, wb_ffn2, l, xs, g2, tm=tm, tn=512, name="ffn_down", **geo)

    return _final_norm(xs, final_norm_g, nbat=nbat, nblk=nblk, lc=lc)
```

```python
import functools

import numpy as np
import jax
import jax.numpy as jnp
from jax import lax
from jax.experimental import pallas as pl
from jax.experimental.pallas import tpu as pltpu

D_MODEL = 2048
GRID_W = 64
N_BRANCH = 4
BRANCH_W = D_MODEL // 4
CHUNK = 64
CONV_W = 5
RMS_EPS = 1e-6
NEG_INF = -1e30
ROPE_BASE = 10000.0
GLA_H = 4
GLA_DV = BRANCH_W // GLA_H
GLA_DK = GLA_DV // 2
GLA_LR = 16
GLA_TAU = 16.0
NA_H = 4
NA_D = BRANCH_W // NA_H
NA_WIN_R = 8
NA_WIN_C = 16
GDN_H = 4
GDN_D = BRANCH_W // GDN_H
M2_P = 64
M2_H = BRANCH_W // M2_P
M2_N = 128
M2_G = 2
M2_CONV_CH = BRANCH_W + 2 * M2_G * M2_N
D_FF = ((8 * D_MODEL + 3 * 256 - 1) // (3 * 256)) * 256
GLA_IN = 2 * GLA_H * GLA_DK + 2 * BRANCH_W + 2 * GLA_LR
NA_IN = 3 * BRANCH_W
GDN_IN = 4 * BRANCH_W + 4 * GDN_H
M2_IN = BRANCH_W + M2_CONV_CH + 2 * M2_H
MIX_IN = GLA_IN + NA_IN + GDN_IN + M2_IN

F32 = jnp.float32
BF16 = jnp.bfloat16

LANE = 128
SUBLANE = 8
V7X_VMEM_BYTES = 64 * 1024 * 1024
VMEM_LIMIT = V7X_VMEM_BYTES - 8 * 1024 * 1024

TOK_BLK = 4 * CHUNK
ROWS_PER_BLK = TOK_BLK // GRID_W
HALO = SUBLANE
MOD_ROWS = 8

P_GDN_QKV = 0
P_GLA_V = 1536
P_M2_XBC = 2048
P_GLA_G = 3072
P_NA_Q, P_NA_K, P_NA_V = 3584, 4096, 4608
P_GDN_Z, P_M2_Z = 5120, 5632
P_GLA_Q, P_GLA_K = 6144, 6400
P_GLA_LR, P_GDN_SC, P_M2_DT = 6656, 6784, 6912
P_COLS = 7168
MIX_TILE = 1024


def _mix_fields():
    gla, na, gdn, m2 = 0, GLA_IN, GLA_IN + NA_IN, GLA_IN + NA_IN + GDN_IN
    return sorted([
        (P_GLA_V, gla + 512, 512), (P_GLA_G, gla + 1024, 512),
        (P_NA_Q, na, 512), (P_NA_K, na + 512, 512), (P_NA_V, na + 1024, 512),
        (P_GDN_QKV, gdn, 1536), (P_GDN_Z, gdn + 1536, 512),
        (P_M2_Z, m2, 512), (P_M2_XBC, m2 + 512, M2_CONV_CH),
        (P_GLA_Q, gla, 256), (P_GLA_K, gla + 256, 256),
        (P_GLA_LR, gla + 1536, 2 * GLA_LR), (P_GDN_SC, gdn + 2048, 4 * GDN_H),
        (P_M2_DT, m2 + 512 + M2_CONV_CH, 2 * M2_H)])


def _cparams(sem):
    return pltpu.CompilerParams(dimension_semantics=sem, vmem_limit_bytes=VMEM_LIMIT)


def _wprep_kernel(w_ref, mix_ref, gate_ref):
    col = 0
    for dst, src, n in _mix_fields():
        if dst > col:
            mix_ref[:, col:dst] = jnp.zeros((mix_ref.shape[0], dst - col), mix_ref.dtype)
        mix_ref[:, dst:dst + n] = w_ref[:, src:src + n].astype(mix_ref.dtype)
        col = dst + n
    mix_ref[:, col:] = jnp.zeros((mix_ref.shape[0], P_COLS - col), mix_ref.dtype)
    gate_ref[...] = w_ref[:, MIX_IN:].astype(gate_ref.dtype)


def _split_in_weights(w_in, layer):
    _, D, N = w_in.shape
    rb = 256
    return pl.pallas_call(
        _wprep_kernel,
        grid=(D // rb,),
        in_specs=[pl.BlockSpec((None, rb, N), lambda r: (layer, r, 0))],
        out_specs=[pl.BlockSpec((rb, P_COLS), lambda r: (r, 0)),
                   pl.BlockSpec((rb, N - MIX_IN), lambda r: (r, 0))],
        out_shape=[jax.ShapeDtypeStruct((D, P_COLS), BF16), jax.ShapeDtypeStruct((D, N - MIX_IN), BF16)],
        compiler_params=_cparams(("arbitrary",)),
        name="split_in_weights",
    )(w_in)


def _sigmoid(x):
    return 1.0 / (1.0 + jnp.exp(-x))


def _silu(x):
    return x * _sigmoid(x)


def _softplus(x):
    return jnp.maximum(x, 0.0) + jnp.log1p(jnp.exp(-jnp.abs(x)))


def _dot(a, b):
    return jnp.dot(a.astype(BF16), b.astype(BF16), preferred_element_type=F32)


def _dot_nt(a, b):
    return lax.dot_general(a.astype(BF16), b.astype(BF16), (((1,), (1,)), ((), ())),
                           preferred_element_type=F32)


def _dot_tn(a, b):
    return lax.dot_general(a.astype(BF16), b.astype(BF16), (((0,), (0,)), ((), ())),
                           preferred_element_type=F32)


def _split3(a):
    hi = a.astype(BF16)
    r1 = a - hi.astype(F32)
    mid = r1.astype(BF16)
    return hi, mid, (r1 - mid.astype(F32)).astype(BF16)


def _select_rows(m01, a):
    d = lambda y: jnp.dot(m01, y, preferred_element_type=F32)
    hi, mid, lo = _split3(a)
    return d(hi) + (d(mid) + d(lo))


def _select_cols(a, m01):
    d = lambda y: jnp.dot(y, m01, preferred_element_type=F32)
    hi, mid, lo = _split3(a)
    return d(hi) + (d(mid) + d(lo))


def _split(a):
    hi = a.astype(BF16)
    return hi, (a - hi.astype(F32)).astype(BF16)


def _dot_split(a, b):
    (ah, al), (bh, bl) = a, b
    d = lambda x, y: jnp.dot(x, y, preferred_element_type=F32)
    return d(ah, bh) + (d(ah, bl) + d(al, bh))


def _iota2(shape, dim):
    return lax.broadcasted_iota(jnp.int32, shape, dim)


def _idiv(x, n):
    assert n & (n - 1) == 0
    return lax.shift_right_logical(x, jnp.int32(n.bit_length() - 1))


def _imod(x, n):
    assert n & (n - 1) == 0
    return x & (n - 1)


def _chunk_mask(rev, strict=False):
    r = _iota2((TOK_BLK, TOK_BLK), 0)
    c = _iota2((TOK_BLK, TOK_BLK), 1)
    same = _idiv(r, CHUNK) == _idiv(c, CHUNK)
    if rev:
        tri = (c > r) if strict else (c >= r)
    else:
        tri = (c < r) if strict else (c <= r)
    return same & tri


def _chunk_order(rev):
    n = TOK_BLK // CHUNK
    return list(range(n - 1, -1, -1)) if rev else list(range(n))


def _row_split(i, tm, s1):
    r0 = i * tm
    return lax.div(r0, jnp.int32(s1)), lax.rem(r0, jnp.int32(s1))


def _pick_mod(ref, b, nb, is_ctx):
    return jnp.where(is_ctx, ref[nb:nb + 1, :], ref[pl.ds(b, 1), :])


def _ada_kernel(c_ref, w_ref, b_ref, o_ref):
    a = _silu(c_ref[...])
    o_ref[...] = _dot(a, w_ref[...]) + b_ref[...]


def _ada(cvec, w_ada, b_ada):
    L, D, N = w_ada.shape
    tn = 1024
    return pl.pallas_call(
        _ada_kernel,
        grid=(L, N // tn),
        in_specs=[pl.BlockSpec((MOD_ROWS, D), lambda l, j: (0, 0)),
                  pl.BlockSpec((None, D, tn), lambda l, j: (l, 0, j)),
                  pl.BlockSpec((None, 1, tn), lambda l, j: (l, 0, j))],
        out_specs=pl.BlockSpec((None, MOD_ROWS, tn), lambda l, j: (l, 0, j)),
        out_shape=jax.ShapeDtypeStruct((L, MOD_ROWS, N), F32),
        compiler_params=_cparams(("arbitrary", "arbitrary")),
        name="ada_mod",
    )(cvec, w_ada, b_ada.reshape(L, 1, N))


def _normmod_kernel(x_ref, g_ref, sc_ref, sh_ref, o_ref, *, nb):
    row = jnp.where(pl.program_id(1) == 0, nb, pl.program_id(0))
    x = x_ref[...]
    y = x * lax.rsqrt(jnp.mean(x * x, axis=-1, keepdims=True) + RMS_EPS) * g_ref[...]
    o_ref[...] = (y * (1.0 + sc_ref[pl.ds(row, 1), :]) + sh_ref[pl.ds(row, 1), :]).astype(o_ref.dtype)


def _normmod(xs, g, sc, sh, *, nbat, nblk):
    R, D = xs.shape
    const = lambda b, s: (0, 0)
    row = lambda b, s: (b * nblk + s, 0)
    return pl.pallas_call(
        functools.partial(_normmod_kernel, nb=nbat),
        grid=(nbat, nblk),
        in_specs=[pl.BlockSpec((TOK_BLK, D), row), pl.BlockSpec((1, D), const),
                  pl.BlockSpec((MOD_ROWS, D), const), pl.BlockSpec((MOD_ROWS, D), const)],
        out_specs=pl.BlockSpec((TOK_BLK, D), row),
        out_shape=jax.ShapeDtypeStruct((R, D), BF16),
        compiler_params=_cparams(("arbitrary", "arbitrary")),
        name="norm_mod",
    )(xs, g.reshape(1, D), sc, sh)


def _concat_norm_kernel(ctx_ref, x_ref, g_ref, sc_ref, sh_ref, xs_ref, h_ref, *, nb):
    b, s = pl.program_id(0), pl.program_id(1)
    is_ctx = s == 0
    x = jnp.where(is_ctx, ctx_ref[...], x_ref[...])
    xs_ref[...] = x
    row = jnp.where(is_ctx, nb, b)
    y = x * lax.rsqrt(jnp.mean(x * x, axis=-1, keepdims=True) + RMS_EPS) * g_ref[...]
    h_ref[...] = (y * (1.0 + sc_ref[pl.ds(row, 1), :]) + sh_ref[pl.ds(row, 1), :]).astype(h_ref.dtype)


def _concat_norm(ctx, x, g, sc, sh, *, nblk):
    nbat, t, D = x.shape
    R = nbat * nblk * TOK_BLK
    const = lambda b, s: (0, 0)
    row = lambda b, s: (b * nblk + s, 0)
    return pl.pallas_call(
        functools.partial(_concat_norm_kernel, nb=nbat),
        grid=(nbat, nblk),
        in_specs=[pl.BlockSpec((None, TOK_BLK, D), lambda b, s: (b, 0, 0)),
                  pl.BlockSpec((None, TOK_BLK, D), lambda b, s: (b, jnp.maximum(s - 1, 0), 0)),
                  pl.BlockSpec((1, D), const), pl.BlockSpec((MOD_ROWS, D), const), pl.BlockSpec((MOD_ROWS, D), const)],
        out_specs=[pl.BlockSpec((TOK_BLK, D), row), pl.BlockSpec((TOK_BLK, D), row)],
        out_shape=[jax.ShapeDtypeStruct((R, D), F32), jax.ShapeDtypeStruct((R, D), BF16)],
        compiler_params=_cparams(("arbitrary", "arbitrary")),
        name="concat_norm",
    )(ctx, x, g.reshape(1, D), sc, sh)


def _mm_kernel(a_ref, w_ref, o_ref):
    o_ref[...] = jnp.dot(a_ref[...], w_ref[...], preferred_element_type=F32).astype(o_ref.dtype)


def _matmul(a, w, *, tm, tn, out_dtype):
    R, K = a.shape
    N = w.shape[1]
    assert R % tm == 0 and N % tn == 0
    return pl.pallas_call(
        _mm_kernel,
        grid=(R // tm, N // tn),
        in_specs=[pl.BlockSpec((tm, K), lambda i, j: (i, 0)),
                  pl.BlockSpec((K, tn), lambda i, j: (0, j))],
        out_specs=pl.BlockSpec((tm, tn), lambda i, j: (i, j)),
        out_shape=jax.ShapeDtypeStruct((R, N), out_dtype),
        compiler_params=_cparams(("arbitrary", "arbitrary")),
        name="mix_proj",
    )(a, w)


def _mm_res_kernel(a_ref, w_ref, x_ref, gt_ref, o_ref, *, tm, s1, lc, nb):
    b, pos0 = _row_split(pl.program_id(0), tm, s1)
    is_ctx = (pos0 + _iota2((tm, 1), 0)) < lc
    acc = jnp.dot(a_ref[...], w_ref[...], preferred_element_type=F32)
    o_ref[...] = x_ref[...] + _pick_mod(gt_ref, b, nb, is_ctx) * acc


def _matmul_residual(a, w, layer, xs, gate, *, tm, tn, s1, lc, nb, name):
    R, K = a.shape
    N = w.shape[2]
    assert R % tm == 0 and N % tn == 0
    return pl.pallas_call(
        functools.partial(_mm_res_kernel, tm=tm, s1=s1, lc=lc, nb=nb),
        grid=(R // tm, N // tn),
        in_specs=[pl.BlockSpec((tm, K), lambda i, j: (i, 0)),
                  pl.BlockSpec((None, K, tn), lambda i, j: (layer, 0, j)),
                  pl.BlockSpec((tm, tn), lambda i, j: (i, j)),
                  pl.BlockSpec((MOD_ROWS, tn), lambda i, j: (0, j))],
        out_specs=pl.BlockSpec((tm, tn), lambda i, j: (i, j)),
        out_shape=jax.ShapeDtypeStruct((R, N), F32),
        compiler_params=_cparams(("arbitrary", "arbitrary")),
        name=name,
    )(a, w, xs, gate)


def _outproj_norm_kernel(a_ref, w_ref, x_ref, gt_ref, g_ref, sc_ref, sh_ref, xo_ref, h_ref, *, tm, s1, lc, nb):
    b, pos0 = _row_split(pl.program_id(0), tm, s1)
    is_ctx = (pos0 + _iota2((tm, 1), 0)) < lc
    acc = jnp.dot(a_ref[...], w_ref[...], preferred_element_type=F32)
    x = x_ref[...] + _pick_mod(gt_ref, b, nb, is_ctx) * acc
    xo_ref[...] = x
    y = x * lax.rsqrt(jnp.mean(x * x, axis=-1, keepdims=True) + RMS_EPS) * g_ref[...]
    h_ref[...] = (y * (1.0 + _pick_mod(sc_ref, b, nb, is_ctx)) + _pick_mod(sh_ref, b, nb, is_ctx)).astype(h_ref.dtype)


def _outproj_norm(a, w, layer, xs, gate, g, sc, sh, *, tm, s1, lc, nb):
    R, K = a.shape
    D = w.shape[2]
    row = lambda i: (i, 0)
    const = lambda i: (0, 0)
    return pl.pallas_call(
        functools.partial(_outproj_norm_kernel, tm=tm, s1=s1, lc=lc, nb=nb),
        grid=(R // tm,),
        in_specs=[pl.BlockSpec((tm, K), row), pl.BlockSpec((None, K, D), lambda i: (layer, 0, 0)),
                  pl.BlockSpec((tm, D), row),
                  pl.BlockSpec((MOD_ROWS, D), const), pl.BlockSpec((1, D), const),
                  pl.BlockSpec((MOD_ROWS, D), const), pl.BlockSpec((MOD_ROWS, D), const)],
        out_specs=[pl.BlockSpec((tm, D), row), pl.BlockSpec((tm, D), row)],
        out_shape=[jax.ShapeDtypeStruct((R, D), F32), jax.ShapeDtypeStruct((R, D), BF16)],
        compiler_params=_cparams(("arbitrary",)),
        name="out_proj",
    )(a, w, xs, gate, g.reshape(1, D), sc, sh)


def _ffn_up_kernel(a_ref, w1_ref, w3_ref, o_ref):
    a = a_ref[...]
    u = jnp.dot(a, w1_ref[...], preferred_element_type=F32)
    v = jnp.dot(a, w3_ref[...], preferred_element_type=F32)
    o_ref[...] = (_silu(u) * v).astype(o_ref.dtype)


def _ffn_up(a, w1, w3, layer, *, tm, tn):
    R, K = a.shape
    N = w1.shape[2]
    assert R % tm == 0 and N % tn == 0
    wspec = pl.BlockSpec((None, K, tn), lambda i, j: (layer, 0, j))
    return pl.pallas_call(
        _ffn_up_kernel,
        grid=(R // tm, N // tn),
        in_specs=[pl.BlockSpec((tm, K), lambda i, j: (i, 0)), wspec, wspec],
        out_specs=pl.BlockSpec((tm, tn), lambda i, j: (i, j)),
        out_shape=jax.ShapeDtypeStruct((R, N), BF16),
        compiler_params=_cparams(("arbitrary", "arbitrary")),
        name="ffn_up",
    )(a, w1, w3)


def _merge_kernel(h_ref, g0_ref, g1_ref, g2_ref, g3_ref, bm_ref, ya_ref, yb_ref, yc_ref, yd_ref, wb_ref, o_ref):
    h = h_ref[...]
    acc = None
    branches = zip((g0_ref, g1_ref, g2_ref, g3_ref), (ya_ref, yb_ref, yc_ref, yd_ref))
    for i, (wg_ref, y_ref) in enumerate(branches):
        gate = _sigmoid(jnp.dot(h, wg_ref[...], preferred_element_type=F32) + bm_ref[i])
        term = gate * jnp.dot(y_ref[...], wb_ref[i], preferred_element_type=F32)
        acc = term if acc is None else acc + term
    o_ref[...] = acc.astype(o_ref.dtype)


def _merge(h, wg, b_merge, ys, wb, layer, *, tm, tn):
    R, D = h.shape
    N = wb.shape[3]
    assert R % tm == 0 and N % tn == 0
    nj = N // tn
    yspec = pl.BlockSpec((tm, BRANCH_W), lambda i, j: (i, 0))
    gspec = lambda br: pl.BlockSpec((D, tn), lambda i, j: (0, br * nj + j))
    return pl.pallas_call(
        _merge_kernel,
        grid=(R // tm, nj),
        in_specs=[pl.BlockSpec((tm, D), lambda i, j: (i, 0)),
                  gspec(0), gspec(1), gspec(2), gspec(3),
                  pl.BlockSpec((N_BRANCH, 1, tn), lambda i, j: (0, 0, j)),
                  yspec, yspec, yspec, yspec,
                  pl.BlockSpec((None, N_BRANCH, BRANCH_W, tn), lambda i, j: (layer, 0, 0, j))],
        out_specs=pl.BlockSpec((tm, tn), lambda i, j: (i, j)),
        out_shape=jax.ShapeDtypeStruct((R, N), BF16),
        compiler_params=_cparams(("arbitrary", "arbitrary")),
        name="merge",
    )(h, wg, wg, wg, wg, b_merge.reshape(N_BRANCH, 1, N), *ys, wb)


def _conv_silu_block(xm, xp, xn, w, bias, blk, nblk):
    prev_ok = blk >= 2
    next_ok = (blk >= 1) & (blk < nblk - 1)
    rows = TOK_BLK + 2 * HALO
    half = CONV_W // 2
    xpad = jnp.concatenate([jnp.where(prev_ok, xp, 0.0), xm, jnp.where(next_ok, xn, 0.0)], axis=0)
    acc = bias + w[half:half + 1, :] * xm
    for j in range(CONV_W):
        if j != half:
            tap = pltpu.roll(xpad, (half - j) % rows, 0)[HALO:HALO + TOK_BLK]
            acc = acc + w[j:j + 1, :] * tap
    return _silu(acc)


def _conv_inputs(p, col0, w, bias, nblk):
    nbat, s1, _ = p.shape
    C = w.shape[1]
    assert col0 % C == 0
    cb = col0 // C
    hb = TOK_BLK // HALO
    nhalo = s1 // HALO
    wpad = jnp.concatenate([w, jnp.zeros((SUBLANE - CONV_W, C), F32)], axis=0)
    specs = [pl.BlockSpec((nbat, TOK_BLK, C), lambda r: (0, r, cb)),
             pl.BlockSpec((nbat, HALO, C), lambda r: (0, jnp.maximum(r * hb - 1, 0), cb)),
             pl.BlockSpec((nbat, HALO, C), lambda r: (0, jnp.minimum((r + 1) * hb, nhalo - 1), cb)),
             _const_spec((SUBLANE, C)), _const_spec((1, C))]
    return specs, [p, p, p, wpad, bias.reshape(1, C)]


def _scan_block(rev, nblk):
    if rev:
        return lambda s: jnp.where(s == 0, 0, nblk - s)
    return lambda s: s


def _blk_spec(nbat, width, col, blk):
    assert col % width == 0
    return pl.BlockSpec((nbat, TOK_BLK, width), lambda s: (0, blk(s), col // width))


def _const_spec(shape):
    return pl.BlockSpec(shape, lambda s: (0,) * len(shape))


def _head_rmsnorm_gate(o, g, gate, n_head, width):
    outs = []
    for h in range(n_head):
        oh = o[:, h * width:(h + 1) * width]
        yh = oh * lax.rsqrt(jnp.mean(oh * oh, axis=-1, keepdims=True) + RMS_EPS) * g
        outs.append(yh * _silu(gate[:, h * width:(h + 1) * width]))
    return jnp.concatenate(outs, axis=1)


def _scan_call(kern, name, in_specs, args, o_f, *, rev, nbat, s1, nblk, state_shape, conv_width=0):
    blk = _scan_block(rev, nblk)
    ospec = pl.BlockSpec((nbat, TOK_BLK, BRANCH_W), lambda s: (0, blk(s), 0))
    out_specs = [ospec]
    out_shape = [jax.ShapeDtypeStruct((nbat, s1, BRANCH_W), BF16 if rev else F32)]
    if rev:
        in_specs = in_specs + [ospec]
        args = args + [o_f]
    elif conv_width:
        out_specs.append(pl.BlockSpec((nbat, TOK_BLK, conv_width), lambda s: (0, s, 0)))
        out_shape.append(jax.ShapeDtypeStruct((nbat, s1, conv_width), F32))
    return pl.pallas_call(
        functools.partial(kern, rev=rev, nbat=nbat, nblk=nblk),
        grid=(nblk,),
        in_specs=in_specs,
        out_specs=out_specs,
        out_shape=out_shape,
        scratch_shapes=[pltpu.VMEM((nbat,) + state_shape, F32)],
        compiler_params=_cparams(("arbitrary",)),
        name=name + ("_bwd" if rev else "_fwd"),
    )(*args)


def _gla_kernel(q_ref, k_ref, v_ref, g_ref, lr_ref, cos_ref, sin_ref, a2_ref, ab_ref, lm_ref, ng_ref,
                *rest, rev, nbat, nblk):
    if rev:
        of_ref, y_ref, st_ref = rest
    else:
        o_ref, st_ref = rest

    @pl.when(pl.program_id(0) == 0)
    def _():
        st_ref[...] = jnp.zeros_like(st_ref)

    hk = GLA_H * GLA_DK
    lane = _iota2((TOK_BLK, hk), 1)
    first_half = _imod(lane, GLA_DK) < (GLA_DK // 2)
    head_of_lane = _idiv(lane, GLA_DK)
    cos = cos_ref[...]
    sin = sin_ref[...]
    mask = _chunk_mask(rev)
    n_chunk = TOK_BLK // CHUNK
    batch = range(nbat)

    def rope(x):
        partner = jnp.where(first_half, pltpu.roll(x, hk - GLA_DK // 2, 1), pltpu.roll(x, GLA_DK // 2, 1))
        return x * cos + partner * sin

    ks, vs, bs, q_ins, o_intras = [], [], [], [], []
    for bi in batch:
        q = rope(q_ref[bi]) * GLA_DK ** -0.5
        k = rope(k_ref[bi])
        v = v_ref[bi]
        loga = -_softplus(-(_dot(lr_ref[bi], a2_ref[...]) + ab_ref[...])) / GLA_TAU
        b = _select_rows(lm_ref[...], loga)
        q_in = q * jnp.exp(b)
        k_in = k * jnp.exp(-b)
        o_heads = []
        for h in range(GLA_H):
            att = _dot_nt(jnp.where(head_of_lane == h, q_in, 0.0), k_in)
            att = jnp.where(mask, att, 0.0)
            o_heads.append(_dot(att, v[:, h * GLA_DV:(h + 1) * GLA_DV]))
        ks.append(k)
        vs.append(v)
        bs.append(b)
        q_ins.append(q_in)
        o_intras.append(jnp.concatenate(o_heads, axis=1))

    sts = [st_ref[bi] for bi in batch]
    diag = _idiv(_iota2(sts[0].shape, 0), GLA_DV) == _idiv(_iota2(sts[0].shape, 1), GLA_DK)
    o_inter = [[None] * n_chunk for _ in batch]
    for c in _chunk_order(rev):
        rows = slice(c * CHUNK, (c + 1) * CHUNK)
        for bi in batch:
            b_c = bs[bi][rows]
            b_last = b_c[0:1] if rev else b_c[CHUNK - 1:CHUNK]
            o_inter[bi][c] = _dot_nt(q_ins[bi][rows], sts[bi])
            k_end = ks[bi][rows] * jnp.exp(b_last - b_c)
            ds = _dot_tn(vs[bi][rows], k_end)
            sts[bi] = sts[bi] * jnp.exp(b_last) + jnp.where(diag, ds, 0.0)
    for bi in batch:
        st_ref[bi] = sts[bi]
        o = o_intras[bi] + jnp.concatenate(o_inter[bi], axis=0)
        if rev:
            y = _head_rmsnorm_gate(of_ref[bi] + o, ng_ref[...], g_ref[bi], GLA_H, GLA_DV)
            y_ref[bi] = y.astype(y_ref.dtype)
        else:
            o_ref[bi] = o


def _cumsum_matrix(rev):
    r = np.arange(TOK_BLK)[:, None]
    c = np.arange(TOK_BLK)[None, :]
    same = (r // CHUNK) == (c // CHUNK)
    tri = (c >= r) if rev else (c <= r)
    return jnp.asarray((same & tri).astype(np.float32), dtype=BF16)


def _gla_dir(p, cos, sin, a2, ab, ng, o_f, *, rev, nblk):
    nbat, s1, _ = p.shape
    blk = _scan_block(rev, nblk)
    d = 1 if rev else 0
    a2d = jnp.zeros((LANE, GLA_H * GLA_DK), F32).at[d * GLA_LR:(d + 1) * GLA_LR].set(a2[d])
    tab = pl.BlockSpec((TOK_BLK, GLA_H * GLA_DK), lambda s: (blk(s), 0))
    in_specs = [_blk_spec(nbat, 256, P_GLA_Q, blk), _blk_spec(nbat, 256, P_GLA_K, blk),
                _blk_spec(nbat, 512, P_GLA_V, blk), _blk_spec(nbat, 512, P_GLA_G, blk),
                _blk_spec(nbat, LANE, P_GLA_LR, blk), tab, tab,
                _const_spec((LANE, 256)), _const_spec((1, 256)), _const_spec((TOK_BLK, TOK_BLK)),
                _const_spec((1, GLA_DV))]
    args = [p, p, p, p, p, cos, sin, a2d, ab[d].reshape(1, -1), _cumsum_matrix(rev), ng.reshape(1, -1)]
    return _scan_call(_gla_kernel, "gla", in_specs, args, o_f, rev=rev, nbat=nbat, s1=s1, nblk=nblk,
                      state_shape=(GLA_H * GLA_DV, GLA_H * GLA_DK))[0]


def _rope_tables(lc, t):
    n_freq = GLA_DK // 4
    freqs = ROPE_BASE ** (-jnp.arange(n_freq, dtype=F32) / n_freq)
    tt = jnp.arange(t)
    row = (tt // GRID_W).astype(F32)
    col = (tt % GRID_W).astype(F32)
    ang = jnp.concatenate([row[:, None] * freqs, col[:, None] * freqs], axis=-1)
    cos, sin = jnp.cos(ang), jnp.sin(ang)
    cos = jnp.concatenate([jnp.ones((lc, GLA_DK // 2), F32), cos], axis=0)
    sin = jnp.concatenate([jnp.zeros((lc, GLA_DK // 2), F32), sin], axis=0)
    cos_h = jnp.concatenate([cos, cos], axis=1)
    sin_h = jnp.concatenate([-sin, sin], axis=1)
    return jnp.tile(cos_h, (1, GLA_H)), jnp.tile(sin_h, (1, GLA_H))


def _na_kernel(q_ref, kp_ref, kc_ref, kn_ref, kx_ref, vp_ref, vc_ref, vn_ref, vx_ref, bias_ref, y_ref, *, nbat):
    scale = NA_D ** -0.5
    for bi in range(nbat):
        outs = []
        for h in range(NA_H):
            hs = slice(h * NA_D, (h + 1) * NA_D)
            qh = (q_ref[bi, :, hs] * scale).astype(BF16)
            s = jnp.concatenate(
                [_dot_nt(qh, kp_ref[bi, :, hs]) + bias_ref[0, h],
                 _dot_nt(qh, kc_ref[bi, :, hs]) + bias_ref[1, h],
                 _dot_nt(qh, kn_ref[bi, :, hs]) + bias_ref[2, h],
                 _dot_nt(qh, kx_ref[bi, :, hs])], axis=1)
            m = jnp.max(s, axis=-1, keepdims=True)
            e = jnp.exp(s - m)
            o = (_dot(e[:, 0:TOK_BLK], vp_ref[bi, :, hs]) + _dot(e[:, TOK_BLK:2 * TOK_BLK], vc_ref[bi, :, hs])
                 + _dot(e[:, 2 * TOK_BLK:3 * TOK_BLK], vn_ref[bi, :, hs])
                 + _dot(e[:, 3 * TOK_BLK:], vx_ref[bi, :, hs]))
            outs.append(o / jnp.sum(e, axis=-1, keepdims=True))
        y_ref[bi] = jnp.concatenate(outs, axis=1).astype(y_ref.dtype)


def _na_bias_tiles(rpb):
    edge = GRID_W - NA_WIN_C
    ext = jnp.concatenate([jnp.repeat(rpb[..., :1], edge, axis=-1), rpb,
                           jnp.repeat(rpb[..., -1:], edge, axis=-1)], axis=-1)
    toep = jnp.stack([ext[..., GRID_W - 1 - qc:2 * GRID_W - 1 - qc] for qc in range(GRID_W)], axis=-2)
    qc = np.arange(GRID_W)[:, None]
    kc = np.arange(GRID_W)[None, :]
    c0 = np.clip(qc - NA_WIN_C // 2, 0, GRID_W - NA_WIN_C)
    col_ok = (kc >= c0) & (kc < c0 + NA_WIN_C)
    toep = jnp.where(jnp.asarray(col_ok), toep, NEG_INF).astype(F32)
    masked = jnp.full((NA_H, GRID_W, GRID_W), NEG_INF, F32)
    kinds = []
    for kind in range(3):
        offs = []
        for off in (-1, 0, 1):
            rows = []
            for qr in range(ROWS_PER_BLK):
                start = (0, qr - NA_WIN_R // 2, ROWS_PER_BLK - NA_WIN_R)[kind]
                cols = []
                for kb in range(ROWS_PER_BLK):
                    kr = kb + ROWS_PER_BLK * off
                    ok = start <= kr < start + NA_WIN_R
                    cols.append(toep[:, kr - qr + NA_WIN_R - 1] if ok else masked)
                rows.append(jnp.concatenate(cols, axis=-1))
            offs.append(jnp.concatenate(rows, axis=-2))
        kinds.append(jnp.stack(offs))
    kinds.append(jnp.full_like(kinds[0], NEG_INF))
    return jnp.stack(kinds)


def _na(p, bias, *, nblk):
    nbat, s1, _ = p.shape

    def kind(s):
        return jnp.where(s == 0, 3, jnp.where(s == 1, 0, jnp.where(s == nblk - 1, 2, 1)))

    prev = lambda s: jnp.maximum(s - 1, 1)
    cur = lambda s: s
    nxt = lambda s: jnp.minimum(s + 1, nblk - 1)
    ctx = lambda s: 0
    spec = lambda col, blk: _blk_spec(nbat, BRANCH_W, col, blk)
    return pl.pallas_call(
        functools.partial(_na_kernel, nbat=nbat),
        grid=(nblk,),
        in_specs=[spec(P_NA_Q, cur),
                  spec(P_NA_K, prev), spec(P_NA_K, cur), spec(P_NA_K, nxt), spec(P_NA_K, ctx),
                  spec(P_NA_V, prev), spec(P_NA_V, cur), spec(P_NA_V, nxt), spec(P_NA_V, ctx),
                  pl.BlockSpec((None, 3, NA_H, TOK_BLK, TOK_BLK), lambda s: (kind(s), 0, 0, 0, 0))],
        out_specs=pl.BlockSpec((nbat, TOK_BLK, BRANCH_W), lambda s: (0, s, 0)),
        out_shape=jax.ShapeDtypeStruct((nbat, s1, BRANCH_W), BF16),
        compiler_params=_cparams(("arbitrary",)),
        name="nbr_attn",
    )(p, p, p, p, p, p, p, p, p, bias)


def _l2norm(x):
    return x * lax.rsqrt(jnp.sum(x * x, axis=-1, keepdims=True) + RMS_EPS)


def _gdn_kernel(*refs, rev, nbat, nblk):
    if rev:
        qkv_ref, z_ref, sc_ref, alog_ref, dtb_ref, lm_ref, ng_ref, of_ref, y_ref, st_ref = refs
        qkv = [qkv_ref[bi] for bi in range(nbat)]
    else:
        (xm_ref, xp_ref, xn_ref, cw_ref, cb_ref, z_ref, sc_ref, alog_ref, dtb_ref, lm_ref, ng_ref,
         o_ref, qkvc_ref, st_ref) = refs
        qkv = []
        for bi in range(nbat):
            qkv.append(_conv_silu_block(xm_ref[bi], xp_ref[bi], xn_ref[bi], cw_ref[...], cb_ref[...],
                                        pl.program_id(0), nblk))
            qkvc_ref[bi] = qkv[bi]

    @pl.when(pl.program_id(0) == 0)
    def _():
        st_ref[...] = jnp.zeros_like(st_ref)

    d = 1 if rev else 0
    m_incl = _chunk_mask(rev)
    m_strict = _chunk_mask(rev, strict=True)
    eye = (_iota2((TOK_BLK, TOK_BLK), 0) == _iota2((TOK_BLK, TOK_BLK), 1)).astype(F32)
    n_chunk = TOK_BLK // CHUNK

    units = [(bi, h) for bi in range(nbat) for h in range(GDN_H)]
    qs, ks, bcols, ebcs, attns, nmats, rhss = [], [], [], [], [], [], []
    for bi in range(nbat):
        sc = sc_ref[bi]
        beta_all = _sigmoid(sc)
        g_all = -jnp.exp(alog_ref[...]) * _softplus(sc + dtb_ref[...])
        b_all = _select_rows(lm_ref[...], g_all)
        b_all_t = b_all.T
        eb_all = jnp.exp(b_all)
        for h in range(GDN_H):
            head = lambda part: qkv[bi][:, part * BRANCH_W + h * GDN_D:part * BRANCH_W + (h + 1) * GDN_D]
            qh = _l2norm(head(0)) * GDN_D ** -0.5
            kh = _l2norm(head(1))
            vh = head(2)
            lb, lg = GDN_H * d + h, 2 * GDN_H + GDN_H * d + h
            beta = beta_all[:, lb:lb + 1]
            bcol = b_all[:, lg:lg + 1]
            brow = b_all_t[lg:lg + 1, :]
            ebc = eb_all[:, lg:lg + 1]
            diff = bcol - brow
            dec_incl = jnp.where(m_incl, jnp.exp(jnp.where(m_incl, diff, 0.0)), 0.0)
            dec_strict = jnp.where(m_strict, dec_incl, 0.0)
            kk = _dot_nt(kh, kh)
            qs.append(qh)
            ks.append(kh)
            bcols.append(bcol)
            ebcs.append(ebc)
            attns.append(_dot_nt(qh, kh) * dec_incl)
            nmats.append(-(beta * kk * dec_strict))
            rhss.append(_split(jnp.concatenate([kh * (beta * ebc), vh * beta], axis=1)))

    bdot = lambda x, y: jnp.dot(x, y, preferred_element_type=F32)
    nsplit = [_split(n) for n in nmats]
    t0s = [eye + n for n in nmats]
    powers = [ns[0] for ns in nsplit]
    for _ in range(5):
        powers = [bdot(m, m).astype(BF16) for m in powers]
        t0s = [t + bdot(t.astype(BF16), m) for t, m in zip(t0s, powers)]
    t0split = [_split(t) for t in t0s]
    resid = [(eye - t) + _dot_split(ns, ts) for t, ns, ts in zip(t0s, nsplit, t0split)]
    tinvs = [t + bdot(ts[0], r.astype(BF16)) for t, ts, r in zip(t0s, t0split, resid)]
    sols = [_dot_split(_split(t), r) for t, r in zip(tinvs, rhss)]
    ws = [s[:, :GDN_D] for s in sols]
    u0s = [s[:, GDN_D:] for s in sols]

    sts = [st_ref[bi, h] for bi, h in units]
    u_parts = [[None] * n_chunk for _ in units]
    o_parts = [[None] * n_chunk for _ in units]
    for c in _chunk_order(rev):
        rows = slice(c * CHUNK, (c + 1) * CHUNK)
        last = c * CHUNK if rev else (c + 1) * CHUNK - 1
        for i in range(len(units)):
            st = sts[i]
            b_last = bcols[i][last:last + 1]
            u = u0s[i][rows] - _dot(ws[i][rows], st)
            u_parts[i][c] = u
            o_parts[i][c] = ebcs[i][rows] * _dot(qs[i][rows], st)
            k_end = ks[i][rows] * jnp.exp(b_last - bcols[i][rows])
            sts[i] = jnp.exp(b_last) * st + _dot_tn(k_end, u)
    o_units = []
    for i, (bi, h) in enumerate(units):
        st_ref[bi, h] = sts[i]
        o_units.append(jnp.concatenate(o_parts[i], axis=0) + _dot(attns[i], jnp.concatenate(u_parts[i], axis=0)))
    for bi in range(nbat):
        o = jnp.concatenate(o_units[bi * GDN_H:(bi + 1) * GDN_H], axis=1)
        if rev:
            y = _head_rmsnorm_gate(of_ref[bi] + o, ng_ref[...], z_ref[bi], GDN_H, GDN_D)
            y_ref[bi] = y.astype(y_ref.dtype)
        else:
            o_ref[bi] = o


def _gdn_dir(p, qkv, conv_w, a_log, dt_bias, ng, o_f, *, rev, nblk):
    nbat, s1, _ = p.shape
    blk = _scan_block(rev, nblk)
    d = 1 if rev else 0
    width = 3 * BRANCH_W
    lane0 = 2 * GDN_H + GDN_H * d
    alog_row = jnp.zeros((1, LANE), F32).at[0, lane0:lane0 + GDN_H].set(a_log[d])
    dtb_row = jnp.zeros((1, LANE), F32).at[0, lane0:lane0 + GDN_H].set(dt_bias[d])
    if rev:
        in_specs, args = [_blk_spec(nbat, width, 0, blk)], [qkv]
    else:
        in_specs, args = _conv_inputs(p, P_GDN_QKV, conv_w, jnp.zeros((width,), F32), nblk)
    in_specs += [_blk_spec(nbat, 512, P_GDN_Z, blk), _blk_spec(nbat, LANE, P_GDN_SC, blk),
                 _const_spec((1, LANE)), _const_spec((1, LANE)), _const_spec((TOK_BLK, TOK_BLK)),
                 _const_spec((1, GDN_D))]
    args += [p, p, alog_row, dtb_row, _cumsum_matrix(rev), ng.reshape(1, -1)]
    return _scan_call(_gdn_kernel, "gdn", in_specs, args, o_f, rev=rev, nbat=nbat, s1=s1, nblk=nblk,
                      state_shape=(GDN_H, GDN_D, GDN_D), conv_width=width)


def _ssd_kernel(*refs, rev, nbat, nblk):
    if rev:
        (xbc_ref, dt_ref, z_ref, alog_ref, dtb_ref, ex_ref, lm_ref, dsk_ref, ng_ref, of_ref,
         y_ref, st_ref) = refs
        xbc = [xbc_ref[bi] for bi in range(nbat)]
    else:
        (xm_ref, xp_ref, xn_ref, cw_ref, cb_ref, dt_ref, z_ref, alog_ref, dtb_ref, ex_ref, lm_ref, dsk_ref, ng_ref,
         o_ref, xbcc_ref, st_ref) = refs
        xbc = []
        for bi in range(nbat):
            xbc.append(_conv_silu_block(xm_ref[bi], xp_ref[bi], xn_ref[bi], cw_ref[...], cb_ref[...],
                                        pl.program_id(0), nblk))
            xbcc_ref[bi] = xbc[bi]

    @pl.when(pl.program_id(0) == 0)
    def _():
        st_ref[...] = jnp.zeros_like(st_ref)

    d = 1 if rev else 0
    heads_per_g = M2_H // M2_G
    gw = heads_per_g * M2_P
    n_chunk = TOK_BLK // CHUNK
    mask = _chunk_mask(rev)
    lane = _iota2((TOK_BLK, gw), 1)
    ex = ex_ref[...]

    units = [(bi, g) for bi in range(nbat) for g in range(M2_G)]
    cqs, bks, xvs, b_es, eb_es, accs = [], [], [], [], [], []
    for bi in range(nbat):
        dt = _softplus(dt_ref[bi] + dtb_ref[...])
        loga = -jnp.exp(alog_ref[...]) * dt
        b8 = _select_rows(lm_ref[...], loga)
        b8_t = b8.T
        xv = xbc[bi][:, :BRANCH_W] * _select_cols(dt, ex)
        b_e = _select_cols(b8, ex)
        eb_e = jnp.exp(b_e)
        for g in range(M2_G):
            gs = slice(g * M2_N, (g + 1) * M2_N)
            xs_g = slice(g * gw, (g + 1) * gw)
            bk = xbc[bi][:, BRANCH_W + g * M2_N:BRANCH_W + (g + 1) * M2_N]
            cq = xbc[bi][:, BRANCH_W + (M2_G + g) * M2_N:BRANCH_W + (M2_G + g + 1) * M2_N]
            scores = _dot_nt(cq, bk)
            xv_g = xv[:, xs_g]
            acc = None
            for hh in range(heads_per_g):
                lh = M2_H * d + heads_per_g * g + hh
                diff = b8[:, lh:lh + 1] - b8_t[lh:lh + 1, :]
                dec = jnp.where(mask, jnp.exp(jnp.where(mask, diff, 0.0)), 0.0)
                term = _dot(scores * dec, jnp.where(_idiv(lane, M2_P) == hh, xv_g, 0.0))
                acc = term if acc is None else acc + term
            cqs.append(cq)
            bks.append(bk)
            xvs.append(xv_g)
            b_es.append(b_e[:, xs_g])
            eb_es.append(eb_e[:, xs_g])
            accs.append(acc)

    sts = [st_ref[bi, g] for bi, g in units]
    o_parts = [[None] * n_chunk for _ in units]
    for c in _chunk_order(rev):
        rows = slice(c * CHUNK, (c + 1) * CHUNK)
        last = c * CHUNK if rev else (c + 1) * CHUNK - 1
        for i in range(len(units)):
            b_c = b_es[i][rows]
            b_last = b_es[i][last:last + 1]
            o_parts[i][c] = eb_es[i][rows] * _dot(cqs[i][rows], sts[i])
            ds = _dot_tn(bks[i][rows], xvs[i][rows] * jnp.exp(b_last - b_c))
            sts[i] = jnp.exp(b_last) * sts[i] + ds
    o_units = []
    for i, (bi, g) in enumerate(units):
        st_ref[bi, g] = sts[i]
        o_units.append(accs[i] + jnp.concatenate(o_parts[i], axis=0))
    for bi in range(nbat):
        o = jnp.concatenate(o_units[bi * M2_G:(bi + 1) * M2_G], axis=1)
        if rev:
            y = (of_ref[bi] + o + dsk_ref[...] * xbc[bi][:, :BRANCH_W]) * _silu(z_ref[bi])
            y = y * lax.rsqrt(jnp.mean(y * y, axis=-1, keepdims=True) + RMS_EPS) * ng_ref[...]
            y_ref[bi] = y.astype(y_ref.dtype)
        else:
            o_ref[bi] = o


def _ssd_dir(p, xbc, conv_w, conv_b, a_log, dt_bias, d_skip, ng, o_f, *, rev, nblk):
    nbat, s1, _ = p.shape
    blk = _scan_block(rev, nblk)
    d = 1 if rev else 0
    lane0 = M2_H * d
    alog_row = jnp.zeros((1, LANE), F32).at[0, lane0:lane0 + M2_H].set(a_log[d])
    dtb_row = jnp.zeros((1, LANE), F32).at[0, lane0:lane0 + M2_H].set(dt_bias[d])
    ex = np.zeros((LANE, BRANCH_W), np.float32)
    for h in range(M2_H):
        ex[lane0 + h, h * M2_P:(h + 1) * M2_P] = 1.0
    dsk_row = jnp.repeat(d_skip, M2_P).reshape(1, BRANCH_W)
    if rev:
        in_specs, args = [_blk_spec(nbat, M2_CONV_CH, 0, blk)], [xbc]
    else:
        in_specs, args = _conv_inputs(p, P_M2_XBC, conv_w, conv_b, nblk)
    in_specs += [_blk_spec(nbat, LANE, P_M2_DT, blk), _blk_spec(nbat, 512, P_M2_Z, blk),
                 _const_spec((1, LANE)), _const_spec((1, LANE)), _const_spec((LANE, BRANCH_W)),
                 _const_spec((TOK_BLK, TOK_BLK)), _const_spec((1, BRANCH_W)), _const_spec((1, BRANCH_W))]
    args += [p, p, alog_row, dtb_row, jnp.asarray(ex, dtype=BF16), _cumsum_matrix(rev), dsk_row, ng.reshape(1, -1)]
    return _scan_call(_ssd_kernel, "ssd", in_specs, args, o_f, rev=rev, nbat=nbat, s1=s1, nblk=nblk,
                      state_shape=(M2_G, M2_N, (M2_H // M2_G) * M2_P), conv_width=M2_CONV_CH)


def _final_norm_kernel(x_ref, g_ref, o_ref):
    x = x_ref[...]
    o_ref[...] = x * lax.rsqrt(jnp.mean(x * x, axis=-1, keepdims=True) + RMS_EPS) * g_ref[...]


def _final_norm(xs, g, *, nbat, nblk, lc):
    D = xs.shape[1]
    cb = lc // TOK_BLK
    nlat = nblk - cb
    out = pl.pallas_call(
        _final_norm_kernel,
        grid=(nbat, nlat),
        in_specs=[pl.BlockSpec((TOK_BLK, D), lambda b, s: (b * nblk + cb + s, 0)),
                  pl.BlockSpec((1, D), lambda b, s: (0, 0))],
        out_specs=pl.BlockSpec((TOK_BLK, D), lambda b, s: (b * nlat + s, 0)),
        out_shape=jax.ShapeDtypeStruct((nbat * nlat * TOK_BLK, D), F32),
        compiler_params=_cparams(("arbitrary", "arbitrary")),
        name="final_norm",
    )(xs, g.reshape(1, D))
    return out.reshape(nbat, nlat * TOK_BLK, D)


def _row_tile(s1, limit):
    for tm in range(limit - limit % 16, 0, -16):
        if s1 % tm == 0:
            return tm
    raise ValueError(f"no row tile for sequence length {s1}")


def kernel(x, c, ctx, c_ctx, norm1_g, norm2_g, w_ada, b_ada, w_in, b_merge, gla_a2, gla_ab, gla_norm_g, na_rpb, gdn_conv, gdn_a_log, gdn_dt_bias, gdn_norm_g, m2_conv, m2_conv_b, m2_a_log, m2_dt_bias, m2_d, m2_norm_g, w_branch, w_out, w_ffn1, w_ffn3, w_ffn2, final_norm_g):
    nbat, t, D = x.shape
    lc = ctx.shape[1]
    depth = w_in.shape[0]
    assert D == D_MODEL and lc == TOK_BLK and t % TOK_BLK == 0 and t // GRID_W >= 3 * ROWS_PER_BLK
    assert nbat + 1 <= MOD_ROWS
    s1 = lc + t
    nblk = s1 // TOK_BLK
    tm = _row_tile(s1, 1056)
    geo = dict(s1=s1, lc=lc, nb=nbat)

    cvec = jnp.concatenate([c, c_ctx[None], jnp.zeros((MOD_ROWS - nbat - 1, D), F32)], axis=0)
    mods = _ada(cvec, w_ada, b_ada).reshape(depth, MOD_ROWS, 6, D).transpose(0, 2, 1, 3)
    cos, sin = _rope_tables(lc, t)
    wb_in, wb_branch, wb_out, wb_ffn1, wb_ffn3, wb_ffn2 = (
        w.astype(BF16) for w in (w_in, w_branch, w_out, w_ffn1, w_ffn3, w_ffn2))
    flat = lambda y: y.reshape(nbat * s1, BRANCH_W)

    for l in range(depth):
        sh1, sc1, g1, sh2, sc2, g2 = (mods[l, i] for i in range(6))

        if l == 0:
            xs, h = _concat_norm(ctx, x, norm1_g[l], sc1, sh1, nblk=nblk)
        else:
            h = _normmod(xs, norm1_g[l], sc1, sh1, nbat=nbat, nblk=nblk)
        w_mix, w_gate = _split_in_weights(wb_in, l)
        p = _matmul(h, w_mix, tm=tm, tn=MIX_TILE, out_dtype=F32).reshape(nbat, s1, P_COLS)

        o_f = _gla_dir(p, cos, sin, gla_a2[l], gla_ab[l], gla_norm_g[l], None, rev=False, nblk=nblk)
        ya = _gla_dir(p, cos, sin, gla_a2[l], gla_ab[l], gla_norm_g[l], o_f, rev=True, nblk=nblk)

        yb = _na(p, _na_bias_tiles(na_rpb[l]), nblk=nblk)

        gdn_par = (gdn_conv[l], gdn_a_log[l], gdn_dt_bias[l], gdn_norm_g[l])
        o_f, qkv = _gdn_dir(p, None, *gdn_par, None, rev=False, nblk=nblk)
        yc, = _gdn_dir(p, qkv, *gdn_par, o_f, rev=True, nblk=nblk)

        ssd_par = (m2_conv[l], m2_conv_b[l], m2_a_log[l], m2_dt_bias[l], m2_d[l], m2_norm_g[l])
        o_f, xbc = _ssd_dir(p, None, *ssd_par, None, rev=False, nblk=nblk)
        yd, = _ssd_dir(p, xbc, *ssd_par, o_f, rev=True, nblk=nblk)

        ys = (flat(ya), flat(yb), flat(yc), flat(yd))
        merged = _merge(h, w_gate, b_merge[l], ys, wb_branch, l, tm=tm, tn=512)
        xs, h2 = _outproj_norm(merged, wb_out, l, xs, g1, norm2_g[l], sc2, sh2,
                               tm=_row_tile(s1, 384), **geo)

        u = _ffn_up(h2, wb_ffn1, wb_ffn3, l, tm=_row_tile(s1, 2112), tn=512)
        xs = _matmul_residual(u, wb_ffn2, l, xs, g2, tm=tm, tn=512, name="ffn_down", **geo)

    return _final_norm(xs, final_norm_g, nbat=nbat, nblk=nblk, lc=lc)
```

```python
import functools

import numpy as np
import jax
import jax.numpy as jnp
from jax import lax
from jax.experimental import pallas as pl
from jax.experimental.pallas import tpu as pltpu

D_MODEL = 2048
GRID_W = 64
N_BRANCH = 4
BRANCH_W = D_MODEL // 4
CHUNK = 64
CONV_W = 5
RMS_EPS = 1e-6
NEG_INF = -1e30
ROPE_BASE = 10000.0
GLA_H = 4
GLA_DV = BRANCH_W // GLA_H
GLA_DK = GLA_DV // 2
GLA_LR = 16
GLA_TAU = 16.0
NA_H = 4
NA_D = BRANCH_W // NA_H
NA_WIN_R = 8
NA_WIN_C = 16
GDN_H = 4
GDN_D = BRANCH_W // GDN_H
M2_P = 64
M2_H = BRANCH_W // M2_P
M2_N = 128
M2_G = 2
M2_CONV_CH = BRANCH_W + 2 * M2_G * M2_N
D_FF = ((8 * D_MODEL + 3 * 256 - 1) // (3 * 256)) * 256
GLA_IN = 2 * GLA_H * GLA_DK + 2 * BRANCH_W + 2 * GLA_LR
NA_IN = 3 * BRANCH_W
GDN_IN = 4 * BRANCH_W + 4 * GDN_H
M2_IN = BRANCH_W + M2_CONV_CH + 2 * M2_H
MIX_IN = GLA_IN + NA_IN + GDN_IN + M2_IN

F32 = jnp.float32
BF16 = jnp.bfloat16

LANE = 128
SUBLANE = 8
V7X_VMEM_BYTES = 64 * 1024 * 1024
VMEM_LIMIT = V7X_VMEM_BYTES - 8 * 1024 * 1024

TOK_BLK = 4 * CHUNK
ROWS_PER_BLK = TOK_BLK // GRID_W
HALO = SUBLANE
MOD_ROWS = 8

P_GDN_QKV = 0
P_GLA_V = 1536
P_M2_XBC = 2048
P_GLA_G = 3072
P_NA_Q, P_NA_K, P_NA_V = 3584, 4096, 4608
P_GDN_Z, P_M2_Z = 5120, 5632
P_GLA_Q, P_GLA_K = 6144, 6400
P_GLA_LR, P_GDN_SC, P_M2_DT = 6656, 6784, 6912
P_COLS = 7168
MIX_TILE = 1024


def _mix_fields():
    gla, na, gdn, m2 = 0, GLA_IN, GLA_IN + NA_IN, GLA_IN + NA_IN + GDN_IN
    return sorted([
        (P_GLA_V, gla + 512, 512), (P_GLA_G, gla + 1024, 512),
        (P_NA_Q, na, 512), (P_NA_K, na + 512, 512), (P_NA_V, na + 1024, 512),
        (P_GDN_QKV, gdn, 1536), (P_GDN_Z, gdn + 1536, 512),
        (P_M2_Z, m2, 512), (P_M2_XBC, m2 + 512, M2_CONV_CH),
        (P_GLA_Q, gla, 256), (P_GLA_K, gla + 256, 256),
        (P_GLA_LR, gla + 1536, 2 * GLA_LR), (P_GDN_SC, gdn + 2048, 4 * GDN_H),
        (P_M2_DT, m2 + 512 + M2_CONV_CH, 2 * M2_H)])


def _cparams(sem):
    return pltpu.CompilerParams(dimension_semantics=sem, vmem_limit_bytes=VMEM_LIMIT)


def _wprep_kernel(w_ref, mix_ref, gate_ref):
    col = 0
    for dst, src, n in _mix_fields():
        if dst > col:
            mix_ref[:, col:dst] = jnp.zeros((mix_ref.shape[0], dst - col), mix_ref.dtype)
        mix_ref[:, dst:dst + n] = w_ref[:, src:src + n].astype(mix_ref.dtype)
        col = dst + n
    mix_ref[:, col:] = jnp.zeros((mix_ref.shape[0], P_COLS - col), mix_ref.dtype)
    gate_ref[...] = w_ref[:, MIX_IN:].astype(gate_ref.dtype)


def _split_in_weights(w_in, layer):
    _, D, N = w_in.shape
    rb = 256
    return pl.pallas_call(
        _wprep_kernel,
        grid=(D // rb,),
        in_specs=[pl.BlockSpec((None, rb, N), lambda r: (layer, r, 0))],
        out_specs=[pl.BlockSpec((rb, P_COLS), lambda r: (r, 0)),
                   pl.BlockSpec((rb, N - MIX_IN), lambda r: (r, 0))],
        out_shape=[jax.ShapeDtypeStruct((D, P_COLS), BF16), jax.ShapeDtypeStruct((D, N - MIX_IN), BF16)],
        compiler_params=_cparams(("arbitrary",)),
        name="split_in_weights",
    )(w_in)


def _sigmoid(x):
    return 1.0 / (1.0 + jnp.exp(-x))


def _silu(x):
    return x * _sigmoid(x)


def _softplus(x):
    return jnp.maximum(x, 0.0) + jnp.log1p(jnp.exp(-jnp.abs(x)))


def _dot(a, b):
    return jnp.dot(a.astype(BF16), b.astype(BF16), preferred_element_type=F32)


def _dot_nt(a, b):
    return lax.dot_general(a.astype(BF16), b.astype(BF16), (((1,), (1,)), ((), ())),
                           preferred_element_type=F32)


def _dot_tn(a, b):
    return lax.dot_general(a.astype(BF16), b.astype(BF16), (((0,), (0,)), ((), ())),
                           preferred_element_type=F32)


def _split3(a):
    hi = a.astype(BF16)
    r1 = a - hi.astype(F32)
    mid = r1.astype(BF16)
    return hi, mid, (r1 - mid.astype(F32)).astype(BF16)


def _select_rows(m01, a):
    d = lambda y: jnp.dot(m01, y, preferred_element_type=F32)
    hi, mid, lo = _split3(a)
    return d(hi) + (d(mid) + d(lo))


def _select_cols(a, m01):
    d = lambda y: jnp.dot(y, m01, preferred_element_type=F32)
    hi, mid, lo = _split3(a)
    return d(hi) + (d(mid) + d(lo))


def _split(a):
    hi = a.astype(BF16)
    return hi, (a - hi.astype(F32)).astype(BF16)


def _dot_split(a, b):
    (ah, al), (bh, bl) = a, b
    d = lambda x, y: jnp.dot(x, y, preferred_element_type=F32)
    return d(ah, bh) + (d(ah, bl) + d(al, bh))


def _iota2(shape, dim):
    return lax.broadcasted_iota(jnp.int32, shape, dim)


def _idiv(x, n):
    assert n & (n - 1) == 0
    return lax.shift_right_logical(x, jnp.int32(n.bit_length() - 1))


def _imod(x, n):
    assert n & (n - 1) == 0
    return x & (n - 1)


def _chunk_mask(rev, strict=False):
    r = _iota2((TOK_BLK, TOK_BLK), 0)
    c = _iota2((TOK_BLK, TOK_BLK), 1)
    same = _idiv(r, CHUNK) == _idiv(c, CHUNK)
    if rev:
        tri = (c > r) if strict else (c >= r)
    else:
        tri = (c < r) if strict else (c <= r)
    return same & tri


def _chunk_order(rev):
    n = TOK_BLK // CHUNK
    return list(range(n - 1, -1, -1)) if rev else list(range(n))


def _row_split(i, tm, s1):
    r0 = i * tm
    return lax.div(r0, jnp.int32(s1)), lax.rem(r0, jnp.int32(s1))


def _pick_mod(ref, b, nb, is_ctx):
    return jnp.where(is_ctx, ref[nb:nb + 1, :], ref[pl.ds(b, 1), :])


def _ada_kernel(c_ref, w_ref, b_ref, o_ref):
    a = _silu(c_ref[...])
    o_ref[...] = _dot(a, w_ref[...]) + b_ref[...]


def _ada(cvec, w_ada, b_ada):
    L, D, N = w_ada.shape
    tn = 1024
    return pl.pallas_call(
        _ada_kernel,
        grid=(L, N // tn),
        in_specs=[pl.BlockSpec((MOD_ROWS, D), lambda l, j: (0, 0)),
                  pl.BlockSpec((None, D, tn), lambda l, j: (l, 0, j)),
                  pl.BlockSpec((None, 1, tn), lambda l, j: (l, 0, j))],
        out_specs=pl.BlockSpec((None, MOD_ROWS, tn), lambda l, j: (l, 0, j)),
        out_shape=jax.ShapeDtypeStruct((L, MOD_ROWS, N), F32),
        compiler_params=_cparams(("arbitrary", "arbitrary")),
        name="ada_mod",
    )(cvec, w_ada, b_ada.reshape(L, 1, N))


def _normmod_kernel(x_ref, g_ref, sc_ref, sh_ref, o_ref, *, nb):
    row = jnp.where(pl.program_id(1) == 0, nb, pl.program_id(0))
    x = x_ref[...]
    y = x * lax.rsqrt(jnp.mean(x * x, axis=-1, keepdims=True) + RMS_EPS) * g_ref[...]
    o_ref[...] = (y * (1.0 + sc_ref[pl.ds(row, 1), :]) + sh_ref[pl.ds(row, 1), :]).astype(o_ref.dtype)


def _normmod(xs, g, sc, sh, *, nbat, nblk):
    R, D = xs.shape
    const = lambda b, s: (0, 0)
    row = lambda b, s: (b * nblk + s, 0)
    return pl.pallas_call(
        functools.partial(_normmod_kernel, nb=nbat),
        grid=(nbat, nblk),
        in_specs=[pl.BlockSpec((TOK_BLK, D), row), pl.BlockSpec((1, D), const),
                  pl.BlockSpec((MOD_ROWS, D), const), pl.BlockSpec((MOD_ROWS, D), const)],
        out_specs=pl.BlockSpec((TOK_BLK, D), row),
        out_shape=jax.ShapeDtypeStruct((R, D), BF16),
        compiler_params=_cparams(("arbitrary", "arbitrary")),
        name="norm_mod",
    )(xs, g.reshape(1, D), sc, sh)


def _concat_norm_kernel(ctx_ref, x_ref, g_ref, sc_ref, sh_ref, xs_ref, h_ref, *, nb):
    b, s = pl.program_id(0), pl.program_id(1)
    is_ctx = s == 0
    x = jnp.where(is_ctx, ctx_ref[...], x_ref[...])
    xs_ref[...] = x
    row = jnp.where(is_ctx, nb, b)
    y = x * lax.rsqrt(jnp.mean(x * x, axis=-1, keepdims=True) + RMS_EPS) * g_ref[...]
    h_ref[...] = (y * (1.0 + sc_ref[pl.ds(row, 1), :]) + sh_ref[pl.ds(row, 1), :]).astype(h_ref.dtype)


def _concat_norm(ctx, x, g, sc, sh, *, nblk):
    nbat, t, D = x.shape
    R = nbat * nblk * TOK_BLK
    const = lambda b, s: (0, 0)
    row = lambda b, s: (b * nblk + s, 0)
    return pl.pallas_call(
        functools.partial(_concat_norm_kernel, nb=nbat),
        grid=(nbat, nblk),
        in_specs=[pl.BlockSpec((None, TOK_BLK, D), lambda b, s: (b, 0, 0)),
                  pl.BlockSpec((None, TOK_BLK, D), lambda b, s: (b, jnp.maximum(s - 1, 0), 0)),
                  pl.BlockSpec((1, D), const), pl.BlockSpec((MOD_ROWS, D), const), pl.BlockSpec((MOD_ROWS, D), const)],
        out_specs=[pl.BlockSpec((TOK_BLK, D), row), pl.BlockSpec((TOK_BLK, D), row)],
        out_shape=[jax.ShapeDtypeStruct((R, D), F32), jax.ShapeDtypeStruct((R, D), BF16)],
        compiler_params=_cparams(("arbitrary", "arbitrary")),
        name="concat_norm",
    )(ctx, x, g.reshape(1, D), sc, sh)


def _mm_kernel(a_ref, w_ref, o_ref):
    o_ref[...] = jnp.dot(a_ref[...], w_ref[...], preferred_element_type=F32).astype(o_ref.dtype)


def _matmul(a, w, *, tm, tn, out_dtype):
    R, K = a.shape
    N = w.shape[1]
    assert R % tm == 0 and N % tn == 0
    return pl.pallas_call(
        _mm_kernel,
        grid=(R // tm, N // tn),
        in_specs=[pl.BlockSpec((tm, K), lambda i, j: (i, 0)),
                  pl.BlockSpec((K, tn), lambda i, j: (0, j))],
        out_specs=pl.BlockSpec((tm, tn), lambda i, j: (i, j)),
        out_shape=jax.ShapeDtypeStruct((R, N), out_dtype),
        compiler_params=_cparams(("arbitrary", "arbitrary")),
        name="mix_proj",
    )(a, w)


def _mm_res_kernel(a_ref, w_ref, x_ref, gt_ref, o_ref, *, tm, s1, lc, nb):
    b, pos0 = _row_split(pl.program_id(1), tm, s1)
    is_ctx = (pos0 + _iota2((tm, 1), 0)) < lc
    acc = jnp.dot(a_ref[...], w_ref[...], preferred_element_type=F32)
    o_ref[...] = x_ref[...] + _pick_mod(gt_ref, b, nb, is_ctx) * acc


def _matmul_residual(a, w, layer, xs, gate, *, tm, tn, s1, lc, nb, name):
    R, K = a.shape
    N = w.shape[2]
    assert R % tm == 0 and N % tn == 0
    return pl.pallas_call(
        functools.partial(_mm_res_kernel, tm=tm, s1=s1, lc=lc, nb=nb),
        grid=(N // tn, R // tm),
        in_specs=[pl.BlockSpec((tm, K), lambda j, i: (i, 0)),
                  pl.BlockSpec((None, K, tn), lambda j, i: (layer, 0, j)),
                  pl.BlockSpec((tm, tn), lambda j, i: (i, j)),
                  pl.BlockSpec((MOD_ROWS, tn), lambda j, i: (0, j))],
        out_specs=pl.BlockSpec((tm, tn), lambda j, i: (i, j)),
        out_shape=jax.ShapeDtypeStruct((R, N), F32),
        compiler_params=_cparams(("arbitrary", "arbitrary")),
        name=name,
    )(a, w, xs, gate)


def _outproj_norm_kernel(a_ref, w_ref, x_ref, gt_ref, g_ref, sc_ref, sh_ref, xo_ref, h_ref, *, tm, s1, lc, nb):
    b, pos0 = _row_split(pl.program_id(0), tm, s1)
    is_ctx = (pos0 + _iota2((tm, 1), 0)) < lc
    acc = jnp.dot(a_ref[...], w_ref[...], preferred_element_type=F32)
    x = x_ref[...] + _pick_mod(gt_ref, b, nb, is_ctx) * acc
    xo_ref[...] = x
    y = x * lax.rsqrt(jnp.mean(x * x, axis=-1, keepdims=True) + RMS_EPS) * g_ref[...]
    h_ref[...] = (y * (1.0 + _pick_mod(sc_ref, b, nb, is_ctx)) + _pick_mod(sh_ref, b, nb, is_ctx)).astype(h_ref.dtype)


def _outproj_norm(a, w, layer, xs, gate, g, sc, sh, *, tm, s1, lc, nb):
    R, K = a.shape
    D = w.shape[2]
    row = lambda i: (i, 0)
    const = lambda i: (0, 0)
    return pl.pallas_call(
        functools.partial(_outproj_norm_kernel, tm=tm, s1=s1, lc=lc, nb=nb),
        grid=(R // tm,),
        in_specs=[pl.BlockSpec((tm, K), row), pl.BlockSpec((None, K, D), lambda i: (layer, 0, 0)),
                  pl.BlockSpec((tm, D), row),
                  pl.BlockSpec((MOD_ROWS, D), const), pl.BlockSpec((1, D), const),
                  pl.BlockSpec((MOD_ROWS, D), const), pl.BlockSpec((MOD_ROWS, D), const)],
        out_specs=[pl.BlockSpec((tm, D), row), pl.BlockSpec((tm, D), row)],
        out_shape=[jax.ShapeDtypeStruct((R, D), F32), jax.ShapeDtypeStruct((R, D), BF16)],
        compiler_params=_cparams(("arbitrary",)),
        name="out_proj",
    )(a, w, xs, gate, g.reshape(1, D), sc, sh)


def _ffn_up_kernel(a_ref, w1_ref, w3_ref, o_ref):
    a = a_ref[...]
    u = jnp.dot(a, w1_ref[...], preferred_element_type=F32)
    v = jnp.dot(a, w3_ref[...], preferred_element_type=F32)
    o_ref[...] = (_silu(u) * v).astype(o_ref.dtype)


def _ffn_up(a, w1, w3, layer, *, tm, tn):
    R, K = a.shape
    N = w1.shape[2]
    assert R % tm == 0 and N % tn == 0
    wspec = pl.BlockSpec((None, K, tn), lambda i, j: (layer, 0, j))
    return pl.pallas_call(
        _ffn_up_kernel,
        grid=(R // tm, N // tn),
        in_specs=[pl.BlockSpec((tm, K), lambda i, j: (i, 0)), wspec, wspec],
        out_specs=pl.BlockSpec((tm, tn), lambda i, j: (i, j)),
        out_shape=jax.ShapeDtypeStruct((R, N), BF16),
        compiler_params=_cparams(("arbitrary", "arbitrary")),
        name="ffn_up",
    )(a, w1, w3)


def _merge_kernel(h_ref, g0_ref, g1_ref, g2_ref, g3_ref, bm_ref, ya_ref, yb_ref, yc_ref, yd_ref, wb_ref, o_ref):
    h = h_ref[...]
    acc = None
    branches = zip((g0_ref, g1_ref, g2_ref, g3_ref), (ya_ref, yb_ref, yc_ref, yd_ref))
    for i, (wg_ref, y_ref) in enumerate(branches):
        gate = _sigmoid(jnp.dot(h, wg_ref[...], preferred_element_type=F32) + bm_ref[i])
        term = gate * jnp.dot(y_ref[...], wb_ref[i], preferred_element_type=F32)
        acc = term if acc is None else acc + term
    o_ref[...] = acc.astype(o_ref.dtype)


def _merge(h, wg, b_merge, ys, wb, layer, *, tm, tn):
    R, D = h.shape
    N = wb.shape[3]
    assert R % tm == 0 and N % tn == 0
    nj = N // tn
    yspec = pl.BlockSpec((tm, BRANCH_W), lambda i, j: (i, 0))
    gspec = lambda br: pl.BlockSpec((D, tn), lambda i, j: (0, br * nj + j))
    return pl.pallas_call(
        _merge_kernel,
        grid=(R // tm, nj),
        in_specs=[pl.BlockSpec((tm, D), lambda i, j: (i, 0)),
                  gspec(0), gspec(1), gspec(2), gspec(3),
                  pl.BlockSpec((N_BRANCH, 1, tn), lambda i, j: (0, 0, j)),
                  yspec, yspec, yspec, yspec,
                  pl.BlockSpec((None, N_BRANCH, BRANCH_W, tn), lambda i, j: (layer, 0, 0, j))],
        out_specs=pl.BlockSpec((tm, tn), lambda i, j: (i, j)),
        out_shape=jax.ShapeDtypeStruct((R, N), BF16),
        compiler_params=_cparams(("arbitrary", "arbitrary")),
        name="merge",
    )(h, wg, wg, wg, wg, b_merge.reshape(N_BRANCH, 1, N), *ys, wb)


def _conv_silu_block(xm, xp, xn, w, bias, blk, nblk):
    prev_ok = blk >= 2
    next_ok = (blk >= 1) & (blk < nblk - 1)
    rows = TOK_BLK + 2 * HALO
    half = CONV_W // 2
    xpad = jnp.concatenate([jnp.where(prev_ok, xp, 0.0), xm, jnp.where(next_ok, xn, 0.0)], axis=0)
    acc = bias + w[half:half + 1, :] * xm
    for j in range(CONV_W):
        if j != half:
            tap = pltpu.roll(xpad, (half - j) % rows, 0)[HALO:HALO + TOK_BLK]
            acc = acc + w[j:j + 1, :] * tap
    return _silu(acc)


def _conv_inputs(p, col0, w, bias, nblk):
    nbat, s1, _ = p.shape
    C = w.shape[1]
    assert col0 % C == 0
    cb = col0 // C
    hb = TOK_BLK // HALO
    nhalo = s1 // HALO
    wpad = jnp.concatenate([w, jnp.zeros((SUBLANE - CONV_W, C), F32)], axis=0)
    specs = [pl.BlockSpec((nbat, TOK_BLK, C), lambda r: (0, r, cb)),
             pl.BlockSpec((nbat, HALO, C), lambda r: (0, jnp.maximum(r * hb - 1, 0), cb)),
             pl.BlockSpec((nbat, HALO, C), lambda r: (0, jnp.minimum((r + 1) * hb, nhalo - 1), cb)),
             _const_spec((SUBLANE, C)), _const_spec((1, C))]
    return specs, [p, p, p, wpad, bias.reshape(1, C)]


def _scan_block(rev, nblk):
    if rev:
        return lambda s: jnp.where(s == 0, 0, nblk - s)
    return lambda s: s


def _blk_spec(nbat, width, col, blk):
    assert col % width == 0
    return pl.BlockSpec((nbat, TOK_BLK, width), lambda s: (0, blk(s), col // width))


def _const_spec(shape):
    return pl.BlockSpec(shape, lambda s: (0,) * len(shape))


def _head_rmsnorm_gate(o, g, gate, n_head, width):
    outs = []
    for h in range(n_head):
        oh = o[:, h * width:(h + 1) * width]
        yh = oh * lax.rsqrt(jnp.mean(oh * oh, axis=-1, keepdims=True) + RMS_EPS) * g
        outs.append(yh * _silu(gate[:, h * width:(h + 1) * width]))
    return jnp.concatenate(outs, axis=1)


def _scan_call(kern, name, in_specs, args, o_f, *, rev, nbat, s1, nblk, state_shape, conv_width=0):
    blk = _scan_block(rev, nblk)
    ospec = pl.BlockSpec((nbat, TOK_BLK, BRANCH_W), lambda s: (0, blk(s), 0))
    out_specs = [ospec]
    out_shape = [jax.ShapeDtypeStruct((nbat, s1, BRANCH_W), BF16 if rev else F32)]
    if rev:
        in_specs = in_specs + [ospec]
        args = args + [o_f]
    elif conv_width:
        out_specs.append(pl.BlockSpec((nbat, TOK_BLK, conv_width), lambda s: (0, s, 0)))
        out_shape.append(jax.ShapeDtypeStruct((nbat, s1, conv_width), F32))
    return pl.pallas_call(
        functools.partial(kern, rev=rev, nbat=nbat, nblk=nblk),
        grid=(nblk,),
        in_specs=in_specs,
        out_specs=out_specs,
        out_shape=out_shape,
        scratch_shapes=[pltpu.VMEM((nbat,) + state_shape, F32)],
        compiler_params=_cparams(("arbitrary",)),
        name=name + ("_bwd" if rev else "_fwd"),
    )(*args)


def _gla_kernel(q_ref, k_ref, v_ref, g_ref, lr_ref, cos_ref, sin_ref, a2_ref, ab_ref, lm_ref, ng_ref,
                *rest, rev, nbat, nblk):
    if rev:
        of_ref, y_ref, st_ref = rest
    else:
        o_ref, st_ref = rest

    @pl.when(pl.program_id(0) == 0)
    def _():
        st_ref[...] = jnp.zeros_like(st_ref)

    hk = GLA_H * GLA_DK
    lane = _iota2((TOK_BLK, hk), 1)
    first_half = _imod(lane, GLA_DK) < (GLA_DK // 2)
    head_of_lane = _idiv(lane, GLA_DK)
    cos = cos_ref[...]
    sin = sin_ref[...]
    mask = _chunk_mask(rev)
    n_chunk = TOK_BLK // CHUNK
    batch = range(nbat)

    def rope(x):
        partner = jnp.where(first_half, pltpu.roll(x, hk - GLA_DK // 2, 1), pltpu.roll(x, GLA_DK // 2, 1))
        return x * cos + partner * sin

    ks, vs, bs, q_ins, o_intras = [], [], [], [], []
    for bi in batch:
        q = rope(q_ref[bi]) * GLA_DK ** -0.5
        k = rope(k_ref[bi])
        v = v_ref[bi]
        loga = -_softplus(-(_dot(lr_ref[bi], a2_ref[...]) + ab_ref[...])) / GLA_TAU
        b = _select_rows(lm_ref[...], loga)
        q_in = q * jnp.exp(b)
        k_in = k * jnp.exp(-b)
        o_heads = []
        for h in range(GLA_H):
            att = _dot_nt(jnp.where(head_of_lane == h, q_in, 0.0), k_in)
            att = jnp.where(mask, att, 0.0)
            o_heads.append(_dot(att, v[:, h * GLA_DV:(h + 1) * GLA_DV]))
        ks.append(k)
        vs.append(v)
        bs.append(b)
        q_ins.append(q_in)
        o_intras.append(jnp.concatenate(o_heads, axis=1))

    sts = [st_ref[bi] for bi in batch]
    diag = _idiv(_iota2(sts[0].shape, 0), GLA_DV) == _idiv(_iota2(sts[0].shape, 1), GLA_DK)
    o_inter = [[None] * n_chunk for _ in batch]
    for c in _chunk_order(rev):
        rows = slice(c * CHUNK, (c + 1) * CHUNK)
        for bi in batch:
            b_c = bs[bi][rows]
            b_last = b_c[0:1] if rev else b_c[CHUNK - 1:CHUNK]
            o_inter[bi][c] = _dot_nt(q_ins[bi][rows], sts[bi])
            k_end = ks[bi][rows] * jnp.exp(b_last - b_c)
            ds = _dot_tn(vs[bi][rows], k_end)
            sts[bi] = sts[bi] * jnp.exp(b_last) + jnp.where(diag, ds, 0.0)
    for bi in batch:
        st_ref[bi] = sts[bi]
        o = o_intras[bi] + jnp.concatenate(o_inter[bi], axis=0)
        if rev:
            y = _head_rmsnorm_gate(of_ref[bi] + o, ng_ref[...], g_ref[bi], GLA_H, GLA_DV)
            y_ref[bi] = y.astype(y_ref.dtype)
        else:
            o_ref[bi] = o


def _cumsum_matrix(rev):
    r = np.arange(TOK_BLK)[:, None]
    c = np.arange(TOK_BLK)[None, :]
    same = (r // CHUNK) == (c // CHUNK)
    tri = (c >= r) if rev else (c <= r)
    return jnp.asarray((same & tri).astype(np.float32), dtype=BF16)


def _gla_dir(p, cos, sin, a2, ab, ng, o_f, *, rev, nblk):
    nbat, s1, _ = p.shape
    blk = _scan_block(rev, nblk)
    d = 1 if rev else 0
    a2d = jnp.zeros((LANE, GLA_H * GLA_DK), F32).at[d * GLA_LR:(d + 1) * GLA_LR].set(a2[d])
    tab = pl.BlockSpec((TOK_BLK, GLA_H * GLA_DK), lambda s: (blk(s), 0))
    in_specs = [_blk_spec(nbat, 256, P_GLA_Q, blk), _blk_spec(nbat, 256, P_GLA_K, blk),
                _blk_spec(nbat, 512, P_GLA_V, blk), _blk_spec(nbat, 512, P_GLA_G, blk),
                _blk_spec(nbat, LANE, P_GLA_LR, blk), tab, tab,
                _const_spec((LANE, 256)), _const_spec((1, 256)), _const_spec((TOK_BLK, TOK_BLK)),
                _const_spec((1, GLA_DV))]
    args = [p, p, p, p, p, cos, sin, a2d, ab[d].reshape(1, -1), _cumsum_matrix(rev), ng.reshape(1, -1)]
    return _scan_call(_gla_kernel, "gla", in_specs, args, o_f, rev=rev, nbat=nbat, s1=s1, nblk=nblk,
                      state_shape=(GLA_H * GLA_DV, GLA_H * GLA_DK))[0]


def _rope_tables(lc, t):
    n_freq = GLA_DK // 4
    freqs = ROPE_BASE ** (-jnp.arange(n_freq, dtype=F32) / n_freq)
    tt = jnp.arange(t)
    row = (tt // GRID_W).astype(F32)
    col = (tt % GRID_W).astype(F32)
    ang = jnp.concatenate([row[:, None] * freqs, col[:, None] * freqs], axis=-1)
    cos, sin = jnp.cos(ang), jnp.sin(ang)
    cos = jnp.concatenate([jnp.ones((lc, GLA_DK // 2), F32), cos], axis=0)
    sin = jnp.concatenate([jnp.zeros((lc, GLA_DK // 2), F32), sin], axis=0)
    cos_h = jnp.concatenate([cos, cos], axis=1)
    sin_h = jnp.concatenate([-sin, sin], axis=1)
    return jnp.tile(cos_h, (1, GLA_H)), jnp.tile(sin_h, (1, GLA_H))


def _na_kernel(q_ref, kp_ref, kc_ref, kn_ref, kx_ref, vp_ref, vc_ref, vn_ref, vx_ref, bias_ref, y_ref, *, nbat):
    scale = NA_D ** -0.5
    for bi in range(nbat):
        outs = []
        for h in range(NA_H):
            hs = slice(h * NA_D, (h + 1) * NA_D)
            qh = (q_ref[bi, :, hs] * scale).astype(BF16)
            s = jnp.concatenate(
                [_dot_nt(qh, kp_ref[bi, :, hs]) + bias_ref[0, h],
                 _dot_nt(qh, kc_ref[bi, :, hs]) + bias_ref[1, h],
                 _dot_nt(qh, kn_ref[bi, :, hs]) + bias_ref[2, h],
                 _dot_nt(qh, kx_ref[bi, :, hs])], axis=1)
            m = jnp.max(s, axis=-1, keepdims=True)
            e = jnp.exp(s - m)
            o = (_dot(e[:, 0:TOK_BLK], vp_ref[bi, :, hs]) + _dot(e[:, TOK_BLK:2 * TOK_BLK], vc_ref[bi, :, hs])
                 + _dot(e[:, 2 * TOK_BLK:3 * TOK_BLK], vn_ref[bi, :, hs])
                 + _dot(e[:, 3 * TOK_BLK:], vx_ref[bi, :, hs]))
            outs.append(o / jnp.sum(e, axis=-1, keepdims=True))
        y_ref[bi] = jnp.concatenate(outs, axis=1).astype(y_ref.dtype)


def _na_bias_tiles(rpb):
    edge = GRID_W - NA_WIN_C
    ext = jnp.concatenate([jnp.repeat(rpb[..., :1], edge, axis=-1), rpb,
                           jnp.repeat(rpb[..., -1:], edge, axis=-1)], axis=-1)
    toep = jnp.stack([ext[..., GRID_W - 1 - qc:2 * GRID_W - 1 - qc] for qc in range(GRID_W)], axis=-2)
    qc = np.arange(GRID_W)[:, None]
    kc = np.arange(GRID_W)[None, :]
    c0 = np.clip(qc - NA_WIN_C // 2, 0, GRID_W - NA_WIN_C)
    col_ok = (kc >= c0) & (kc < c0 + NA_WIN_C)
    toep = jnp.where(jnp.asarray(col_ok), toep, NEG_INF).astype(F32)
    masked = jnp.full((NA_H, GRID_W, GRID_W), NEG_INF, F32)
    kinds = []
    for kind in range(3):
        offs = []
        for off in (-1, 0, 1):
            rows = []
            for qr in range(ROWS_PER_BLK):
                start = (0, qr - NA_WIN_R // 2, ROWS_PER_BLK - NA_WIN_R)[kind]
                cols = []
                for kb in range(ROWS_PER_BLK):
                    kr = kb + ROWS_PER_BLK * off
                    ok = start <= kr < start + NA_WIN_R
                    cols.append(toep[:, kr - qr + NA_WIN_R - 1] if ok else masked)
                rows.append(jnp.concatenate(cols, axis=-1))
            offs.append(jnp.concatenate(rows, axis=-2))
        kinds.append(jnp.stack(offs))
    kinds.append(jnp.full_like(kinds[0], NEG_INF))
    return jnp.stack(kinds)


def _na(p, bias, *, nblk):
    nbat, s1, _ = p.shape

    def kind(s):
        return jnp.where(s == 0, 3, jnp.where(s == 1, 0, jnp.where(s == nblk - 1, 2, 1)))

    prev = lambda s: jnp.maximum(s - 1, 1)
    cur = lambda s: s
    nxt = lambda s: jnp.minimum(s + 1, nblk - 1)
    ctx = lambda s: 0
    spec = lambda col, blk: _blk_spec(nbat, BRANCH_W, col, blk)
    return pl.pallas_call(
        functools.partial(_na_kernel, nbat=nbat),
        grid=(nblk,),
        in_specs=[spec(P_NA_Q, cur),
                  spec(P_NA_K, prev), spec(P_NA_K, cur), spec(P_NA_K, nxt), spec(P_NA_K, ctx),
                  spec(P_NA_V, prev), spec(P_NA_V, cur), spec(P_NA_V, nxt), spec(P_NA_V, ctx),
                  pl.BlockSpec((None, 3, NA_H, TOK_BLK, TOK_BLK), lambda s: (kind(s), 0, 0, 0, 0))],
        out_specs=pl.BlockSpec((nbat, TOK_BLK, BRANCH_W), lambda s: (0, s, 0)),
        out_shape=jax.ShapeDtypeStruct((nbat, s1, BRANCH_W), BF16),
        compiler_params=_cparams(("arbitrary",)),
        name="nbr_attn",
    )(p, p, p, p, p, p, p, p, p, bias)


def _l2norm(x):
    return x * lax.rsqrt(jnp.sum(x * x, axis=-1, keepdims=True) + RMS_EPS)


def _gdn_kernel(*refs, rev, nbat, nblk):
    if rev:
        qkv_ref, z_ref, sc_ref, alog_ref, dtb_ref, lm_ref, ng_ref, of_ref, y_ref, st_ref = refs
        qkv = [qkv_ref[bi] for bi in range(nbat)]
    else:
        (xm_ref, xp_ref, xn_ref, cw_ref, cb_ref, z_ref, sc_ref, alog_ref, dtb_ref, lm_ref, ng_ref,
         o_ref, qkvc_ref, st_ref) = refs
        qkv = []
        for bi in range(nbat):
            qkv.append(_conv_silu_block(xm_ref[bi], xp_ref[bi], xn_ref[bi], cw_ref[...], cb_ref[...],
                                        pl.program_id(0), nblk))
            qkvc_ref[bi] = qkv[bi]

    @pl.when(pl.program_id(0) == 0)
    def _():
        st_ref[...] = jnp.zeros_like(st_ref)

    d = 1 if rev else 0
    m_incl = _chunk_mask(rev)
    m_strict = _chunk_mask(rev, strict=True)
    eye = (_iota2((TOK_BLK, TOK_BLK), 0) == _iota2((TOK_BLK, TOK_BLK), 1)).astype(F32)
    n_chunk = TOK_BLK // CHUNK

    units = [(bi, h) for bi in range(nbat) for h in range(GDN_H)]
    qs, ks, bcols, ebcs, attns, nmats, rhss = [], [], [], [], [], [], []
    for bi in range(nbat):
        sc = sc_ref[bi]
        beta_all = _sigmoid(sc)
        g_all = -jnp.exp(alog_ref[...]) * _softplus(sc + dtb_ref[...])
        b_all = _select_rows(lm_ref[...], g_all)
        b_all_t = b_all.T
        eb_all = jnp.exp(b_all)
        for h in range(GDN_H):
            head = lambda part: qkv[bi][:, part * BRANCH_W + h * GDN_D:part * BRANCH_W + (h + 1) * GDN_D]
            qh = _l2norm(head(0)) * GDN_D ** -0.5
            kh = _l2norm(head(1))
            vh = head(2)
            lb, lg = GDN_H * d + h, 2 * GDN_H + GDN_H * d + h
            beta = beta_all[:, lb:lb + 1]
            bcol = b_all[:, lg:lg + 1]
            brow = b_all_t[lg:lg + 1, :]
            ebc = eb_all[:, lg:lg + 1]
            diff = bcol - brow
            dec_incl = jnp.where(m_incl, jnp.exp(jnp.where(m_incl, diff, 0.0)), 0.0)
            dec_strict = jnp.where(m_strict, dec_incl, 0.0)
            kk = _dot_nt(kh, kh)
            qs.append(qh)
            ks.append(kh)
            bcols.append(bcol)
            ebcs.append(ebc)
            attns.append(_dot_nt(qh, kh) * dec_incl)
            nmats.append(-(beta * kk * dec_strict))
            rhss.append(_split(jnp.concatenate([kh * (beta * ebc), vh * beta], axis=1)))

    bdot = lambda x, y: jnp.dot(x, y, preferred_element_type=F32)
    nsplit = [_split(n) for n in nmats]
    t0s = [eye + n for n in nmats]
    powers = [ns[0] for ns in nsplit]
    for _ in range(5):
        powers = [bdot(m, m).astype(BF16) for m in powers]
        t0s = [t + bdot(t.astype(BF16), m) for t, m in zip(t0s, powers)]
    t0split = [_split(t) for t in t0s]
    resid = [(eye - t) + _dot_split(ns, ts) for t, ns, ts in zip(t0s, nsplit, t0split)]
    tinvs = [t + bdot(ts[0], r.astype(BF16)) for t, ts, r in zip(t0s, t0split, resid)]
    sols = [_dot_split(_split(t), r) for t, r in zip(tinvs, rhss)]
    ws = [s[:, :GDN_D] for s in sols]
    u0s = [s[:, GDN_D:] for s in sols]

    sts = [st_ref[bi, h] for bi, h in units]
    u_parts = [[None] * n_chunk for _ in units]
    o_parts = [[None] * n_chunk for _ in units]
    for c in _chunk_order(rev):
        rows = slice(c * CHUNK, (c + 1) * CHUNK)
        last = c * CHUNK if rev else (c + 1) * CHUNK - 1
        for i in range(len(units)):
            st = sts[i]
            b_last = bcols[i][last:last + 1]
            u = u0s[i][rows] - _dot(ws[i][rows], st)
            u_parts[i][c] = u
            o_parts[i][c] = ebcs[i][rows] * _dot(qs[i][rows], st)
            k_end = ks[i][rows] * jnp.exp(b_last - bcols[i][rows])
            sts[i] = jnp.exp(b_last) * st + _dot_tn(k_end, u)
    o_units = []
    for i, (bi, h) in enumerate(units):
        st_ref[bi, h] = sts[i]
        o_units.append(jnp.concatenate(o_parts[i], axis=0) + _dot(attns[i], jnp.concatenate(u_parts[i], axis=0)))
    for bi in range(nbat):
        o = jnp.concatenate(o_units[bi * GDN_H:(bi + 1) * GDN_H], axis=1)
        if rev:
            y = _head_rmsnorm_gate(of_ref[bi] + o, ng_ref[...], z_ref[bi], GDN_H, GDN_D)
            y_ref[bi] = y.astype(y_ref.dtype)
        else:
            o_ref[bi] = o


def _gdn_dir(p, qkv, conv_w, a_log, dt_bias, ng, o_f, *, rev, nblk):
    nbat, s1, _ = p.shape
    blk = _scan_block(rev, nblk)
    d = 1 if rev else 0
    width = 3 * BRANCH_W
    lane0 = 2 * GDN_H + GDN_H * d
    alog_row = jnp.zeros((1, LANE), F32).at[0, lane0:lane0 + GDN_H].set(a_log[d])
    dtb_row = jnp.zeros((1, LANE), F32).at[0, lane0:lane0 + GDN_H].set(dt_bias[d])
    if rev:
        in_specs, args = [_blk_spec(nbat, width, 0, blk)], [qkv]
    else:
        in_specs, args = _conv_inputs(p, P_GDN_QKV, conv_w, jnp.zeros((width,), F32), nblk)
    in_specs += [_blk_spec(nbat, 512, P_GDN_Z, blk), _blk_spec(nbat, LANE, P_GDN_SC, blk),
                 _const_spec((1, LANE)), _const_spec((1, LANE)), _const_spec((TOK_BLK, TOK_BLK)),
                 _const_spec((1, GDN_D))]
    args += [p, p, alog_row, dtb_row, _cumsum_matrix(rev), ng.reshape(1, -1)]
    return _scan_call(_gdn_kernel, "gdn", in_specs, args, o_f, rev=rev, nbat=nbat, s1=s1, nblk=nblk,
                      state_shape=(GDN_H, GDN_D, GDN_D), conv_width=width)


def _ssd_kernel(*refs, rev, nbat, nblk):
    if rev:
        (xbc_ref, dt_ref, z_ref, alog_ref, dtb_ref, ex_ref, lm_ref, dsk_ref, ng_ref, of_ref,
         y_ref, st_ref) = refs
        xbc = [xbc_ref[bi] for bi in range(nbat)]
    else:
        (xm_ref, xp_ref, xn_ref, cw_ref, cb_ref, dt_ref, z_ref, alog_ref, dtb_ref, ex_ref, lm_ref, dsk_ref, ng_ref,
         o_ref, xbcc_ref, st_ref) = refs
        xbc = []
        for bi in range(nbat):
            xbc.append(_conv_silu_block(xm_ref[bi], xp_ref[bi], xn_ref[bi], cw_ref[...], cb_ref[...],
                                        pl.program_id(0), nblk))
            xbcc_ref[bi] = xbc[bi]

    @pl.when(pl.program_id(0) == 0)
    def _():
        st_ref[...] = jnp.zeros_like(st_ref)

    d = 1 if rev else 0
    heads_per_g = M2_H // M2_G
    gw = heads_per_g * M2_P
    n_chunk = TOK_BLK // CHUNK
    mask = _chunk_mask(rev)
    lane = _iota2((TOK_BLK, gw), 1)
    ex = ex_ref[...]

    units = [(bi, g) for bi in range(nbat) for g in range(M2_G)]
    cqs, bks, xvs, b_es, eb_es, accs = [], [], [], [], [], []
    for bi in range(nbat):
        dt = _softplus(dt_ref[bi] + dtb_ref[...])
        loga = -jnp.exp(alog_ref[...]) * dt
        b8 = _select_rows(lm_ref[...], loga)
        b8_t = b8.T
        xv = xbc[bi][:, :BRANCH_W] * _select_cols(dt, ex)
        b_e = _select_cols(b8, ex)
        eb_e = jnp.exp(b_e)
        for g in range(M2_G):
            gs = slice(g * M2_N, (g + 1) * M2_N)
            xs_g = slice(g * gw, (g + 1) * gw)
            bk = xbc[bi][:, BRANCH_W + g * M2_N:BRANCH_W + (g + 1) * M2_N]
            cq = xbc[bi][:, BRANCH_W + (M2_G + g) * M2_N:BRANCH_W + (M2_G + g + 1) * M2_N]
            scores = _dot_nt(cq, bk)
            xv_g = xv[:, xs_g]
            acc = None
            for hh in range(heads_per_g):
                lh = M2_H * d + heads_per_g * g + hh
                diff = b8[:, lh:lh + 1] - b8_t[lh:lh + 1, :]
                dec = jnp.where(mask, jnp.exp(jnp.where(mask, diff, 0.0)), 0.0)
                term = _dot(scores * dec, jnp.where(_idiv(lane, M2_P) == hh, xv_g, 0.0))
                acc = term if acc is None else acc + term
            cqs.append(cq)
            bks.append(bk)
            xvs.append(xv_g)
            b_es.append(b_e[:, xs_g])
            eb_es.append(eb_e[:, xs_g])
            accs.append(acc)

    sts = [st_ref[bi, g] for bi, g in units]
    o_parts = [[None] * n_chunk for _ in units]
    for c in _chunk_order(rev):
        rows = slice(c * CHUNK, (c + 1) * CHUNK)
        last = c * CHUNK if rev else (c + 1) * CHUNK - 1
        for i in range(len(units)):
            b_c = b_es[i][rows]
            b_last = b_es[i][last:last + 1]
            o_parts[i][c] = eb_es[i][rows] * _dot(cqs[i][rows], sts[i])
            ds = _dot_tn(bks[i][rows], xvs[i][rows] * jnp.exp(b_last - b_c))
            sts[i] = jnp.exp(b_last) * sts[i] + ds
    o_units = []
    for i, (bi, g) in enumerate(units):
        st_ref[bi, g] = sts[i]
        o_units.append(accs[i] + jnp.concatenate(o_parts[i], axis=0))
    for bi in range(nbat):
        o = jnp.concatenate(o_units[bi * M2_G:(bi + 1) * M2_G], axis=1)
        if rev:
            y = (of_ref[bi] + o + dsk_ref[...] * xbc[bi][:, :BRANCH_W]) * _silu(z_ref[bi])
            y = y * lax.rsqrt(jnp.mean(y * y, axis=-1, keepdims=True) + RMS_EPS) * ng_ref[...]
            y_ref[bi] = y.astype(y_ref.dtype)
        else:
            o_ref[bi] = o


def _ssd_dir(p, xbc, conv_w, conv_b, a_log, dt_bias, d_skip, ng, o_f, *, rev, nblk):
    nbat, s1, _ = p.shape
    blk = _scan_block(rev, nblk)
    d = 1 if rev else 0
    lane0 = M2_H * d
    alog_row = jnp.zeros((1, LANE), F32).at[0, lane0:lane0 + M2_H].set(a_log[d])
    dtb_row = jnp.zeros((1, LANE), F32).at[0, lane0:lane0 + M2_H].set(dt_bias[d])
    ex = np.zeros((LANE, BRANCH_W), np.float32)
    for h in range(M2_H):
        ex[lane0 + h, h * M2_P:(h + 1) * M2_P] = 1.0
    dsk_row = jnp.repeat(d_skip, M2_P).reshape(1, BRANCH_W)
    if rev:
        in_specs, args = [_blk_spec(nbat, M2_CONV_CH, 0, blk)], [xbc]
    else:
        in_specs, args = _conv_inputs(p, P_M2_XBC, conv_w, conv_b, nblk)
    in_specs += [_blk_spec(nbat, LANE, P_M2_DT, blk), _blk_spec(nbat, 512, P_M2_Z, blk),
                 _const_spec((1, LANE)), _const_spec((1, LANE)), _const_spec((LANE, BRANCH_W)),
                 _const_spec((TOK_BLK, TOK_BLK)), _const_spec((1, BRANCH_W)), _const_spec((1, BRANCH_W))]
    args += [p, p, alog_row, dtb_row, jnp.asarray(ex, dtype=BF16), _cumsum_matrix(rev), dsk_row, ng.reshape(1, -1)]
    return _scan_call(_ssd_kernel, "ssd", in_specs, args, o_f, rev=rev, nbat=nbat, s1=s1, nblk=nblk,
                      state_shape=(M2_G, M2_N, (M2_H // M2_G) * M2_P), conv_width=M2_CONV_CH)


def _final_norm_kernel(x_ref, g_ref, o_ref):
    x = x_ref[...]
    o_ref[...] = x * lax.rsqrt(jnp.mean(x * x, axis=-1, keepdims=True) + RMS_EPS) * g_ref[...]


def _final_norm(xs, g, *, nbat, nblk, lc):
    D = xs.shape[1]
    cb = lc // TOK_BLK
    nlat = nblk - cb
    out = pl.pallas_call(
        _final_norm_kernel,
        grid=(nbat, nlat),
        in_specs=[pl.BlockSpec((TOK_BLK, D), lambda b, s: (b * nblk + cb + s, 0)),
                  pl.BlockSpec((1, D), lambda b, s: (0, 0))],
        out_specs=pl.BlockSpec((TOK_BLK, D), lambda b, s: (b * nlat + s, 0)),
        out_shape=jax.ShapeDtypeStruct((nbat * nlat * TOK_BLK, D), F32),
        compiler_params=_cparams(("arbitrary", "arbitrary")),
        name="final_norm",
    )(xs, g.reshape(1, D))
    return out.reshape(nbat, nlat * TOK_BLK, D)


def _row_tile(s1, limit):
    for tm in range(limit - limit % 16, 0, -16):
        if s1 % tm == 0:
            return tm
    raise ValueError(f"no row tile for sequence length {s1}")


def kernel(x, c, ctx, c_ctx, norm1_g, norm2_g, w_ada, b_ada, w_in, b_merge, gla_a2, gla_ab, gla_norm_g, na_rpb, gdn_conv, gdn_a_log, gdn_dt_bias, gdn_norm_g, m2_conv, m2_conv_b, m2_a_log, m2_dt_bias, m2_d, m2_norm_g, w_branch, w_out, w_ffn1, w_ffn3, w_ffn2, final_norm_g):
    nbat, t, D = x.shape
    lc = ctx.shape[1]
    depth = w_in.shape[0]
    assert D == D_MODEL and lc == TOK_BLK and t % TOK_BLK == 0 and t // GRID_W >= 3 * ROWS_PER_BLK
    assert nbat + 1 <= MOD_ROWS
    s1 = lc + t
    nblk = s1 // TOK_BLK
    tm = _row_tile(s1, 1056)
    geo = dict(s1=s1, lc=lc, nb=nbat)

    cvec = jnp.concatenate([c, c_ctx[None], jnp.zeros((MOD_ROWS - nbat - 1, D), F32)], axis=0)
    mods = _ada(cvec, w_ada, b_ada).reshape(depth, MOD_ROWS, 6, D).transpose(0, 2, 1, 3)
    cos, sin = _rope_tables(lc, t)
    wb_in, wb_branch, wb_out, wb_ffn1, wb_ffn3, wb_ffn2 = (
        w.astype(BF16) for w in (w_in, w_branch, w_out, w_ffn1, w_ffn3, w_ffn2))
    flat = lambda y: y.reshape(nbat * s1, BRANCH_W)

    for l in range(depth):
        sh1, sc1, g1, sh2, sc2, g2 = (mods[l, i] for i in range(6))

        if l == 0:
            xs, h = _concat_norm(ctx, x, norm1_g[l], sc1, sh1, nblk=nblk)
        else:
            h = _normmod(xs, norm1_g[l], sc1, sh1, nbat=nbat, nblk=nblk)
        w_mix, w_gate = _split_in_weights(wb_in, l)
        p = _matmul(h, w_mix, tm=tm, tn=MIX_TILE, out_dtype=F32).reshape(nbat, s1, P_COLS)

        o_f = _gla_dir(p, cos, sin, gla_a2[l], gla_ab[l], gla_norm_g[l], None, rev=False, nblk=nblk)
        ya = _gla_dir(p, cos, sin, gla_a2[l], gla_ab[l], gla_norm_g[l], o_f, rev=True, nblk=nblk)

        yb = _na(p, _na_bias_tiles(na_rpb[l]), nblk=nblk)

        gdn_par = (gdn_conv[l], gdn_a_log[l], gdn_dt_bias[l], gdn_norm_g[l])
        o_f, qkv = _gdn_dir(p, None, *gdn_par, None, rev=False, nblk=nblk)
        yc, = _gdn_dir(p, qkv, *gdn_par, o_f, rev=True, nblk=nblk)

        ssd_par = (m2_conv[l], m2_conv_b[l], m2_a_log[l], m2_dt_bias[l], m2_d[l], m2_norm_g[l])
        o_f, xbc = _ssd_dir(p, None, *ssd_par, None, rev=False, nblk=nblk)
        yd, = _ssd_dir(p, xbc, *ssd_par, o_f, rev=True, nblk=nblk)

        ys = (flat(ya), flat(yb), flat(yc), flat(yd))
        merged = _merge(h, w_gate, b_merge[l], ys, wb_branch, l, tm=tm, tn=512)
        xs, h2 = _outproj_norm(merged, wb_out, l, xs, g1, norm2_g[l], sc2, sh2,
                               tm=_row_tile(s1, 384), **geo)

        u = _ffn_up(h2, wb_ffn1, wb_ffn3, l, tm=_row_tile(s1, 2112), tn=512)
        xs = _matmul_residual(u, wb_ffn2, l, xs, g2, tm=tm, tn=512, name="ffn_down", **geo)

    return _final_norm(xs, final_norm_g, nbat=nbat, nblk=nblk, lc=lc)
```

```python
import functools

import numpy as np
import jax
import jax.numpy as jnp
from jax import lax
from jax.experimental import pallas as pl
from jax.experimental.pallas import tpu as pltpu

D_MODEL = 2048
GRID_W = 64
N_BRANCH = 4
BRANCH_W = D_MODEL // 4
CHUNK = 64
CONV_W = 5
RMS_EPS = 1e-6
NEG_INF = -1e30
ROPE_BASE = 10000.0
GLA_H = 4
GLA_DV = BRANCH_W // GLA_H
GLA_DK = GLA_DV // 2
GLA_LR = 16
GLA_TAU = 16.0
NA_H = 4
NA_D = BRANCH_W // NA_H
NA_WIN_R = 8
NA_WIN_C = 16
GDN_H = 4
GDN_D = BRANCH_W // GDN_H
M2_P = 64
M2_H = BRANCH_W // M2_P
M2_N = 128
M2_G = 2
M2_CONV_CH = BRANCH_W + 2 * M2_G * M2_N
D_FF = ((8 * D_MODEL + 3 * 256 - 1) // (3 * 256)) * 256
GLA_IN = 2 * GLA_H * GLA_DK + 2 * BRANCH_W + 2 * GLA_LR
NA_IN = 3 * BRANCH_W
GDN_IN = 4 * BRANCH_W + 4 * GDN_H
M2_IN = BRANCH_W + M2_CONV_CH + 2 * M2_H
MIX_IN = GLA_IN + NA_IN + GDN_IN + M2_IN

F32 = jnp.float32
BF16 = jnp.bfloat16

LANE = 128
SUBLANE = 8
V7X_VMEM_BYTES = 64 * 1024 * 1024
VMEM_LIMIT = V7X_VMEM_BYTES - 8 * 1024 * 1024

TOK_BLK = 4 * CHUNK
ROWS_PER_BLK = TOK_BLK // GRID_W
HALO = SUBLANE
MOD_ROWS = 8

P_GDN_QKV = 0
P_GLA_V = 1536
P_M2_XBC = 2048
P_GLA_G = 3072
P_NA_Q, P_NA_K, P_NA_V = 3584, 4096, 4608
P_GDN_Z, P_M2_Z = 5120, 5632
P_GLA_Q, P_GLA_K = 6144, 6400
P_GLA_LR, P_GDN_SC, P_M2_DT = 6656, 6784, 6912
P_COLS = 7168
MIX_TILE = 1024


def _mix_fields():
    gla, na, gdn, m2 = 0, GLA_IN, GLA_IN + NA_IN, GLA_IN + NA_IN + GDN_IN
    return sorted([
        (P_GLA_V, gla + 512, 512), (P_GLA_G, gla + 1024, 512),
        (P_NA_Q, na, 512), (P_NA_K, na + 512, 512), (P_NA_V, na + 1024, 512),
        (P_GDN_QKV, gdn, 1536), (P_GDN_Z, gdn + 1536, 512),
        (P_M2_Z, m2, 512), (P_M2_XBC, m2 + 512, M2_CONV_CH),
        (P_GLA_Q, gla, 256), (P_GLA_K, gla + 256, 256),
        (P_GLA_LR, gla + 1536, 2 * GLA_LR), (P_GDN_SC, gdn + 2048, 4 * GDN_H),
        (P_M2_DT, m2 + 512 + M2_CONV_CH, 2 * M2_H)])


def _cparams(sem):
    return pltpu.CompilerParams(dimension_semantics=sem, vmem_limit_bytes=VMEM_LIMIT)


def _wprep_kernel(w_ref, mix_ref, gate_ref):
    col = 0
    for dst, src, n in _mix_fields():
        if dst > col:
            mix_ref[:, col:dst] = jnp.zeros((mix_ref.shape[0], dst - col), mix_ref.dtype)
        mix_ref[:, dst:dst + n] = w_ref[:, src:src + n].astype(mix_ref.dtype)
        col = dst + n
    mix_ref[:, col:] = jnp.zeros((mix_ref.shape[0], P_COLS - col), mix_ref.dtype)
    gate_ref[...] = w_ref[:, MIX_IN:].astype(gate_ref.dtype)


def _split_in_weights(w_in, layer):
    _, D, N = w_in.shape
    rb = 256
    return pl.pallas_call(
        _wprep_kernel,
        grid=(D // rb,),
        in_specs=[pl.BlockSpec((None, rb, N), lambda r: (layer, r, 0))],
        out_specs=[pl.BlockSpec((rb, P_COLS), lambda r: (r, 0)),
                   pl.BlockSpec((rb, N - MIX_IN), lambda r: (r, 0))],
        out_shape=[jax.ShapeDtypeStruct((D, P_COLS), BF16), jax.ShapeDtypeStruct((D, N - MIX_IN), BF16)],
        compiler_params=_cparams(("arbitrary",)),
        name="split_in_weights",
    )(w_in)


def _sigmoid(x):
    return 1.0 / (1.0 + jnp.exp(-x))


def _silu(x):
    return x * _sigmoid(x)


def _softplus(x):
    return jnp.maximum(x, 0.0) + jnp.log1p(jnp.exp(-jnp.abs(x)))


def _dot(a, b):
    return jnp.dot(a.astype(BF16), b.astype(BF16), preferred_element_type=F32)


def _dot_nt(a, b):
    return lax.dot_general(a.astype(BF16), b.astype(BF16), (((1,), (1,)), ((), ())),
                           preferred_element_type=F32)


def _dot_tn(a, b):
    return lax.dot_general(a.astype(BF16), b.astype(BF16), (((0,), (0,)), ((), ())),
                           preferred_element_type=F32)


def _split3(a):
    hi = a.astype(BF16)
    r1 = a - hi.astype(F32)
    mid = r1.astype(BF16)
    return hi, mid, (r1 - mid.astype(F32)).astype(BF16)


def _select_rows(m01, a):
    d = lambda y: jnp.dot(m01, y, preferred_element_type=F32)
    hi, mid, lo = _split3(a)
    return d(hi) + (d(mid) + d(lo))


def _select_cols(a, m01):
    d = lambda y: jnp.dot(y, m01, preferred_element_type=F32)
    hi, mid, lo = _split3(a)
    return d(hi) + (d(mid) + d(lo))


def _split(a):
    hi = a.astype(BF16)
    return hi, (a - hi.astype(F32)).astype(BF16)


def _dot_split(a, b):
    (ah, al), (bh, bl) = a, b
    d = lambda x, y: jnp.dot(x, y, preferred_element_type=F32)
    return d(ah, bh) + (d(ah, bl) + d(al, bh))


def _iota2(shape, dim):
    return lax.broadcasted_iota(jnp.int32, shape, dim)


def _idiv(x, n):
    assert n & (n - 1) == 0
    return lax.shift_right_logical(x, jnp.int32(n.bit_length() - 1))


def _imod(x, n):
    assert n & (n - 1) == 0
    return x & (n - 1)


def _chunk_mask(rev, strict=False):
    r = _iota2((TOK_BLK, TOK_BLK), 0)
    c = _iota2((TOK_BLK, TOK_BLK), 1)
    same = _idiv(r, CHUNK) == _idiv(c, CHUNK)
    if rev:
        tri = (c > r) if strict else (c >= r)
    else:
        tri = (c < r) if strict else (c <= r)
    return same & tri


def _chunk_order(rev):
    n = TOK_BLK // CHUNK
    return list(range(n - 1, -1, -1)) if rev else list(range(n))


def _row_split(i, tm, s1):
    r0 = i * tm
    return lax.div(r0, jnp.int32(s1)), lax.rem(r0, jnp.int32(s1))


def _pick_mod(ref, b, nb, is_ctx):
    return jnp.where(is_ctx, ref[nb:nb + 1, :], ref[pl.ds(b, 1), :])


def _ada_kernel(c_ref, w_ref, b_ref, o_ref):
    a = _silu(c_ref[...])
    o_ref[...] = _dot(a, w_ref[...]) + b_ref[...]


def _ada(cvec, w_ada, b_ada):
    L, D, N = w_ada.shape
    tn = 1024
    return pl.pallas_call(
        _ada_kernel,
        grid=(L, N // tn),
        in_specs=[pl.BlockSpec((MOD_ROWS, D), lambda l, j: (0, 0)),
                  pl.BlockSpec((None, D, tn), lambda l, j: (l, 0, j)),
                  pl.BlockSpec((None, 1, tn), lambda l, j: (l, 0, j))],
        out_specs=pl.BlockSpec((None, MOD_ROWS, tn), lambda l, j: (l, 0, j)),
        out_shape=jax.ShapeDtypeStruct((L, MOD_ROWS, N), F32),
        compiler_params=_cparams(("arbitrary", "arbitrary")),
        name="ada_mod",
    )(cvec, w_ada, b_ada.reshape(L, 1, N))


def _normmod_kernel(x_ref, g_ref, sc_ref, sh_ref, o_ref, *, nb):
    row = jnp.where(pl.program_id(1) == 0, nb, pl.program_id(0))
    x = x_ref[...]
    y = x * lax.rsqrt(jnp.mean(x * x, axis=-1, keepdims=True) + RMS_EPS) * g_ref[...]
    o_ref[...] = (y * (1.0 + sc_ref[pl.ds(row, 1), :]) + sh_ref[pl.ds(row, 1), :]).astype(o_ref.dtype)


def _normmod(xs, g, sc, sh, *, nbat, nblk):
    R, D = xs.shape
    const = lambda b, s: (0, 0)
    row = lambda b, s: (b * nblk + s, 0)
    return pl.pallas_call(
        functools.partial(_normmod_kernel, nb=nbat),
        grid=(nbat, nblk),
        in_specs=[pl.BlockSpec((TOK_BLK, D), row), pl.BlockSpec((1, D), const),
                  pl.BlockSpec((MOD_ROWS, D), const), pl.BlockSpec((MOD_ROWS, D), const)],
        out_specs=pl.BlockSpec((TOK_BLK, D), row),
        out_shape=jax.ShapeDtypeStruct((R, D), BF16),
        compiler_params=_cparams(("arbitrary", "arbitrary")),
        name="norm_mod",
    )(xs, g.reshape(1, D), sc, sh)


def _concat_norm_kernel(ctx_ref, x_ref, g_ref, sc_ref, sh_ref, xs_ref, h_ref, *, nb):
    b, s = pl.program_id(0), pl.program_id(1)
    is_ctx = s == 0
    x = jnp.where(is_ctx, ctx_ref[...], x_ref[...])
    xs_ref[...] = x
    row = jnp.where(is_ctx, nb, b)
    y = x * lax.rsqrt(jnp.mean(x * x, axis=-1, keepdims=True) + RMS_EPS) * g_ref[...]
    h_ref[...] = (y * (1.0 + sc_ref[pl.ds(row, 1), :]) + sh_ref[pl.ds(row, 1), :]).astype(h_ref.dtype)


def _concat_norm(ctx, x, g, sc, sh, *, nblk):
    nbat, t, D = x.shape
    R = nbat * nblk * TOK_BLK
    const = lambda b, s: (0, 0)
    row = lambda b, s: (b * nblk + s, 0)
    return pl.pallas_call(
        functools.partial(_concat_norm_kernel, nb=nbat),
        grid=(nbat, nblk),
        in_specs=[pl.BlockSpec((None, TOK_BLK, D), lambda b, s: (b, 0, 0)),
                  pl.BlockSpec((None, TOK_BLK, D), lambda b, s: (b, jnp.maximum(s - 1, 0), 0)),
                  pl.BlockSpec((1, D), const), pl.BlockSpec((MOD_ROWS, D), const), pl.BlockSpec((MOD_ROWS, D), const)],
        out_specs=[pl.BlockSpec((TOK_BLK, D), row), pl.BlockSpec((TOK_BLK, D), row)],
        out_shape=[jax.ShapeDtypeStruct((R, D), F32), jax.ShapeDtypeStruct((R, D), BF16)],
        compiler_params=_cparams(("arbitrary", "arbitrary")),
        name="concat_norm",
    )(ctx, x, g.reshape(1, D), sc, sh)


def _mm_kernel(a_ref, w_ref, o_ref):
    o_ref[...] = jnp.dot(a_ref[...], w_ref[...], preferred_element_type=F32).astype(o_ref.dtype)


def _matmul(a, w, *, tm, tn, out_dtype):
    R, K = a.shape
    N = w.shape[1]
    assert R % tm == 0 and N % tn == 0
    return pl.pallas_call(
        _mm_kernel,
        grid=(R // tm, N // tn),
        in_specs=[pl.BlockSpec((tm, K), lambda i, j: (i, 0)),
                  pl.BlockSpec((K, tn), lambda i, j: (0, j))],
        out_specs=pl.BlockSpec((tm, tn), lambda i, j: (i, j)),
        out_shape=jax.ShapeDtypeStruct((R, N), out_dtype),
        compiler_params=_cparams(("arbitrary", "arbitrary")),
        name="mix_proj",
    )(a, w)


def _mm_res_kernel(a_ref, w_ref, x_ref, gt_ref, o_ref, *, tm, s1, lc, nb):
    b, pos0 = _row_split(pl.program_id(1), tm, s1)
    is_ctx = (pos0 + _iota2((tm, 1), 0)) < lc
    acc = jnp.dot(a_ref[...], w_ref[...], preferred_element_type=F32)
    o_ref[...] = x_ref[...] + _pick_mod(gt_ref, b, nb, is_ctx) * acc


def _matmul_residual(a, w, layer, xs, gate, *, tm, tn, s1, lc, nb, name):
    R, K = a.shape
    N = w.shape[2]
    assert R % tm == 0 and N % tn == 0
    return pl.pallas_call(
        functools.partial(_mm_res_kernel, tm=tm, s1=s1, lc=lc, nb=nb),
        grid=(N // tn, R // tm),
        in_specs=[pl.BlockSpec((tm, K), lambda j, i: (i, 0)),
                  pl.BlockSpec((None, K, tn), lambda j, i: (layer, 0, j)),
                  pl.BlockSpec((tm, tn), lambda j, i: (i, j)),
                  pl.BlockSpec((MOD_ROWS, tn), lambda j, i: (0, j))],
        out_specs=pl.BlockSpec((tm, tn), lambda j, i: (i, j)),
        out_shape=jax.ShapeDtypeStruct((R, N), F32),
        compiler_params=_cparams(("arbitrary", "arbitrary")),
        name=name,
    )(a, w, xs, gate)


def _outproj_norm_kernel(a_ref, w_ref, x_ref, gt_ref, g_ref, sc_ref, sh_ref, xo_ref, h_ref, *, tm, s1, lc, nb):
    b, pos0 = _row_split(pl.program_id(0), tm, s1)
    is_ctx = (pos0 + _iota2((tm, 1), 0)) < lc
    acc = jnp.dot(a_ref[...], w_ref[...], preferred_element_type=F32)
    x = x_ref[...] + _pick_mod(gt_ref, b, nb, is_ctx) * acc
    xo_ref[...] = x
    y = x * lax.rsqrt(jnp.mean(x * x, axis=-1, keepdims=True) + RMS_EPS) * g_ref[...]
    h_ref[...] = (y * (1.0 + _pick_mod(sc_ref, b, nb, is_ctx)) + _pick_mod(sh_ref, b, nb, is_ctx)).astype(h_ref.dtype)


def _outproj_norm(a, w, layer, xs, gate, g, sc, sh, *, tm, s1, lc, nb):
    R, K = a.shape
    D = w.shape[2]
    row = lambda i: (i, 0)
    const = lambda i: (0, 0)
    return pl.pallas_call(
        functools.partial(_outproj_norm_kernel, tm=tm, s1=s1, lc=lc, nb=nb),
        grid=(R // tm,),
        in_specs=[pl.BlockSpec((tm, K), row), pl.BlockSpec((None, K, D), lambda i: (layer, 0, 0)),
                  pl.BlockSpec((tm, D), row),
                  pl.BlockSpec((MOD_ROWS, D), const), pl.BlockSpec((1, D), const),
                  pl.BlockSpec((MOD_ROWS, D), const), pl.BlockSpec((MOD_ROWS, D), const)],
        out_specs=[pl.BlockSpec((tm, D), row), pl.BlockSpec((tm, D), row)],
        out_shape=[jax.ShapeDtypeStruct((R, D), F32), jax.ShapeDtypeStruct((R, D), BF16)],
        compiler_params=_cparams(("arbitrary",)),
        name="out_proj",
    )(a, w, xs, gate, g.reshape(1, D), sc, sh)


def _ffn_up_kernel(a_ref, w1_ref, w3_ref, o_ref):
    a = a_ref[...]
    u = jnp.dot(a, w1_ref[...], preferred_element_type=F32)
    v = jnp.dot(a, w3_ref[...], preferred_element_type=F32)
    o_ref[...] = (_silu(u) * v).astype(o_ref.dtype)


def _ffn_up(a, w1, w3, layer, *, tm, tn):
    R, K = a.shape
    N = w1.shape[2]
    assert R % tm == 0 and N % tn == 0
    wspec = pl.BlockSpec((None, K, tn), lambda j, i: (layer, 0, j))
    return pl.pallas_call(
        _ffn_up_kernel,
        grid=(N // tn, R // tm),
        in_specs=[pl.BlockSpec((tm, K), lambda j, i: (i, 0)), wspec, wspec],
        out_specs=pl.BlockSpec((tm, tn), lambda j, i: (i, j)),
        out_shape=jax.ShapeDtypeStruct((R, N), BF16),
        compiler_params=_cparams(("arbitrary", "arbitrary")),
        name="ffn_up",
    )(a, w1, w3)


def _merge_kernel(h_ref, g0_ref, g1_ref, g2_ref, g3_ref, bm_ref, ya_ref, yb_ref, yc_ref, yd_ref, wb_ref, o_ref):
    h = h_ref[...]
    acc = None
    branches = zip((g0_ref, g1_ref, g2_ref, g3_ref), (ya_ref, yb_ref, yc_ref, yd_ref))
    for i, (wg_ref, y_ref) in enumerate(branches):
        gate = _sigmoid(jnp.dot(h, wg_ref[...], preferred_element_type=F32) + bm_ref[i])
        term = gate * jnp.dot(y_ref[...], wb_ref[i], preferred_element_type=F32)
        acc = term if acc is None else acc + term
    o_ref[...] = acc.astype(o_ref.dtype)


def _merge(h, wg, b_merge, ys, wb, layer, *, tm, tn):
    R, D = h.shape
    N = wb.shape[3]
    assert R % tm == 0 and N % tn == 0
    nj = N // tn
    yspec = pl.BlockSpec((tm, BRANCH_W), lambda i, j: (i, 0))
    gspec = lambda br: pl.BlockSpec((D, tn), lambda i, j: (0, br * nj + j))
    return pl.pallas_call(
        _merge_kernel,
        grid=(R // tm, nj),
        in_specs=[pl.BlockSpec((tm, D), lambda i, j: (i, 0)),
                  gspec(0), gspec(1), gspec(2), gspec(3),
                  pl.BlockSpec((N_BRANCH, 1, tn), lambda i, j: (0, 0, j)),
                  yspec, yspec, yspec, yspec,
                  pl.BlockSpec((None, N_BRANCH, BRANCH_W, tn), lambda i, j: (layer, 0, 0, j))],
        out_specs=pl.BlockSpec((tm, tn), lambda i, j: (i, j)),
        out_shape=jax.ShapeDtypeStruct((R, N), BF16),
        compiler_params=_cparams(("arbitrary", "arbitrary")),
        name="merge",
    )(h, wg, wg, wg, wg, b_merge.reshape(N_BRANCH, 1, N), *ys, wb)


def _conv_silu_block(xm, xp, xn, w, bias, blk, nblk):
    prev_ok = blk >= 2
    next_ok = (blk >= 1) & (blk < nblk - 1)
    rows = TOK_BLK + 2 * HALO
    half = CONV_W // 2
    xpad = jnp.concatenate([jnp.where(prev_ok, xp, 0.0), xm, jnp.where(next_ok, xn, 0.0)], axis=0)
    acc = bias + w[half:half + 1, :] * xm
    for j in range(CONV_W):
        if j != half:
            tap = pltpu.roll(xpad, (half - j) % rows, 0)[HALO:HALO + TOK_BLK]
            acc = acc + w[j:j + 1, :] * tap
    return _silu(acc)


def _conv_inputs(p, col0, w, bias, nblk):
    nbat, s1, _ = p.shape
    C = w.shape[1]
    assert col0 % C == 0
    cb = col0 // C
    hb = TOK_BLK // HALO
    nhalo = s1 // HALO
    wpad = jnp.concatenate([w, jnp.zeros((SUBLANE - CONV_W, C), F32)], axis=0)
    specs = [pl.BlockSpec((nbat, TOK_BLK, C), lambda r: (0, r, cb)),
             pl.BlockSpec((nbat, HALO, C), lambda r: (0, jnp.maximum(r * hb - 1, 0), cb)),
             pl.BlockSpec((nbat, HALO, C), lambda r: (0, jnp.minimum((r + 1) * hb, nhalo - 1), cb)),
             _const_spec((SUBLANE, C)), _const_spec((1, C))]
    return specs, [p, p, p, wpad, bias.reshape(1, C)]


def _scan_block(rev, nblk):
    if rev:
        return lambda s: jnp.where(s == 0, 0, nblk - s)
    return lambda s: s


def _blk_spec(nbat, width, col, blk):
    assert col % width == 0
    return pl.BlockSpec((nbat, TOK_BLK, width), lambda s: (0, blk(s), col // width))


def _const_spec(shape):
    return pl.BlockSpec(shape, lambda s: (0,) * len(shape))


def _head_rmsnorm_gate(o, g, gate, n_head, width):
    outs = []
    for h in range(n_head):
        oh = o[:, h * width:(h + 1) * width]
        yh = oh * lax.rsqrt(jnp.mean(oh * oh, axis=-1, keepdims=True) + RMS_EPS) * g
        outs.append(yh * _silu(gate[:, h * width:(h + 1) * width]))
    return jnp.concatenate(outs, axis=1)


def _scan_call(kern, name, in_specs, args, o_f, *, rev, nbat, s1, nblk, state_shape, conv_width=0):
    blk = _scan_block(rev, nblk)
    ospec = pl.BlockSpec((nbat, TOK_BLK, BRANCH_W), lambda s: (0, blk(s), 0))
    out_specs = [ospec]
    out_shape = [jax.ShapeDtypeStruct((nbat, s1, BRANCH_W), BF16 if rev else F32)]
    if rev:
        in_specs = in_specs + [ospec]
        args = args + [o_f]
    elif conv_width:
        out_specs.append(pl.BlockSpec((nbat, TOK_BLK, conv_width), lambda s: (0, s, 0)))
        out_shape.append(jax.ShapeDtypeStruct((nbat, s1, conv_width), F32))
    return pl.pallas_call(
        functools.partial(kern, rev=rev, nbat=nbat, nblk=nblk),
        grid=(nblk,),
        in_specs=in_specs,
        out_specs=out_specs,
        out_shape=out_shape,
        scratch_shapes=[pltpu.VMEM((nbat,) + state_shape, F32)],
        compiler_params=_cparams(("arbitrary",)),
        name=name + ("_bwd" if rev else "_fwd"),
    )(*args)


def _gla_kernel(q_ref, k_ref, v_ref, g_ref, lr_ref, cos_ref, sin_ref, a2_ref, ab_ref, lm_ref, ng_ref,
                *rest, rev, nbat, nblk):
    if rev:
        of_ref, y_ref, st_ref = rest
    else:
        o_ref, st_ref = rest

    @pl.when(pl.program_id(0) == 0)
    def _():
        st_ref[...] = jnp.zeros_like(st_ref)

    hk = GLA_H * GLA_DK
    lane = _iota2((TOK_BLK, hk), 1)
    first_half = _imod(lane, GLA_DK) < (GLA_DK // 2)
    head_of_lane = _idiv(lane, GLA_DK)
    cos = cos_ref[...]
    sin = sin_ref[...]
    mask = _chunk_mask(rev)
    n_chunk = TOK_BLK // CHUNK
    batch = range(nbat)

    def rope(x):
        partner = jnp.where(first_half, pltpu.roll(x, hk - GLA_DK // 2, 1), pltpu.roll(x, GLA_DK // 2, 1))
        return x * cos + partner * sin

    ks, vs, bs, q_ins, o_intras = [], [], [], [], []
    for bi in batch:
        q = rope(q_ref[bi]) * GLA_DK ** -0.5
        k = rope(k_ref[bi])
        v = v_ref[bi]
        loga = -_softplus(-(_dot(lr_ref[bi], a2_ref[...]) + ab_ref[...])) / GLA_TAU
        b = _select_rows(lm_ref[...], loga)
        q_in = q * jnp.exp(b)
        k_in = k * jnp.exp(-b)
        o_heads = []
        for h in range(GLA_H):
            att = _dot_nt(jnp.where(head_of_lane == h, q_in, 0.0), k_in)
            att = jnp.where(mask, att, 0.0)
            o_heads.append(_dot(att, v[:, h * GLA_DV:(h + 1) * GLA_DV]))
        ks.append(k)
        vs.append(v)
        bs.append(b)
        q_ins.append(q_in)
        o_intras.append(jnp.concatenate(o_heads, axis=1))

    sts = [st_ref[bi] for bi in batch]
    diag = _idiv(_iota2(sts[0].shape, 0), GLA_DV) == _idiv(_iota2(sts[0].shape, 1), GLA_DK)
    o_inter = [[None] * n_chunk for _ in batch]
    for c in _chunk_order(rev):
        rows = slice(c * CHUNK, (c + 1) * CHUNK)
        for bi in batch:
            b_c = bs[bi][rows]
            b_last = b_c[0:1] if rev else b_c[CHUNK - 1:CHUNK]
            o_inter[bi][c] = _dot_nt(q_ins[bi][rows], sts[bi])
            k_end = ks[bi][rows] * jnp.exp(b_last - b_c)
            ds = _dot_tn(vs[bi][rows], k_end)
            sts[bi] = sts[bi] * jnp.exp(b_last) + jnp.where(diag, ds, 0.0)
    for bi in batch:
        st_ref[bi] = sts[bi]
        o = o_intras[bi] + jnp.concatenate(o_inter[bi], axis=0)
        if rev:
            y = _head_rmsnorm_gate(of_ref[bi] + o, ng_ref[...], g_ref[bi], GLA_H, GLA_DV)
            y_ref[bi] = y.astype(y_ref.dtype)
        else:
            o_ref[bi] = o


def _cumsum_matrix(rev):
    r = np.arange(TOK_BLK)[:, None]
    c = np.arange(TOK_BLK)[None, :]
    same = (r // CHUNK) == (c // CHUNK)
    tri = (c >= r) if rev else (c <= r)
    return jnp.asarray((same & tri).astype(np.float32), dtype=BF16)


def _gla_dir(p, cos, sin, a2, ab, ng, o_f, *, rev, nblk):
    nbat, s1, _ = p.shape
    blk = _scan_block(rev, nblk)
    d = 1 if rev else 0
    a2d = jnp.zeros((LANE, GLA_H * GLA_DK), F32).at[d * GLA_LR:(d + 1) * GLA_LR].set(a2[d])
    tab = pl.BlockSpec((TOK_BLK, GLA_H * GLA_DK), lambda s: (blk(s), 0))
    in_specs = [_blk_spec(nbat, 256, P_GLA_Q, blk), _blk_spec(nbat, 256, P_GLA_K, blk),
                _blk_spec(nbat, 512, P_GLA_V, blk), _blk_spec(nbat, 512, P_GLA_G, blk),
                _blk_spec(nbat, LANE, P_GLA_LR, blk), tab, tab,
                _const_spec((LANE, 256)), _const_spec((1, 256)), _const_spec((TOK_BLK, TOK_BLK)),
                _const_spec((1, GLA_DV))]
    args = [p, p, p, p, p, cos, sin, a2d, ab[d].reshape(1, -1), _cumsum_matrix(rev), ng.reshape(1, -1)]
    return _scan_call(_gla_kernel, "gla", in_specs, args, o_f, rev=rev, nbat=nbat, s1=s1, nblk=nblk,
                      state_shape=(GLA_H * GLA_DV, GLA_H * GLA_DK))[0]


def _rope_tables(lc, t):
    n_freq = GLA_DK // 4
    freqs = ROPE_BASE ** (-jnp.arange(n_freq, dtype=F32) / n_freq)
    tt = jnp.arange(t)
    row = (tt // GRID_W).astype(F32)
    col = (tt % GRID_W).astype(F32)
    ang = jnp.concatenate([row[:, None] * freqs, col[:, None] * freqs], axis=-1)
    cos, sin = jnp.cos(ang), jnp.sin(ang)
    cos = jnp.concatenate([jnp.ones((lc, GLA_DK // 2), F32), cos], axis=0)
    sin = jnp.concatenate([jnp.zeros((lc, GLA_DK // 2), F32), sin], axis=0)
    cos_h = jnp.concatenate([cos, cos], axis=1)
    sin_h = jnp.concatenate([-sin, sin], axis=1)
    return jnp.tile(cos_h, (1, GLA_H)), jnp.tile(sin_h, (1, GLA_H))


def _na_kernel(q_ref, kp_ref, kc_ref, kn_ref, kx_ref, vp_ref, vc_ref, vn_ref, vx_ref, bias_ref, y_ref, *, nbat):
    scale = NA_D ** -0.5
    for bi in range(nbat):
        outs = []
        for h in range(NA_H):
            hs = slice(h * NA_D, (h + 1) * NA_D)
            qh = (q_ref[bi, :, hs] * scale).astype(BF16)
            s = jnp.concatenate(
                [_dot_nt(qh, kp_ref[bi, :, hs]) + bias_ref[0, h],
                 _dot_nt(qh, kc_ref[bi, :, hs]) + bias_ref[1, h],
                 _dot_nt(qh, kn_ref[bi, :, hs]) + bias_ref[2, h],
                 _dot_nt(qh, kx_ref[bi, :, hs])], axis=1)
            m = jnp.max(s, axis=-1, keepdims=True)
            e = jnp.exp(s - m)
            o = (_dot(e[:, 0:TOK_BLK], vp_ref[bi, :, hs]) + _dot(e[:, TOK_BLK:2 * TOK_BLK], vc_ref[bi, :, hs])
                 + _dot(e[:, 2 * TOK_BLK:3 * TOK_BLK], vn_ref[bi, :, hs])
                 + _dot(e[:, 3 * TOK_BLK:], vx_ref[bi, :, hs]))
            outs.append(o / jnp.sum(e, axis=-1, keepdims=True))
        y_ref[bi] = jnp.concatenate(outs, axis=1).astype(y_ref.dtype)


def _na_bias_tiles(rpb):
    edge = GRID_W - NA_WIN_C
    ext = jnp.concatenate([jnp.repeat(rpb[..., :1], edge, axis=-1), rpb,
                           jnp.repeat(rpb[..., -1:], edge, axis=-1)], axis=-1)
    toep = jnp.stack([ext[..., GRID_W - 1 - qc:2 * GRID_W - 1 - qc] for qc in range(GRID_W)], axis=-2)
    qc = np.arange(GRID_W)[:, None]
    kc = np.arange(GRID_W)[None, :]
    c0 = np.clip(qc - NA_WIN_C // 2, 0, GRID_W - NA_WIN_C)
    col_ok = (kc >= c0) & (kc < c0 + NA_WIN_C)
    toep = jnp.where(jnp.asarray(col_ok), toep, NEG_INF).astype(F32)
    masked = jnp.full((NA_H, GRID_W, GRID_W), NEG_INF, F32)
    kinds = []
    for kind in range(3):
        offs = []
        for off in (-1, 0, 1):
            rows = []
            for qr in range(ROWS_PER_BLK):
                start = (0, qr - NA_WIN_R // 2, ROWS_PER_BLK - NA_WIN_R)[kind]
                cols = []
                for kb in range(ROWS_PER_BLK):
                    kr = kb + ROWS_PER_BLK * off
                    ok = start <= kr < start + NA_WIN_R
                    cols.append(toep[:, kr - qr + NA_WIN_R - 1] if ok else masked)
                rows.append(jnp.concatenate(cols, axis=-1))
            offs.append(jnp.concatenate(rows, axis=-2))
        kinds.append(jnp.stack(offs))
    kinds.append(jnp.full_like(kinds[0], NEG_INF))
    return jnp.stack(kinds)


def _na(p, bias, *, nblk):
    nbat, s1, _ = p.shape

    def kind(s):
        return jnp.where(s == 0, 3, jnp.where(s == 1, 0, jnp.where(s == nblk - 1, 2, 1)))

    prev = lambda s: jnp.maximum(s - 1, 1)
    cur = lambda s: s
    nxt = lambda s: jnp.minimum(s + 1, nblk - 1)
    ctx = lambda s: 0
    spec = lambda col, blk: _blk_spec(nbat, BRANCH_W, col, blk)
    return pl.pallas_call(
        functools.partial(_na_kernel, nbat=nbat),
        grid=(nblk,),
        in_specs=[spec(P_NA_Q, cur),
                  spec(P_NA_K, prev), spec(P_NA_K, cur), spec(P_NA_K, nxt), spec(P_NA_K, ctx),
                  spec(P_NA_V, prev), spec(P_NA_V, cur), spec(P_NA_V, nxt), spec(P_NA_V, ctx),
                  pl.BlockSpec((None, 3, NA_H, TOK_BLK, TOK_BLK), lambda s: (kind(s), 0, 0, 0, 0))],
        out_specs=pl.BlockSpec((nbat, TOK_BLK, BRANCH_W), lambda s: (0, s, 0)),
        out_shape=jax.ShapeDtypeStruct((nbat, s1, BRANCH_W), BF16),
        compiler_params=_cparams(("arbitrary",)),
        name="nbr_attn",
    )(p, p, p, p, p, p, p, p, p, bias)


def _l2norm(x):
    return x * lax.rsqrt(jnp.sum(x * x, axis=-1, keepdims=True) + RMS_EPS)


def _gdn_kernel(*refs, rev, nbat, nblk):
    if rev:
        qkv_ref, z_ref, sc_ref, alog_ref, dtb_ref, lm_ref, ng_ref, of_ref, y_ref, st_ref = refs
        qkv = [qkv_ref[bi] for bi in range(nbat)]
    else:
        (xm_ref, xp_ref, xn_ref, cw_ref, cb_ref, z_ref, sc_ref, alog_ref, dtb_ref, lm_ref, ng_ref,
         o_ref, qkvc_ref, st_ref) = refs
        qkv = []
        for bi in range(nbat):
            qkv.append(_conv_silu_block(xm_ref[bi], xp_ref[bi], xn_ref[bi], cw_ref[...], cb_ref[...],
                                        pl.program_id(0), nblk))
            qkvc_ref[bi] = qkv[bi]

    @pl.when(pl.program_id(0) == 0)
    def _():
        st_ref[...] = jnp.zeros_like(st_ref)

    d = 1 if rev else 0
    m_incl = _chunk_mask(rev)
    m_strict = _chunk_mask(rev, strict=True)
    eye = (_iota2((TOK_BLK, TOK_BLK), 0) == _iota2((TOK_BLK, TOK_BLK), 1)).astype(F32)
    n_chunk = TOK_BLK // CHUNK

    units = [(bi, h) for bi in range(nbat) for h in range(GDN_H)]
    qs, ks, bcols, ebcs, attns, nmats, rhss = [], [], [], [], [], [], []
    for bi in range(nbat):
        sc = sc_ref[bi]
        beta_all = _sigmoid(sc)
        g_all = -jnp.exp(alog_ref[...]) * _softplus(sc + dtb_ref[...])
        b_all = _select_rows(lm_ref[...], g_all)
        b_all_t = b_all.T
        eb_all = jnp.exp(b_all)
        for h in range(GDN_H):
            head = lambda part: qkv[bi][:, part * BRANCH_W + h * GDN_D:part * BRANCH_W + (h + 1) * GDN_D]
            qh = _l2norm(head(0)) * GDN_D ** -0.5
            kh = _l2norm(head(1))
            vh = head(2)
            lb, lg = GDN_H * d + h, 2 * GDN_H + GDN_H * d + h
            beta = beta_all[:, lb:lb + 1]
            bcol = b_all[:, lg:lg + 1]
            brow = b_all_t[lg:lg + 1, :]
            ebc = eb_all[:, lg:lg + 1]
            diff = bcol - brow
            dec_incl = jnp.where(m_incl, jnp.exp(jnp.where(m_incl, diff, 0.0)), 0.0)
            dec_strict = jnp.where(m_strict, dec_incl, 0.0)
            kk = _dot_nt(kh, kh)
            qs.append(qh)
            ks.append(kh)
            bcols.append(bcol)
            ebcs.append(ebc)
            attns.append(_dot_nt(qh, kh) * dec_incl)
            nmats.append(-(beta * kk * dec_strict))
            rhss.append(_split(jnp.concatenate([kh * (beta * ebc), vh * beta], axis=1)))

    bdot = lambda x, y: jnp.dot(x, y, preferred_element_type=F32)
    nsplit = [_split(n) for n in nmats]
    t0s = [eye + n for n in nmats]
    powers = [ns[0] for ns in nsplit]
    for _ in range(5):
        powers = [bdot(m, m).astype(BF16) for m in powers]
        t0s = [t + bdot(t.astype(BF16), m) for t, m in zip(t0s, powers)]
    t0split = [_split(t) for t in t0s]
    resid = [(eye - t) + _dot_split(ns, ts) for t, ns, ts in zip(t0s, nsplit, t0split)]
    tinvs = [t + bdot(ts[0], r.astype(BF16)) for t, ts, r in zip(t0s, t0split, resid)]
    sols = [_dot_split(_split(t), r) for t, r in zip(tinvs, rhss)]
    ws = [s[:, :GDN_D] for s in sols]
    u0s = [s[:, GDN_D:] for s in sols]

    sts = [st_ref[bi, h] for bi, h in units]
    u_parts = [[None] * n_chunk for _ in units]
    o_parts = [[None] * n_chunk for _ in units]
    for c in _chunk_order(rev):
        rows = slice(c * CHUNK, (c + 1) * CHUNK)
        last = c * CHUNK if rev else (c + 1) * CHUNK - 1
        for i in range(len(units)):
            st = sts[i]
            b_last = bcols[i][last:last + 1]
            u = u0s[i][rows] - _dot(ws[i][rows], st)
            u_parts[i][c] = u
            o_parts[i][c] = ebcs[i][rows] * _dot(qs[i][rows], st)
            k_end = ks[i][rows] * jnp.exp(b_last - bcols[i][rows])
            sts[i] = jnp.exp(b_last) * st + _dot_tn(k_end, u)
    o_units = []
    for i, (bi, h) in enumerate(units):
        st_ref[bi, h] = sts[i]
        o_units.append(jnp.concatenate(o_parts[i], axis=0) + _dot(attns[i], jnp.concatenate(u_parts[i], axis=0)))
    for bi in range(nbat):
        o = jnp.concatenate(o_units[bi * GDN_H:(bi + 1) * GDN_H], axis=1)
        if rev:
            y = _head_rmsnorm_gate(of_ref[bi] + o, ng_ref[...], z_ref[bi], GDN_H, GDN_D)
            y_ref[bi] = y.astype(y_ref.dtype)
        else:
            o_ref[bi] = o


def _gdn_dir(p, qkv, conv_w, a_log, dt_bias, ng, o_f, *, rev, nblk):
    nbat, s1, _ = p.shape
    blk = _scan_block(rev, nblk)
    d = 1 if rev else 0
    width = 3 * BRANCH_W
    lane0 = 2 * GDN_H + GDN_H * d
    alog_row = jnp.zeros((1, LANE), F32).at[0, lane0:lane0 + GDN_H].set(a_log[d])
    dtb_row = jnp.zeros((1, LANE), F32).at[0, lane0:lane0 + GDN_H].set(dt_bias[d])
    if rev:
        in_specs, args = [_blk_spec(nbat, width, 0, blk)], [qkv]
    else:
        in_specs, args = _conv_inputs(p, P_GDN_QKV, conv_w, jnp.zeros((width,), F32), nblk)
    in_specs += [_blk_spec(nbat, 512, P_GDN_Z, blk), _blk_spec(nbat, LANE, P_GDN_SC, blk),
                 _const_spec((1, LANE)), _const_spec((1, LANE)), _const_spec((TOK_BLK, TOK_BLK)),
                 _const_spec((1, GDN_D))]
    args += [p, p, alog_row, dtb_row, _cumsum_matrix(rev), ng.reshape(1, -1)]
    return _scan_call(_gdn_kernel, "gdn", in_specs, args, o_f, rev=rev, nbat=nbat, s1=s1, nblk=nblk,
                      state_shape=(GDN_H, GDN_D, GDN_D), conv_width=width)


def _ssd_kernel(*refs, rev, nbat, nblk):
    if rev:
        (xbc_ref, dt_ref, z_ref, alog_ref, dtb_ref, ex_ref, lm_ref, dsk_ref, ng_ref, of_ref,
         y_ref, st_ref) = refs
        xbc = [xbc_ref[bi] for bi in range(nbat)]
    else:
        (xm_ref, xp_ref, xn_ref, cw_ref, cb_ref, dt_ref, z_ref, alog_ref, dtb_ref, ex_ref, lm_ref, dsk_ref, ng_ref,
         o_ref, xbcc_ref, st_ref) = refs
        xbc = []
        for bi in range(nbat):
            xbc.append(_conv_silu_block(xm_ref[bi], xp_ref[bi], xn_ref[bi], cw_ref[...], cb_ref[...],
                                        pl.program_id(0), nblk))
            xbcc_ref[bi] = xbc[bi]

    @pl.when(pl.program_id(0) == 0)
    def _():
        st_ref[...] = jnp.zeros_like(st_ref)

    d = 1 if rev else 0
    heads_per_g = M2_H // M2_G
    gw = heads_per_g * M2_P
    n_chunk = TOK_BLK // CHUNK
    mask = _chunk_mask(rev)
    lane = _iota2((TOK_BLK, gw), 1)
    ex = ex_ref[...]

    units = [(bi, g) for bi in range(nbat) for g in range(M2_G)]
    cqs, bks, xvs, b_es, eb_es, accs = [], [], [], [], [], []
    for bi in range(nbat):
        dt = _softplus(dt_ref[bi] + dtb_ref[...])
        loga = -jnp.exp(alog_ref[...]) * dt
        b8 = _select_rows(lm_ref[...], loga)
        b8_t = b8.T
        xv = xbc[bi][:, :BRANCH_W] * _select_cols(dt, ex)
        b_e = _select_cols(b8, ex)
        eb_e = jnp.exp(b_e)
        for g in range(M2_G):
            gs = slice(g * M2_N, (g + 1) * M2_N)
            xs_g = slice(g * gw, (g + 1) * gw)
            bk = xbc[bi][:, BRANCH_W + g * M2_N:BRANCH_W + (g + 1) * M2_N]
            cq = xbc[bi][:, BRANCH_W + (M2_G + g) * M2_N:BRANCH_W + (M2_G + g + 1) * M2_N]
            scores = _dot_nt(cq, bk)
            xv_g = xv[:, xs_g]
            acc = None
            for hh in range(heads_per_g):
                lh = M2_H * d + heads_per_g * g + hh
                diff = b8[:, lh:lh + 1] - b8_t[lh:lh + 1, :]
                dec = jnp.where(mask, jnp.exp(jnp.where(mask, diff, 0.0)), 0.0)
                term = _dot(scores * dec, jnp.where(_idiv(lane, M2_P) == hh, xv_g, 0.0))
                acc = term if acc is None else acc + term
            cqs.append(cq)
            bks.append(bk)
            xvs.append(xv_g)
            b_es.append(b_e[:, xs_g])
            eb_es.append(eb_e[:, xs_g])
            accs.append(acc)

    sts = [st_ref[bi, g] for bi, g in units]
    o_parts = [[None] * n_chunk for _ in units]
    for c in _chunk_order(rev):
        rows = slice(c * CHUNK, (c + 1) * CHUNK)
        last = c * CHUNK if rev else (c + 1) * CHUNK - 1
        for i in range(len(units)):
            b_c = b_es[i][rows]
            b_last = b_es[i][last:last + 1]
            o_parts[i][c] = eb_es[i][rows] * _dot(cqs[i][rows], sts[i])
            ds = _dot_tn(bks[i][rows], xvs[i][rows] * jnp.exp(b_last - b_c))
            sts[i] = jnp.exp(b_last) * sts[i] + ds
    o_units = []
    for i, (bi, g) in enumerate(units):
        st_ref[bi, g] = sts[i]
        o_units.append(accs[i] + jnp.concatenate(o_parts[i], axis=0))
    for bi in range(nbat):
        o = jnp.concatenate(o_units[bi * M2_G:(bi + 1) * M2_G], axis=1)
        if rev:
            y = (of_ref[bi] + o + dsk_ref[...] * xbc[bi][:, :BRANCH_W]) * _silu(z_ref[bi])
            y = y * lax.rsqrt(jnp.mean(y * y, axis=-1, keepdims=True) + RMS_EPS) * ng_ref[...]
            y_ref[bi] = y.astype(y_ref.dtype)
        else:
            o_ref[bi] = o


def _ssd_dir(p, xbc, conv_w, conv_b, a_log, dt_bias, d_skip, ng, o_f, *, rev, nblk):
    nbat, s1, _ = p.shape
    blk = _scan_block(rev, nblk)
    d = 1 if rev else 0
    lane0 = M2_H * d
    alog_row = jnp.zeros((1, LANE), F32).at[0, lane0:lane0 + M2_H].set(a_log[d])
    dtb_row = jnp.zeros((1, LANE), F32).at[0, lane0:lane0 + M2_H].set(dt_bias[d])
    ex = np.zeros((LANE, BRANCH_W), np.float32)
    for h in range(M2_H):
        ex[lane0 + h, h * M2_P:(h + 1) * M2_P] = 1.0
    dsk_row = jnp.repeat(d_skip, M2_P).reshape(1, BRANCH_W)
    if rev:
        in_specs, args = [_blk_spec(nbat, M2_CONV_CH, 0, blk)], [xbc]
    else:
        in_specs, args = _conv_inputs(p, P_M2_XBC, conv_w, conv_b, nblk)
    in_specs += [_blk_spec(nbat, LANE, P_M2_DT, blk), _blk_spec(nbat, 512, P_M2_Z, blk),
                 _const_spec((1, LANE)), _const_spec((1, LANE)), _const_spec((LANE, BRANCH_W)),
                 _const_spec((TOK_BLK, TOK_BLK)), _const_spec((1, BRANCH_W)), _const_spec((1, BRANCH_W))]
    args += [p, p, alog_row, dtb_row, jnp.asarray(ex, dtype=BF16), _cumsum_matrix(rev), dsk_row, ng.reshape(1, -1)]
    return _scan_call(_ssd_kernel, "ssd", in_specs, args, o_f, rev=rev, nbat=nbat, s1=s1, nblk=nblk,
                      state_shape=(M2_G, M2_N, (M2_H // M2_G) * M2_P), conv_width=M2_CONV_CH)


def _final_norm_kernel(x_ref, g_ref, o_ref):
    x = x_ref[...]
    o_ref[...] = x * lax.rsqrt(jnp.mean(x * x, axis=-1, keepdims=True) + RMS_EPS) * g_ref[...]


def _final_norm(xs, g, *, nbat, nblk, lc):
    D = xs.shape[1]
    cb = lc // TOK_BLK
    nlat = nblk - cb
    out = pl.pallas_call(
        _final_norm_kernel,
        grid=(nbat, nlat),
        in_specs=[pl.BlockSpec((TOK_BLK, D), lambda b, s: (b * nblk + cb + s, 0)),
                  pl.BlockSpec((1, D), lambda b, s: (0, 0))],
        out_specs=pl.BlockSpec((TOK_BLK, D), lambda b, s: (b * nlat + s, 0)),
        out_shape=jax.ShapeDtypeStruct((nbat * nlat * TOK_BLK, D), F32),
        compiler_params=_cparams(("arbitrary", "arbitrary")),
        name="final_norm",
    )(xs, g.reshape(1, D))
    return out.reshape(nbat, nlat * TOK_BLK, D)


def _row_tile(s1, limit):
    for tm in range(limit - limit % 16, 0, -16):
        if s1 % tm == 0:
            return tm
    raise ValueError(f"no row tile for sequence length {s1}")


def kernel(x, c, ctx, c_ctx, norm1_g, norm2_g, w_ada, b_ada, w_in, b_merge, gla_a2, gla_ab, gla_norm_g, na_rpb, gdn_conv, gdn_a_log, gdn_dt_bias, gdn_norm_g, m2_conv, m2_conv_b, m2_a_log, m2_dt_bias, m2_d, m2_norm_g, w_branch, w_out, w_ffn1, w_ffn3, w_ffn2, final_norm_g):
    nbat, t, D = x.shape
    lc = ctx.shape[1]
    depth = w_in.shape[0]
    assert D == D_MODEL and lc == TOK_BLK and t % TOK_BLK == 0 and t // GRID_W >= 3 * ROWS_PER_BLK
    assert nbat + 1 <= MOD_ROWS
    s1 = lc + t
    nblk = s1 // TOK_BLK
    tm = _row_tile(s1, 1056)
    geo = dict(s1=s1, lc=lc, nb=nbat)

    cvec = jnp.concatenate([c, c_ctx[None], jnp.zeros((MOD_ROWS - nbat - 1, D), F32)], axis=0)
    mods = _ada(cvec, w_ada, b_ada).reshape(depth, MOD_ROWS, 6, D).transpose(0, 2, 1, 3)
    cos, sin = _rope_tables(lc, t)
    wb_in, wb_branch, wb_out, wb_ffn1, wb_ffn3, wb_ffn2 = (
        w.astype(BF16) for w in (w_in, w_branch, w_out, w_ffn1, w_ffn3, w_ffn2))
    flat = lambda y: y.reshape(nbat * s1, BRANCH_W)

    for l in range(depth):
        sh1, sc1, g1, sh2, sc2, g2 = (mods[l, i] for i in range(6))

        if l == 0:
            xs, h = _concat_norm(ctx, x, norm1_g[l], sc1, sh1, nblk=nblk)
        else:
            h = _normmod(xs, norm1_g[l], sc1, sh1, nbat=nbat, nblk=nblk)
        w_mix, w_gate = _split_in_weights(wb_in, l)
        p = _matmul(h, w_mix, tm=tm, tn=MIX_TILE, out_dtype=F32).reshape(nbat, s1, P_COLS)

        o_f = _gla_dir(p, cos, sin, gla_a2[l], gla_ab[l], gla_norm_g[l], None, rev=False, nblk=nblk)
        ya = _gla_dir(p, cos, sin, gla_a2[l], gla_ab[l], gla_norm_g[l], o_f, rev=True, nblk=nblk)

        yb = _na(p, _na_bias_tiles(na_rpb[l]), nblk=nblk)

        gdn_par = (gdn_conv[l], gdn_a_log[l], gdn_dt_bias[l], gdn_norm_g[l])
        o_f, qkv = _gdn_dir(p, None, *gdn_par, None, rev=False, nblk=nblk)
        yc, = _gdn_dir(p, qkv, *gdn_par, o_f, rev=True, nblk=nblk)

        ssd_par = (m2_conv[l], m2_conv_b[l], m2_a_log[l], m2_dt_bias[l], m2_d[l], m2_norm_g[l])
        o_f, xbc = _ssd_dir(p, None, *ssd_par, None, rev=False, nblk=nblk)
        yd, = _ssd_dir(p, xbc, *ssd_par, o_f, rev=True, nblk=nblk)

        ys = (flat(ya), flat(yb), flat(yc), flat(yd))
        merged = _merge(h, w_gate, b_merge[l], ys, wb_branch, l, tm=tm, tn=512)
        xs, h2 = _outproj_norm(merged, wb_out, l, xs, g1, norm2_g[l], sc2, sh2,
                               tm=_row_tile(s1, 384), **geo)

        u = _ffn_up(h2, wb_ffn1, wb_ffn3, l, tm=_row_tile(s1, 2112), tn=512)
        xs = _matmul_residual(u, wb_ffn2, l, xs, g2, tm=tm, tn=512, name="ffn_down", **geo)

    return _final_norm(xs, final_norm_g, nbat=nbat, nblk=nblk, lc=lc)
```

```python
import functools

import numpy as np
import jax
import jax.numpy as jnp
from jax import lax
from jax.experimental import pallas as pl
from jax.experimental.pallas import tpu as pltpu

D_MODEL = 2048
GRID_W = 64
N_BRANCH = 4
BRANCH_W = D_MODEL // 4
CHUNK = 64
CONV_W = 5
RMS_EPS = 1e-6
NEG_INF = -1e30
ROPE_BASE = 10000.0
GLA_H = 4
GLA_DV = BRANCH_W // GLA_H
GLA_DK = GLA_DV // 2
GLA_LR = 16
GLA_TAU = 16.0
NA_H = 4
NA_D = BRANCH_W // NA_H
NA_WIN_R = 8
NA_WIN_C = 16
GDN_H = 4
GDN_D = BRANCH_W // GDN_H
M2_P = 64
M2_H = BRANCH_W // M2_P
M2_N = 128
M2_G = 2
M2_CONV_CH = BRANCH_W + 2 * M2_G * M2_N
D_FF = ((8 * D_MODEL + 3 * 256 - 1) // (3 * 256)) * 256
GLA_IN = 2 * GLA_H * GLA_DK + 2 * BRANCH_W + 2 * GLA_LR
NA_IN = 3 * BRANCH_W
GDN_IN = 4 * BRANCH_W + 4 * GDN_H
M2_IN = BRANCH_W + M2_CONV_CH + 2 * M2_H
MIX_IN = GLA_IN + NA_IN + GDN_IN + M2_IN

F32 = jnp.float32
BF16 = jnp.bfloat16

LANE = 128
SUBLANE = 8
V7X_VMEM_BYTES = 64 * 1024 * 1024
VMEM_LIMIT = V7X_VMEM_BYTES - 8 * 1024 * 1024

TOK_BLK = 4 * CHUNK
ROWS_PER_BLK = TOK_BLK // GRID_W
HALO = SUBLANE
MOD_ROWS = 8

P_GDN_QKV = 0
P_GLA_V = 1536
P_M2_XBC = 2048
P_GLA_G = 3072
P_NA_Q, P_NA_K, P_NA_V = 3584, 4096, 4608
P_GDN_Z, P_M2_Z = 5120, 5632
P_GLA_Q, P_GLA_K = 6144, 6400
P_GLA_LR, P_GDN_SC, P_M2_DT = 6656, 6784, 6912
P_COLS = 7168
MIX_TILE = 1024


def _mix_fields():
    gla, na, gdn, m2 = 0, GLA_IN, GLA_IN + NA_IN, GLA_IN + NA_IN + GDN_IN
    return sorted([
        (P_GLA_V, gla + 512, 512), (P_GLA_G, gla + 1024, 512),
        (P_NA_Q, na, 512), (P_NA_K, na + 512, 512), (P_NA_V, na + 1024, 512),
        (P_GDN_QKV, gdn, 1536), (P_GDN_Z, gdn + 1536, 512),
        (P_M2_Z, m2, 512), (P_M2_XBC, m2 + 512, M2_CONV_CH),
        (P_GLA_Q, gla, 256), (P_GLA_K, gla + 256, 256),
        (P_GLA_LR, gla + 1536, 2 * GLA_LR), (P_GDN_SC, gdn + 2048, 4 * GDN_H),
        (P_M2_DT, m2 + 512 + M2_CONV_CH, 2 * M2_H)])


def _cparams(sem):
    return pltpu.CompilerParams(dimension_semantics=sem, vmem_limit_bytes=VMEM_LIMIT)


def _wprep_kernel(w_ref, mix_ref, gate_ref):
    col = 0
    for dst, src, n in _mix_fields():
        if dst > col:
            mix_ref[:, col:dst] = jnp.zeros((mix_ref.shape[0], dst - col), mix_ref.dtype)
        mix_ref[:, dst:dst + n] = w_ref[:, src:src + n].astype(mix_ref.dtype)
        col = dst + n
    mix_ref[:, col:] = jnp.zeros((mix_ref.shape[0], P_COLS - col), mix_ref.dtype)
    gate_ref[...] = w_ref[:, MIX_IN:].astype(gate_ref.dtype)


def _split_in_weights(w_in, layer):
    _, D, N = w_in.shape
    rb = 256
    return pl.pallas_call(
        _wprep_kernel,
        grid=(D // rb,),
        in_specs=[pl.BlockSpec((None, rb, N), lambda r: (layer, r, 0))],
        out_specs=[pl.BlockSpec((rb, P_COLS), lambda r: (r, 0)),
                   pl.BlockSpec((rb, N - MIX_IN), lambda r: (r, 0))],
        out_shape=[jax.ShapeDtypeStruct((D, P_COLS), BF16), jax.ShapeDtypeStruct((D, N - MIX_IN), BF16)],
        compiler_params=_cparams(("arbitrary",)),
        name="split_in_weights",
    )(w_in)


def _sigmoid(x):
    return 1.0 / (1.0 + jnp.exp(-x))


def _silu(x):
    return x * _sigmoid(x)


def _softplus(x):
    return jnp.maximum(x, 0.0) + jnp.log1p(jnp.exp(-jnp.abs(x)))


def _dot(a, b):
    return jnp.dot(a.astype(BF16), b.astype(BF16), preferred_element_type=F32)


def _dot_nt(a, b):
    return lax.dot_general(a.astype(BF16), b.astype(BF16), (((1,), (1,)), ((), ())),
                           preferred_element_type=F32)


def _dot_tn(a, b):
    return lax.dot_general(a.astype(BF16), b.astype(BF16), (((0,), (0,)), ((), ())),
                           preferred_element_type=F32)


def _split3(a):
    hi = a.astype(BF16)
    r1 = a - hi.astype(F32)
    mid = r1.astype(BF16)
    return hi, mid, (r1 - mid.astype(F32)).astype(BF16)


def _select_rows(m01, a):
    d = lambda y: jnp.dot(m01, y, preferred_element_type=F32)
    hi, mid, lo = _split3(a)
    return d(hi) + (d(mid) + d(lo))


def _select_cols(a, m01):
    d = lambda y: jnp.dot(y, m01, preferred_element_type=F32)
    hi, mid, lo = _split3(a)
    return d(hi) + (d(mid) + d(lo))


def _split(a):
    hi = a.astype(BF16)
    return hi, (a - hi.astype(F32)).astype(BF16)


def _dot_split(a, b):
    (ah, al), (bh, bl) = a, b
    d = lambda x, y: jnp.dot(x, y, preferred_element_type=F32)
    return d(ah, bh) + (d(ah, bl) + d(al, bh))


def _iota2(shape, dim):
    return lax.broadcasted_iota(jnp.int32, shape, dim)


def _idiv(x, n):
    assert n & (n - 1) == 0
    return lax.shift_right_logical(x, jnp.int32(n.bit_length() - 1))


def _imod(x, n):
    assert n & (n - 1) == 0
    return x & (n - 1)


def _chunk_mask(rev, strict=False):
    r = _iota2((TOK_BLK, TOK_BLK), 0)
    c = _iota2((TOK_BLK, TOK_BLK), 1)
    same = _idiv(r, CHUNK) == _idiv(c, CHUNK)
    if rev:
        tri = (c > r) if strict else (c >= r)
    else:
        tri = (c < r) if strict else (c <= r)
    return same & tri


def _chunk_order(rev):
    n = TOK_BLK // CHUNK
    return list(range(n - 1, -1, -1)) if rev else list(range(n))


def _row_split(i, tm, s1):
    r0 = i * tm
    return lax.div(r0, jnp.int32(s1)), lax.rem(r0, jnp.int32(s1))


def _pick_mod(ref, b, nb, is_ctx):
    return jnp.where(is_ctx, ref[nb:nb + 1, :], ref[pl.ds(b, 1), :])


def _ada_kernel(c_ref, w_ref, b_ref, o_ref):
    a = _silu(c_ref[...])
    o_ref[...] = _dot(a, w_ref[...]) + b_ref[...]


def _ada(cvec, w_ada, b_ada):
    L, D, N = w_ada.shape
    tn = 1024
    return pl.pallas_call(
        _ada_kernel,
        grid=(L, N // tn),
        in_specs=[pl.BlockSpec((MOD_ROWS, D), lambda l, j: (0, 0)),
                  pl.BlockSpec((None, D, tn), lambda l, j: (l, 0, j)),
                  pl.BlockSpec((None, 1, tn), lambda l, j: (l, 0, j))],
        out_specs=pl.BlockSpec((None, MOD_ROWS, tn), lambda l, j: (l, 0, j)),
        out_shape=jax.ShapeDtypeStruct((L, MOD_ROWS, N), F32),
        compiler_params=_cparams(("arbitrary", "arbitrary")),
        name="ada_mod",
    )(cvec, w_ada, b_ada.reshape(L, 1, N))


def _normmod_kernel(x_ref, g_ref, sc_ref, sh_ref, o_ref, *, nb):
    row = jnp.where(pl.program_id(1) == 0, nb, pl.program_id(0))
    x = x_ref[...]
    y = x * lax.rsqrt(jnp.mean(x * x, axis=-1, keepdims=True) + RMS_EPS) * g_ref[...]
    o_ref[...] = (y * (1.0 + sc_ref[pl.ds(row, 1), :]) + sh_ref[pl.ds(row, 1), :]).astype(o_ref.dtype)


def _normmod(xs, g, sc, sh, *, nbat, nblk):
    R, D = xs.shape
    const = lambda b, s: (0, 0)
    row = lambda b, s: (b * nblk + s, 0)
    return pl.pallas_call(
        functools.partial(_normmod_kernel, nb=nbat),
        grid=(nbat, nblk),
        in_specs=[pl.BlockSpec((TOK_BLK, D), row), pl.BlockSpec((1, D), const),
                  pl.BlockSpec((MOD_ROWS, D), const), pl.BlockSpec((MOD_ROWS, D), const)],
        out_specs=pl.BlockSpec((TOK_BLK, D), row),
        out_shape=jax.ShapeDtypeStruct((R, D), BF16),
        compiler_params=_cparams(("arbitrary", "arbitrary")),
        name="norm_mod",
    )(xs, g.reshape(1, D), sc, sh)


def _concat_norm_kernel(ctx_ref, x_ref, g_ref, sc_ref, sh_ref, xs_ref, h_ref, *, nb):
    b, s = pl.program_id(0), pl.program_id(1)
    is_ctx = s == 0
    x = jnp.where(is_ctx, ctx_ref[...], x_ref[...])
    xs_ref[...] = x
    row = jnp.where(is_ctx, nb, b)
    y = x * lax.rsqrt(jnp.mean(x * x, axis=-1, keepdims=True) + RMS_EPS) * g_ref[...]
    h_ref[...] = (y * (1.0 + sc_ref[pl.ds(row, 1), :]) + sh_ref[pl.ds(row, 1), :]).astype(h_ref.dtype)


def _concat_norm(ctx, x, g, sc, sh, *, nblk):
    nbat, t, D = x.shape
    R = nbat * nblk * TOK_BLK
    const = lambda b, s: (0, 0)
    row = lambda b, s: (b * nblk + s, 0)
    return pl.pallas_call(
        functools.partial(_concat_norm_kernel, nb=nbat),
        grid=(nbat, nblk),
        in_specs=[pl.BlockSpec((None, TOK_BLK, D), lambda b, s: (b, 0, 0)),
                  pl.BlockSpec((None, TOK_BLK, D), lambda b, s: (b, jnp.maximum(s - 1, 0), 0)),
                  pl.BlockSpec((1, D), const), pl.BlockSpec((MOD_ROWS, D), const), pl.BlockSpec((MOD_ROWS, D), const)],
        out_specs=[pl.BlockSpec((TOK_BLK, D), row), pl.BlockSpec((TOK_BLK, D), row)],
        out_shape=[jax.ShapeDtypeStruct((R, D), F32), jax.ShapeDtypeStruct((R, D), BF16)],
        compiler_params=_cparams(("arbitrary", "arbitrary")),
        name="concat_norm",
    )(ctx, x, g.reshape(1, D), sc, sh)


def _mm_kernel(a_ref, w_ref, o_ref):
    o_ref[...] = jnp.dot(a_ref[...], w_ref[...], preferred_element_type=F32).astype(o_ref.dtype)


def _matmul(a, w, *, tm, tn, out_dtype):
    R, K = a.shape
    N = w.shape[1]
    assert R % tm == 0 and N % tn == 0
    return pl.pallas_call(
        _mm_kernel,
        grid=(R // tm, N // tn),
        in_specs=[pl.BlockSpec((tm, K), lambda i, j: (i, 0)),
                  pl.BlockSpec((K, tn), lambda i, j: (0, j))],
        out_specs=pl.BlockSpec((tm, tn), lambda i, j: (i, j)),
        out_shape=jax.ShapeDtypeStruct((R, N), out_dtype),
        compiler_params=_cparams(("arbitrary", "arbitrary")),
        name="mix_proj",
    )(a, w)


def _mm_res_kernel(a_ref, w_ref, x_ref, gt_ref, o_ref, *, tm, s1, lc, nb):
    b, pos0 = _row_split(pl.program_id(1), tm, s1)
    is_ctx = (pos0 + _iota2((tm, 1), 0)) < lc
    acc = jnp.dot(a_ref[...], w_ref[...], preferred_element_type=F32)
    o_ref[...] = x_ref[...] + _pick_mod(gt_ref, b, nb, is_ctx) * acc


def _matmul_residual(a, w, layer, xs, gate, *, tm, tn, s1, lc, nb, name):
    R, K = a.shape
    N = w.shape[2]
    assert R % tm == 0 and N % tn == 0
    return pl.pallas_call(
        functools.partial(_mm_res_kernel, tm=tm, s1=s1, lc=lc, nb=nb),
        grid=(N // tn, R // tm),
        in_specs=[pl.BlockSpec((tm, K), lambda j, i: (i, 0)),
                  pl.BlockSpec((None, K, tn), lambda j, i: (layer, 0, j)),
                  pl.BlockSpec((tm, tn), lambda j, i: (i, j)),
                  pl.BlockSpec((MOD_ROWS, tn), lambda j, i: (0, j))],
        out_specs=pl.BlockSpec((tm, tn), lambda j, i: (i, j)),
        out_shape=jax.ShapeDtypeStruct((R, N), F32),
        compiler_params=_cparams(("arbitrary", "arbitrary")),
        name=name,
    )(a, w, xs, gate)


def _outproj_norm_kernel(a_ref, w_ref, x_ref, gt_ref, g_ref, sc_ref, sh_ref, xo_ref, h_ref, *, tm, s1, lc, nb):
    b, pos0 = _row_split(pl.program_id(0), tm, s1)
    is_ctx = (pos0 + _iota2((tm, 1), 0)) < lc
    acc = jnp.dot(a_ref[...], w_ref[...], preferred_element_type=F32)
    x = x_ref[...] + _pick_mod(gt_ref, b, nb, is_ctx) * acc
    xo_ref[...] = x
    y = x * lax.rsqrt(jnp.mean(x * x, axis=-1, keepdims=True) + RMS_EPS) * g_ref[...]
    h_ref[...] = (y * (1.0 + _pick_mod(sc_ref, b, nb, is_ctx)) + _pick_mod(sh_ref, b, nb, is_ctx)).astype(h_ref.dtype)


def _outproj_norm(a, w, layer, xs, gate, g, sc, sh, *, tm, s1, lc, nb):
    R, K = a.shape
    D = w.shape[2]
    row = lambda i: (i, 0)
    const = lambda i: (0, 0)
    return pl.pallas_call(
        functools.partial(_outproj_norm_kernel, tm=tm, s1=s1, lc=lc, nb=nb),
        grid=(R // tm,),
        in_specs=[pl.BlockSpec((tm, K), row),
                  pl.BlockSpec((None, K, D), lambda i: (layer, 0, 0), pipeline_mode=pl.Buffered(1)),
                  pl.BlockSpec((tm, D), row),
                  pl.BlockSpec((MOD_ROWS, D), const), pl.BlockSpec((1, D), const),
                  pl.BlockSpec((MOD_ROWS, D), const), pl.BlockSpec((MOD_ROWS, D), const)],
        out_specs=[pl.BlockSpec((tm, D), row), pl.BlockSpec((tm, D), row)],
        out_shape=[jax.ShapeDtypeStruct((R, D), F32), jax.ShapeDtypeStruct((R, D), BF16)],
        compiler_params=_cparams(("arbitrary",)),
        name="out_proj",
    )(a, w, xs, gate, g.reshape(1, D), sc, sh)


def _ffn_up_kernel(a_ref, w1_ref, w3_ref, o_ref):
    a = a_ref[...]
    u = jnp.dot(a, w1_ref[...], preferred_element_type=F32)
    v = jnp.dot(a, w3_ref[...], preferred_element_type=F32)
    o_ref[...] = (_silu(u) * v).astype(o_ref.dtype)


def _ffn_up(a, w1, w3, layer, *, tm, tn):
    R, K = a.shape
    N = w1.shape[2]
    assert R % tm == 0 and N % tn == 0
    wspec = pl.BlockSpec((None, K, tn), lambda i, j: (layer, 0, j))
    return pl.pallas_call(
        _ffn_up_kernel,
        grid=(R // tm, N // tn),
        in_specs=[pl.BlockSpec((tm, K), lambda i, j: (i, 0)), wspec, wspec],
        out_specs=pl.BlockSpec((tm, tn), lambda i, j: (i, j)),
        out_shape=jax.ShapeDtypeStruct((R, N), BF16),
        compiler_params=_cparams(("arbitrary", "arbitrary")),
        name="ffn_up",
    )(a, w1, w3)


def _merge_kernel(h_ref, g0_ref, g1_ref, g2_ref, g3_ref, bm_ref, ya_ref, yb_ref, yc_ref, yd_ref, wb_ref, o_ref):
    h = h_ref[...]
    acc = None
    branches = zip((g0_ref, g1_ref, g2_ref, g3_ref), (ya_ref, yb_ref, yc_ref, yd_ref))
    for i, (wg_ref, y_ref) in enumerate(branches):
        gate = _sigmoid(jnp.dot(h, wg_ref[...], preferred_element_type=F32) + bm_ref[i])
        term = gate * jnp.dot(y_ref[...], wb_ref[i], preferred_element_type=F32)
        acc = term if acc is None else acc + term
    o_ref[...] = acc.astype(o_ref.dtype)


def _merge(h, wg, b_merge, ys, wb, layer, *, tm, tn):
    R, D = h.shape
    N = wb.shape[3]
    assert R % tm == 0 and N % tn == 0
    nj = N // tn
    yspec = pl.BlockSpec((tm, BRANCH_W), lambda i, j: (i, 0))
    gspec = lambda br: pl.BlockSpec((D, tn), lambda i, j: (0, br * nj + j))
    return pl.pallas_call(
        _merge_kernel,
        grid=(R // tm, nj),
        in_specs=[pl.BlockSpec((tm, D), lambda i, j: (i, 0)),
                  gspec(0), gspec(1), gspec(2), gspec(3),
                  pl.BlockSpec((N_BRANCH, 1, tn), lambda i, j: (0, 0, j)),
                  yspec, yspec, yspec, yspec,
                  pl.BlockSpec((None, N_BRANCH, BRANCH_W, tn), lambda i, j: (layer, 0, 0, j))],
        out_specs=pl.BlockSpec((tm, tn), lambda i, j: (i, j)),
        out_shape=jax.ShapeDtypeStruct((R, N), BF16),
        compiler_params=_cparams(("arbitrary", "arbitrary")),
        name="merge",
    )(h, wg, wg, wg, wg, b_merge.reshape(N_BRANCH, 1, N), *ys, wb)


def _conv_silu_block(xm, xp, xn, w, bias, blk, nblk):
    prev_ok = blk >= 2
    next_ok = (blk >= 1) & (blk < nblk - 1)
    rows = TOK_BLK + 2 * HALO
    half = CONV_W // 2
    xpad = jnp.concatenate([jnp.where(prev_ok, xp, 0.0), xm, jnp.where(next_ok, xn, 0.0)], axis=0)
    acc = bias + w[half:half + 1, :] * xm
    for j in range(CONV_W):
        if j != half:
            tap = pltpu.roll(xpad, (half - j) % rows, 0)[HALO:HALO + TOK_BLK]
            acc = acc + w[j:j + 1, :] * tap
    return _silu(acc)


def _conv_inputs(p, col0, w, bias, nblk):
    nbat, s1, _ = p.shape
    C = w.shape[1]
    assert col0 % C == 0
    cb = col0 // C
    hb = TOK_BLK // HALO
    nhalo = s1 // HALO
    wpad = jnp.concatenate([w, jnp.zeros((SUBLANE - CONV_W, C), F32)], axis=0)
    specs = [pl.BlockSpec((nbat, TOK_BLK, C), lambda r: (0, r, cb)),
             pl.BlockSpec((nbat, HALO, C), lambda r: (0, jnp.maximum(r * hb - 1, 0), cb)),
             pl.BlockSpec((nbat, HALO, C), lambda r: (0, jnp.minimum((r + 1) * hb, nhalo - 1), cb)),
             _const_spec((SUBLANE, C)), _const_spec((1, C))]
    return specs, [p, p, p, wpad, bias.reshape(1, C)]


def _scan_block(rev, nblk):
    if rev:
        return lambda s: jnp.where(s == 0, 0, nblk - s)
    return lambda s: s


def _blk_spec(nbat, width, col, blk):
    assert col % width == 0
    return pl.BlockSpec((nbat, TOK_BLK, width), lambda s: (0, blk(s), col // width))


def _const_spec(shape):
    return pl.BlockSpec(shape, lambda s: (0,) * len(shape))


def _head_rmsnorm_gate(o, g, gate, n_head, width):
    outs = []
    for h in range(n_head):
        oh = o[:, h * width:(h + 1) * width]
        yh = oh * lax.rsqrt(jnp.mean(oh * oh, axis=-1, keepdims=True) + RMS_EPS) * g
        outs.append(yh * _silu(gate[:, h * width:(h + 1) * width]))
    return jnp.concatenate(outs, axis=1)


def _scan_call(kern, name, in_specs, args, o_f, *, rev, nbat, s1, nblk, state_shape, conv_width=0):
    blk = _scan_block(rev, nblk)
    ospec = pl.BlockSpec((nbat, TOK_BLK, BRANCH_W), lambda s: (0, blk(s), 0))
    out_specs = [ospec]
    out_shape = [jax.ShapeDtypeStruct((nbat, s1, BRANCH_W), BF16 if rev else F32)]
    if rev:
        in_specs = in_specs + [ospec]
        args = args + [o_f]
    elif conv_width:
        out_specs.append(pl.BlockSpec((nbat, TOK_BLK, conv_width), lambda s: (0, s, 0)))
        out_shape.append(jax.ShapeDtypeStruct((nbat, s1, conv_width), F32))
    return pl.pallas_call(
        functools.partial(kern, rev=rev, nbat=nbat, nblk=nblk),
        grid=(nblk,),
        in_specs=in_specs,
        out_specs=out_specs,
        out_shape=out_shape,
        scratch_shapes=[pltpu.VMEM((nbat,) + state_shape, F32)],
        compiler_params=_cparams(("arbitrary",)),
        name=name + ("_bwd" if rev else "_fwd"),
    )(*args)


def _gla_kernel(q_ref, k_ref, v_ref, g_ref, lr_ref, cos_ref, sin_ref, a2_ref, ab_ref, lm_ref, ng_ref,
                *rest, rev, nbat, nblk):
    if rev:
        of_ref, y_ref, st_ref = rest
    else:
        o_ref, st_ref = rest

    @pl.when(pl.program_id(0) == 0)
    def _():
        st_ref[...] = jnp.zeros_like(st_ref)

    hk = GLA_H * GLA_DK
    lane = _iota2((TOK_BLK, hk), 1)
    first_half = _imod(lane, GLA_DK) < (GLA_DK // 2)
    head_of_lane = _idiv(lane, GLA_DK)
    cos = cos_ref[...]
    sin = sin_ref[...]
    mask = _chunk_mask(rev)
    n_chunk = TOK_BLK // CHUNK
    batch = range(nbat)

    def rope(x):
        partner = jnp.where(first_half, pltpu.roll(x, hk - GLA_DK // 2, 1), pltpu.roll(x, GLA_DK // 2, 1))
        return x * cos + partner * sin

    ks, vs, bs, q_ins, o_intras = [], [], [], [], []
    for bi in batch:
        q = rope(q_ref[bi]) * GLA_DK ** -0.5
        k = rope(k_ref[bi])
        v = v_ref[bi]
        loga = -_softplus(-(_dot(lr_ref[bi], a2_ref[...]) + ab_ref[...])) / GLA_TAU
        b = _select_rows(lm_ref[...], loga)
        q_in = q * jnp.exp(b)
        k_in = k * jnp.exp(-b)
        o_heads = []
        for h in range(GLA_H):
            att = _dot_nt(jnp.where(head_of_lane == h, q_in, 0.0), k_in)
            att = jnp.where(mask, att, 0.0)
            o_heads.append(_dot(att, v[:, h * GLA_DV:(h + 1) * GLA_DV]))
        ks.append(k)
        vs.append(v)
        bs.append(b)
        q_ins.append(q_in)
        o_intras.append(jnp.concatenate(o_heads, axis=1))

    sts = [st_ref[bi] for bi in batch]
    diag = _idiv(_iota2(sts[0].shape, 0), GLA_DV) == _idiv(_iota2(sts[0].shape, 1), GLA_DK)
    o_inter = [[None] * n_chunk for _ in batch]
    for c in _chunk_order(rev):
        rows = slice(c * CHUNK, (c + 1) * CHUNK)
        for bi in batch:
            b_c = bs[bi][rows]
            b_last = b_c[0:1] if rev else b_c[CHUNK - 1:CHUNK]
            o_inter[bi][c] = _dot_nt(q_ins[bi][rows], sts[bi])
            k_end = ks[bi][rows] * jnp.exp(b_last - b_c)
            ds = _dot_tn(vs[bi][rows], k_end)
            sts[bi] = sts[bi] * jnp.exp(b_last) + jnp.where(diag, ds, 0.0)
    for bi in batch:
        st_ref[bi] = sts[bi]
        o = o_intras[bi] + jnp.concatenate(o_inter[bi], axis=0)
        if rev:
            y = _head_rmsnorm_gate(of_ref[bi] + o, ng_ref[...], g_ref[bi], GLA_H, GLA_DV)
            y_ref[bi] = y.astype(y_ref.dtype)
        else:
            o_ref[bi] = o


def _cumsum_matrix(rev):
    r = np.arange(TOK_BLK)[:, None]
    c = np.arange(TOK_BLK)[None, :]
    same = (r // CHUNK) == (c // CHUNK)
    tri = (c >= r) if rev else (c <= r)
    return jnp.asarray((same & tri).astype(np.float32), dtype=BF16)


def _gla_dir(p, cos, sin, a2, ab, ng, o_f, *, rev, nblk):
    nbat, s1, _ = p.shape
    blk = _scan_block(rev, nblk)
    d = 1 if rev else 0
    a2d = jnp.zeros((LANE, GLA_H * GLA_DK), F32).at[d * GLA_LR:(d + 1) * GLA_LR].set(a2[d])
    tab = pl.BlockSpec((TOK_BLK, GLA_H * GLA_DK), lambda s: (blk(s), 0))
    in_specs = [_blk_spec(nbat, 256, P_GLA_Q, blk), _blk_spec(nbat, 256, P_GLA_K, blk),
                _blk_spec(nbat, 512, P_GLA_V, blk), _blk_spec(nbat, 512, P_GLA_G, blk),
                _blk_spec(nbat, LANE, P_GLA_LR, blk), tab, tab,
                _const_spec((LANE, 256)), _const_spec((1, 256)), _const_spec((TOK_BLK, TOK_BLK)),
                _const_spec((1, GLA_DV))]
    args = [p, p, p, p, p, cos, sin, a2d, ab[d].reshape(1, -1), _cumsum_matrix(rev), ng.reshape(1, -1)]
    return _scan_call(_gla_kernel, "gla", in_specs, args, o_f, rev=rev, nbat=nbat, s1=s1, nblk=nblk,
                      state_shape=(GLA_H * GLA_DV, GLA_H * GLA_DK))[0]


def _rope_tables(lc, t):
    n_freq = GLA_DK // 4
    freqs = ROPE_BASE ** (-jnp.arange(n_freq, dtype=F32) / n_freq)
    tt = jnp.arange(t)
    row = (tt // GRID_W).astype(F32)
    col = (tt % GRID_W).astype(F32)
    ang = jnp.concatenate([row[:, None] * freqs, col[:, None] * freqs], axis=-1)
    cos, sin = jnp.cos(ang), jnp.sin(ang)
    cos = jnp.concatenate([jnp.ones((lc, GLA_DK // 2), F32), cos], axis=0)
    sin = jnp.concatenate([jnp.zeros((lc, GLA_DK // 2), F32), sin], axis=0)
    cos_h = jnp.concatenate([cos, cos], axis=1)
    sin_h = jnp.concatenate([-sin, sin], axis=1)
    return jnp.tile(cos_h, (1, GLA_H)), jnp.tile(sin_h, (1, GLA_H))


def _na_kernel(q_ref, kp_ref, kc_ref, kn_ref, kx_ref, vp_ref, vc_ref, vn_ref, vx_ref, bias_ref, y_ref, *, nbat):
    scale = NA_D ** -0.5
    for bi in range(nbat):
        outs = []
        for h in range(NA_H):
            hs = slice(h * NA_D, (h + 1) * NA_D)
            qh = (q_ref[bi, :, hs] * scale).astype(BF16)
            s = jnp.concatenate(
                [_dot_nt(qh, kp_ref[bi, :, hs]) + bias_ref[0, h],
                 _dot_nt(qh, kc_ref[bi, :, hs]) + bias_ref[1, h],
                 _dot_nt(qh, kn_ref[bi, :, hs]) + bias_ref[2, h],
                 _dot_nt(qh, kx_ref[bi, :, hs])], axis=1)
            m = jnp.max(s, axis=-1, keepdims=True)
            e = jnp.exp(s - m)
            o = (_dot(e[:, 0:TOK_BLK], vp_ref[bi, :, hs]) + _dot(e[:, TOK_BLK:2 * TOK_BLK], vc_ref[bi, :, hs])
                 + _dot(e[:, 2 * TOK_BLK:3 * TOK_BLK], vn_ref[bi, :, hs])
                 + _dot(e[:, 3 * TOK_BLK:], vx_ref[bi, :, hs]))
            outs.append(o / jnp.sum(e, axis=-1, keepdims=True))
        y_ref[bi] = jnp.concatenate(outs, axis=1).astype(y_ref.dtype)


def _na_bias_tiles(rpb):
    edge = GRID_W - NA_WIN_C
    ext = jnp.concatenate([jnp.repeat(rpb[..., :1], edge, axis=-1), rpb,
                           jnp.repeat(rpb[..., -1:], edge, axis=-1)], axis=-1)
    toep = jnp.stack([ext[..., GRID_W - 1 - qc:2 * GRID_W - 1 - qc] for qc in range(GRID_W)], axis=-2)
    qc = np.arange(GRID_W)[:, None]
    kc = np.arange(GRID_W)[None, :]
    c0 = np.clip(qc - NA_WIN_C // 2, 0, GRID_W - NA_WIN_C)
    col_ok = (kc >= c0) & (kc < c0 + NA_WIN_C)
    toep = jnp.where(jnp.asarray(col_ok), toep, NEG_INF).astype(F32)
    masked = jnp.full((NA_H, GRID_W, GRID_W), NEG_INF, F32)
    kinds = []
    for kind in range(3):
        offs = []
        for off in (-1, 0, 1):
            rows = []
            for qr in range(ROWS_PER_BLK):
                start = (0, qr - NA_WIN_R // 2, ROWS_PER_BLK - NA_WIN_R)[kind]
                cols = []
                for kb in range(ROWS_PER_BLK):
                    kr = kb + ROWS_PER_BLK * off
                    ok = start <= kr < start + NA_WIN_R
                    cols.append(toep[:, kr - qr + NA_WIN_R - 1] if ok else masked)
                rows.append(jnp.concatenate(cols, axis=-1))
            offs.append(jnp.concatenate(rows, axis=-2))
        kinds.append(jnp.stack(offs))
    kinds.append(jnp.full_like(kinds[0], NEG_INF))
    return jnp.stack(kinds)


def _na(p, bias, *, nblk):
    nbat, s1, _ = p.shape

    def kind(s):
        return jnp.where(s == 0, 3, jnp.where(s == 1, 0, jnp.where(s == nblk - 1, 2, 1)))

    prev = lambda s: jnp.maximum(s - 1, 1)
    cur = lambda s: s
    nxt = lambda s: jnp.minimum(s + 1, nblk - 1)
    ctx = lambda s: 0
    spec = lambda col, blk: _blk_spec(nbat, BRANCH_W, col, blk)
    return pl.pallas_call(
        functools.partial(_na_kernel, nbat=nbat),
        grid=(nblk,),
        in_specs=[spec(P_NA_Q, cur),
                  spec(P_NA_K, prev), spec(P_NA_K, cur), spec(P_NA_K, nxt), spec(P_NA_K, ctx),
                  spec(P_NA_V, prev), spec(P_NA_V, cur), spec(P_NA_V, nxt), spec(P_NA_V, ctx),
                  pl.BlockSpec((None, 3, NA_H, TOK_BLK, TOK_BLK), lambda s: (kind(s), 0, 0, 0, 0))],
        out_specs=pl.BlockSpec((nbat, TOK_BLK, BRANCH_W), lambda s: (0, s, 0)),
        out_shape=jax.ShapeDtypeStruct((nbat, s1, BRANCH_W), BF16),
        compiler_params=_cparams(("arbitrary",)),
        name="nbr_attn",
    )(p, p, p, p, p, p, p, p, p, bias)


def _l2norm(x):
    return x * lax.rsqrt(jnp.sum(x * x, axis=-1, keepdims=True) + RMS_EPS)


def _gdn_kernel(*refs, rev, nbat, nblk):
    if rev:
        qkv_ref, z_ref, sc_ref, alog_ref, dtb_ref, lm_ref, ng_ref, of_ref, y_ref, st_ref = refs
        qkv = [qkv_ref[bi] for bi in range(nbat)]
    else:
        (xm_ref, xp_ref, xn_ref, cw_ref, cb_ref, z_ref, sc_ref, alog_ref, dtb_ref, lm_ref, ng_ref,
         o_ref, qkvc_ref, st_ref) = refs
        qkv = []
        for bi in range(nbat):
            qkv.append(_conv_silu_block(xm_ref[bi], xp_ref[bi], xn_ref[bi], cw_ref[...], cb_ref[...],
                                        pl.program_id(0), nblk))
            qkvc_ref[bi] = qkv[bi]

    @pl.when(pl.program_id(0) == 0)
    def _():
        st_ref[...] = jnp.zeros_like(st_ref)

    d = 1 if rev else 0
    m_incl = _chunk_mask(rev)
    m_strict = _chunk_mask(rev, strict=True)
    eye = (_iota2((TOK_BLK, TOK_BLK), 0) == _iota2((TOK_BLK, TOK_BLK), 1)).astype(F32)
    n_chunk = TOK_BLK // CHUNK

    units = [(bi, h) for bi in range(nbat) for h in range(GDN_H)]
    qs, ks, bcols, ebcs, attns, nmats, rhss = [], [], [], [], [], [], []
    for bi in range(nbat):
        sc = sc_ref[bi]
        beta_all = _sigmoid(sc)
        g_all = -jnp.exp(alog_ref[...]) * _softplus(sc + dtb_ref[...])
        b_all = _select_rows(lm_ref[...], g_all)
        b_all_t = b_all.T
        eb_all = jnp.exp(b_all)
        for h in range(GDN_H):
            head = lambda part: qkv[bi][:, part * BRANCH_W + h * GDN_D:part * BRANCH_W + (h + 1) * GDN_D]
            qh = _l2norm(head(0)) * GDN_D ** -0.5
            kh = _l2norm(head(1))
            vh = head(2)
            lb, lg = GDN_H * d + h, 2 * GDN_H + GDN_H * d + h
            beta = beta_all[:, lb:lb + 1]
            bcol = b_all[:, lg:lg + 1]
            brow = b_all_t[lg:lg + 1, :]
            ebc = eb_all[:, lg:lg + 1]
            diff = bcol - brow
            dec_incl = jnp.where(m_incl, jnp.exp(jnp.where(m_incl, diff, 0.0)), 0.0)
            dec_strict = jnp.where(m_strict, dec_incl, 0.0)
            kk = _dot_nt(kh, kh)
            qs.append(qh)
            ks.append(kh)
            bcols.append(bcol)
            ebcs.append(ebc)
            attns.append(_dot_nt(qh, kh) * dec_incl)
            nmats.append(-(beta * kk * dec_strict))
            rhss.append(_split(jnp.concatenate([kh * (beta * ebc), vh * beta], axis=1)))

    bdot = lambda x, y: jnp.dot(x, y, preferred_element_type=F32)
    nsplit = [_split(n) for n in nmats]
    t0s = [eye + n for n in nmats]
    powers = [ns[0] for ns in nsplit]
    for _ in range(5):
        powers = [bdot(m, m).astype(BF16) for m in powers]
        t0s = [t + bdot(t.astype(BF16), m) for t, m in zip(t0s, powers)]
    t0split = [_split(t) for t in t0s]
    resid = [(eye - t) + _dot_split(ns, ts) for t, ns, ts in zip(t0s, nsplit, t0split)]
    tinvs = [t + bdot(ts[0], r.astype(BF16)) for t, ts, r in zip(t0s, t0split, resid)]
    sols = [_dot_split(_split(t), r) for t, r in zip(tinvs, rhss)]
    ws = [s[:, :GDN_D] for s in sols]
    u0s = [s[:, GDN_D:] for s in sols]

    sts = [st_ref[bi, h] for bi, h in units]
    u_parts = [[None] * n_chunk for _ in units]
    o_parts = [[None] * n_chunk for _ in units]
    for c in _chunk_order(rev):
        rows = slice(c * CHUNK, (c + 1) * CHUNK)
        last = c * CHUNK if rev else (c + 1) * CHUNK - 1
        for i in range(len(units)):
            st = sts[i]
            b_last = bcols[i][last:last + 1]
            u = u0s[i][rows] - _dot(ws[i][rows], st)
            u_parts[i][c] = u
            o_parts[i][c] = ebcs[i][rows] * _dot(qs[i][rows], st)
            k_end = ks[i][rows] * jnp.exp(b_last - bcols[i][rows])
            sts[i] = jnp.exp(b_last) * st + _dot_tn(k_end, u)
    o_units = []
    for i, (bi, h) in enumerate(units):
        st_ref[bi, h] = sts[i]
        o_units.append(jnp.concatenate(o_parts[i], axis=0) + _dot(attns[i], jnp.concatenate(u_parts[i], axis=0)))
    for bi in range(nbat):
        o = jnp.concatenate(o_units[bi * GDN_H:(bi + 1) * GDN_H], axis=1)
        if rev:
            y = _head_rmsnorm_gate(of_ref[bi] + o, ng_ref[...], z_ref[bi], GDN_H, GDN_D)
            y_ref[bi] = y.astype(y_ref.dtype)
        else:
            o_ref[bi] = o


def _gdn_dir(p, qkv, conv_w, a_log, dt_bias, ng, o_f, *, rev, nblk):
    nbat, s1, _ = p.shape
    blk = _scan_block(rev, nblk)
    d = 1 if rev else 0
    width = 3 * BRANCH_W
    lane0 = 2 * GDN_H + GDN_H * d
    alog_row = jnp.zeros((1, LANE), F32).at[0, lane0:lane0 + GDN_H].set(a_log[d])
    dtb_row = jnp.zeros((1, LANE), F32).at[0, lane0:lane0 + GDN_H].set(dt_bias[d])
    if rev:
        in_specs, args = [_blk_spec(nbat, width, 0, blk)], [qkv]
    else:
        in_specs, args = _conv_inputs(p, P_GDN_QKV, conv_w, jnp.zeros((width,), F32), nblk)
    in_specs += [_blk_spec(nbat, 512, P_GDN_Z, blk), _blk_spec(nbat, LANE, P_GDN_SC, blk),
                 _const_spec((1, LANE)), _const_spec((1, LANE)), _const_spec((TOK_BLK, TOK_BLK)),
                 _const_spec((1, GDN_D))]
    args += [p, p, alog_row, dtb_row, _cumsum_matrix(rev), ng.reshape(1, -1)]
    return _scan_call(_gdn_kernel, "gdn", in_specs, args, o_f, rev=rev, nbat=nbat, s1=s1, nblk=nblk,
                      state_shape=(GDN_H, GDN_D, GDN_D), conv_width=width)


def _ssd_kernel(*refs, rev, nbat, nblk):
    if rev:
        (xbc_ref, dt_ref, z_ref, alog_ref, dtb_ref, ex_ref, lm_ref, dsk_ref, ng_ref, of_ref,
         y_ref, st_ref) = refs
        xbc = [xbc_ref[bi] for bi in range(nbat)]
    else:
        (xm_ref, xp_ref, xn_ref, cw_ref, cb_ref, dt_ref, z_ref, alog_ref, dtb_ref, ex_ref, lm_ref, dsk_ref, ng_ref,
         o_ref, xbcc_ref, st_ref) = refs
        xbc = []
        for bi in range(nbat):
            xbc.append(_conv_silu_block(xm_ref[bi], xp_ref[bi], xn_ref[bi], cw_ref[...], cb_ref[...],
                                        pl.program_id(0), nblk))
            xbcc_ref[bi] = xbc[bi]

    @pl.when(pl.program_id(0) == 0)
    def _():
        st_ref[...] = jnp.zeros_like(st_ref)

    d = 1 if rev else 0
    heads_per_g = M2_H // M2_G
    gw = heads_per_g * M2_P
    n_chunk = TOK_BLK // CHUNK
    mask = _chunk_mask(rev)
    lane = _iota2((TOK_BLK, gw), 1)
    ex = ex_ref[...]

    units = [(bi, g) for bi in range(nbat) for g in range(M2_G)]
    cqs, bks, xvs, b_es, eb_es, accs = [], [], [], [], [], []
    for bi in range(nbat):
        dt = _softplus(dt_ref[bi] + dtb_ref[...])
        loga = -jnp.exp(alog_ref[...]) * dt
        b8 = _select_rows(lm_ref[...], loga)
        b8_t = b8.T
        xv = xbc[bi][:, :BRANCH_W] * _select_cols(dt, ex)
        b_e = _select_cols(b8, ex)
        eb_e = jnp.exp(b_e)
        for g in range(M2_G):
            gs = slice(g * M2_N, (g + 1) * M2_N)
            xs_g = slice(g * gw, (g + 1) * gw)
            bk = xbc[bi][:, BRANCH_W + g * M2_N:BRANCH_W + (g + 1) * M2_N]
            cq = xbc[bi][:, BRANCH_W + (M2_G + g) * M2_N:BRANCH_W + (M2_G + g + 1) * M2_N]
            scores = _dot_nt(cq, bk)
            xv_g = xv[:, xs_g]
            acc = None
            for hh in range(heads_per_g):
                lh = M2_H * d + heads_per_g * g + hh
                diff = b8[:, lh:lh + 1] - b8_t[lh:lh + 1, :]
                dec = jnp.where(mask, jnp.exp(jnp.where(mask, diff, 0.0)), 0.0)
                term = _dot(scores * dec, jnp.where(_idiv(lane, M2_P) == hh, xv_g, 0.0))
                acc = term if acc is None else acc + term
            cqs.append(cq)
            bks.append(bk)
            xvs.append(xv_g)
            b_es.append(b_e[:, xs_g])
            eb_es.append(eb_e[:, xs_g])
            accs.append(acc)

    sts = [st_ref[bi, g] for bi, g in units]
    o_parts = [[None] * n_chunk for _ in units]
    for c in _chunk_order(rev):
        rows = slice(c * CHUNK, (c + 1) * CHUNK)
        last = c * CHUNK if rev else (c + 1) * CHUNK - 1
        for i in range(len(units)):
            b_c = b_es[i][rows]
            b_last = b_es[i][last:last + 1]
            o_parts[i][c] = eb_es[i][rows] * _dot(cqs[i][rows], sts[i])
            ds = _dot_tn(bks[i][rows], xvs[i][rows] * jnp.exp(b_last - b_c))
            sts[i] = jnp.exp(b_last) * sts[i] + ds
    o_units = []
    for i, (bi, g) in enumerate(units):
        st_ref[bi, g] = sts[i]
        o_units.append(accs[i] + jnp.concatenate(o_parts[i], axis=0))
    for bi in range(nbat):
        o = jnp.concatenate(o_units[bi * M2_G:(bi + 1) * M2_G], axis=1)
        if rev:
            y = (of_ref[bi] + o + dsk_ref[...] * xbc[bi][:, :BRANCH_W]) * _silu(z_ref[bi])
            y = y * lax.rsqrt(jnp.mean(y * y, axis=-1, keepdims=True) + RMS_EPS) * ng_ref[...]
            y_ref[bi] = y.astype(y_ref.dtype)
        else:
            o_ref[bi] = o


def _ssd_dir(p, xbc, conv_w, conv_b, a_log, dt_bias, d_skip, ng, o_f, *, rev, nblk):
    nbat, s1, _ = p.shape
    blk = _scan_block(rev, nblk)
    d = 1 if rev else 0
    lane0 = M2_H * d
    alog_row = jnp.zeros((1, LANE), F32).at[0, lane0:lane0 + M2_H].set(a_log[d])
    dtb_row = jnp.zeros((1, LANE), F32).at[0, lane0:lane0 + M2_H].set(dt_bias[d])
    ex = np.zeros((LANE, BRANCH_W), np.float32)
    for h in range(M2_H):
        ex[lane0 + h, h * M2_P:(h + 1) * M2_P] = 1.0
    dsk_row = jnp.repeat(d_skip, M2_P).reshape(1, BRANCH_W)
    if rev:
        in_specs, args = [_blk_spec(nbat, M2_CONV_CH, 0, blk)], [xbc]
    else:
        in_specs, args = _conv_inputs(p, P_M2_XBC, conv_w, conv_b, nblk)
    in_specs += [_blk_spec(nbat, LANE, P_M2_DT, blk), _blk_spec(nbat, 512, P_M2_Z, blk),
                 _const_spec((1, LANE)), _const_spec((1, LANE)), _const_spec((LANE, BRANCH_W)),
                 _const_spec((TOK_BLK, TOK_BLK)), _const_spec((1, BRANCH_W)), _const_spec((1, BRANCH_W))]
    args += [p, p, alog_row, dtb_row, jnp.asarray(ex, dtype=BF16), _cumsum_matrix(rev), dsk_row, ng.reshape(1, -1)]
    return _scan_call(_ssd_kernel, "ssd", in_specs, args, o_f, rev=rev, nbat=nbat, s1=s1, nblk=nblk,
                      state_shape=(M2_G, M2_N, (M2_H // M2_G) * M2_P), conv_width=M2_CONV_CH)


def _final_norm_kernel(x_ref, g_ref, o_ref):
    x = x_ref[...]
    o_ref[...] = x * lax.rsqrt(jnp.mean(x * x, axis=-1, keepdims=True) + RMS_EPS) * g_ref[...]


def _final_norm(xs, g, *, nbat, nblk, lc):
    D = xs.shape[1]
    cb = lc // TOK_BLK
    nlat = nblk - cb
    out = pl.pallas_call(
        _final_norm_kernel,
        grid=(nbat, nlat),
        in_specs=[pl.BlockSpec((TOK_BLK, D), lambda b, s: (b * nblk + cb + s, 0)),
                  pl.BlockSpec((1, D), lambda b, s: (0, 0))],
        out_specs=pl.BlockSpec((TOK_BLK, D), lambda b, s: (b * nlat + s, 0)),
        out_shape=jax.ShapeDtypeStruct((nbat * nlat * TOK_BLK, D), F32),
        compiler_params=_cparams(("arbitrary", "arbitrary")),
        name="final_norm",
    )(xs, g.reshape(1, D))
    return out.reshape(nbat, nlat * TOK_BLK, D)


def _row_tile(s1, limit):
    for tm in range(limit - limit % 16, 0, -16):
        if s1 % tm == 0:
            return tm
    raise ValueError(f"no row tile for sequence length {s1}")


def kernel(x, c, ctx, c_ctx, norm1_g, norm2_g, w_ada, b_ada, w_in, b_merge, gla_a2, gla_ab, gla_norm_g, na_rpb, gdn_conv, gdn_a_log, gdn_dt_bias, gdn_norm_g, m2_conv, m2_conv_b, m2_a_log, m2_dt_bias, m2_d, m2_norm_g, w_branch, w_out, w_ffn1, w_ffn3, w_ffn2, final_norm_g):
    nbat, t, D = x.shape
    lc = ctx.shape[1]
    depth = w_in.shape[0]
    assert D == D_MODEL and lc == TOK_BLK and t % TOK_BLK == 0 and t // GRID_W >= 3 * ROWS_PER_BLK
    assert nbat + 1 <= MOD_ROWS
    s1 = lc + t
    nblk = s1 // TOK_BLK
    tm = _row_tile(s1, 1056)
    geo = dict(s1=s1, lc=lc, nb=nbat)

    cvec = jnp.concatenate([c, c_ctx[None], jnp.zeros((MOD_ROWS - nbat - 1, D), F32)], axis=0)
    mods = _ada(cvec, w_ada, b_ada).reshape(depth, MOD_ROWS, 6, D).transpose(0, 2, 1, 3)
    cos, sin = _rope_tables(lc, t)
    wb_in, wb_branch, wb_out, wb_ffn1, wb_ffn3, wb_ffn2 = (
        w.astype(BF16) for w in (w_in, w_branch, w_out, w_ffn1, w_ffn3, w_ffn2))
    flat = lambda y: y.reshape(nbat * s1, BRANCH_W)

    for l in range(depth):
        sh1, sc1, g1, sh2, sc2, g2 = (mods[l, i] for i in range(6))

        if l == 0:
            xs, h = _concat_norm(ctx, x, norm1_g[l], sc1, sh1, nblk=nblk)
        else:
            h = _normmod(xs, norm1_g[l], sc1, sh1, nbat=nbat, nblk=nblk)
        w_mix, w_gate = _split_in_weights(wb_in, l)
        p = _matmul(h, w_mix, tm=tm, tn=MIX_TILE, out_dtype=F32).reshape(nbat, s1, P_COLS)

        o_f = _gla_dir(p, cos, sin, gla_a2[l], gla_ab[l], gla_norm_g[l], None, rev=False, nblk=nblk)
        ya = _gla_dir(p, cos, sin, gla_a2[l], gla_ab[l], gla_norm_g[l], o_f, rev=True, nblk=nblk)

        yb = _na(p, _na_bias_tiles(na_rpb[l]), nblk=nblk)

        gdn_par = (gdn_conv[l], gdn_a_log[l], gdn_dt_bias[l], gdn_norm_g[l])
        o_f, qkv = _gdn_dir(p, None, *gdn_par, None, rev=False, nblk=nblk)
        yc, = _gdn_dir(p, qkv, *gdn_par, o_f, rev=True, nblk=nblk)

        ssd_par = (m2_conv[l], m2_conv_b[l], m2_a_log[l], m2_dt_bias[l], m2_d[l], m2_norm_g[l])
        o_f, xbc = _ssd_dir(p, None, *ssd_par, None, rev=False, nblk=nblk)
        yd, = _ssd_dir(p, xbc, *ssd_par, o_f, rev=True, nblk=nblk)

        ys = (flat(ya), flat(yb), flat(yc), flat(yd))
        merged = _merge(h, w_gate, b_merge[l], ys, wb_branch, l, tm=tm, tn=512)
        xs, h2 = _outproj_norm(merged, wb_out, l, xs, g1, norm2_g[l], sc2, sh2,
                               tm=_row_tile(s1, 528), **geo)

        u = _ffn_up(h2, wb_ffn1, wb_ffn3, l, tm=_row_tile(s1, 2112), tn=512)
        xs = _matmul_residual(u, wb_ffn2, l, xs, g2, tm=tm, tn=512, name="ffn_down", **geo)

    return _final_norm(xs, final_norm_g, nbat=nbat, nblk=nblk, lc=lc)
```
